```python
import math
import jax, jax.numpy as jnp
from jax import lax
import numpy as np

D_MODEL = 1024
BATCH = 8
SEQ = 2048
DEPTH = 2
DEC_BATCH = 128
DEC_SEQ = 8
PAST_LEN = 16384
PAGE_SIZE = 128

MIX_WIDTH = D_MODEL
HG_WIDTH = MIX_WIDTH // 2
HG_DV = 128
HG_HEADS = HG_WIDTH // HG_DV
HG_DK = 128
HG_FDIM = HG_HEADS * HG_DK
HG_CHUNK = 64
SSD_WIDTH = MIX_WIDTH - HG_WIDTH
SSD_HEADDIM = 64
SSD_HEADS = SSD_WIDTH // SSD_HEADDIM
SSD_GROUPS = 2
SSD_HPG = SSD_HEADS // SSD_GROUPS
SSD_STATE = 128
SSD_CHUNK = 128
CONV_WIDTH = 4
CONV_DIM = SSD_WIDTH + 2 * SSD_GROUPS * SSD_STATE
SPLIT_SIZES = (HG_FDIM, HG_FDIM, HG_WIDTH, HG_WIDTH, SSD_WIDTH, CONV_DIM, SSD_HEADS)
SPLIT_IDX = tuple(int(v) for v in np.cumsum(SPLIT_SIZES)[:-1])
IN_PROJ_DIM = int(sum(SPLIT_SIZES))
D_FF = 2816
N_EXPERTS = 8
TOP_K = 2
D_FF_EXPERT = 3584
N_DENSE = (DEPTH + 1) // 2
N_MOE = DEPTH // 2
EPS = 1e-6

kernel_name = "hymba_style_hgrn2_ssd_decoder_step"


def rmsnorm(x, g):
    xf = x.astype(jnp.float32)
    y = xf * lax.rsqrt(jnp.mean(xf * xf, axis=-1, keepdims=True) + EPS)
    return (y * g.astype(jnp.float32)).astype(x.dtype)


def _to_chunks(a, c):
    bsz, L = a.shape[:2]
    n = -(-L // c)
    a = jnp.pad(a, [(0, 0), (0, n * c - L)] + [(0, 0)] * (a.ndim - 2))
    a = a.reshape((bsz, n, c) + a.shape[2:])
    return jnp.moveaxis(a, 1, 0)


def _from_chunks(a, L):
    a = jnp.moveaxis(a, 0, 1)
    bsz, n, c = a.shape[:3]
    return a.reshape((bsz, n * c) + a.shape[3:])[:, :L]


def hgrn2_chunked(q, k, v, logf, s0):
    L = q.shape[1]
    c = min(HG_CHUNK, L)
    causal = jnp.tril(jnp.ones((c, c), dtype=bool))[None, :, :, None, None]
    xs = tuple(_to_chunks(t.astype(jnp.float32), c) for t in (q, k, v, logf))

    def step(s, inp):
        qb, kb, vb, gb = inp
        G = jnp.cumsum(gb, axis=1)
        o_inter = jnp.einsum('bthk,bhkv->bthv', qb * jnp.exp(G), s)
        diff = G[:, :, None] - G[:, None, :]
        decay = jnp.exp(jnp.where(causal, diff, -jnp.inf))
        scores = jnp.einsum('bthk,btshk->btsh', qb, decay * kb[:, None])
        o_intra = jnp.einsum('btsh,bshv->bthv', scores, vb)
        g_last = G[:, -1]
        k_dec = kb * jnp.exp(g_last[:, None] - G)
        s_new = jnp.exp(g_last)[..., None] * s + jnp.einsum('bshk,bshv->bhkv', k_dec, vb)
        return s_new, o_inter + o_intra

    s_fin, o = lax.scan(step, s0.astype(jnp.float32), xs)
    return _from_chunks(o, L), s_fin


def ssd_chunked(xdt, a, bm, cm, h0):
    bsz, L = xdt.shape[:2]
    c = min(SSD_CHUNK, L)
    causal = jnp.tril(jnp.ones((c, c), dtype=bool))[None, :, :, None, None]
    xs = tuple(_to_chunks(t.astype(jnp.float32), c) for t in (xdt, a, bm, cm))

    def step(h, inp):
        xb, ab, bb, cb = inp
        xg = xb.reshape(bsz, c, SSD_GROUPS, SSD_HPG, SSD_HEADDIM)
        acum = jnp.cumsum(ab, axis=1).reshape(bsz, c, SSD_GROUPS, SSD_HPG)
        hg = h.reshape(bsz, SSD_GROUPS, SSD_HPG, SSD_STATE, SSD_HEADDIM)
        y_inter = jnp.einsum('btgn,bgrnp->btgrp', cb, hg) * jnp.exp(acum)[..., None]
        diff = acum[:, :, None] - acum[:, None, :]
        decay = jnp.exp(jnp.where(causal, diff, -jnp.inf))
        cb_bs = jnp.einsum('btgn,bsgn->btsg', cb, bb)
        y_intra = jnp.einsum('btsgr,bsgrp->btgrp', cb_bs[..., None] * decay, xg)
        a_last = acum[:, -1]
        w = jnp.exp(a_last[:, None] - acum)
        h_new = jnp.exp(a_last)[..., None, None] * hg + jnp.einsum('bsgn,bsgrp->bgrnp', bb, xg * w[..., None])
        return h_new.reshape(h.shape), (y_inter + y_intra).reshape(bsz, c, SSD_HEADS, SSD_HEADDIM)

    h_fin, y = lax.scan(step, h0.astype(jnp.float32), xs)
    return _from_chunks(y, L), h_fin


def parallel_mixer(hn, lb, w_in, conv_w, conv_b, a_log, dt_bias, d_skip, hg_norm, ssd_norm, w_out,
                   s_hg, s_ssd, conv_buf):
    bsz, L, _ = hn.shape
    dtype = hn.dtype
    proj = jnp.einsum('bld,de->ble', hn, w_in)
    q, f, i, og, z, xbc, dt = jnp.split(proj, SPLIT_IDX, axis=-1)

    q = jax.nn.silu(q).reshape(bsz, L, HG_HEADS, HG_DK)
    fz = f.astype(jnp.float32)
    lbf = lb.astype(jnp.float32)
    log_f = jnp.logaddexp(jnp.log(lbf), jnp.log1p(-lbf) + jax.nn.log_sigmoid(fz))
    k_in = (1.0 - lbf) * jax.nn.sigmoid(-fz)
    log_f = log_f.reshape(bsz, L, HG_HEADS, HG_DK)
    k_in = k_in.reshape(bsz, L, HG_HEADS, HG_DK)
    v = i.reshape(bsz, L, HG_HEADS, HG_DV)
    o_hg, s_hg_new = hgrn2_chunked(q, k_in, v, log_f, s_hg)
    o_hg = rmsnorm(o_hg, hg_norm.reshape(HG_HEADS, HG_DV)).reshape(bsz, L, HG_WIDTH)
    o_hg = (o_hg * jax.nn.silu(og.astype(jnp.float32))).astype(dtype)

    xpad = jnp.concatenate([conv_buf.astype(dtype), xbc], axis=1)
    conv = conv_b[None, None, :]
    for tap in range(CONV_WIDTH):
        conv = conv + conv_w[tap][None, None, :] * xpad[:, tap:tap + L]
    conv = jax.nn.silu(conv)
    conv_buf_new = xpad[:, xpad.shape[1] - (CONV_WIDTH - 1):]
    xs, bm, cm = jnp.split(conv, (SSD_WIDTH, SSD_WIDTH + SSD_GROUPS * SSD_STATE), axis=-1)
    xs = xs.reshape(bsz, L, SSD_HEADS, SSD_HEADDIM)
    bm = bm.reshape(bsz, L, SSD_GROUPS, SSD_STATE)
    cm = cm.reshape(bsz, L, SSD_GROUPS, SSD_STATE)
    dtf = jax.nn.softplus(dt.astype(jnp.float32) + dt_bias.astype(jnp.float32))
    A = -jnp.exp(a_log.astype(jnp.float32))
    xf = xs.astype(jnp.float32)
    y, s_ssd_new = ssd_chunked(xf * dtf[..., None], dtf * A, bm, cm, s_ssd)
    y = y + d_skip.astype(jnp.float32)[:, None] * xf
    y = y.reshape(bsz, L, SSD_WIDTH) * jax.nn.silu(z.astype(jnp.float32))
    y = rmsnorm(y.reshape(bsz, L, SSD_GROUPS, SSD_WIDTH // SSD_GROUPS),
                ssd_norm.reshape(SSD_GROUPS, SSD_WIDTH // SSD_GROUPS)).reshape(bsz, L, SSD_WIDTH)

    mixed = jnp.concatenate([o_hg, y.astype(dtype)], axis=-1)
    out = jnp.einsum('ble,ed->bld', mixed, w_out)
    return out, s_hg_new, s_ssd_new, conv_buf_new


def swiglu(h, w_in, w_out):
    g, u = jnp.split(jnp.einsum('bld,df->blf', h, w_in), 2, axis=-1)
    return jnp.einsum('blf,fd->bld', jax.nn.silu(g) * u, w_out)


def moe_swiglu(h, w_router, w_exp_in, w_exp_out):
    logits = jnp.einsum('bld,de->ble', h, w_router).astype(jnp.float32)
    top_v, top_i = lax.top_k(logits, TOP_K)
    gates = jax.nn.softmax(top_v, axis=-1)
    dense_gate = jnp.sum(jax.nn.one_hot(top_i, N_EXPERTS, dtype=jnp.float32) * gates[..., None], axis=-2)
    dense_gate = dense_gate.astype(h.dtype)
    y = jnp.zeros_like(h)
    for e in range(N_EXPERTS):
        y = y + dense_gate[..., e:e + 1] * swiglu(h, w_exp_in[e], w_exp_out[e])
    return y


def trunk(x, s_hg, s_ssd, conv_buf, norm_mix, w_in, conv_w, conv_b, a_log, dt_bias, d_skip,
          lb_param, hg_norm, ssd_norm, w_out, norm_ffn, w_ffn_in, w_ffn_out, w_router,
          w_exp_in, w_exp_out, norm_final):
    lb_p = jax.nn.softmax(lb_param.astype(jnp.float32), axis=0)
    h = x
    new_hg, new_ssd, new_conv = [], [], []
    for l in range(DEPTH):
        lb = jnp.sum(lb_p[1:l + 1], axis=0)
        mix, sh, ss, cb = parallel_mixer(rmsnorm(h, norm_mix[l]), lb, w_in[l], conv_w[l], conv_b[l],
                                         a_log[l], dt_bias[l], d_skip[l], hg_norm[l], ssd_norm[l],
                                         w_out[l], s_hg[l], s_ssd[l], conv_buf[l])
        h = h + mix
        hn = rmsnorm(h, norm_ffn[l])
        if l % 2 == 0:
            h = h + swiglu(hn, w_ffn_in[l // 2], w_ffn_out[l // 2])
        else:
            h = h + moe_swiglu(hn, w_router[l // 2], w_exp_in[l // 2], w_exp_out[l // 2])
        new_hg.append(sh.astype(s_hg.dtype))
        new_ssd.append(ss.astype(s_ssd.dtype))
        new_conv.append(cb.astype(conv_buf.dtype))
    return rmsnorm(h, norm_final), jnp.stack(new_hg), jnp.stack(new_ssd), jnp.stack(new_conv)


def setup_inputs(seed: int = 0) -> dict:
    key = jax.random.key(seed)
    ks = jax.random.split(key, 24)
    f32 = jnp.float32
    nrm = lambda k, shape, s: jax.random.normal(k, shape, f32) * s
    dt0 = jnp.exp(jax.random.uniform(ks[9], (DEPTH, SSD_HEADS), f32, math.log(1e-3), math.log(1e-1)))
    return {
        "x_prompt": nrm(ks[0], (BATCH, SEQ, D_MODEL), 1.0),
        "x_sample": nrm(ks[1], (DEC_BATCH, DEC_SEQ, D_MODEL), 1.0),
        "state_hgrn": nrm(ks[2], (DEPTH, DEC_BATCH, HG_HEADS, HG_DK, HG_DV), 0.5),
        "state_ssm": nrm(ks[3], (DEPTH, DEC_BATCH, SSD_HEADS, SSD_STATE, SSD_HEADDIM), 0.1),
        "state_conv": nrm(ks[4], (DEPTH, DEC_BATCH, CONV_WIDTH - 1, CONV_DIM), 1.0),
        "norm_mix": 1.0 + nrm(ks[5], (DEPTH, D_MODEL), 0.01),
        "w_in": nrm(ks[6], (DEPTH, D_MODEL, IN_PROJ_DIM), D_MODEL ** -0.5),
        "conv_w": nrm(ks[7], (DEPTH, CONV_WIDTH, CONV_DIM), CONV_WIDTH ** -0.5),
        "conv_b": nrm(ks[8], (DEPTH, CONV_DIM), 0.01),
        "a_log": jnp.log(jax.random.uniform(ks[10], (DEPTH, SSD_HEADS), f32, 1.0, 16.0)),
        "dt_bias": dt0 + jnp.log(-jnp.expm1(-dt0)),
        "d_skip": 1.0 + nrm(ks[11], (DEPTH, SSD_HEADS), 0.01),
        "lb_param": nrm(ks[12], (DEPTH, HG_FDIM), 0.1),
        "hg_norm": 1.0 + nrm(ks[13], (DEPTH, HG_WIDTH), 0.01),
        "ssd_norm": 1.0 + nrm(ks[14], (DEPTH, SSD_WIDTH), 0.01),
        "w_out": nrm(ks[15], (DEPTH, MIX_WIDTH, D_MODEL), MIX_WIDTH ** -0.5),
        "norm_ffn": 1.0 + nrm(ks[16], (DEPTH, D_MODEL), 0.01),
        "w_ffn_in": nrm(ks[17], (N_DENSE, D_MODEL, 2 * D_FF), D_MODEL ** -0.5),
        "w_ffn_out": nrm(ks[18], (N_DENSE, D_FF, D_MODEL), D_FF ** -0.5),
        "w_router": nrm(ks[19], (N_MOE, D_MODEL, N_EXPERTS), D_MODEL ** -0.5),
        "w_exp_in": nrm(ks[20], (N_MOE, N_EXPERTS, D_MODEL, 2 * D_FF_EXPERT), D_MODEL ** -0.5),
        "w_exp_out": nrm(ks[21], (N_MOE, N_EXPERTS, D_FF_EXPERT, D_MODEL), D_FF_EXPERT ** -0.5),
        "norm_final": 1.0 + nrm(ks[22], (D_MODEL,), 0.01),
    }


def reference(x_prompt, x_sample, state_hgrn, state_ssm, state_conv, norm_mix, w_in, conv_w, conv_b,
              a_log, dt_bias, d_skip, lb_param, hg_norm, ssd_norm, w_out, norm_ffn, w_ffn_in,
              w_ffn_out, w_router, w_exp_in, w_exp_out, norm_final):
    weights = (norm_mix, w_in, conv_w, conv_b, a_log, dt_bias, d_skip, lb_param, hg_norm, ssd_norm,
               w_out, norm_ffn, w_ffn_in, w_ffn_out, w_router, w_exp_in, w_exp_out, norm_final)
    hg0 = jnp.zeros((DEPTH, BATCH) + state_hgrn.shape[2:], state_hgrn.dtype)
    ssm0 = jnp.zeros((DEPTH, BATCH) + state_ssm.shape[2:], state_ssm.dtype)
    conv0 = jnp.zeros((DEPTH, BATCH) + state_conv.shape[2:], state_conv.dtype)
    y_prompt, hg_p, ssm_p, conv_p = trunk(x_prompt, hg0, ssm0, conv0, *weights)
    y_sample, hg_s, ssm_s, conv_s = trunk(x_sample, state_hgrn, state_ssm, state_conv, *weights)
    return (y_prompt, y_sample, hg_p, ssm_p, conv_p, hg_s, ssm_s, conv_s)
```

```python
import functools

import jax
import jax.numpy as jnp
import numpy as np
from jax import lax
from jax.experimental import pallas as pl
from jax.experimental.pallas import tpu as pltpu

F32 = jnp.float32
BF16 = jnp.bfloat16
HIGHEST = lax.Precision.HIGHEST

D_MODEL = 1024
BATCH = 8
SEQ = 2048
DEPTH = 2
DEC_BATCH = 128
DEC_SEQ = 8
HG_HEADS = 4
HG_D = 128
HG_W = HG_HEADS * HG_D
SSD_HEADS = 8
SSD_P = 64
SSD_N = 128
SSD_W = SSD_HEADS * SSD_P
SSD_PAIRS = SSD_HEADS // 2
CONV_W = 4
CONV_DIM = 1024
D_FF = 2816
N_EXPERTS = 8
D_FF_EXPERT = 3584
EPS = 1e-6

LANES = 128
C_Q, C_F, C_I, C_OG, C_Z, C_XBC, C_DT = 0, 512, 1024, 1536, 2048, 2560, 3584
PROJ_W = C_DT + LANES
HG_CHUNK = 64
SSD_CHUNK = 128
SAMPLE_SEQS = 8
HG_SAFE_RANGE = 80.0

T_PROMPT = BATCH * SEQ
T_SAMPLE = DEC_BATCH * DEC_SEQ
T_ALL = T_PROMPT + T_SAMPLE

VMEM_LIMIT = 56 * 1024 * 1024


def _sigmoid(x):
    return 1.0 / (1.0 + jnp.exp(-x))


def _silu(x):
    return x * _sigmoid(x)


def _softplus(x):
    return jnp.maximum(x, 0.0) + jnp.log1p(jnp.exp(-jnp.abs(x)))


def _log_sigmoid(x):
    return jnp.minimum(x, 0.0) - jnp.log1p(jnp.exp(-jnp.abs(x)))


def _rms(x, g):
    return x * lax.rsqrt(jnp.mean(x * x, axis=-1, keepdims=True) + EPS) * g


def _dot(a, b):
    return jnp.dot(a.astype(BF16), b.astype(BF16), preferred_element_type=F32)


def _dot_nt(a, b):
    return lax.dot_general(a.astype(BF16), b.astype(BF16), (((1,), (1,)), ((), ())),
                           preferred_element_type=F32)


def _dot_exact(a, b):
    return jnp.dot(a, b, precision=HIGHEST, preferred_element_type=F32)


def _dot_nt_exact(a, b):
    return lax.dot_general(a, b, (((1,), (1,)), ((), ())), precision=HIGHEST,
                           preferred_element_type=F32)


def _iota2(shape, dim):
    return lax.broadcasted_iota(jnp.int32, shape, dim)


def _seq_masks(rows, seq_len):
    t = _iota2((rows, rows), 0)
    s = _iota2((rows, rows), 1)
    same = (t // seq_len) == (s // seq_len)
    causal = same & (s <= t)
    ref = same & ((s % seq_len) < seq_len // 2)
    return causal, ref, same


def _proj_body(h_ref, g_ref, w_ref, o_ref):
    hn = _rms(h_ref[...], g_ref[...])
    o_ref[...] = _dot(hn, w_ref[...])


def _proj_call(h, g, w_bf):
    tm = 512
    return pl.pallas_call(
        _proj_body,
        grid=(T_ALL // tm,),
        in_specs=[pl.BlockSpec((tm, D_MODEL), lambda i: (i, 0)),
                  pl.BlockSpec((1, D_MODEL), lambda i: (0, 0)),
                  pl.BlockSpec((D_MODEL, PROJ_W), lambda i: (0, 0))],
        out_specs=pl.BlockSpec((tm, PROJ_W), lambda i: (i, 0)),
        out_shape=jax.ShapeDtypeStruct((T_ALL, PROJ_W), F32),
        compiler_params=pltpu.CompilerParams(dimension_semantics=("arbitrary",),
                                             vmem_limit_bytes=VMEM_LIMIT),
        name="in_proj",
    )(h, g, w_bf)


def _hgrn_gates(p_q, p_f, la, lc, omlb):
    q = _silu(p_q)
    b = lc + _log_sigmoid(p_f)
    logf = jnp.maximum(la, b) + jnp.log1p(jnp.exp(-jnp.abs(la - b)))
    kin = omlb * _sigmoid(-p_f)
    return q, logf, kin


def _hgrn_decays(logf, causal, ref, same):
    rows = logf.shape[0]
    m = jnp.concatenate([causal.astype(F32), ref.astype(F32), same.astype(F32)], axis=0)
    g3 = _dot_exact(m, logf)
    return g3[:rows], g3[rows:2 * rows], g3[2 * rows:]


def _hgrn_head_out(o, hgn, og):
    return _rms(o, hgn) * _silu(og)


def _ssd_inputs(conv, p_dt, dtb, a_neg):
    conv = _silu(conv)
    xs = conv[:, :SSD_W]
    bm = conv[:, SSD_W:SSD_W + 2 * SSD_N]
    cm = conv[:, SSD_W + 2 * SSD_N:]
    dtf = _softplus(p_dt + dtb)
    a = dtf * a_neg
    return xs, bm, cm, dtf, a


def _lane_bcast(x, lane, width):
    return jnp.broadcast_to(x[:, lane:lane + 1], (x.shape[0], width))


def _ssd_finish(y, xs, z, dsk, ssn):
    y = (y + dsk * xs) * _silu(z)
    half = SSD_W // 2
    return jnp.concatenate([_rms(y[:, :half], ssn[:, :half]), _rms(y[:, half:], ssn[:, half:])],
                           axis=1)


def _scan_prompt_body(p_ref, la_ref, lc_ref, omlb_ref, hgn_ref, cw_ref, cb_ref, dtb_ref, an_ref,
                      dsk_ref, ssn_ref,
                      mix_ref, ohg_ref, ossm_ref, oconv_ref,
                      st_ref, hp_ref, xpad_ref):
    j = pl.program_id(1)
    tl = SSD_CHUNK

    @pl.when(j == 0)
    def _():
        st_ref[...] = jnp.zeros_like(st_ref)
        hp_ref[...] = jnp.zeros_like(hp_ref)
        xpad_ref[0:8, :] = jnp.zeros((8, CONV_DIM), F32)

    causal, ref, same = _seq_masks(HG_CHUNK, HG_CHUNK)
    for c in range(tl // HG_CHUNK):
        r0 = c * HG_CHUNK
        rows = slice(r0, r0 + HG_CHUNK)
        q, logf, kin = _hgrn_gates(p_ref[rows, C_Q:C_Q + HG_W], p_ref[rows, C_F:C_F + HG_W],
                                   la_ref[...], lc_ref[...], omlb_ref[...])
        v = p_ref[rows, C_I:C_I + HG_W]
        g, gmid, glast = _hgrn_decays(logf, causal, ref, same)
        qt = q * jnp.exp(g - gmid)
        kt = kin * jnp.exp(gmid - g)
        qh = q * jnp.exp(g)
        kd = kin * jnp.exp(glast - g)
        ds = jnp.exp(glast[0:1, :])
        for h in range(HG_HEADS):
            sl = slice(h * HG_D, (h + 1) * HG_D)
            sc = jnp.where(causal, _dot_nt(qt[:, sl], kt[:, sl]), 0.0)
            st = st_ref[h]
            o = _dot(sc, v[:, sl]) + _dot_nt(qh[:, sl], st)
            st_ref[h] = st * ds[:, sl] + _dot(v[:, sl].T, kd[:, sl])
            og = p_ref[rows, C_OG + h * HG_D:C_OG + (h + 1) * HG_D]
            mix_ref[rows, sl] = _hgrn_head_out(o, hgn_ref[:, sl], og).astype(BF16)

    xpad_ref[8:8 + tl, :] = p_ref[:, C_XBC:C_XBC + CONV_DIM]
    conv = cb_ref[...] + cw_ref[0:1, :] * xpad_ref[5:5 + tl, :]
    for tap in range(1, CONV_W):
        conv = conv + cw_ref[tap:tap + 1, :] * xpad_ref[5 + tap:5 + tap + tl, :]

    @pl.when(j == pl.num_programs(1) - 1)
    def _():
        oconv_ref[0] = xpad_ref[5 + tl:8 + tl, :]

    xpad_ref[0:8, :] = xpad_ref[tl:tl + 8, :]

    xs, bm, cm, dtf, a = _ssd_inputs(conv, p_ref[:, C_DT:C_DT + LANES], dtb_ref[...], an_ref[...])
    tri, _, _ = _seq_masks(tl, tl)
    trif = tri.astype(F32)
    acum = _dot_exact(trif, a)
    acum_t = _dot_nt_exact(a.T, trif)
    lo = _iota2((tl, LANES), 1) < SSD_P
    ys = []
    for pair in range(SSD_PAIRS):
        grp = pair // 2
        cg = cm[:, grp * SSD_N:(grp + 1) * SSD_N]
        bg = bm[:, grp * SSD_N:(grp + 1) * SSD_N]
        cb = _dot_nt(cg, bg)
        bg_t = bg.T
        psl = slice(pair * LANES, (pair + 1) * LANES)
        r0, r1 = 2 * pair, 2 * pair + 1
        xp = xs[:, psl] * jnp.where(lo, _lane_bcast(dtf, r0, LANES), _lane_bcast(dtf, r1, LANES))
        intra, upd, einter = [], [], []
        for r in (r0, r1):
            col = _lane_bcast(acum, r, tl)
            row = jnp.broadcast_to(acum_t[r:r + 1, :], (tl, tl))
            dec = jnp.exp(jnp.where(tri, col - row, -jnp.inf))
            intra.append(_dot(cb * dec, xp))
            wrow = jnp.exp(acum_t[r:r + 1, tl - 1:tl] - acum_t[r:r + 1, :])
            upd.append(_dot(bg_t * wrow, xp))
            einter.append(jnp.exp(col))
        hp = hp_ref[pair]
        y = jnp.where(lo, intra[0], intra[1]) + jnp.where(lo, einter[0], einter[1]) * _dot(cg, hp)
        alast = acum[tl - 1:tl, :]
        ea = jnp.exp(jnp.where(lo[0:1, :], _lane_bcast(alast, r0, LANES),
                               _lane_bcast(alast, r1, LANES)))
        hp_ref[pair] = ea * hp + jnp.where(lo, upd[0], upd[1])
        ys.append(y)
    y = jnp.concatenate(ys, axis=1)
    y = _ssd_finish(y, xs, p_ref[:, C_Z:C_Z + SSD_W], dsk_ref[...], ssn_ref[...])
    mix_ref[:, HG_W:] = y.astype(BF16)

    @pl.when(j == pl.num_programs(1) - 1)
    def _():
        for h in range(HG_HEADS):
            ohg_ref[0, h] = st_ref[h].T
        for pair in range(SSD_PAIRS):
            hp = hp_ref[pair]
            ossm_ref[0, 2 * pair] = hp[:, :SSD_P]
            ossm_ref[0, 2 * pair + 1] = hp[:, SSD_P:]


def _row_spec(width):
    return pl.BlockSpec((1, width), lambda b, j: (0, 0))


def _scan_prompt_call(proj, prm):
    tl = SSD_CHUNK
    nt = SEQ // tl
    in_specs = [pl.BlockSpec((tl, PROJ_W), lambda b, j: (b * nt + j, 0)),
                _row_spec(HG_W), _row_spec(HG_W), _row_spec(HG_W), _row_spec(HG_W),
                pl.BlockSpec((CONV_W, CONV_DIM), lambda b, j: (0, 0)), _row_spec(CONV_DIM),
                _row_spec(LANES), _row_spec(LANES), _row_spec(SSD_W), _row_spec(SSD_W)]
    out_specs = [pl.BlockSpec((tl, D_MODEL), lambda b, j: (b * nt + j, 0)),
                 pl.BlockSpec((1, HG_HEADS, HG_D, HG_D), lambda b, j: (b, 0, 0, 0)),
                 pl.BlockSpec((1, SSD_HEADS, SSD_N, SSD_P), lambda b, j: (b, 0, 0, 0)),
                 pl.BlockSpec((1, CONV_W - 1, CONV_DIM), lambda b, j: (b, 0, 0))]
    out_shape = [jax.ShapeDtypeStruct((T_ALL, D_MODEL), BF16),
                 jax.ShapeDtypeStruct((BATCH, HG_HEADS, HG_D, HG_D), F32),
                 jax.ShapeDtypeStruct((BATCH, SSD_HEADS, SSD_N, SSD_P), F32),
                 jax.ShapeDtypeStruct((BATCH, CONV_W - 1, CONV_DIM), F32)]
    return pl.pallas_call(
        _scan_prompt_body,
        grid=(BATCH, nt),
        in_specs=in_specs, out_specs=out_specs, out_shape=out_shape,
        scratch_shapes=[pltpu.VMEM((HG_HEADS, HG_D, HG_D), F32),
                        pltpu.VMEM((SSD_PAIRS, SSD_N, LANES), F32),
                        pltpu.VMEM((tl + 8, CONV_DIM), F32)],
        compiler_params=pltpu.CompilerParams(dimension_semantics=("arbitrary", "arbitrary"),
                                             vmem_limit_bytes=VMEM_LIMIT),
        name="scan_prompt",
    )(proj, prm["la"], prm["lc"], prm["omlb"], prm["hgn"], prm["cw"], prm["cb"], prm["dtb"],
      prm["an"], prm["dsk"], prm["ssn"])


def _scan_sample_body(p_ref, shg_ref, sssm_ref, sconv_ref, la_ref, lc_ref, omlb_ref, hgn_ref,
                      cw_ref, cb_ref, dtb_ref, an_ref, dsk_ref, ssn_ref, mixin_ref,
                      mix_ref, ohg_ref, ossm_ref, oconv_ref, xpad_ref):
    del mixin_ref
    nb, sl_len = SAMPLE_SEQS, DEC_SEQ
    rows = nb * sl_len
    causal, ref, same = _seq_masks(rows, sl_len)
    rowseq = _iota2((rows, LANES), 0) // sl_len

    q, logf, kin = _hgrn_gates(p_ref[:, C_Q:C_Q + HG_W], p_ref[:, C_F:C_F + HG_W],
                               la_ref[...], lc_ref[...], omlb_ref[...])
    v = p_ref[:, C_I:C_I + HG_W]
    g, gmid, glast = _hgrn_decays(logf, causal, ref, same)
    qt = q * jnp.exp(g - gmid)
    kt = kin * jnp.exp(gmid - g)
    qh = q * jnp.exp(g)
    kd = kin * jnp.exp(glast - g)
    ds_t = jnp.exp(glast).T
    kd_t = kd.T
    for h in range(HG_HEADS):
        sl = slice(h * HG_D, (h + 1) * HG_D)
        sc = jnp.where(causal, _dot_nt(qt[:, sl], kt[:, sl]), 0.0)
        o_intra = _dot(sc, v[:, sl])
        o_inter = []
        for b in range(nb):
            s_old = shg_ref[b, h]
            o_inter.append(_dot(qh[b * sl_len:(b + 1) * sl_len, sl], s_old))
            vb = jnp.where(rowseq == b, v[:, sl], 0.0)
            dcol = _lane_bcast(ds_t[sl, :], b * sl_len, HG_D)
            ohg_ref[b, h] = dcol * s_old + _dot(kd_t[sl, :], vb)
        o = o_intra + jnp.concatenate(o_inter, axis=0)
        og = p_ref[:, C_OG + h * HG_D:C_OG + (h + 1) * HG_D]
        mix_ref[:, sl] = _hgrn_head_out(o, hgn_ref[:, sl], og).astype(BF16)

    convs = []
    for b in range(nb):
        base = 16 * b
        xpad_ref[base + 5:base + 8, :] = sconv_ref[b]
        xpad_ref[base + 8:base + 16, :] = p_ref[b * sl_len:(b + 1) * sl_len, C_XBC:C_XBC + CONV_DIM]
        cv = cb_ref[...] + cw_ref[0:1, :] * xpad_ref[base + 5:base + 13, :]
        for tap in range(1, CONV_W):
            cv = cv + cw_ref[tap:tap + 1, :] * xpad_ref[base + 5 + tap:base + 13 + tap, :]
        convs.append(cv)
        oconv_ref[b] = xpad_ref[base + 13:base + 16, :]
    conv = jnp.concatenate(convs, axis=0)

    xs, bm, cm, dtf, a = _ssd_inputs(conv, p_ref[:, C_DT:C_DT + LANES], dtb_ref[...], an_ref[...])
    causf = causal.astype(F32)
    acum = _dot_exact(causf, a)
    atot = _dot_exact(same.astype(F32), a)
    acum_t = _dot_nt_exact(a.T, causf)
    wall = jnp.exp(atot - acum)
    eall = jnp.exp(acum)
    eatot = jnp.exp(atot)
    lo = _iota2((rows, LANES), 1) < SSD_P
    ys = []
    for pair in range(SSD_PAIRS):
        grp = pair // 2
        cg = cm[:, grp * SSD_N:(grp + 1) * SSD_N]
        bg = bm[:, grp * SSD_N:(grp + 1) * SSD_N]
        cb = _dot_nt(cg, bg)
        bg_t = bg.T
        psl = slice(pair * LANES, (pair + 1) * LANES)
        r0, r1 = 2 * pair, 2 * pair + 1
        xp = xs[:, psl] * jnp.where(lo, _lane_bcast(dtf, r0, LANES), _lane_bcast(dtf, r1, LANES))
        xw = xp * jnp.where(lo, _lane_bcast(wall, r0, LANES), _lane_bcast(wall, r1, LANES))
        intra = []
        for r in (r0, r1):
            col = _lane_bcast(acum, r, rows)
            row = jnp.broadcast_to(acum_t[r:r + 1, :], (rows, rows))
            dec = jnp.exp(jnp.where(causal, col - row, -jnp.inf))
            intra.append(_dot(cb * dec, xp))
        y_intra = jnp.where(lo, intra[0], intra[1])
        y_inter = []
        for b in range(nb):
            tr = slice(b * sl_len, (b + 1) * sl_len)
            h0 = sssm_ref[b, r0]
            h1 = sssm_ref[b, r1]
            yi = jnp.concatenate([_dot(cg[tr, :], h0), _dot(cg[tr, :], h1)], axis=1)
            y_inter.append(yi)
            upd = _dot(bg_t, jnp.where(rowseq == b, xw, 0.0))
            ea0 = jnp.broadcast_to(eatot[b * sl_len:b * sl_len + 1, r0:r0 + 1], (SSD_N, SSD_P))
            ea1 = jnp.broadcast_to(eatot[b * sl_len:b * sl_len + 1, r1:r1 + 1], (SSD_N, SSD_P))
            ossm_ref[b, r0] = ea0 * h0 + upd[:, :SSD_P]
            ossm_ref[b, r1] = ea1 * h1 + upd[:, SSD_P:]
        e_pair = jnp.where(lo, _lane_bcast(eall, r0, LANES), _lane_bcast(eall, r1, LANES))
        ys.append(y_intra + e_pair * jnp.concatenate(y_inter, axis=0))
    y = jnp.concatenate(ys, axis=1)
    y = _ssd_finish(y, xs, p_ref[:, C_Z:C_Z + SSD_W], dsk_ref[...], ssn_ref[...])
    mix_ref[:, HG_W:] = y.astype(BF16)


def _scan_sample_call(proj, mixed, s_hg, s_ssm, s_conv, prm):
    nb = SAMPLE_SEQS
    rows = nb * DEC_SEQ
    row0 = T_PROMPT // rows
    c0 = lambda i: (0, 0)
    rs = lambda w: pl.BlockSpec((1, w), c0)
    in_specs = [pl.BlockSpec((rows, PROJ_W), lambda i: (row0 + i, 0)),
                pl.BlockSpec((nb, HG_HEADS, HG_D, HG_D), lambda i: (i, 0, 0, 0)),
                pl.BlockSpec((nb, SSD_HEADS, SSD_N, SSD_P), lambda i: (i, 0, 0, 0)),
                pl.BlockSpec((nb, CONV_W - 1, CONV_DIM), lambda i: (i, 0, 0)),
                rs(HG_W), rs(HG_W), rs(HG_W), rs(HG_W),
                pl.BlockSpec((CONV_W, CONV_DIM), c0), rs(CONV_DIM),
                rs(LANES), rs(LANES), rs(SSD_W), rs(SSD_W),
                pl.BlockSpec(memory_space=pl.ANY)]
    out_specs = [pl.BlockSpec((rows, D_MODEL), lambda i: (row0 + i, 0)),
                 pl.BlockSpec((nb, HG_HEADS, HG_D, HG_D), lambda i: (i, 0, 0, 0)),
                 pl.BlockSpec((nb, SSD_HEADS, SSD_N, SSD_P), lambda i: (i, 0, 0, 0)),
                 pl.BlockSpec((nb, CONV_W - 1, CONV_DIM), lambda i: (i, 0, 0))]
    out_shape = [jax.ShapeDtypeStruct((T_ALL, D_MODEL), BF16),
                 jax.ShapeDtypeStruct((DEC_BATCH, HG_HEADS, HG_D, HG_D), F32),
                 jax.ShapeDtypeStruct((DEC_BATCH, SSD_HEADS, SSD_N, SSD_P), F32),
                 jax.ShapeDtypeStruct((DEC_BATCH, CONV_W - 1, CONV_DIM), F32)]
    return pl.pallas_call(
        _scan_sample_body,
        grid=(DEC_BATCH // nb,),
        in_specs=in_specs, out_specs=out_specs, out_shape=out_shape,
        scratch_shapes=[pltpu.VMEM((16 * nb, CONV_DIM), F32)],
        input_output_aliases={14: 0},
        compiler_params=pltpu.CompilerParams(dimension_semantics=("arbitrary",),
                                             vmem_limit_bytes=VMEM_LIMIT),
        name="scan_sample",
    )(proj, s_hg, s_ssm, s_conv, prm["la"], prm["lc"], prm["omlb"], prm["hgn"], prm["cw"],
      prm["cb"], prm["dtb"], prm["an"], prm["dsk"], prm["ssn"], mixed)


def _out_ffn_body(h_ref, mix_ref, wo_ref, g_ref, wg_ref, wu_ref, wd_ref, o_ref, hn_ref, acc_ref):
    f = pl.program_id(1)

    @pl.when(f == 0)
    def _():
        h1 = h_ref[...] + _dot(mix_ref[...], wo_ref[...])
        acc_ref[...] = h1
        hn_ref[...] = _rms(h1, g_ref[...]).astype(BF16)

    x = hn_ref[...]
    gate = _dot(x, wg_ref[...])
    up = _dot(x, wu_ref[...])
    acc_ref[...] += _dot(_silu(gate) * up, wd_ref[...])

    @pl.when(f == pl.num_programs(1) - 1)
    def _():
        o_ref[...] = acc_ref[...]


def _out_ffn_call(h, mixed, wo_bf, g, wi_bf, wd_bf):
    tm, tf = 512, D_FF // 2
    nf = D_FF // tf
    return pl.pallas_call(
        _out_ffn_body,
        grid=(T_ALL // tm, nf),
        in_specs=[pl.BlockSpec((tm, D_MODEL), lambda i, f: (i, 0)),
                  pl.BlockSpec((tm, D_MODEL), lambda i, f: (i, 0)),
                  pl.BlockSpec((D_MODEL, D_MODEL), lambda i, f: (0, 0)),
                  pl.BlockSpec((1, D_MODEL), lambda i, f: (0, 0)),
                  pl.BlockSpec((D_MODEL, tf), lambda i, f: (0, f)),
                  pl.BlockSpec((D_MODEL, tf), lambda i, f: (0, nf + f)),
                  pl.BlockSpec((tf, D_MODEL), lambda i, f: (f, 0))],
        out_specs=pl.BlockSpec((tm, D_MODEL), lambda i, f: (i, 0)),
        out_shape=jax.ShapeDtypeStruct((T_ALL, D_MODEL), F32),
        scratch_shapes=[pltpu.VMEM((tm, D_MODEL), BF16), pltpu.VMEM((tm, D_MODEL), F32)],
        compiler_params=pltpu.CompilerParams(dimension_semantics=("arbitrary", "arbitrary"),
                                             vmem_limit_bytes=VMEM_LIMIT),
        name="out_ffn",
    )(h, mixed, wo_bf, g, wi_bf, wi_bf, wd_bf)


def _out_router_body(h_ref, mix_ref, wo_ref, g_ref, wr_ref, h1_ref, hn_ref, gate_ref):
    h1 = h_ref[...] + _dot(mix_ref[...], wo_ref[...])
    h1_ref[...] = h1
    hn = _rms(h1, g_ref[...])
    hn_ref[...] = hn.astype(BF16)
    logits = _dot_exact(hn, wr_ref[...])
    lane = _iota2(logits.shape, 1)
    lg = jnp.where(lane < N_EXPERTS, logits, -jnp.inf)
    m1 = jnp.max(lg, axis=1, keepdims=True)
    i1 = jnp.min(jnp.where(lg == m1, lane, LANES), axis=1, keepdims=True)
    lg2 = jnp.where(lane == i1, -jnp.inf, lg)
    m2 = jnp.max(lg2, axis=1, keepdims=True)
    i2 = jnp.min(jnp.where(lg2 == m2, lane, LANES), axis=1, keepdims=True)
    e2 = jnp.exp(m2 - m1)
    g1 = 1.0 / (1.0 + e2)
    g2 = e2 / (1.0 + e2)
    gate_ref[...] = jnp.where(lane == i1, g1, 0.0) + jnp.where(lane == i2, g2, 0.0)


def _out_router_call(h, mixed, wo_bf, g, wr_pad):
    tm = 512
    return pl.pallas_call(
        _out_router_body,
        grid=(T_ALL // tm,),
        in_specs=[pl.BlockSpec((tm, D_MODEL), lambda i: (i, 0)),
                  pl.BlockSpec((tm, D_MODEL), lambda i: (i, 0)),
                  pl.BlockSpec((D_MODEL, D_MODEL), lambda i: (0, 0)),
                  pl.BlockSpec((1, D_MODEL), lambda i: (0, 0)),
                  pl.BlockSpec((D_MODEL, LANES), lambda i: (0, 0))],
        out_specs=[pl.BlockSpec((tm, D_MODEL), lambda i: (i, 0)),
                   pl.BlockSpec((tm, D_MODEL), lambda i: (i, 0)),
                   pl.BlockSpec((tm, LANES), lambda i: (i, 0))],
        out_shape=[jax.ShapeDtypeStruct((T_ALL, D_MODEL), F32),
                   jax.ShapeDtypeStruct((T_ALL, D_MODEL), BF16),
                   jax.ShapeDtypeStruct((T_ALL, LANES), F32)],
        compiler_params=pltpu.CompilerParams(dimension_semantics=("arbitrary",),
                                             vmem_limit_bytes=VMEM_LIMIT),
        name="out_router",
    )(h, mixed, wo_bf, g, wr_pad)


def _moe_body(hn_ref, gate_ref, h1_ref, wg_ref, wu_ref, wd_ref, gf_ref, o_ref, acc_ref):
    e = pl.program_id(1)
    f = pl.program_id(2)

    @pl.when((e == 0) & (f == 0))
    def _():
        acc_ref[...] = h1_ref[...]

    x = hn_ref[...]
    gate = _dot(x, wg_ref[0])
    up = _dot(x, wu_ref[0])
    y = _dot(_silu(gate) * up, wd_ref[0])
    lane = _iota2(gate_ref.shape, 1)
    ge = jnp.sum(jnp.where(lane == e, gate_ref[...], 0.0), axis=1, keepdims=True)
    acc_ref[...] += ge * y

    @pl.when((e == pl.num_programs(1) - 1) & (f == pl.num_programs(2) - 1))
    def _():
        o_ref[...] = _rms(acc_ref[...], gf_ref[...])


def _moe_call(hn_bf, gate, h1, w_in_e, w_out_e, gfin):
    tm, tf = 1024, 512
    nf = D_FF_EXPERT // tf
    return pl.pallas_call(
        _moe_body,
        grid=(T_ALL // tm, N_EXPERTS, nf),
        in_specs=[pl.BlockSpec((tm, D_MODEL), lambda i, e, f: (i, 0)),
                  pl.BlockSpec((tm, LANES), lambda i, e, f: (i, 0)),
                  pl.BlockSpec((tm, D_MODEL), lambda i, e, f: (i, 0)),
                  pl.BlockSpec((1, D_MODEL, tf), lambda i, e, f: (e, 0, f)),
                  pl.BlockSpec((1, D_MODEL, tf), lambda i, e, f: (e, 0, nf + f)),
                  pl.BlockSpec((1, tf, D_MODEL), lambda i, e, f: (e, f, 0)),
                  pl.BlockSpec((1, D_MODEL), lambda i, e, f: (0, 0))],
        out_specs=pl.BlockSpec((tm, D_MODEL), lambda i, e, f: (i, 0)),
        out_shape=jax.ShapeDtypeStruct((T_ALL, D_MODEL), F32),
        scratch_shapes=[pltpu.VMEM((tm, D_MODEL), F32)],
        compiler_params=pltpu.CompilerParams(
            dimension_semantics=("arbitrary", "arbitrary", "arbitrary"),
            vmem_limit_bytes=VMEM_LIMIT),
        name="moe",
    )(hn_bf, gate, h1, w_in_e, w_in_e, w_out_e, gfin)


def _row(x, width=None):
    x = x.astype(F32).reshape(1, -1)
    if width is not None and x.shape[1] < width:
        x = jnp.pad(x, ((0, 0), (0, width - x.shape[1])))
    return x


def _layer_params(l, lb_p, conv_w, conv_b, a_log, dt_bias, d_skip, hg_norm, ssd_norm):
    lb = jnp.sum(lb_p[1:l + 1], axis=0)
    return dict(
        la=_row(jnp.log(lb)), lc=_row(jnp.log1p(-lb)), omlb=_row(1.0 - lb),
        hgn=_row(hg_norm[l]), cw=conv_w[l].astype(F32), cb=_row(conv_b[l]),
        dtb=_row(dt_bias[l], LANES), an=_row(-jnp.exp(a_log[l].astype(F32)), LANES),
        dsk=_row(jnp.repeat(d_skip[l].astype(F32), SSD_P)), ssn=_row(ssd_norm[l]))


def kernel(x_prompt, x_sample, state_hgrn, state_ssm, state_conv, norm_mix, w_in, conv_w, conv_b,
           a_log, dt_bias, d_skip, lb_param, hg_norm, ssd_norm, w_out, norm_ffn, w_ffn_in,
           w_ffn_out, w_router, w_exp_in, w_exp_out, norm_final):
    h = jnp.concatenate([x_prompt.reshape(T_PROMPT, D_MODEL), x_sample.reshape(T_SAMPLE, D_MODEL)],
                        axis=0)
    lb_p = jax.nn.softmax(lb_param.astype(F32), axis=0)
    outs = {k: [] for k in ("hg_p", "ssm_p", "conv_p", "hg_s", "ssm_s", "conv_s")}
    for l in range(DEPTH):
        prm = _layer_params(l, lb_p, conv_w, conv_b, a_log, dt_bias, d_skip, hg_norm, ssd_norm)
        w_in_bf = jnp.pad(w_in[l], ((0, 0), (0, PROJ_W - w_in.shape[2]))).astype(BF16)
        proj = _proj_call(h, _row(norm_mix[l]), w_in_bf)
        mixed, hg_p, ssm_p, conv_p = _scan_prompt_call(proj, prm)
        mixed, hg_s, ssm_s, conv_s = _scan_sample_call(proj, mixed, state_hgrn[l], state_ssm[l],
                                                       state_conv[l], prm)
        for k, val in zip(outs, (hg_p, ssm_p, conv_p, hg_s, ssm_s, conv_s)):
            outs[k].append(val)
        wo_bf = w_out[l].astype(BF16)
        if l % 2 == 0:
            h = _out_ffn_call(h, mixed, wo_bf, _row(norm_ffn[l]), w_ffn_in[l // 2].astype(BF16),
                              w_ffn_out[l // 2].astype(BF16))
        else:
            wr_pad = jnp.pad(w_router[l // 2].astype(F32), ((0, 0), (0, LANES - N_EXPERTS)))
            h1, hn_bf, gate = _out_router_call(h, mixed, wo_bf, _row(norm_ffn[l]), wr_pad)
            h = _moe_call(hn_bf, gate, h1, w_exp_in[l // 2], w_exp_out[l // 2], _row(norm_final))
    y_prompt = h[:T_PROMPT].reshape(BATCH, SEQ, D_MODEL)
    y_sample = h[T_PROMPT:].reshape(DEC_BATCH, DEC_SEQ, D_MODEL)
    return (y_prompt, y_sample, jnp.stack(outs["hg_p"]), jnp.stack(outs["ssm_p"]),
            jnp.stack(outs["conv_p"]), jnp.stack(outs["hg_s"]), jnp.stack(outs["ssm_s"]),
            jnp.stack(outs["conv_s"]))
```

```python
import functools

import jax
import jax.numpy as jnp
import numpy as np
from jax import lax
from jax.experimental import pallas as pl
from jax.experimental.pallas import tpu as pltpu

F32 = jnp.float32
BF16 = jnp.bfloat16
HIGHEST = lax.Precision.HIGHEST

D_MODEL = 1024
BATCH = 8
SEQ = 2048
DEPTH = 2
DEC_BATCH = 128
DEC_SEQ = 8
HG_HEADS = 4
HG_D = 128
HG_W = HG_HEADS * HG_D
SSD_HEADS = 8
SSD_P = 64
SSD_N = 128
SSD_W = SSD_HEADS * SSD_P
SSD_PAIRS = SSD_HEADS // 2
CONV_W = 4
CONV_DIM = 1024
D_FF = 2816
N_EXPERTS = 8
D_FF_EXPERT = 3584
EPS = 1e-6

LANES = 128
C_Q, C_F, C_I, C_OG, C_Z, C_XBC, C_DT = 0, 512, 1024, 1536, 2048, 2560, 3584
PROJ_W = C_DT + LANES
HG_CHUNK = 64
SSD_CHUNK = 128
SAMPLE_SEQS = 8
HG_SAFE_RANGE = 80.0

T_PROMPT = BATCH * SEQ
T_SAMPLE = DEC_BATCH * DEC_SEQ
T_ALL = T_PROMPT + T_SAMPLE

SUBLANES = 8
TOP_K = 2
MOE_TOKENS = 256
MOE_TILES = T_ALL // MOE_TOKENS
MOE_SLOTS = TOP_K * MOE_TOKENS + N_EXPERTS * SUBLANES
GMM_ROWS = 1024
GMM_SUB = 256
GMM_TILES = (TOP_K * T_ALL + MOE_TILES * N_EXPERTS * (SUBLANES - 1)
             + N_EXPERTS * (GMM_ROWS - 1)) // GMM_ROWS + 1

VMEM_LIMIT = 56 * 1024 * 1024


def _sigmoid(x):
    return 1.0 / (1.0 + jnp.exp(-x))


def _silu(x):
    return x * _sigmoid(x)


def _softplus(x):
    return jnp.maximum(x, 0.0) + jnp.log1p(jnp.exp(-jnp.abs(x)))


def _log_sigmoid(x):
    return jnp.minimum(x, 0.0) - jnp.log1p(jnp.exp(-jnp.abs(x)))


def _rms(x, g):
    return x * lax.rsqrt(jnp.mean(x * x, axis=-1, keepdims=True) + EPS) * g


def _dot(a, b):
    return jnp.dot(a.astype(BF16), b.astype(BF16), preferred_element_type=F32)


def _dot_nt(a, b):
    return lax.dot_general(a.astype(BF16), b.astype(BF16), (((1,), (1,)), ((), ())),
                           preferred_element_type=F32)


def _dot_exact(a, b):
    return jnp.dot(a, b, precision=HIGHEST, preferred_element_type=F32)


def _dot_nt_exact(a, b):
    return lax.dot_general(a, b, (((1,), (1,)), ((), ())), precision=HIGHEST,
                           preferred_element_type=F32)


def _iota2(shape, dim):
    return lax.broadcasted_iota(jnp.int32, shape, dim)


def _seq_masks(rows, seq_len):
    t = _iota2((rows, rows), 0)
    s = _iota2((rows, rows), 1)
    same = (t // seq_len) == (s // seq_len)
    causal = same & (s <= t)
    ref = same & ((s % seq_len) < seq_len // 2)
    return causal, ref, same


def _row_specs(arrs, tm):
    if len(arrs) == 1:
        return [pl.BlockSpec((tm, arrs[0].shape[1]), lambda i, *_: (i, 0))]
    npt = T_PROMPT // tm
    return [pl.BlockSpec((tm, arrs[0].shape[1]), lambda i, *_: (jnp.minimum(i, npt - 1), 0)),
            pl.BlockSpec((tm, arrs[1].shape[1]), lambda i, *_: (jnp.maximum(i - npt, 0), 0))]


def _rows(i, refs):
    if len(refs) == 1:
        return refs[0][...]
    return jnp.where(i < T_PROMPT // refs[0].shape[0], refs[0][...], refs[1][...])


def _proj_body(n_h, *refs):
    h_refs, (g_ref, w_ref, o_ref) = refs[:n_h], refs[n_h:]
    hn = _rms(_rows(pl.program_id(0), h_refs), g_ref[...])
    o_ref[...] = _dot(hn, w_ref[...])


def _proj_call(h, g, w_bf):
    tm = 512
    return pl.pallas_call(
        functools.partial(_proj_body, len(h)),
        grid=(T_ALL // tm,),
        in_specs=_row_specs(h, tm) + [pl.BlockSpec((1, D_MODEL), lambda i: (0, 0)),
                                      pl.BlockSpec((D_MODEL, PROJ_W), lambda i: (0, 0))],
        out_specs=pl.BlockSpec((tm, PROJ_W), lambda i: (i, 0)),
        out_shape=jax.ShapeDtypeStruct((T_ALL, PROJ_W), F32),
        compiler_params=pltpu.CompilerParams(dimension_semantics=("arbitrary",),
                                             vmem_limit_bytes=VMEM_LIMIT),
        name="in_proj",
    )(*h, g, w_bf)


def _hgrn_gates(p_q, p_f, la, lc, omlb):
    q = _silu(p_q)
    b = lc + _log_sigmoid(p_f)
    logf = jnp.maximum(la, b) + jnp.log1p(jnp.exp(-jnp.abs(la - b)))
    kin = omlb * _sigmoid(-p_f)
    return q, logf, kin


def _hgrn_decays(logf, causal, ref, same):
    rows = logf.shape[0]
    m = jnp.concatenate([causal.astype(F32), ref.astype(F32), same.astype(F32)], axis=0)
    g3 = _dot_exact(m, logf)
    return g3[:rows], g3[rows:2 * rows], g3[2 * rows:]


def _hgrn_head_out(o, hgn, og):
    return _rms(o, hgn) * _silu(og)


def _ssd_inputs(conv, p_dt, dtb, a_neg):
    conv = _silu(conv)
    xs = conv[:, :SSD_W]
    bm = conv[:, SSD_W:SSD_W + 2 * SSD_N]
    cm = conv[:, SSD_W + 2 * SSD_N:]
    dtf = _softplus(p_dt + dtb)
    a = dtf * a_neg
    return xs, bm, cm, dtf, a


def _lane_bcast(x, lane, width):
    return jnp.broadcast_to(x[:, lane:lane + 1], (x.shape[0], width))


def _ssd_finish(y, xs, z, dsk, ssn):
    y = (y + dsk * xs) * _silu(z)
    half = SSD_W // 2
    return jnp.concatenate([_rms(y[:, :half], ssn[:, :half]), _rms(y[:, half:], ssn[:, half:])],
                           axis=1)


def _scan_prompt_body(p_ref, la_ref, lc_ref, omlb_ref, hgn_ref, cw_ref, cb_ref, dtb_ref, an_ref,
                      dsk_ref, ssn_ref,
                      mix_ref, ohg_ref, ossm_ref, oconv_ref,
                      st_ref, hp_ref, xpad_ref):
    j = pl.program_id(1)
    tl = SSD_CHUNK

    @pl.when(j == 0)
    def _():
        st_ref[...] = jnp.zeros_like(st_ref)
        hp_ref[...] = jnp.zeros_like(hp_ref)
        xpad_ref[0:8, :] = jnp.zeros((8, CONV_DIM), F32)

    causal, ref, same = _seq_masks(HG_CHUNK, HG_CHUNK)
    for c in range(tl // HG_CHUNK):
        r0 = c * HG_CHUNK
        rows = slice(r0, r0 + HG_CHUNK)
        q, logf, kin = _hgrn_gates(p_ref[rows, C_Q:C_Q + HG_W], p_ref[rows, C_F:C_F + HG_W],
                                   la_ref[...], lc_ref[...], omlb_ref[...])
        v = p_ref[rows, C_I:C_I + HG_W]
        g, gmid, glast = _hgrn_decays(logf, causal, ref, same)
        qt = q * jnp.exp(g - gmid)
        kt = kin * jnp.exp(gmid - g)
        qh = q * jnp.exp(g)
        kd = kin * jnp.exp(glast - g)
        ds = jnp.exp(glast[0:1, :])
        for h in range(HG_HEADS):
            sl = slice(h * HG_D, (h + 1) * HG_D)
            sc = jnp.where(causal, _dot_nt(qt[:, sl], kt[:, sl]), 0.0)
            st = st_ref[h]
            o = _dot(sc, v[:, sl]) + _dot_nt(qh[:, sl], st)
            st_ref[h] = st * ds[:, sl] + _dot(v[:, sl].T, kd[:, sl])
            og = p_ref[rows, C_OG + h * HG_D:C_OG + (h + 1) * HG_D]
            mix_ref[rows, sl] = _hgrn_head_out(o, hgn_ref[:, sl], og).astype(BF16)

    xpad_ref[8:8 + tl, :] = p_ref[:, C_XBC:C_XBC + CONV_DIM]
    conv = cb_ref[...] + cw_ref[0:1, :] * xpad_ref[5:5 + tl, :]
    for tap in range(1, CONV_W):
        conv = conv + cw_ref[tap:tap + 1, :] * xpad_ref[5 + tap:5 + tap + tl, :]

    @pl.when(j == pl.num_programs(1) - 1)
    def _():
        oconv_ref[0] = xpad_ref[5 + tl:8 + tl, :]

    xpad_ref[0:8, :] = xpad_ref[tl:tl + 8, :]

    xs, bm, cm, dtf, a = _ssd_inputs(conv, p_ref[:, C_DT:C_DT + LANES], dtb_ref[...], an_ref[...])
    tri, _, _ = _seq_masks(tl, tl)
    trif = tri.astype(F32)
    acum = _dot_exact(trif, a)
    acum_t = _dot_nt_exact(a.T, trif)
    lo = _iota2((tl, LANES), 1) < SSD_P
    ys = []
    for pair in range(SSD_PAIRS):
        grp = pair // 2
        cg = cm[:, grp * SSD_N:(grp + 1) * SSD_N]
        bg = bm[:, grp * SSD_N:(grp + 1) * SSD_N]
        cb = _dot_nt(cg, bg)
        bg_t = bg.T
        psl = slice(pair * LANES, (pair + 1) * LANES)
        r0, r1 = 2 * pair, 2 * pair + 1
        xp = xs[:, psl] * jnp.where(lo, _lane_bcast(dtf, r0, LANES), _lane_bcast(dtf, r1, LANES))
        intra, upd, einter = [], [], []
        for r in (r0, r1):
            col = _lane_bcast(acum, r, tl)
            row = jnp.broadcast_to(acum_t[r:r + 1, :], (tl, tl))
            dec = jnp.exp(jnp.where(tri, col - row, -jnp.inf))
            intra.append(_dot(cb * dec, xp))
            wrow = jnp.exp(acum_t[r:r + 1, tl - 1:tl] - acum_t[r:r + 1, :])
            upd.append(_dot(bg_t * wrow, xp))
            einter.append(jnp.exp(col))
        hp = hp_ref[pair]
        y = jnp.where(lo, intra[0], intra[1]) + jnp.where(lo, einter[0], einter[1]) * _dot(cg, hp)
        alast = acum[tl - 1:tl, :]
        ea = jnp.exp(jnp.where(lo[0:1, :], _lane_bcast(alast, r0, LANES),
                               _lane_bcast(alast, r1, LANES)))
        hp_ref[pair] = ea * hp + jnp.where(lo, upd[0], upd[1])
        ys.append(y)
    y = jnp.concatenate(ys, axis=1)
    y = _ssd_finish(y, xs, p_ref[:, C_Z:C_Z + SSD_W], dsk_ref[...], ssn_ref[...])
    mix_ref[:, HG_W:] = y.astype(BF16)

    @pl.when(j == pl.num_programs(1) - 1)
    def _():
        for h in range(HG_HEADS):
            ohg_ref[0, h] = st_ref[h].T
        for pair in range(SSD_PAIRS):
            hp = hp_ref[pair]
            ossm_ref[0, 2 * pair] = hp[:, :SSD_P]
            ossm_ref[0, 2 * pair + 1] = hp[:, SSD_P:]


def _row_spec(width):
    return pl.BlockSpec((1, width), lambda b, j: (0, 0))


def _scan_prompt_call(proj, prm):
    tl = SSD_CHUNK
    nt = SEQ // tl
    in_specs = [pl.BlockSpec((tl, PROJ_W), lambda b, j: (b * nt + j, 0)),
                _row_spec(HG_W), _row_spec(HG_W), _row_spec(HG_W), _row_spec(HG_W),
                pl.BlockSpec((CONV_W, CONV_DIM), lambda b, j: (0, 0)), _row_spec(CONV_DIM),
                _row_spec(LANES), _row_spec(LANES), _row_spec(SSD_W), _row_spec(SSD_W)]
    out_specs = [pl.BlockSpec((tl, D_MODEL), lambda b, j: (b * nt + j, 0)),
                 pl.BlockSpec((1, HG_HEADS, HG_D, HG_D), lambda b, j: (b, 0, 0, 0)),
                 pl.BlockSpec((1, SSD_HEADS, SSD_N, SSD_P), lambda b, j: (b, 0, 0, 0)),
                 pl.BlockSpec((1, CONV_W - 1, CONV_DIM), lambda b, j: (b, 0, 0))]
    out_shape = [jax.ShapeDtypeStruct((T_PROMPT, D_MODEL), BF16),
                 jax.ShapeDtypeStruct((BATCH, HG_HEADS, HG_D, HG_D), F32),
                 jax.ShapeDtypeStruct((BATCH, SSD_HEADS, SSD_N, SSD_P), F32),
                 jax.ShapeDtypeStruct((BATCH, CONV_W - 1, CONV_DIM), F32)]
    return pl.pallas_call(
        _scan_prompt_body,
        grid=(BATCH, nt),
        in_specs=in_specs, out_specs=out_specs, out_shape=out_shape,
        scratch_shapes=[pltpu.VMEM((HG_HEADS, HG_D, HG_D), F32),
                        pltpu.VMEM((SSD_PAIRS, SSD_N, LANES), F32),
                        pltpu.VMEM((tl + 8, CONV_DIM), F32)],
        compiler_params=pltpu.CompilerParams(dimension_semantics=("arbitrary", "arbitrary"),
                                             vmem_limit_bytes=VMEM_LIMIT),
        name="scan_prompt",
    )(proj, prm["la"], prm["lc"], prm["omlb"], prm["hgn"], prm["cw"], prm["cb"], prm["dtb"],
      prm["an"], prm["dsk"], prm["ssn"])


def _scan_sample_body(p_ref, shg_ref, sssm_ref, sconv_ref, la_ref, lc_ref, omlb_ref, hgn_ref,
                      cw_ref, cb_ref, dtb_ref, an_ref, dsk_ref, ssn_ref,
                      mix_ref, ohg_ref, ossm_ref, oconv_ref, xpad_ref):
    nb, sl_len = SAMPLE_SEQS, DEC_SEQ
    rows = nb * sl_len
    causal, ref, same = _seq_masks(rows, sl_len)
    rowseq = _iota2((rows, LANES), 0) // sl_len

    q, logf, kin = _hgrn_gates(p_ref[:, C_Q:C_Q + HG_W], p_ref[:, C_F:C_F + HG_W],
                               la_ref[...], lc_ref[...], omlb_ref[...])
    v = p_ref[:, C_I:C_I + HG_W]
    g, gmid, glast = _hgrn_decays(logf, causal, ref, same)
    qt = q * jnp.exp(g - gmid)
    kt = kin * jnp.exp(gmid - g)
    qh = q * jnp.exp(g)
    kd = kin * jnp.exp(glast - g)
    ds_t = jnp.exp(glast).T
    kd_t = kd.T
    for h in range(HG_HEADS):
        sl = slice(h * HG_D, (h + 1) * HG_D)
        sc = jnp.where(causal, _dot_nt(qt[:, sl], kt[:, sl]), 0.0)
        o_intra = _dot(sc, v[:, sl])
        o_inter = []
        for b in range(nb):
            s_old = shg_ref[b, h]
            o_inter.append(_dot(qh[b * sl_len:(b + 1) * sl_len, sl], s_old))
            vb = jnp.where(rowseq == b, v[:, sl], 0.0)
            dcol = _lane_bcast(ds_t[sl, :], b * sl_len, HG_D)
            ohg_ref[b, h] = dcol * s_old + _dot(kd_t[sl, :], vb)
        o = o_intra + jnp.concatenate(o_inter, axis=0)
        og = p_ref[:, C_OG + h * HG_D:C_OG + (h + 1) * HG_D]
        mix_ref[:, sl] = _hgrn_head_out(o, hgn_ref[:, sl], og).astype(BF16)

    convs = []
    for b in range(nb):
        base = 16 * b
        xpad_ref[base + 5:base + 8, :] = sconv_ref[b]
        xpad_ref[base + 8:base + 16, :] = p_ref[b * sl_len:(b + 1) * sl_len, C_XBC:C_XBC + CONV_DIM]
        cv = cb_ref[...] + cw_ref[0:1, :] * xpad_ref[base + 5:base + 13, :]
        for tap in range(1, CONV_W):
            cv = cv + cw_ref[tap:tap + 1, :] * xpad_ref[base + 5 + tap:base + 13 + tap, :]
        convs.append(cv)
        oconv_ref[b] = xpad_ref[base + 13:base + 16, :]
    conv = jnp.concatenate(convs, axis=0)

    xs, bm, cm, dtf, a = _ssd_inputs(conv, p_ref[:, C_DT:C_DT + LANES], dtb_ref[...], an_ref[...])
    causf = causal.astype(F32)
    acum = _dot_exact(causf, a)
    atot = _dot_exact(same.astype(F32), a)
    acum_t = _dot_nt_exact(a.T, causf)
    wall = jnp.exp(atot - acum)
    eall = jnp.exp(acum)
    eatot = jnp.exp(atot)
    lo = _iota2((rows, LANES), 1) < SSD_P
    ys = []
    for pair in range(SSD_PAIRS):
        grp = pair // 2
        cg = cm[:, grp * SSD_N:(grp + 1) * SSD_N]
        bg = bm[:, grp * SSD_N:(grp + 1) * SSD_N]
        cb = _dot_nt(cg, bg)
        bg_t = bg.T
        psl = slice(pair * LANES, (pair + 1) * LANES)
        r0, r1 = 2 * pair, 2 * pair + 1
        xp = xs[:, psl] * jnp.where(lo, _lane_bcast(dtf, r0, LANES), _lane_bcast(dtf, r1, LANES))
        xw = xp * jnp.where(lo, _lane_bcast(wall, r0, LANES), _lane_bcast(wall, r1, LANES))
        intra = []
        for r in (r0, r1):
            col = _lane_bcast(acum, r, rows)
            row = jnp.broadcast_to(acum_t[r:r + 1, :], (rows, rows))
            dec = jnp.exp(jnp.where(causal, col - row, -jnp.inf))
            intra.append(_dot(cb * dec, xp))
        y_intra = jnp.where(lo, intra[0], intra[1])
        y_inter = []
        for b in range(nb):
            tr = slice(b * sl_len, (b + 1) * sl_len)
            h0 = sssm_ref[b, r0]
            h1 = sssm_ref[b, r1]
            yi = jnp.concatenate([_dot(cg[tr, :], h0), _dot(cg[tr, :], h1)], axis=1)
            y_inter.append(yi)
            upd = _dot(bg_t, jnp.where(rowseq == b, xw, 0.0))
            ea0 = jnp.broadcast_to(eatot[b * sl_len:b * sl_len + 1, r0:r0 + 1], (SSD_N, SSD_P))
            ea1 = jnp.broadcast_to(eatot[b * sl_len:b * sl_len + 1, r1:r1 + 1], (SSD_N, SSD_P))
            ossm_ref[b, r0] = ea0 * h0 + upd[:, :SSD_P]
            ossm_ref[b, r1] = ea1 * h1 + upd[:, SSD_P:]
        e_pair = jnp.where(lo, _lane_bcast(eall, r0, LANES), _lane_bcast(eall, r1, LANES))
        ys.append(y_intra + e_pair * jnp.concatenate(y_inter, axis=0))
    y = jnp.concatenate(ys, axis=1)
    y = _ssd_finish(y, xs, p_ref[:, C_Z:C_Z + SSD_W], dsk_ref[...], ssn_ref[...])
    mix_ref[:, HG_W:] = y.astype(BF16)


def _scan_sample_call(proj, s_hg, s_ssm, s_conv, prm):
    nb = SAMPLE_SEQS
    rows = nb * DEC_SEQ
    row0 = T_PROMPT // rows
    c0 = lambda i: (0, 0)
    rs = lambda w: pl.BlockSpec((1, w), c0)
    in_specs = [pl.BlockSpec((rows, PROJ_W), lambda i: (row0 + i, 0)),
                pl.BlockSpec((nb, HG_HEADS, HG_D, HG_D), lambda i: (i, 0, 0, 0)),
                pl.BlockSpec((nb, SSD_HEADS, SSD_N, SSD_P), lambda i: (i, 0, 0, 0)),
                pl.BlockSpec((nb, CONV_W - 1, CONV_DIM), lambda i: (i, 0, 0)),
                rs(HG_W), rs(HG_W), rs(HG_W), rs(HG_W),
                pl.BlockSpec((CONV_W, CONV_DIM), c0), rs(CONV_DIM),
                rs(LANES), rs(LANES), rs(SSD_W), rs(SSD_W)]
    out_specs = [pl.BlockSpec((rows, D_MODEL), lambda i: (i, 0)),
                 pl.BlockSpec((nb, HG_HEADS, HG_D, HG_D), lambda i: (i, 0, 0, 0)),
                 pl.BlockSpec((nb, SSD_HEADS, SSD_N, SSD_P), lambda i: (i, 0, 0, 0)),
                 pl.BlockSpec((nb, CONV_W - 1, CONV_DIM), lambda i: (i, 0, 0))]
    out_shape = [jax.ShapeDtypeStruct((T_SAMPLE, D_MODEL), BF16),
                 jax.ShapeDtypeStruct((DEC_BATCH, HG_HEADS, HG_D, HG_D), F32),
                 jax.ShapeDtypeStruct((DEC_BATCH, SSD_HEADS, SSD_N, SSD_P), F32),
                 jax.ShapeDtypeStruct((DEC_BATCH, CONV_W - 1, CONV_DIM), F32)]
    return pl.pallas_call(
        _scan_sample_body,
        grid=(DEC_BATCH // nb,),
        in_specs=in_specs, out_specs=out_specs, out_shape=out_shape,
        scratch_shapes=[pltpu.VMEM((16 * nb, CONV_DIM), F32)],
        compiler_params=pltpu.CompilerParams(dimension_semantics=("arbitrary",),
                                             vmem_limit_bytes=VMEM_LIMIT),
        name="scan_sample",
    )(proj, s_hg, s_ssm, s_conv, prm["la"], prm["lc"], prm["omlb"], prm["hgn"], prm["cw"],
      prm["cb"], prm["dtb"], prm["an"], prm["dsk"], prm["ssn"])


def _out_ffn_body(n_h, *refs):
    h_refs, mix_refs = refs[:n_h], refs[n_h:n_h + 2]
    wo_ref, g_ref, wg_ref, wu_ref, wd_ref, o_ref, hn_ref, acc_ref = refs[n_h + 2:]
    i = pl.program_id(0)
    f = pl.program_id(1)

    @pl.when(f == 0)
    def _():
        h1 = _rows(i, h_refs) + _dot(_rows(i, mix_refs), wo_ref[...])
        acc_ref[...] = h1
        hn_ref[...] = _rms(h1, g_ref[...]).astype(BF16)

    x = hn_ref[...]
    gate = _dot(x, wg_ref[...])
    up = _dot(x, wu_ref[...])
    acc_ref[...] += _dot(_silu(gate) * up, wd_ref[...])

    @pl.when(f == pl.num_programs(1) - 1)
    def _():
        o_ref[...] = acc_ref[...]


def _out_ffn_call(h, mixed, wo_bf, g, wi_bf, wd_bf):
    tm, tf = 512, D_FF // 2
    nf = D_FF // tf
    return pl.pallas_call(
        functools.partial(_out_ffn_body, len(h)),
        grid=(T_ALL // tm, nf),
        in_specs=_row_specs(h, tm) + _row_specs(mixed, tm) + [
                  pl.BlockSpec((D_MODEL, D_MODEL), lambda i, f: (0, 0)),
                  pl.BlockSpec((1, D_MODEL), lambda i, f: (0, 0)),
                  pl.BlockSpec((D_MODEL, tf), lambda i, f: (0, f)),
                  pl.BlockSpec((D_MODEL, tf), lambda i, f: (0, nf + f)),
                  pl.BlockSpec((tf, D_MODEL), lambda i, f: (f, 0))],
        out_specs=pl.BlockSpec((tm, D_MODEL), lambda i, f: (i, 0)),
        out_shape=jax.ShapeDtypeStruct((T_ALL, D_MODEL), F32),
        scratch_shapes=[pltpu.VMEM((tm, D_MODEL), BF16), pltpu.VMEM((tm, D_MODEL), F32)],
        compiler_params=pltpu.CompilerParams(dimension_semantics=("arbitrary", "arbitrary"),
                                             vmem_limit_bytes=VMEM_LIMIT),
        name="out_ffn",
    )(*h, *mixed, wo_bf, g, wi_bf, wi_bf, wd_bf)


def _out_router_body(h_ref, mixp_ref, mixs_ref, wo_ref, g_ref, wr_ref,
                     h1_ref, hn_ref, route_ref, seg_ref):
    h1 = h_ref[...] + _dot(_rows(pl.program_id(0), (mixp_ref, mixs_ref)), wo_ref[...])
    h1_ref[...] = h1
    hn = _rms(h1, g_ref[...])
    hn_ref[...] = hn.astype(BF16)
    logits = _dot_exact(hn, wr_ref[...])
    lane = _iota2(logits.shape, 1)
    lg = jnp.where(lane < N_EXPERTS, logits, -jnp.inf)
    m1 = jnp.max(lg, axis=1, keepdims=True)
    i1 = jnp.min(jnp.where(lg == m1, lane, LANES), axis=1, keepdims=True)
    lg2 = jnp.where(lane == i1, -jnp.inf, lg)
    m2 = jnp.max(lg2, axis=1, keepdims=True)
    i2 = jnp.min(jnp.where(lg2 == m2, lane, LANES), axis=1, keepdims=True)
    e2 = jnp.exp(m2 - m1)
    g1 = 1.0 / (1.0 + e2)
    g2 = e2 / (1.0 + e2)

    tm = logits.shape[0]
    chosen = jnp.where((lane == i1) | (lane == i2), 1.0, 0.0)
    earlier = (_iota2((tm, tm), 1) < _iota2((tm, tm), 0)).astype(F32)
    rank = _dot(earlier, chosen)
    cnt = jnp.sum(chosen, axis=0, keepdims=True)
    seg = jnp.floor((cnt + (SUBLANES - 1)) * (1.0 / SUBLANES)) * SUBLANES
    below = (_iota2((LANES, LANES), 0) < _iota2((LANES, LANES), 1)).astype(F32)
    loc = _dot(jnp.broadcast_to(seg, (SUBLANES, LANES)), below)[0:1, :]
    pos = loc + rank
    p1 = jnp.sum(jnp.where(lane == i1, pos, 0.0), axis=1, keepdims=True)
    p2 = jnp.sum(jnp.where(lane == i2, pos, 0.0), axis=1, keepdims=True)
    route_ref[...] = jnp.where(lane == 0, p1, jnp.where(lane == 1, p2, jnp.where(
        lane == 2, g1, jnp.where(lane == 3, g2, 0.0))))
    seg_ref[0] = jnp.broadcast_to(seg, (SUBLANES, LANES)).astype(jnp.int32)


def _out_router_call(h, mixed, wo_bf, g, wr_pad):
    tm = MOE_TOKENS
    return pl.pallas_call(
        _out_router_body,
        grid=(MOE_TILES,),
        in_specs=[pl.BlockSpec((tm, D_MODEL), lambda i: (i, 0))] + _row_specs(mixed, tm) + [
                  pl.BlockSpec((D_MODEL, D_MODEL), lambda i: (0, 0)),
                  pl.BlockSpec((1, D_MODEL), lambda i: (0, 0)),
                  pl.BlockSpec((D_MODEL, LANES), lambda i: (0, 0))],
        out_specs=[pl.BlockSpec((tm, D_MODEL), lambda i: (i, 0)),
                   pl.BlockSpec((tm, D_MODEL), lambda i: (i, 0)),
                   pl.BlockSpec((tm, LANES), lambda i: (i, 0)),
                   pl.BlockSpec((1, SUBLANES, LANES), lambda i: (i, 0, 0))],
        out_shape=[jax.ShapeDtypeStruct((T_ALL, D_MODEL), F32),
                   jax.ShapeDtypeStruct((T_ALL, D_MODEL), BF16),
                   jax.ShapeDtypeStruct((T_ALL, LANES), F32),
                   jax.ShapeDtypeStruct((MOE_TILES, SUBLANES, LANES), jnp.int32)],
        compiler_params=pltpu.CompilerParams(dimension_semantics=("arbitrary",),
                                             vmem_limit_bytes=VMEM_LIMIT),
        name="out_router",
    )(h, *mixed, wo_bf, g, wr_pad)


def _row_tile_copy(tile_ref, hbm_ref, tile_row, hbm_row, sem, to_hbm):
    t = tile_ref.at[pl.ds(pl.multiple_of(tile_row, SUBLANES), SUBLANES), :]
    g = hbm_ref.at[pl.ds(pl.multiple_of(hbm_row, SUBLANES), SUBLANES), :]
    return pltpu.make_async_copy(t, g, sem) if to_hbm else pltpu.make_async_copy(g, t, sem)


def _seg_copies(i, n_ref, loc_ref, start_ref, tile_ref, hbm_ref, sem, to_hbm):
    for e in range(N_EXPERTS):
        k = i * N_EXPERTS + e
        lo = loc_ref[k]
        st = start_ref[k]

        def body(c, carry, lo=lo, st=st):
            _row_tile_copy(tile_ref, hbm_ref, lo + c * SUBLANES, st + c * SUBLANES, sem,
                           to_hbm).start()
            return carry

        lax.fori_loop(0, n_ref[k], body, 0)


def _seg_wait(i, n_ref, tile_ref, hbm_ref, sem, to_hbm):
    total = n_ref[i * N_EXPERTS]
    for e in range(1, N_EXPERTS):
        total = total + n_ref[i * N_EXPERTS + e]

    def body(c, carry):
        _row_tile_copy(tile_ref, hbm_ref, 0, 0, sem, to_hbm).wait()
        return carry

    lax.fori_loop(0, total, body, 0)


def _dispatch_body(n_ref, loc_ref, start_ref, zn_ref, zstart_ref, zb_ref, zbn_ref,
                   hn_ref, route_ref, xs_ref, stage_ref, zero_ref, sem):
    i = pl.program_id(0)
    last = pl.num_programs(0) - 1
    slot = i % 2

    @pl.when(i >= 2)
    def _():
        _seg_wait(i - 2, n_ref, stage_ref.at[slot], xs_ref, sem.at[slot], True)

    rt = route_ref[...].T
    s = _iota2((MOE_SLOTS, MOE_TOKENS), 0).astype(F32)
    perm = jnp.where((s == rt[0:1, :]) | (s == rt[1:2, :]), 1.0, 0.0)
    stage_ref[slot] = _dot(perm, hn_ref[...])
    _seg_copies(i, n_ref, loc_ref, start_ref, stage_ref.at[slot], xs_ref, sem.at[slot], True)

    @pl.when(i == last)
    def _():
        zero_ref[...] = jnp.zeros_like(zero_ref)

        def blk_copy(b):
            return pltpu.make_async_copy(
                zero_ref, xs_ref.at[pl.ds(pl.multiple_of(b * GMM_SUB, GMM_SUB), GMM_SUB), :],
                sem.at[3])

        def bbody(b, carry):
            @pl.when(zb_ref[b] == 1)
            def _():
                blk_copy(b).start()
            return carry

        lax.fori_loop(0, GMM_TILES * GMM_ROWS // GMM_SUB, bbody, 0)
        nz = zn_ref[0]
        for e in range(N_EXPERTS):
            st = zstart_ref[e]

            def body(c, carry, st=st):
                _row_tile_copy(zero_ref, xs_ref, 0, st + c * SUBLANES, sem.at[2], True).start()
                return carry

            lax.fori_loop(0, zn_ref[e], body, 0)
            if e:
                nz = nz + zn_ref[e]

        @pl.when(last >= 1)
        def _():
            _seg_wait(i - 1, n_ref, stage_ref.at[1 - slot], xs_ref, sem.at[1 - slot], True)

        _seg_wait(i, n_ref, stage_ref.at[slot], xs_ref, sem.at[slot], True)

        def zbody(c, carry):
            _row_tile_copy(zero_ref, xs_ref, 0, 0, sem.at[2], True).wait()
            return carry

        lax.fori_loop(0, nz, zbody, 0)

        def bwait(c, carry):
            blk_copy(0).wait()
            return carry

        lax.fori_loop(0, zbn_ref[0], bwait, 0)


def _dispatch_call(sched, hn_bf, route):
    grid_spec = pltpu.PrefetchScalarGridSpec(
        num_scalar_prefetch=7,
        grid=(MOE_TILES,),
        in_specs=[pl.BlockSpec((MOE_TOKENS, D_MODEL), lambda i, *_: (i, 0)),
                  pl.BlockSpec((MOE_TOKENS, LANES), lambda i, *_: (i, 0))],
        out_specs=pl.BlockSpec(memory_space=pl.ANY),
        scratch_shapes=[pltpu.VMEM((2, MOE_SLOTS, D_MODEL), F32),
                        pltpu.VMEM((GMM_SUB, D_MODEL), F32),
                        pltpu.SemaphoreType.DMA((4,))])
    return pl.pallas_call(
        _dispatch_body,
        grid_spec=grid_spec,
        out_shape=jax.ShapeDtypeStruct((GMM_TILES * GMM_ROWS, D_MODEL), F32),
        compiler_params=pltpu.CompilerParams(dimension_semantics=("arbitrary",),
                                             vmem_limit_bytes=VMEM_LIMIT),
        name="moe_dispatch",
    )(sched["nch"], sched["loc"], sched["start"], sched["zn"], sched["zstart"], sched["zb"],
      sched["zbn"], hn_bf, route)


def _experts_body(te_ref, tv_ref, nu_ref, x_ref, wg_ref, wu_ref, wd_ref, o_ref,
                  xb_ref, wgb_ref, wub_ref, wdb_ref, acc_ref):
    del te_ref, nu_ref
    i = pl.program_id(0)
    f = pl.program_id(1)
    lastf = pl.num_programs(1) - 1
    nv = tv_ref[i]

    @pl.when(nv > 0)
    def _():
        wgb_ref[...] = wg_ref[0].astype(BF16)
        wub_ref[...] = wu_ref[0].astype(BF16)
        wdb_ref[...] = wd_ref[0].astype(BF16)

    for sub in range(GMM_ROWS // GMM_SUB):
        rows = slice(sub * GMM_SUB, (sub + 1) * GMM_SUB)

        @pl.when(sub * GMM_SUB < nv)
        def _(rows=rows):
            @pl.when(f == 0)
            def _():
                xb_ref[rows, :] = x_ref[rows, :].astype(BF16)
                acc_ref[rows, :] = jnp.zeros((GMM_SUB, D_MODEL), F32)

            x = xb_ref[rows, :]
            gate = jnp.dot(x, wgb_ref[...], preferred_element_type=F32)
            up = jnp.dot(x, wub_ref[...], preferred_element_type=F32)
            acc_ref[rows, :] += jnp.dot((_silu(gate) * up).astype(BF16), wdb_ref[...],
                                        preferred_element_type=F32)

            @pl.when(f == lastf)
            def _():
                o_ref[rows, :] = acc_ref[rows, :]

        @pl.when((sub * GMM_SUB >= nv) & (f == lastf))
        def _(rows=rows):
            o_ref[rows, :] = jnp.zeros((GMM_SUB, D_MODEL), F32)


def _experts_call(sched, x_sorted, w_in_e, w_out_e):
    tf = 512
    nf = D_FF_EXPERT // tf

    def used(i, nu):
        return jnp.minimum(i, nu[0] - 1)

    def fidx(i, f, nu):
        return jnp.where(i < nu[0], f, nf - 1)

    grid_spec = pltpu.PrefetchScalarGridSpec(
        num_scalar_prefetch=3,
        grid=(GMM_TILES, nf),
        in_specs=[pl.BlockSpec((GMM_ROWS, D_MODEL), lambda i, f, te, tv, nu: (used(i, nu), 0)),
                  pl.BlockSpec((1, D_MODEL, tf), lambda i, f, te, tv, nu: (te[i], 0, fidx(i, f, nu))),
                  pl.BlockSpec((1, D_MODEL, tf),
                               lambda i, f, te, tv, nu: (te[i], 0, nf + fidx(i, f, nu))),
                  pl.BlockSpec((1, tf, D_MODEL), lambda i, f, te, tv, nu: (te[i], fidx(i, f, nu), 0))],
        out_specs=pl.BlockSpec((GMM_ROWS, D_MODEL), lambda i, f, te, tv, nu: (i, 0)),
        scratch_shapes=[pltpu.VMEM((GMM_ROWS, D_MODEL), BF16),
                        pltpu.VMEM((D_MODEL, tf), BF16), pltpu.VMEM((D_MODEL, tf), BF16),
                        pltpu.VMEM((tf, D_MODEL), BF16),
                        pltpu.VMEM((GMM_ROWS, D_MODEL), F32)])
    return pl.pallas_call(
        _experts_body,
        grid_spec=grid_spec,
        out_shape=jax.ShapeDtypeStruct((GMM_TILES * GMM_ROWS, D_MODEL), F32),
        compiler_params=pltpu.CompilerParams(dimension_semantics=("arbitrary", "arbitrary"),
                                             vmem_limit_bytes=VMEM_LIMIT),
        name="moe_experts",
    )(sched["te"], sched["tv"], sched["nu"], x_sorted, w_in_e, w_in_e, w_out_e)


def _combine_body(n_ref, loc_ref, start_ref, h1_ref, route_ref, gf_ref, ys_ref, op_ref, os_ref,
                  buf_ref, sem):
    i = pl.program_id(0)
    last = pl.num_programs(0) - 1
    slot = i % 2

    @pl.when(i == 0)
    def _():
        buf_ref[...] = jnp.zeros_like(buf_ref)
        _seg_copies(0, n_ref, loc_ref, start_ref, buf_ref.at[0], ys_ref, sem.at[0], False)

    @pl.when(i < last)
    def _():
        _seg_copies(i + 1, n_ref, loc_ref, start_ref, buf_ref.at[1 - slot], ys_ref,
                    sem.at[1 - slot], False)

    _seg_wait(i, n_ref, buf_ref.at[slot], ys_ref, sem.at[slot], False)

    route = route_ref[...]
    s = _iota2((MOE_TOKENS, MOE_SLOTS), 1).astype(F32)
    rows = buf_ref[slot]
    y1 = _dot(jnp.where(s == route[:, 0:1], 1.0, 0.0), rows)
    y2 = _dot(jnp.where(s == route[:, 1:2], 1.0, 0.0), rows)
    y = _rms(h1_ref[...] + route[:, 2:3] * y1 + route[:, 3:4] * y2, gf_ref[...])

    @pl.when(i < T_PROMPT // MOE_TOKENS)
    def _():
        op_ref[...] = y

    @pl.when(i >= T_PROMPT // MOE_TOKENS)
    def _():
        os_ref[...] = y


def _combine_call(sched, h1, route, gfin, y_sorted):
    npt = T_PROMPT // MOE_TOKENS
    grid_spec = pltpu.PrefetchScalarGridSpec(
        num_scalar_prefetch=3,
        grid=(MOE_TILES,),
        in_specs=[pl.BlockSpec((MOE_TOKENS, D_MODEL), lambda i, *_: (i, 0)),
                  pl.BlockSpec((MOE_TOKENS, LANES), lambda i, *_: (i, 0)),
                  pl.BlockSpec((1, D_MODEL), lambda i, *_: (0, 0)),
                  pl.BlockSpec(memory_space=pl.ANY)],
        out_specs=[pl.BlockSpec((MOE_TOKENS, D_MODEL), lambda i, *_: (jnp.minimum(i, npt - 1), 0)),
                   pl.BlockSpec((MOE_TOKENS, D_MODEL), lambda i, *_: (jnp.maximum(i - npt, 0), 0))],
        scratch_shapes=[pltpu.VMEM((2, MOE_SLOTS, D_MODEL), F32),
                        pltpu.SemaphoreType.DMA((2,))])
    return pl.pallas_call(
        _combine_body,
        grid_spec=grid_spec,
        out_shape=[jax.ShapeDtypeStruct((T_PROMPT, D_MODEL), F32),
                   jax.ShapeDtypeStruct((T_SAMPLE, D_MODEL), F32)],
        compiler_params=pltpu.CompilerParams(dimension_semantics=("arbitrary",),
                                             vmem_limit_bytes=VMEM_LIMIT),
        name="moe_combine",
    )(sched["nch"], sched["loc"], sched["start"], h1, route, gfin, y_sorted)


def _moe_schedule(seg):
    tot = jnp.sum(seg, axis=0)
    region = (tot + GMM_ROWS - 1) // GMM_ROWS * GMM_ROWS
    base = jnp.cumsum(region) - region
    start = base[None, :] + jnp.cumsum(seg, axis=0) - seg
    loc = jnp.cumsum(seg, axis=1) - seg
    ntile = region // GMM_ROWS
    cum = jnp.cumsum(ntile)
    nu = cum[-1]
    i = jnp.arange(GMM_TILES, dtype=jnp.int32)
    te = jnp.minimum(jnp.sum(i[:, None] >= cum[None, :], axis=1), N_EXPERTS - 1).astype(jnp.int32)
    tv = jnp.clip(tot[te] - (i - (cum - ntile)[te]) * GMM_ROWS, 0, GMM_ROWS)
    tv = jnp.where(i < nu, tv, 0)
    te = jnp.where(i < nu, te, te[jnp.maximum(nu - 1, 0)])
    bstart = jnp.arange(GMM_TILES * GMM_ROWS // GMM_SUB, dtype=jnp.int32)[:, None] * GMM_SUB
    used_end = (base + (tot + GMM_SUB - 1) // GMM_SUB * GMM_SUB)[None, :]
    zb = jnp.any((bstart >= used_end) & (bstart < (base + region)[None, :]), axis=1)
    zb = (zb | (bstart[:, 0] >= jnp.sum(region))).astype(jnp.int32)
    return dict(nch=(seg // SUBLANES).reshape(-1), loc=loc.reshape(-1), start=start.reshape(-1),
                zn=((-tot) % GMM_SUB) // SUBLANES, zstart=base + tot,
                zb=zb, zbn=jnp.sum(zb).reshape(1),
                te=te, tv=tv.astype(jnp.int32), nu=nu.reshape(1).astype(jnp.int32))


def _row(x, width=None):
    x = x.astype(F32).reshape(1, -1)
    if width is not None and x.shape[1] < width:
        x = jnp.pad(x, ((0, 0), (0, width - x.shape[1])))
    return x


def _layer_params(l, lb_p, conv_w, conv_b, a_log, dt_bias, d_skip, hg_norm, ssd_norm):
    lb = jnp.sum(lb_p[1:l + 1], axis=0)
    return dict(
        la=_row(jnp.log(lb)), lc=_row(jnp.log1p(-lb)), omlb=_row(1.0 - lb),
        hgn=_row(hg_norm[l]), cw=conv_w[l].astype(F32), cb=_row(conv_b[l]),
        dtb=_row(dt_bias[l], LANES), an=_row(-jnp.exp(a_log[l].astype(F32)), LANES),
        dsk=_row(jnp.repeat(d_skip[l].astype(F32), SSD_P)), ssn=_row(ssd_norm[l]))


def kernel(x_prompt, x_sample, state_hgrn, state_ssm, state_conv, norm_mix, w_in, conv_w, conv_b,
           a_log, dt_bias, d_skip, lb_param, hg_norm, ssd_norm, w_out, norm_ffn, w_ffn_in,
           w_ffn_out, w_router, w_exp_in, w_exp_out, norm_final):
    h = (x_prompt.reshape(T_PROMPT, D_MODEL), x_sample.reshape(T_SAMPLE, D_MODEL))
    lb_p = jax.nn.softmax(lb_param.astype(F32), axis=0)
    outs = {k: [] for k in ("hg_p", "ssm_p", "conv_p", "hg_s", "ssm_s", "conv_s")}
    for l in range(DEPTH):
        prm = _layer_params(l, lb_p, conv_w, conv_b, a_log, dt_bias, d_skip, hg_norm, ssd_norm)
        w_in_bf = jnp.pad(w_in[l], ((0, 0), (0, PROJ_W - w_in.shape[2]))).astype(BF16)
        proj = _proj_call(h, _row(norm_mix[l]), w_in_bf)
        mixed_p, hg_p, ssm_p, conv_p = _scan_prompt_call(proj, prm)
        mixed_s, hg_s, ssm_s, conv_s = _scan_sample_call(proj, state_hgrn[l], state_ssm[l],
                                                         state_conv[l], prm)
        mixed = (mixed_p, mixed_s)
        for k, val in zip(outs, (hg_p, ssm_p, conv_p, hg_s, ssm_s, conv_s)):
            outs[k].append(val)
        wo_bf = w_out[l].astype(BF16)
        if l % 2 == 0:
            h = (_out_ffn_call(h, mixed, wo_bf, _row(norm_ffn[l]), w_ffn_in[l // 2].astype(BF16),
                               w_ffn_out[l // 2].astype(BF16)),)
        else:
            wr_pad = jnp.pad(w_router[l // 2].astype(F32), ((0, 0), (0, LANES - N_EXPERTS)))
            h1, hn_bf, route, seg = _out_router_call(h[0], mixed, wo_bf, _row(norm_ffn[l]), wr_pad)
            sched = _moe_schedule(seg[:, 0, :N_EXPERTS])
            x_sorted = _dispatch_call(sched, hn_bf, route)
            y_sorted = _experts_call(sched, x_sorted, w_exp_in[l // 2], w_exp_out[l // 2])
            h = _combine_call(sched, h1, route, _row(norm_final), y_sorted)
    y_prompt = h[0].reshape(BATCH, SEQ, D_MODEL)
    y_sample = h[1].reshape(DEC_BATCH, DEC_SEQ, D_MODEL)
    return (y_prompt, y_sample, jnp.stack(outs["hg_p"]), jnp.stack(outs["ssm_p"]),
            jnp.stack(outs["conv_p"]), jnp.stack(outs["hg_s"]), jnp.stack(outs["ssm_s"]),
            jnp.stack(outs["conv_s"]))
```

```python
import functools

import jax
import jax.numpy as jnp
import numpy as np
from jax import lax
from jax.experimental import pallas as pl
from jax.experimental.pallas import tpu as pltpu

F32 = jnp.float32
BF16 = jnp.bfloat16
HIGHEST = lax.Precision.HIGHEST

D_MODEL = 1024
BATCH = 8
SEQ = 2048
DEPTH = 2
DEC_BATCH = 128
DEC_SEQ = 8
HG_HEADS = 4
HG_D = 128
HG_W = HG_HEADS * HG_D
SSD_HEADS = 8
SSD_P = 64
SSD_N = 128
SSD_W = SSD_HEADS * SSD_P
SSD_PAIRS = SSD_HEADS // 2
CONV_W = 4
CONV_DIM = 1024
D_FF = 2816
N_EXPERTS = 8
D_FF_EXPERT = 3584
EPS = 1e-6

LANES = 128
C_Q, C_F, C_I, C_OG, C_Z, C_XBC, C_DT = 0, 512, 1024, 1536, 2048, 2560, 3584
PROJ_W = C_DT + LANES
HG_CHUNK = 64
SSD_CHUNK = 128
SAMPLE_SEQS = 8
HG_SAFE_RANGE = 80.0

T_PROMPT = BATCH * SEQ
T_SAMPLE = DEC_BATCH * DEC_SEQ
T_ALL = T_PROMPT + T_SAMPLE

SUBLANES = 8
TOP_K = 2
MOE_TOKENS = 256
MOE_TILES = T_ALL // MOE_TOKENS
MOE_SLOTS = TOP_K * MOE_TOKENS + N_EXPERTS * SUBLANES
GMM_ROWS = 1024
GMM_SUB = 256
GMM_TILES = (TOP_K * T_ALL + MOE_TILES * N_EXPERTS * (SUBLANES - 1)
             + N_EXPERTS * (GMM_ROWS - 1)) // GMM_ROWS + 1

VMEM_LIMIT = 56 * 1024 * 1024


def _sigmoid(x):
    return 1.0 / (1.0 + jnp.exp(-x))


def _silu(x):
    return x * _sigmoid(x)


def _softplus(x):
    return jnp.maximum(x, 0.0) + jnp.log1p(jnp.exp(-jnp.abs(x)))


def _log_sigmoid(x):
    return jnp.minimum(x, 0.0) - jnp.log1p(jnp.exp(-jnp.abs(x)))


def _rms(x, g):
    return x * lax.rsqrt(jnp.mean(x * x, axis=-1, keepdims=True) + EPS) * g


def _dot(a, b):
    return jnp.dot(a.astype(BF16), b.astype(BF16), preferred_element_type=F32)


def _dot_nt(a, b):
    return lax.dot_general(a.astype(BF16), b.astype(BF16), (((1,), (1,)), ((), ())),
                           preferred_element_type=F32)


def _dot_exact(a, b):
    return jnp.dot(a, b, precision=HIGHEST, preferred_element_type=F32)


def _dot_nt_exact(a, b):
    return lax.dot_general(a, b, (((1,), (1,)), ((), ())), precision=HIGHEST,
                           preferred_element_type=F32)


def _iota2(shape, dim):
    return lax.broadcasted_iota(jnp.int32, shape, dim)


def _seq_masks(rows, seq_len):
    t = _iota2((rows, rows), 0)
    s = _iota2((rows, rows), 1)
    same = (t // seq_len) == (s // seq_len)
    causal = same & (s <= t)
    ref = same & ((s % seq_len) < seq_len // 2)
    return causal, ref, same


def _row_specs(arrs, tm):
    if len(arrs) == 1:
        return [pl.BlockSpec((tm, arrs[0].shape[1]), lambda i, *_: (i, 0))]
    npt = T_PROMPT // tm
    return [pl.BlockSpec((tm, arrs[0].shape[1]), lambda i, *_: (jnp.minimum(i, npt - 1), 0)),
            pl.BlockSpec((tm, arrs[1].shape[1]), lambda i, *_: (jnp.maximum(i - npt, 0), 0))]


def _rows(i, refs):
    if len(refs) == 1:
        return refs[0][...]
    return jnp.where(i < T_PROMPT // refs[0].shape[0], refs[0][...], refs[1][...])


def _proj_body(n_h, *refs):
    h_refs, (g_ref, w_ref, o_ref) = refs[:n_h], refs[n_h:]
    hn = _rms(_rows(pl.program_id(0), h_refs), g_ref[...])
    o_ref[...] = _dot(hn, w_ref[...])


def _proj_call(h, g, w_bf):
    tm = 512
    return pl.pallas_call(
        functools.partial(_proj_body, len(h)),
        grid=(T_ALL // tm,),
        in_specs=_row_specs(h, tm) + [pl.BlockSpec((1, D_MODEL), lambda i: (0, 0)),
                                      pl.BlockSpec((D_MODEL, PROJ_W), lambda i: (0, 0))],
        out_specs=pl.BlockSpec((tm, PROJ_W), lambda i: (i, 0)),
        out_shape=jax.ShapeDtypeStruct((T_ALL, PROJ_W), F32),
        compiler_params=pltpu.CompilerParams(dimension_semantics=("arbitrary",),
                                             vmem_limit_bytes=VMEM_LIMIT),
        name="in_proj",
    )(*h, g, w_bf)


def _hgrn_gates(p_q, p_f, la, lc, omlb):
    q = _silu(p_q)
    b = lc + _log_sigmoid(p_f)
    logf = jnp.maximum(la, b) + jnp.log1p(jnp.exp(-jnp.abs(la - b)))
    kin = omlb * _sigmoid(-p_f)
    return q, logf, kin


def _hgrn_decays(logf, causal, ref, same):
    rows = logf.shape[0]
    m = jnp.concatenate([causal.astype(F32), ref.astype(F32), same.astype(F32)], axis=0)
    g3 = _dot_exact(m, logf)
    return g3[:rows], g3[rows:2 * rows], g3[2 * rows:]


def _hgrn_head_out(o, hgn, og):
    return _rms(o, hgn) * _silu(og)


def _ssd_inputs(conv, p_dt, dtb, a_neg):
    conv = _silu(conv)
    xs = conv[:, :SSD_W]
    bm = conv[:, SSD_W:SSD_W + 2 * SSD_N]
    cm = conv[:, SSD_W + 2 * SSD_N:]
    dtf = _softplus(p_dt + dtb)
    a = dtf * a_neg
    return xs, bm, cm, dtf, a


def _lane_bcast(x, lane, width):
    return jnp.broadcast_to(x[:, lane:lane + 1], (x.shape[0], width))


def _ssd_finish(y, xs, z, dsk, ssn):
    y = (y + dsk * xs) * _silu(z)
    half = SSD_W // 2
    return jnp.concatenate([_rms(y[:, :half], ssn[:, :half]), _rms(y[:, half:], ssn[:, half:])],
                           axis=1)


def _scan_prompt_body(p_ref, la_ref, lc_ref, omlb_ref, hgn_ref, cw_ref, cb_ref, dtb_ref, an_ref,
                      dsk_ref, ssn_ref,
                      mix_ref, ohg_ref, ossm_ref, oconv_ref,
                      st_ref, hp_ref, xpad_ref):
    j = pl.program_id(1)
    tl = SSD_CHUNK

    @pl.when(j == 0)
    def _():
        st_ref[...] = jnp.zeros_like(st_ref)
        hp_ref[...] = jnp.zeros_like(hp_ref)
        xpad_ref[0:8, :] = jnp.zeros((8, CONV_DIM), F32)

    causal, ref, same = _seq_masks(HG_CHUNK, HG_CHUNK)
    for c in range(tl // HG_CHUNK):
        r0 = c * HG_CHUNK
        rows = slice(r0, r0 + HG_CHUNK)
        q, logf, kin = _hgrn_gates(p_ref[rows, C_Q:C_Q + HG_W], p_ref[rows, C_F:C_F + HG_W],
                                   la_ref[...], lc_ref[...], omlb_ref[...])
        v = p_ref[rows, C_I:C_I + HG_W]
        g, gmid, glast = _hgrn_decays(logf, causal, ref, same)
        qt = q * jnp.exp(g - gmid)
        kt = kin * jnp.exp(gmid - g)
        qh = q * jnp.exp(g)
        kd = kin * jnp.exp(glast - g)
        ds = jnp.exp(glast[0:1, :])
        for h in range(HG_HEADS):
            sl = slice(h * HG_D, (h + 1) * HG_D)
            sc = jnp.where(causal, _dot_nt(qt[:, sl], kt[:, sl]), 0.0)
            st = st_ref[h]
            o = _dot(sc, v[:, sl]) + _dot_nt(qh[:, sl], st)
            st_ref[h] = st * ds[:, sl] + _dot(v[:, sl].T, kd[:, sl])
            og = p_ref[rows, C_OG + h * HG_D:C_OG + (h + 1) * HG_D]
            mix_ref[rows, sl] = _hgrn_head_out(o, hgn_ref[:, sl], og).astype(BF16)

    xpad_ref[8:8 + tl, :] = p_ref[:, C_XBC:C_XBC + CONV_DIM]
    conv = cb_ref[...] + cw_ref[0:1, :] * xpad_ref[5:5 + tl, :]
    for tap in range(1, CONV_W):
        conv = conv + cw_ref[tap:tap + 1, :] * xpad_ref[5 + tap:5 + tap + tl, :]

    @pl.when(j == pl.num_programs(1) - 1)
    def _():
        oconv_ref[0] = xpad_ref[5 + tl:8 + tl, :]

    xpad_ref[0:8, :] = xpad_ref[tl:tl + 8, :]

    xs, bm, cm, dtf, a = _ssd_inputs(conv, p_ref[:, C_DT:C_DT + LANES], dtb_ref[...], an_ref[...])
    tri, _, _ = _seq_masks(tl, tl)
    trif = tri.astype(F32)
    acum = _dot_exact(trif, a)
    acum_t = _dot_nt_exact(a.T, trif)
    lo = _iota2((tl, LANES), 1) < SSD_P
    top = _iota2((LANES, tl), 0) < SSD_P
    ys = []
    for pair in range(SSD_PAIRS):
        grp = pair // 2
        cg = cm[:, grp * SSD_N:(grp + 1) * SSD_N]
        bg = bm[:, grp * SSD_N:(grp + 1) * SSD_N]
        cb = _dot_nt(cg, bg)
        psl = slice(pair * LANES, (pair + 1) * LANES)
        r0, r1 = 2 * pair, 2 * pair + 1
        xp = xs[:, psl] * jnp.where(lo, _lane_bcast(dtf, r0, LANES), _lane_bcast(dtf, r1, LANES))
        intra, einter, wrow, ea = [], [], [], []
        for r in (r0, r1):
            col = _lane_bcast(acum, r, tl)
            row = jnp.broadcast_to(acum_t[r:r + 1, :], (tl, tl))
            dec = jnp.exp(jnp.where(tri, col - row, -jnp.inf))
            intra.append(_dot(cb * dec, xp))
            einter.append(jnp.exp(col))
            alast = acum_t[r:r + 1, tl - 1:tl]
            wrow.append(jnp.exp(alast - acum_t[r:r + 1, :]))
            ea.append(jnp.exp(alast))
        hp = hp_ref[pair]
        y = jnp.where(lo, intra[0], intra[1]) + jnp.where(lo, einter[0], einter[1]) * _dot_nt(cg, hp)
        xw_t = xp.T * jnp.where(top, wrow[0], wrow[1])
        hp_ref[pair] = jnp.where(top, ea[0], ea[1]) * hp + _dot(xw_t, bg)
        ys.append(y)
    y = jnp.concatenate(ys, axis=1)
    y = _ssd_finish(y, xs, p_ref[:, C_Z:C_Z + SSD_W], dsk_ref[...], ssn_ref[...])
    mix_ref[:, HG_W:] = y.astype(BF16)

    @pl.when(j == pl.num_programs(1) - 1)
    def _():
        for h in range(HG_HEADS):
            ohg_ref[0, h] = st_ref[h].T
        ossm_ref[0] = hp_ref[...]


def _row_spec(width):
    return pl.BlockSpec((1, width), lambda b, j: (0, 0))


def _scan_prompt_call(proj, prm):
    tl = SSD_CHUNK
    nt = SEQ // tl
    in_specs = [pl.BlockSpec((tl, PROJ_W), lambda b, j: (b * nt + j, 0)),
                _row_spec(HG_W), _row_spec(HG_W), _row_spec(HG_W), _row_spec(HG_W),
                pl.BlockSpec((CONV_W, CONV_DIM), lambda b, j: (0, 0)), _row_spec(CONV_DIM),
                _row_spec(LANES), _row_spec(LANES), _row_spec(SSD_W), _row_spec(SSD_W)]
    out_specs = [pl.BlockSpec((tl, D_MODEL), lambda b, j: (b * nt + j, 0)),
                 pl.BlockSpec((1, HG_HEADS, HG_D, HG_D), lambda b, j: (b, 0, 0, 0)),
                 pl.BlockSpec((1, SSD_PAIRS, LANES, SSD_N), lambda b, j: (b, 0, 0, 0)),
                 pl.BlockSpec((1, CONV_W - 1, CONV_DIM), lambda b, j: (b, 0, 0))]
    out_shape = [jax.ShapeDtypeStruct((T_PROMPT, D_MODEL), BF16),
                 jax.ShapeDtypeStruct((BATCH, HG_HEADS, HG_D, HG_D), F32),
                 jax.ShapeDtypeStruct((BATCH, SSD_PAIRS, LANES, SSD_N), F32),
                 jax.ShapeDtypeStruct((BATCH, CONV_W - 1, CONV_DIM), F32)]
    return pl.pallas_call(
        _scan_prompt_body,
        grid=(BATCH, nt),
        in_specs=in_specs, out_specs=out_specs, out_shape=out_shape,
        scratch_shapes=[pltpu.VMEM((HG_HEADS, HG_D, HG_D), F32),
                        pltpu.VMEM((SSD_PAIRS, SSD_N, LANES), F32),
                        pltpu.VMEM((tl + 8, CONV_DIM), F32)],
        compiler_params=pltpu.CompilerParams(dimension_semantics=("arbitrary", "arbitrary"),
                                             vmem_limit_bytes=VMEM_LIMIT),
        name="scan_prompt",
    )(proj, prm["la"], prm["lc"], prm["omlb"], prm["hgn"], prm["cw"], prm["cb"], prm["dtb"],
      prm["an"], prm["dsk"], prm["ssn"])


def _scan_sample_body(n_prev, p_ref, shg_ref, sssm_ref, sconv_ref, la_ref, lc_ref, omlb_ref,
                      hgn_ref, cw_ref, cb_ref, dtb_ref, an_ref, dsk_ref, ssn_ref, *refs):
    if n_prev:
        phg_ref, pssm_ref = refs[:2]
        refs = refs[2:]
    mix_ref, ohg_ref, ossm_ref, oconv_ref, xpad_ref = refs
    if n_prev:
        ohg_ref[0:n_prev] = phg_ref[...]
        ossm_ref[0:n_prev] = pssm_ref[...]
    nb, sl_len = SAMPLE_SEQS, DEC_SEQ
    rows = nb * sl_len
    causal, ref, same = _seq_masks(rows, sl_len)
    rowseq = _iota2((rows, LANES), 0) // sl_len

    q, logf, kin = _hgrn_gates(p_ref[:, C_Q:C_Q + HG_W], p_ref[:, C_F:C_F + HG_W],
                               la_ref[...], lc_ref[...], omlb_ref[...])
    v = p_ref[:, C_I:C_I + HG_W]
    g, gmid, glast = _hgrn_decays(logf, causal, ref, same)
    qt = q * jnp.exp(g - gmid)
    kt = kin * jnp.exp(gmid - g)
    qh = q * jnp.exp(g)
    kd = kin * jnp.exp(glast - g)
    ds_t = jnp.exp(glast).T
    kd_t = kd.T
    for h in range(HG_HEADS):
        sl = slice(h * HG_D, (h + 1) * HG_D)
        sc = jnp.where(causal, _dot_nt(qt[:, sl], kt[:, sl]), 0.0)
        o_intra = _dot(sc, v[:, sl])
        o_inter = []
        for b in range(nb):
            s_old = shg_ref[b, h]
            o_inter.append(_dot(qh[b * sl_len:(b + 1) * sl_len, sl], s_old))
            vb = jnp.where(rowseq == b, v[:, sl], 0.0)
            dcol = _lane_bcast(ds_t[sl, :], b * sl_len, HG_D)
            ohg_ref[n_prev, b, h] = dcol * s_old + _dot(kd_t[sl, :], vb)
        o = o_intra + jnp.concatenate(o_inter, axis=0)
        og = p_ref[:, C_OG + h * HG_D:C_OG + (h + 1) * HG_D]
        mix_ref[:, sl] = _hgrn_head_out(o, hgn_ref[:, sl], og).astype(BF16)

    convs = []
    for b in range(nb):
        base = 16 * b
        xpad_ref[base + 5:base + 8, :] = sconv_ref[b]
        xpad_ref[base + 8:base + 16, :] = p_ref[b * sl_len:(b + 1) * sl_len, C_XBC:C_XBC + CONV_DIM]
        cv = cb_ref[...] + cw_ref[0:1, :] * xpad_ref[base + 5:base + 13, :]
        for tap in range(1, CONV_W):
            cv = cv + cw_ref[tap:tap + 1, :] * xpad_ref[base + 5 + tap:base + 13 + tap, :]
        convs.append(cv)
        oconv_ref[b] = xpad_ref[base + 13:base + 16, :]
    conv = jnp.concatenate(convs, axis=0)

    xs, bm, cm, dtf, a = _ssd_inputs(conv, p_ref[:, C_DT:C_DT + LANES], dtb_ref[...], an_ref[...])
    causf = causal.astype(F32)
    acum = _dot_exact(causf, a)
    a_t = a.T
    acum_t = _dot_nt_exact(a_t, causf)
    atot_t = _dot_nt_exact(a_t, same.astype(F32))
    wall_t = jnp.exp(atot_t - acum_t)
    eatot_t = jnp.exp(atot_t)
    eall = jnp.exp(acum)
    lo = _iota2((rows, LANES), 1) < SSD_P
    top = _iota2((LANES, rows), 0) < SSD_P
    top_sq = _iota2((LANES, SSD_N), 0) < SSD_P
    colseq = _iota2((LANES, rows), 1) // sl_len
    ys = []
    for pair in range(SSD_PAIRS):
        grp = pair // 2
        cg = cm[:, grp * SSD_N:(grp + 1) * SSD_N]
        bg = bm[:, grp * SSD_N:(grp + 1) * SSD_N]
        cb = _dot_nt(cg, bg)
        psl = slice(pair * LANES, (pair + 1) * LANES)
        r0, r1 = 2 * pair, 2 * pair + 1
        xp = xs[:, psl] * jnp.where(lo, _lane_bcast(dtf, r0, LANES), _lane_bcast(dtf, r1, LANES))
        xw_t = xp.T * jnp.where(top, wall_t[r0:r0 + 1, :], wall_t[r1:r1 + 1, :])
        intra = []
        for r in (r0, r1):
            col = _lane_bcast(acum, r, rows)
            row = jnp.broadcast_to(acum_t[r:r + 1, :], (rows, rows))
            dec = jnp.exp(jnp.where(causal, col - row, -jnp.inf))
            intra.append(_dot(cb * dec, xp))
        y_intra = jnp.where(lo, intra[0], intra[1])
        y_inter = []
        for b in range(nb):
            tr = slice(b * sl_len, (b + 1) * sl_len)
            h_old = sssm_ref[b, pair]
            y_inter.append(_dot_nt(cg[tr, :], h_old))
            upd = _dot(jnp.where(colseq == b, xw_t, 0.0), bg)
            c0 = b * sl_len
            ea = jnp.where(top_sq, eatot_t[r0:r0 + 1, c0:c0 + 1], eatot_t[r1:r1 + 1, c0:c0 + 1])
            ossm_ref[n_prev, b, pair] = ea * h_old + upd
        e_pair = jnp.where(lo, _lane_bcast(eall, r0, LANES), _lane_bcast(eall, r1, LANES))
        ys.append(y_intra + e_pair * jnp.concatenate(y_inter, axis=0))
    y = jnp.concatenate(ys, axis=1)
    y = _ssd_finish(y, xs, p_ref[:, C_Z:C_Z + SSD_W], dsk_ref[...], ssn_ref[...])
    mix_ref[:, HG_W:] = y.astype(BF16)


def _scan_sample_call(l, proj, s_hg, s_ssm_t, s_conv, prm, prev):
    nb = SAMPLE_SEQS
    rows = nb * DEC_SEQ
    row0 = T_PROMPT // rows
    c0 = lambda i: (0, 0)
    rs = lambda w: pl.BlockSpec((1, w), c0)
    state_blk = (nb, HG_HEADS, HG_D, HG_D)
    in_specs = [pl.BlockSpec((rows, PROJ_W), lambda i: (row0 + i, 0)),
                pl.BlockSpec((None,) + state_blk, lambda i: (l, i, 0, 0, 0)),
                pl.BlockSpec((None,) + state_blk, lambda i: (l, i, 0, 0, 0)),
                pl.BlockSpec((None, nb, CONV_W - 1, CONV_DIM), lambda i: (l, i, 0, 0)),
                rs(HG_W), rs(HG_W), rs(HG_W), rs(HG_W),
                pl.BlockSpec((CONV_W, CONV_DIM), c0), rs(CONV_DIM),
                rs(LANES), rs(LANES), rs(SSD_W), rs(SSD_W)]
    in_specs += [pl.BlockSpec((l,) + state_blk, lambda i: (0, i, 0, 0, 0))] * (2 if l else 0)
    out_specs = [pl.BlockSpec((rows, D_MODEL), lambda i: (i, 0)),
                 pl.BlockSpec((l + 1,) + state_blk, lambda i: (0, i, 0, 0, 0)),
                 pl.BlockSpec((l + 1,) + state_blk, lambda i: (0, i, 0, 0, 0)),
                 pl.BlockSpec((nb, CONV_W - 1, CONV_DIM), lambda i: (i, 0, 0))]
    stacked = jax.ShapeDtypeStruct((l + 1, DEC_BATCH) + state_blk[1:], F32)
    out_shape = [jax.ShapeDtypeStruct((T_SAMPLE, D_MODEL), BF16), stacked, stacked,
                 jax.ShapeDtypeStruct((DEC_BATCH, CONV_W - 1, CONV_DIM), F32)]
    return pl.pallas_call(
        functools.partial(_scan_sample_body, l),
        grid=(DEC_BATCH // nb,),
        in_specs=in_specs, out_specs=out_specs, out_shape=out_shape,
        scratch_shapes=[pltpu.VMEM((16 * nb, CONV_DIM), F32)],
        compiler_params=pltpu.CompilerParams(dimension_semantics=("arbitrary",),
                                             vmem_limit_bytes=VMEM_LIMIT),
        name="scan_sample",
    )(proj, s_hg, s_ssm_t, s_conv, prm["la"], prm["lc"], prm["omlb"], prm["hgn"], prm["cw"],
      prm["cb"], prm["dtb"], prm["an"], prm["dsk"], prm["ssn"], *(prev if l else ()))


def _out_ffn_body(n_h, *refs):
    h_refs, mix_refs = refs[:n_h], refs[n_h:n_h + 2]
    wo_ref, g_ref, wg_ref, wu_ref, wd_ref, o_ref, hn_ref, acc_ref = refs[n_h + 2:]
    i = pl.program_id(0)
    f = pl.program_id(1)

    @pl.when(f == 0)
    def _():
        h1 = _rows(i, h_refs) + _dot(_rows(i, mix_refs), wo_ref[...])
        acc_ref[...] = h1
        hn_ref[...] = _rms(h1, g_ref[...]).astype(BF16)

    x = hn_ref[...]
    gate = _dot(x, wg_ref[...])
    up = _dot(x, wu_ref[...])
    acc_ref[...] += _dot(_silu(gate) * up, wd_ref[...])

    @pl.when(f == pl.num_programs(1) - 1)
    def _():
        o_ref[...] = acc_ref[...]


def _out_ffn_call(h, mixed, wo_bf, g, wi_bf, wd_bf):
    tm, tf = 512, D_FF // 2
    nf = D_FF // tf
    return pl.pallas_call(
        functools.partial(_out_ffn_body, len(h)),
        grid=(T_ALL // tm, nf),
        in_specs=_row_specs(h, tm) + _row_specs(mixed, tm) + [
                  pl.BlockSpec((D_MODEL, D_MODEL), lambda i, f: (0, 0)),
                  pl.BlockSpec((1, D_MODEL), lambda i, f: (0, 0)),
                  pl.BlockSpec((D_MODEL, tf), lambda i, f: (0, f)),
                  pl.BlockSpec((D_MODEL, tf), lambda i, f: (0, nf + f)),
                  pl.BlockSpec((tf, D_MODEL), lambda i, f: (f, 0))],
        out_specs=pl.BlockSpec((tm, D_MODEL), lambda i, f: (i, 0)),
        out_shape=jax.ShapeDtypeStruct((T_ALL, D_MODEL), F32),
        scratch_shapes=[pltpu.VMEM((tm, D_MODEL), BF16), pltpu.VMEM((tm, D_MODEL), F32)],
        compiler_params=pltpu.CompilerParams(dimension_semantics=("arbitrary", "arbitrary"),
                                             vmem_limit_bytes=VMEM_LIMIT),
        name="out_ffn",
    )(*h, *mixed, wo_bf, g, wi_bf, wi_bf, wd_bf)


def _out_router_body(h_ref, mixp_ref, mixs_ref, wo_ref, g_ref, wr_ref,
                     h1_ref, hn_ref, route_ref, seg_ref):
    h1 = h_ref[...] + _dot(_rows(pl.program_id(0), (mixp_ref, mixs_ref)), wo_ref[...])
    h1_ref[...] = h1
    hn = _rms(h1, g_ref[...])
    hn_ref[...] = hn.astype(BF16)
    logits = _dot_exact(hn, wr_ref[...])
    lane = _iota2(logits.shape, 1)
    lg = jnp.where(lane < N_EXPERTS, logits, -jnp.inf)
    m1 = jnp.max(lg, axis=1, keepdims=True)
    i1 = jnp.min(jnp.where(lg == m1, lane, LANES), axis=1, keepdims=True)
    lg2 = jnp.where(lane == i1, -jnp.inf, lg)
    m2 = jnp.max(lg2, axis=1, keepdims=True)
    i2 = jnp.min(jnp.where(lg2 == m2, lane, LANES), axis=1, keepdims=True)
    e2 = jnp.exp(m2 - m1)
    g1 = 1.0 / (1.0 + e2)
    g2 = e2 / (1.0 + e2)

    tm = logits.shape[0]
    chosen = jnp.where((lane == i1) | (lane == i2), 1.0, 0.0)
    earlier = (_iota2((tm, tm), 1) < _iota2((tm, tm), 0)).astype(F32)
    rank = _dot(earlier, chosen)
    cnt = jnp.sum(chosen, axis=0, keepdims=True)
    seg = jnp.floor((cnt + (SUBLANES - 1)) * (1.0 / SUBLANES)) * SUBLANES
    below = (_iota2((LANES, LANES), 0) < _iota2((LANES, LANES), 1)).astype(F32)
    loc = _dot(jnp.broadcast_to(seg, (SUBLANES, LANES)), below)[0:1, :]
    pos = loc + rank
    p1 = jnp.sum(jnp.where(lane == i1, pos, 0.0), axis=1, keepdims=True)
    p2 = jnp.sum(jnp.where(lane == i2, pos, 0.0), axis=1, keepdims=True)
    route_ref[...] = jnp.where(lane == 0, p1, jnp.where(lane == 1, p2, jnp.where(
        lane == 2, g1, jnp.where(lane == 3, g2, 0.0))))
    seg_ref[0] = jnp.broadcast_to(seg, (SUBLANES, LANES)).astype(jnp.int32)


def _out_router_call(h, mixed, wo_bf, g, wr_pad):
    tm = MOE_TOKENS
    return pl.pallas_call(
        _out_router_body,
        grid=(MOE_TILES,),
        in_specs=[pl.BlockSpec((tm, D_MODEL), lambda i: (i, 0))] + _row_specs(mixed, tm) + [
                  pl.BlockSpec((D_MODEL, D_MODEL), lambda i: (0, 0)),
                  pl.BlockSpec((1, D_MODEL), lambda i: (0, 0)),
                  pl.BlockSpec((D_MODEL, LANES), lambda i: (0, 0))],
        out_specs=[pl.BlockSpec((tm, D_MODEL), lambda i: (i, 0)),
                   pl.BlockSpec((tm, D_MODEL), lambda i: (i, 0)),
                   pl.BlockSpec((tm, LANES), lambda i: (i, 0)),
                   pl.BlockSpec((1, SUBLANES, LANES), lambda i: (i, 0, 0))],
        out_shape=[jax.ShapeDtypeStruct((T_ALL, D_MODEL), F32),
                   jax.ShapeDtypeStruct((T_ALL, D_MODEL), BF16),
                   jax.ShapeDtypeStruct((T_ALL, LANES), F32),
                   jax.ShapeDtypeStruct((MOE_TILES, SUBLANES, LANES), jnp.int32)],
        compiler_params=pltpu.CompilerParams(dimension_semantics=("arbitrary",),
                                             vmem_limit_bytes=VMEM_LIMIT),
        name="out_router",
    )(h, *mixed, wo_bf, g, wr_pad)


def _row_tile_copy(tile_ref, hbm_ref, tile_row, hbm_row, sem, to_hbm):
    t = tile_ref.at[pl.ds(pl.multiple_of(tile_row, SUBLANES), SUBLANES), :]
    g = hbm_ref.at[pl.ds(pl.multiple_of(hbm_row, SUBLANES), SUBLANES), :]
    return pltpu.make_async_copy(t, g, sem) if to_hbm else pltpu.make_async_copy(g, t, sem)


def _seg_copies(i, n_ref, loc_ref, start_ref, tile_ref, hbm_ref, sem, to_hbm):
    for e in range(N_EXPERTS):
        k = i * N_EXPERTS + e
        lo = loc_ref[k]
        st = start_ref[k]

        def body(c, carry, lo=lo, st=st):
            _row_tile_copy(tile_ref, hbm_ref, lo + c * SUBLANES, st + c * SUBLANES, sem,
                           to_hbm).start()
            return carry

        lax.fori_loop(0, n_ref[k], body, 0)


def _seg_wait(i, n_ref, tile_ref, hbm_ref, sem, to_hbm):
    total = n_ref[i * N_EXPERTS]
    for e in range(1, N_EXPERTS):
        total = total + n_ref[i * N_EXPERTS + e]

    def body(c, carry):
        _row_tile_copy(tile_ref, hbm_ref, 0, 0, sem, to_hbm).wait()
        return carry

    lax.fori_loop(0, total, body, 0)


def _dispatch_body(n_ref, loc_ref, start_ref, zn_ref, zstart_ref, zb_ref, zbn_ref,
                   hn_ref, route_ref, xs_ref, stage_ref, zero_ref, sem):
    i = pl.program_id(0)
    last = pl.num_programs(0) - 1
    slot = i % 2

    @pl.when(i >= 2)
    def _():
        _seg_wait(i - 2, n_ref, stage_ref.at[slot], xs_ref, sem.at[slot], True)

    rt = route_ref[...].T
    s = _iota2((MOE_SLOTS, MOE_TOKENS), 0).astype(F32)
    perm = jnp.where((s == rt[0:1, :]) | (s == rt[1:2, :]), 1.0, 0.0)
    stage_ref[slot] = _dot(perm, hn_ref[...])
    _seg_copies(i, n_ref, loc_ref, start_ref, stage_ref.at[slot], xs_ref, sem.at[slot], True)

    @pl.when(i == last)
    def _():
        zero_ref[...] = jnp.zeros_like(zero_ref)

        def blk_copy(b):
            return pltpu.make_async_copy(
                zero_ref, xs_ref.at[pl.ds(pl.multiple_of(b * GMM_SUB, GMM_SUB), GMM_SUB), :],
                sem.at[3])

        def bbody(b, carry):
            @pl.when(zb_ref[b] == 1)
            def _():
                blk_copy(b).start()
            return carry

        lax.fori_loop(0, GMM_TILES * GMM_ROWS // GMM_SUB, bbody, 0)
        nz = zn_ref[0]
        for e in range(N_EXPERTS):
            st = zstart_ref[e]

            def body(c, carry, st=st):
                _row_tile_copy(zero_ref, xs_ref, 0, st + c * SUBLANES, sem.at[2], True).start()
                return carry

            lax.fori_loop(0, zn_ref[e], body, 0)
            if e:
                nz = nz + zn_ref[e]

        @pl.when(last >= 1)
        def _():
            _seg_wait(i - 1, n_ref, stage_ref.at[1 - slot], xs_ref, sem.at[1 - slot], True)

        _seg_wait(i, n_ref, stage_ref.at[slot], xs_ref, sem.at[slot], True)

        def zbody(c, carry):
            _row_tile_copy(zero_ref, xs_ref, 0, 0, sem.at[2], True).wait()
            return carry

        lax.fori_loop(0, nz, zbody, 0)

        def bwait(c, carry):
            blk_copy(0).wait()
            return carry

        lax.fori_loop(0, zbn_ref[0], bwait, 0)


def _dispatch_call(sched, hn_bf, route):
    grid_spec = pltpu.PrefetchScalarGridSpec(
        num_scalar_prefetch=7,
        grid=(MOE_TILES,),
        in_specs=[pl.BlockSpec((MOE_TOKENS, D_MODEL), lambda i, *_: (i, 0)),
                  pl.BlockSpec((MOE_TOKENS, LANES), lambda i, *_: (i, 0))],
        out_specs=pl.BlockSpec(memory_space=pl.ANY),
        scratch_shapes=[pltpu.VMEM((2, MOE_SLOTS, D_MODEL), F32),
                        pltpu.VMEM((GMM_SUB, D_MODEL), F32),
                        pltpu.SemaphoreType.DMA((4,))])
    return pl.pallas_call(
        _dispatch_body,
        grid_spec=grid_spec,
        out_shape=jax.ShapeDtypeStruct((GMM_TILES * GMM_ROWS, D_MODEL), F32),
        compiler_params=pltpu.CompilerParams(dimension_semantics=("arbitrary",),
                                             vmem_limit_bytes=VMEM_LIMIT),
        name="moe_dispatch",
    )(sched["nch"], sched["loc"], sched["start"], sched["zn"], sched["zstart"], sched["zb"],
      sched["zbn"], hn_bf, route)


def _swiglu_part(x, wg, wu, wd):
    gate = jnp.dot(x, wg, preferred_element_type=F32)
    up = jnp.dot(x, wu, preferred_element_type=F32)
    return jnp.dot((_silu(gate) * up).astype(BF16), wd, preferred_element_type=F32)


def _experts_body(te_ref, tv_ref, nu_ref, x_ref, wg_ref, wu_ref, wd_ref, o_ref,
                  wgb_ref, wub_ref, wdb_ref):
    del te_ref, nu_ref
    i = pl.program_id(0)
    f = pl.program_id(1)
    nv = tv_ref[i]

    @pl.when(f == 0)
    def _():
        o_ref[...] = jnp.zeros_like(o_ref)

    @pl.when(nv == GMM_ROWS)
    def _():
        o_ref[...] += _swiglu_part(x_ref[...].astype(BF16), wg_ref[0].astype(BF16),
                                   wu_ref[0].astype(BF16), wd_ref[0].astype(BF16))

    @pl.when((nv > 0) & (nv < GMM_ROWS))
    def _():
        wgb_ref[...] = wg_ref[0].astype(BF16)
        wub_ref[...] = wu_ref[0].astype(BF16)
        wdb_ref[...] = wd_ref[0].astype(BF16)
        for sub in range(GMM_ROWS // GMM_SUB):
            rows = slice(sub * GMM_SUB, (sub + 1) * GMM_SUB)

            @pl.when(sub * GMM_SUB < nv)
            def _(rows=rows):
                o_ref[rows, :] += _swiglu_part(x_ref[rows, :].astype(BF16), wgb_ref[...],
                                               wub_ref[...], wdb_ref[...])


def _experts_call(sched, x_sorted, w_in_e, w_out_e):
    tf = 512
    nf = D_FF_EXPERT // tf

    def used(i, nu):
        return jnp.minimum(i, nu[0] - 1)

    def fidx(i, f, nu):
        return jnp.where(i < nu[0], f, nf - 1)

    grid_spec = pltpu.PrefetchScalarGridSpec(
        num_scalar_prefetch=3,
        grid=(GMM_TILES, nf),
        in_specs=[pl.BlockSpec((GMM_ROWS, D_MODEL), lambda i, f, te, tv, nu: (used(i, nu), 0)),
                  pl.BlockSpec((1, D_MODEL, tf), lambda i, f, te, tv, nu: (te[i], 0, fidx(i, f, nu))),
                  pl.BlockSpec((1, D_MODEL, tf),
                               lambda i, f, te, tv, nu: (te[i], 0, nf + fidx(i, f, nu))),
                  pl.BlockSpec((1, tf, D_MODEL), lambda i, f, te, tv, nu: (te[i], fidx(i, f, nu), 0))],
        out_specs=pl.BlockSpec((GMM_ROWS, D_MODEL), lambda i, f, te, tv, nu: (i, 0)),
        scratch_shapes=[pltpu.VMEM((D_MODEL, tf), BF16), pltpu.VMEM((D_MODEL, tf), BF16),
                        pltpu.VMEM((tf, D_MODEL), BF16)])
    return pl.pallas_call(
        _experts_body,
        grid_spec=grid_spec,
        out_shape=jax.ShapeDtypeStruct((GMM_TILES * GMM_ROWS, D_MODEL), F32),
        compiler_params=pltpu.CompilerParams(dimension_semantics=("arbitrary", "arbitrary"),
                                             vmem_limit_bytes=VMEM_LIMIT),
        name="moe_experts",
    )(sched["te"], sched["tv"], sched["nu"], x_sorted, w_in_e, w_in_e, w_out_e)


def _combine_body(n_ref, loc_ref, start_ref, h1_ref, route_ref, gf_ref, ys_ref, op_ref, os_ref,
                  buf_ref, sem):
    i = pl.program_id(0)
    last = pl.num_programs(0) - 1
    slot = i % 2

    @pl.when(i == 0)
    def _():
        buf_ref[...] = jnp.zeros_like(buf_ref)
        _seg_copies(0, n_ref, loc_ref, start_ref, buf_ref.at[0], ys_ref, sem.at[0], False)

    @pl.when(i < last)
    def _():
        _seg_copies(i + 1, n_ref, loc_ref, start_ref, buf_ref.at[1 - slot], ys_ref,
                    sem.at[1 - slot], False)

    _seg_wait(i, n_ref, buf_ref.at[slot], ys_ref, sem.at[slot], False)

    route = route_ref[...]
    s = _iota2((MOE_TOKENS, MOE_SLOTS), 1).astype(F32)
    rows = buf_ref[slot]
    y1 = _dot(jnp.where(s == route[:, 0:1], 1.0, 0.0), rows)
    y2 = _dot(jnp.where(s == route[:, 1:2], 1.0, 0.0), rows)
    y = _rms(h1_ref[...] + route[:, 2:3] * y1 + route[:, 3:4] * y2, gf_ref[...])

    @pl.when(i < T_PROMPT // MOE_TOKENS)
    def _():
        op_ref[...] = y

    @pl.when(i >= T_PROMPT // MOE_TOKENS)
    def _():
        os_ref[...] = y


def _combine_call(sched, h1, route, gfin, y_sorted):
    npt = T_PROMPT // MOE_TOKENS
    grid_spec = pltpu.PrefetchScalarGridSpec(
        num_scalar_prefetch=3,
        grid=(MOE_TILES,),
        in_specs=[pl.BlockSpec((MOE_TOKENS, D_MODEL), lambda i, *_: (i, 0)),
                  pl.BlockSpec((MOE_TOKENS, LANES), lambda i, *_: (i, 0)),
                  pl.BlockSpec((1, D_MODEL), lambda i, *_: (0, 0)),
                  pl.BlockSpec(memory_space=pl.ANY)],
        out_specs=[pl.BlockSpec((MOE_TOKENS, D_MODEL), lambda i, *_: (jnp.minimum(i, npt - 1), 0)),
                   pl.BlockSpec((MOE_TOKENS, D_MODEL), lambda i, *_: (jnp.maximum(i - npt, 0), 0))],
        scratch_shapes=[pltpu.VMEM((2, MOE_SLOTS, D_MODEL), F32),
                        pltpu.SemaphoreType.DMA((2,))])
    return pl.pallas_call(
        _combine_body,
        grid_spec=grid_spec,
        out_shape=[jax.ShapeDtypeStruct((T_PROMPT, D_MODEL), F32),
                   jax.ShapeDtypeStruct((T_SAMPLE, D_MODEL), F32)],
        compiler_params=pltpu.CompilerParams(dimension_semantics=("arbitrary",),
                                             vmem_limit_bytes=VMEM_LIMIT),
        name="moe_combine",
    )(sched["nch"], sched["loc"], sched["start"], h1, route, gfin, y_sorted)


def _moe_schedule(seg):
    tot = jnp.sum(seg, axis=0)
    region = (tot + GMM_ROWS - 1) // GMM_ROWS * GMM_ROWS
    base = jnp.cumsum(region) - region
    start = base[None, :] + jnp.cumsum(seg, axis=0) - seg
    loc = jnp.cumsum(seg, axis=1) - seg
    ntile = region // GMM_ROWS
    cum = jnp.cumsum(ntile)
    nu = cum[-1]
    i = jnp.arange(GMM_TILES, dtype=jnp.int32)
    te = jnp.minimum(jnp.sum(i[:, None] >= cum[None, :], axis=1), N_EXPERTS - 1).astype(jnp.int32)
    tv = jnp.clip(tot[te] - (i - (cum - ntile)[te]) * GMM_ROWS, 0, GMM_ROWS)
    tv = jnp.where(i < nu, tv, 0)
    te = jnp.where(i < nu, te, te[jnp.maximum(nu - 1, 0)])
    bstart = jnp.arange(GMM_TILES * GMM_ROWS // GMM_SUB, dtype=jnp.int32)[:, None] * GMM_SUB
    used_end = (base + (tot + GMM_SUB - 1) // GMM_SUB * GMM_SUB)[None, :]
    zb = jnp.any((bstart >= used_end) & (bstart < (base + region)[None, :]), axis=1)
    zb = (zb | (bstart[:, 0] >= jnp.sum(region))).astype(jnp.int32)
    return dict(nch=(seg // SUBLANES).reshape(-1), loc=loc.reshape(-1), start=start.reshape(-1),
                zn=((-tot) % GMM_SUB) // SUBLANES, zstart=base + tot,
                zb=zb, zbn=jnp.sum(zb).reshape(1),
                te=te, tv=tv.astype(jnp.int32), nu=nu.reshape(1).astype(jnp.int32))


def _row(x, width=None):
    x = x.astype(F32).reshape(1, -1)
    if width is not None and x.shape[1] < width:
        x = jnp.pad(x, ((0, 0), (0, width - x.shape[1])))
    return x


def _layer_params(l, lb_p, conv_w, conv_b, a_log, dt_bias, d_skip, hg_norm, ssd_norm):
    lb = jnp.sum(lb_p[1:l + 1], axis=0)
    return dict(
        la=_row(jnp.log(lb)), lc=_row(jnp.log1p(-lb)), omlb=_row(1.0 - lb),
        hgn=_row(hg_norm[l]), cw=conv_w[l].astype(F32), cb=_row(conv_b[l]),
        dtb=_row(dt_bias[l], LANES), an=_row(-jnp.exp(a_log[l].astype(F32)), LANES),
        dsk=_row(jnp.repeat(d_skip[l].astype(F32), SSD_P)), ssn=_row(ssd_norm[l]))


def kernel(x_prompt, x_sample, state_hgrn, state_ssm, state_conv, norm_mix, w_in, conv_w, conv_b,
           a_log, dt_bias, d_skip, lb_param, hg_norm, ssd_norm, w_out, norm_ffn, w_ffn_in,
           w_ffn_out, w_router, w_exp_in, w_exp_out, norm_final):
    h = (x_prompt.reshape(T_PROMPT, D_MODEL), x_sample.reshape(T_SAMPLE, D_MODEL))
    lb_p = jax.nn.softmax(lb_param.astype(F32), axis=0)
    packed = (SSD_PAIRS, LANES, SSD_N)
    s_ssm_t = jnp.swapaxes(state_ssm, 3, 4).reshape((DEPTH, DEC_BATCH) + packed)
    outs = {k: [] for k in ("hg_p", "ssm_p", "conv_p", "conv_s")}
    sample_states = None
    for l in range(DEPTH):
        prm = _layer_params(l, lb_p, conv_w, conv_b, a_log, dt_bias, d_skip, hg_norm, ssd_norm)
        w_in_bf = jnp.pad(w_in[l], ((0, 0), (0, PROJ_W - w_in.shape[2]))).astype(BF16)
        proj = _proj_call(h, _row(norm_mix[l]), w_in_bf)
        mixed_p, hg_p, ssm_p, conv_p = _scan_prompt_call(proj, prm)
        mixed_s, hg_s, ssm_s, conv_s = _scan_sample_call(l, proj, state_hgrn, s_ssm_t, state_conv,
                                                         prm, sample_states)
        sample_states = (hg_s, ssm_s)
        mixed = (mixed_p, mixed_s)
        for k, val in zip(outs, (hg_p, ssm_p, conv_p, conv_s)):
            outs[k].append(val)
        wo_bf = w_out[l].astype(BF16)
        if l % 2 == 0:
            h = (_out_ffn_call(h, mixed, wo_bf, _row(norm_ffn[l]), w_ffn_in[l // 2].astype(BF16),
                               w_ffn_out[l // 2].astype(BF16)),)
        else:
            wr_pad = jnp.pad(w_router[l // 2].astype(F32), ((0, 0), (0, LANES - N_EXPERTS)))
            h1, hn_bf, route, seg = _out_router_call(h[0], mixed, wo_bf, _row(norm_ffn[l]), wr_pad)
            sched = _moe_schedule(seg[:, 0, :N_EXPERTS])
            x_sorted = _dispatch_call(sched, hn_bf, route)
            y_sorted = _experts_call(sched, x_sorted, w_exp_in[l // 2], w_exp_out[l // 2])
            h = _combine_call(sched, h1, route, _row(norm_final), y_sorted)
    y_prompt = h[0].reshape(BATCH, SEQ, D_MODEL)
    y_sample = h[1].reshape(DEC_BATCH, DEC_SEQ, D_MODEL)
    def unpack_ssm(s_t, batch):
        return jnp.swapaxes(s_t.reshape(DEPTH, batch, SSD_HEADS, SSD_P, SSD_N), 3, 4)

    return (y_prompt, y_sample, jnp.stack(outs["hg_p"]), unpack_ssm(jnp.stack(outs["ssm_p"]), BATCH),
            jnp.stack(outs["conv_p"]), sample_states[0], unpack_ssm(sample_states[1], DEC_BATCH),
            jnp.stack(outs["conv_s"]))
```

```python
import functools

import jax
import jax.numpy as jnp
import numpy as np
from jax import lax
from jax.experimental import pallas as pl
from jax.experimental.pallas import tpu as pltpu

F32 = jnp.float32
BF16 = jnp.bfloat16

D_MODEL = 1024
BATCH = 8
SEQ = 2048
DEPTH = 2
DEC_BATCH = 128
DEC_SEQ = 8
HG_HEADS = 4
HG_D = 128
HG_W = HG_HEADS * HG_D
SSD_HEADS = 8
SSD_P = 64
SSD_N = 128
SSD_W = SSD_HEADS * SSD_P
SSD_PAIRS = SSD_HEADS // 2
CONV_W = 4
CONV_DIM = 1024
D_FF = 2816
N_EXPERTS = 8
D_FF_EXPERT = 3584
EPS = 1e-6

LANES = 128
C_Q, C_F, C_I, C_OG, C_Z, C_XBC, C_DT = 0, 512, 1024, 1536, 2048, 2560, 3584
PROJ_W = C_DT + LANES
HG_CHUNK = 64
SSD_CHUNK = 128
SAMPLE_SEQS = 8
PROMPT_SEQS = 4
HG_SAFE_RANGE = 80.0

T_PROMPT = BATCH * SEQ
T_SAMPLE = DEC_BATCH * DEC_SEQ
T_ALL = T_PROMPT + T_SAMPLE

SUBLANES = 8
TOP_K = 2
MOE_TOKENS = 256
MOE_TILES = T_ALL // MOE_TOKENS
MOE_SLOTS = TOP_K * MOE_TOKENS + N_EXPERTS * SUBLANES
GMM_ROWS = 1024
GMM_SUB = 256
GMM_TILES = (TOP_K * T_ALL + MOE_TILES * N_EXPERTS * (SUBLANES - 1)
             + N_EXPERTS * (GMM_ROWS - 1)) // GMM_ROWS + 1

VMEM_LIMIT = 56 * 1024 * 1024


def _sigmoid(x):
    return 0.5 * jnp.tanh(0.5 * x) + 0.5


def _silu(x):
    return x * _sigmoid(x)


def _softplus(x):
    return jnp.maximum(x, 0.0) + jnp.log(1.0 + jnp.exp(-jnp.abs(x)))


def _rms(x, g):
    return x * lax.rsqrt(jnp.mean(x * x, axis=-1, keepdims=True) + EPS) * g


def _dot(a, b):
    return jnp.dot(a.astype(BF16), b.astype(BF16), preferred_element_type=F32)


def _dot_nt(a, b):
    return lax.dot_general(a.astype(BF16), b.astype(BF16), (((1,), (1,)), ((), ())),
                           preferred_element_type=F32)


def _split3(x):
    x1 = x.astype(BF16)
    r1 = x - x1.astype(F32)
    x2 = r1.astype(BF16)
    x3 = (r1 - x2.astype(F32)).astype(BF16)
    return x1, x2, x3


def _mask_dot(m, x):
    mb = m.astype(BF16)
    x1, x2, x3 = _split3(x)
    return (jnp.dot(mb, x1, preferred_element_type=F32) + jnp.dot(mb, x2, preferred_element_type=F32)
            + jnp.dot(mb, x3, preferred_element_type=F32))


def _mask_dot_nt(x, m):
    mb = m.astype(BF16)
    dn = (((1,), (1,)), ((), ()))
    x1, x2, x3 = _split3(x)
    return (lax.dot_general(x1, mb, dn, preferred_element_type=F32)
            + lax.dot_general(x2, mb, dn, preferred_element_type=F32)
            + lax.dot_general(x3, mb, dn, preferred_element_type=F32))


def _dot_f32x3(a, b):
    a1, a2, _ = _split3(a)
    b1, b2, _ = _split3(b)
    return (jnp.dot(a1, b1, preferred_element_type=F32) + jnp.dot(a1, b2, preferred_element_type=F32)
            + jnp.dot(a2, b1, preferred_element_type=F32))


def _iota2(shape, dim):
    return lax.broadcasted_iota(jnp.int32, shape, dim)


def _seq_masks(rows, seq_len):
    t = _iota2((rows, rows), 0)
    s = _iota2((rows, rows), 1)
    same = (t // seq_len) == (s // seq_len)
    causal = same & (s <= t)
    ref = same & ((s % seq_len) < seq_len // 2)
    return causal, ref, same


def _row_specs(arrs, tm):
    if len(arrs) == 1:
        return [pl.BlockSpec((tm, arrs[0].shape[1]), lambda i, *_: (i, 0))]
    npt = T_PROMPT // tm
    return [pl.BlockSpec((tm, arrs[0].shape[1]), lambda i, *_: (jnp.minimum(i, npt - 1), 0)),
            pl.BlockSpec((tm, arrs[1].shape[1]), lambda i, *_: (jnp.maximum(i - npt, 0), 0))]


def _rows(i, refs):
    if len(refs) == 1:
        return refs[0][...]
    return jnp.where(i < T_PROMPT // refs[0].shape[0], refs[0][...], refs[1][...])


def _proj_body(n_h, *refs):
    h_refs, (g_ref, w_ref, o_ref) = refs[:n_h], refs[n_h:]
    hn = _rms(_rows(pl.program_id(0), h_refs), g_ref[...])
    o_ref[...] = _dot(hn, w_ref[...])


def _proj_call(h, g, w_bf):
    tm = 512
    return pl.pallas_call(
        functools.partial(_proj_body, len(h)),
        grid=(T_ALL // tm,),
        in_specs=_row_specs(h, tm) + [pl.BlockSpec((1, D_MODEL), lambda i: (0, 0)),
                                      pl.BlockSpec((D_MODEL, PROJ_W), lambda i: (0, 0))],
        out_specs=pl.BlockSpec((tm, PROJ_W), lambda i: (i, 0)),
        out_shape=jax.ShapeDtypeStruct((T_ALL, PROJ_W), F32),
        compiler_params=pltpu.CompilerParams(dimension_semantics=("arbitrary",),
                                             vmem_limit_bytes=VMEM_LIMIT),
        name="in_proj",
    )(*h, g, w_bf)


def _hgrn_gates(p_q, p_f, la, lc, omlb):
    q = _silu(p_q)
    e = jnp.exp(-jnp.abs(p_f))
    b = lc + jnp.minimum(p_f, 0.0) - jnp.log(1.0 + e)
    logf = jnp.maximum(la, b) + jnp.log(1.0 + jnp.exp(-jnp.abs(la - b)))
    r = 1.0 / (1.0 + e)
    kin = omlb * jnp.where(p_f >= 0.0, e * r, r)
    return q, logf, kin


def _hgrn_decays(logf, causal, ref, same):
    rows = logf.shape[0]
    if ref is None:
        g = _mask_dot(causal, logf)
        return g, g[rows // 2 - 1:rows // 2, :], g[rows - 1:rows, :]
    m = jnp.concatenate([causal.astype(F32), ref.astype(F32), same.astype(F32)], axis=0)
    g3 = _mask_dot(m, logf)
    return g3[:rows], g3[rows:2 * rows], g3[2 * rows:]


def _hgrn_exact_intra(ex_ref, q, kin, g, v, seq_len):
    rows = q.shape[0]
    ex_ref[0] = g
    ex_ref[1] = q
    ex_ref[2] = kin
    ex_ref[3] = v
    s_idx = _iota2((rows, HG_W), 0)

    def body(t, carry):
        gt = ex_ref[0, pl.ds(t, 1), :]
        qt = ex_ref[1, pl.ds(t, 1), :]
        live = (s_idx <= t) & (s_idx // seq_len == t // seq_len)
        w = jnp.where(live, qt * ex_ref[2] * jnp.exp(jnp.minimum(gt - ex_ref[0], 0.0)), 0.0)
        outs = []
        for h in range(HG_HEADS):
            sl = slice(h * HG_D, (h + 1) * HG_D)
            score = jnp.sum(w[:, sl], axis=1, keepdims=True)
            outs.append(jnp.sum(score * ex_ref[3, :, sl], axis=0, keepdims=True))
        ex_ref[4, pl.ds(t, 1), :] = jnp.concatenate(outs, axis=1)
        return carry

    lax.fori_loop(0, rows, body, 0)
    return ex_ref[4]


def _hgrn_head_out(o, hgn, og):
    return _rms(o, hgn) * _silu(og)


def _ssd_inputs(conv, p_dt, dtb, a_neg):
    conv = _silu(conv)
    xs = conv[:, :SSD_W]
    bm = conv[:, SSD_W:SSD_W + 2 * SSD_N]
    cm = conv[:, SSD_W + 2 * SSD_N:]
    dtf = _softplus(p_dt + dtb)
    a = dtf * a_neg
    return xs, bm, cm, dtf, a


def _lane_bcast(x, lane, width):
    return jnp.broadcast_to(x[:, lane:lane + 1], (x.shape[0], width))


def _ssd_finish(y, xs, z, dsk, ssn):
    y = (y + dsk * xs) * _silu(z)
    half = SSD_W // 2
    return jnp.concatenate([_rms(y[:, :half], ssn[:, :half]), _rms(y[:, half:], ssn[:, half:])],
                           axis=1)


def _scan_prompt_body(*refs):
    ns = PROMPT_SEQS
    p_refs, refs = refs[:ns], refs[ns:]
    (la_ref, lc_ref, omlb_ref), params = refs[:3], refs[3:10]
    mix_ref, ohg_ref, ossm_ref, oconv_ref, st_ref, hp_ref, xpad_ref, ex_ref = refs[10:]
    j = pl.program_id(1)
    tl = SSD_CHUNK

    @pl.when(j == 0)
    def _():
        st_ref[...] = jnp.zeros_like(st_ref)
        hp_ref[...] = jnp.zeros_like(hp_ref)
        xpad_ref[:, 0:8, :] = jnp.zeros((ns, 8, CONV_DIM), F32)

    gates, worst = [], None
    for s in range(ns):
        gates.append([])
        for c in range(tl // HG_CHUNK):
            rows = slice(c * HG_CHUNK, (c + 1) * HG_CHUNK)
            gt = _hgrn_gates(p_refs[s][rows, C_Q:C_Q + HG_W], p_refs[s][rows, C_F:C_F + HG_W],
                             la_ref[...], lc_ref[...], omlb_ref[...])
            gates[s].append(gt)
            half = HG_CHUNK // 2
            for part in (gt[1][:half], gt[1][half:]):
                tot = jnp.sum(part, axis=0, keepdims=True)
                worst = tot if worst is None else jnp.minimum(worst, tot)
    safe = jnp.min(worst) >= -HG_SAFE_RANGE

    for exact in (False, True):
        @pl.when(safe != exact)
        def _(exact=exact):
            for s in range(ns):
                _scan_prompt_tile(gates[s], exact, ex_ref, p_refs[s], *params, mix_ref.at[s],
                                  st_ref.at[s], hp_ref.at[s], xpad_ref.at[s])

    @pl.when(j == pl.num_programs(1) - 1)
    def _():
        for s in range(ns):
            oconv_ref[s] = xpad_ref[s, 5:8, :]
            for h in range(HG_HEADS):
                ohg_ref[s, h] = st_ref[s, h].T
        ossm_ref[...] = hp_ref[...]


def _scan_prompt_tile(gates, exact, ex_ref, p_ref, hgn_ref, cw_ref, cb_ref, dtb_ref, an_ref,
                      dsk_ref, ssn_ref, mix_ref, st_ref, hp_ref, xpad_ref):
    tl = SSD_CHUNK
    causal, _, _ = _seq_masks(HG_CHUNK, HG_CHUNK)
    for c in range(tl // HG_CHUNK):
        r0 = c * HG_CHUNK
        rows = slice(r0, r0 + HG_CHUNK)
        q, logf, kin = gates[c]
        v = p_ref[rows, C_I:C_I + HG_W]
        g, gmid, glast = _hgrn_decays(logf, causal, None, None)
        if exact:
            qh = q * jnp.exp(g)
            kd = kin * jnp.exp(glast - g)
            o_intra = _hgrn_exact_intra(ex_ref, q, kin, g, v, HG_CHUNK)
        else:
            qt = q * jnp.exp(g - gmid)
            kt = kin * jnp.exp(gmid - g)
            qh = qt * jnp.exp(gmid)
            kd = kt * jnp.exp(glast - gmid)
        ds = jnp.exp(glast[0:1, :])
        for h in range(HG_HEADS):
            sl = slice(h * HG_D, (h + 1) * HG_D)
            st = st_ref[h]
            if exact:
                o = o_intra[:, sl] + _dot_nt(qh[:, sl], st)
            else:
                sc = jnp.where(causal, _dot_nt(qt[:, sl], kt[:, sl]), 0.0)
                o = _dot(sc, v[:, sl]) + _dot_nt(qh[:, sl], st)
            st_ref[h] = st * ds[:, sl] + _dot(v[:, sl].T, kd[:, sl])
            og = p_ref[rows, C_OG + h * HG_D:C_OG + (h + 1) * HG_D]
            mix_ref[rows, sl] = _hgrn_head_out(o, hgn_ref[:, sl], og).astype(BF16)

    xpad_ref[8:8 + tl, :] = p_ref[:, C_XBC:C_XBC + CONV_DIM]
    conv = cb_ref[...] + cw_ref[0:1, :] * xpad_ref[5:5 + tl, :]
    for tap in range(1, CONV_W):
        conv = conv + cw_ref[tap:tap + 1, :] * xpad_ref[5 + tap:5 + tap + tl, :]
    xpad_ref[0:8, :] = xpad_ref[tl:tl + 8, :]

    xs, bm, cm, dtf, a = _ssd_inputs(conv, p_ref[:, C_DT:C_DT + LANES], dtb_ref[...], an_ref[...])
    tri, _, _ = _seq_masks(tl, tl)
    trif = tri.astype(F32)
    acum = _mask_dot(trif, a)
    acum_t = _mask_dot_nt(a.T, trif)
    lo = _iota2((tl, LANES), 1) < SSD_P
    top = _iota2((LANES, tl), 0) < SSD_P
    ys = []
    for pair in range(SSD_PAIRS):
        grp = pair // 2
        cg = cm[:, grp * SSD_N:(grp + 1) * SSD_N]
        bg = bm[:, grp * SSD_N:(grp + 1) * SSD_N]
        cb = _dot_nt(cg, bg)
        psl = slice(pair * LANES, (pair + 1) * LANES)
        r0, r1 = 2 * pair, 2 * pair + 1
        xp = xs[:, psl] * jnp.where(lo, _lane_bcast(dtf, r0, LANES), _lane_bcast(dtf, r1, LANES))
        intra, einter, wrow, ea = [], [], [], []
        for r in (r0, r1):
            col = _lane_bcast(acum, r, tl)
            row = jnp.broadcast_to(acum_t[r:r + 1, :], (tl, tl))
            dec = jnp.exp(jnp.where(tri, col - row, -jnp.inf))
            intra.append(_dot(cb * dec, xp))
            einter.append(jnp.exp(col))
            alast = acum_t[r:r + 1, tl - 1:tl]
            wrow.append(jnp.exp(alast - acum_t[r:r + 1, :]))
            ea.append(jnp.exp(alast))
        hp = hp_ref[pair]
        y = jnp.where(lo, intra[0], intra[1]) + jnp.where(lo, einter[0], einter[1]) * _dot_nt(cg, hp)
        xw_t = xp.T * jnp.where(top, wrow[0], wrow[1])
        hp_ref[pair] = jnp.where(top, ea[0], ea[1]) * hp + _dot(xw_t, bg)
        ys.append(y)
    y = jnp.concatenate(ys, axis=1)
    y = _ssd_finish(y, xs, p_ref[:, C_Z:C_Z + SSD_W], dsk_ref[...], ssn_ref[...])
    mix_ref[:, HG_W:] = y.astype(BF16)


def _row_spec(width):
    return pl.BlockSpec((1, width), lambda b, j: (0, 0))


def _scan_prompt_call(proj, prm):
    tl = SSD_CHUNK
    nt = SEQ // tl
    ns = PROMPT_SEQS
    state_blk = (ns, HG_HEADS, HG_D, HG_D)
    in_specs = [pl.BlockSpec((tl, PROJ_W), lambda b, j, s=s: ((b * ns + s) * nt + j, 0))
                for s in range(ns)]
    in_specs += [_row_spec(HG_W), _row_spec(HG_W), _row_spec(HG_W), _row_spec(HG_W),
                 pl.BlockSpec((CONV_W, CONV_DIM), lambda b, j: (0, 0)), _row_spec(CONV_DIM),
                 _row_spec(LANES), _row_spec(LANES), _row_spec(SSD_W), _row_spec(SSD_W)]
    out_specs = [pl.BlockSpec((ns, tl, D_MODEL), lambda b, j: (b, j, 0)),
                 pl.BlockSpec(state_blk, lambda b, j: (b, 0, 0, 0)),
                 pl.BlockSpec(state_blk, lambda b, j: (b, 0, 0, 0)),
                 pl.BlockSpec((ns, CONV_W - 1, CONV_DIM), lambda b, j: (b, 0, 0))]
    out_shape = [jax.ShapeDtypeStruct((BATCH, SEQ, D_MODEL), BF16),
                 jax.ShapeDtypeStruct((BATCH,) + state_blk[1:], F32),
                 jax.ShapeDtypeStruct((BATCH,) + state_blk[1:], F32),
                 jax.ShapeDtypeStruct((BATCH, CONV_W - 1, CONV_DIM), F32)]
    mixed, hg, ssm_t, conv = pl.pallas_call(
        _scan_prompt_body,
        grid=(BATCH // ns, nt),
        in_specs=in_specs, out_specs=out_specs, out_shape=out_shape,
        scratch_shapes=[pltpu.VMEM(state_blk, F32), pltpu.VMEM(state_blk, F32),
                        pltpu.VMEM((ns, tl + 8, CONV_DIM), F32),
                        pltpu.VMEM((5, HG_CHUNK, HG_W), F32)],
        compiler_params=pltpu.CompilerParams(dimension_semantics=("arbitrary", "arbitrary"),
                                             vmem_limit_bytes=VMEM_LIMIT),
        name="scan_prompt",
    )(*([proj] * ns), prm["la"], prm["lc"], prm["omlb"], prm["hgn"], prm["cw"], prm["cb"],
      prm["dtb"], prm["an"], prm["dsk"], prm["ssn"])
    return mixed.reshape(T_PROMPT, D_MODEL), hg, ssm_t, conv


def _scan_sample_body(n_prev, p_ref, shg_ref, sssm_ref, sconv_ref, la_ref, lc_ref, omlb_ref,
                      hgn_ref, cw_ref, cb_ref, dtb_ref, an_ref, dsk_ref, ssn_ref, *refs):
    if n_prev:
        phg_ref, pssm_ref = refs[:2]
        refs = refs[2:]
    mix_ref, ohg_ref, ossm_ref, oconv_ref, xpad_ref, ex_ref = refs
    if n_prev:
        ohg_ref[0:n_prev] = phg_ref[...]
        ossm_ref[0:n_prev] = pssm_ref[...]
    nb, sl_len = SAMPLE_SEQS, DEC_SEQ
    rows = nb * sl_len
    causal, ref, same = _seq_masks(rows, sl_len)
    rowseq = _iota2((rows, LANES), 0) // sl_len

    q, logf, kin = _hgrn_gates(p_ref[:, C_Q:C_Q + HG_W], p_ref[:, C_F:C_F + HG_W],
                               la_ref[...], lc_ref[...], omlb_ref[...])
    v = p_ref[:, C_I:C_I + HG_W]
    g, gmid, glast = _hgrn_decays(logf, causal, ref, same)
    safe = jnp.max(jnp.abs(g - gmid)) <= HG_SAFE_RANGE

    @pl.when(safe)
    def _():
        qt = q * jnp.exp(g - gmid)
        kt = kin * jnp.exp(gmid - g)
        for h in range(HG_HEADS):
            sl = slice(h * HG_D, (h + 1) * HG_D)
            sc = jnp.where(causal, _dot_nt(qt[:, sl], kt[:, sl]), 0.0)
            ex_ref[4, :, sl] = _dot(sc, v[:, sl])

    @pl.when(jnp.logical_not(safe))
    def _():
        _hgrn_exact_intra(ex_ref, q, kin, g, v, sl_len)

    qh = q * jnp.exp(g)
    kd = kin * jnp.exp(glast - g)
    ds_t = jnp.exp(glast).T
    kd_t = kd.T
    for h in range(HG_HEADS):
        sl = slice(h * HG_D, (h + 1) * HG_D)
        o_intra = ex_ref[4, :, sl]
        o_inter = []
        for b in range(nb):
            s_old = shg_ref[b, h]
            o_inter.append(_dot(qh[b * sl_len:(b + 1) * sl_len, sl], s_old))
            vb = jnp.where(rowseq == b, v[:, sl], 0.0)
            dcol = _lane_bcast(ds_t[sl, :], b * sl_len, HG_D)
            ohg_ref[n_prev, b, h] = dcol * s_old + _dot(kd_t[sl, :], vb)
        o = o_intra + jnp.concatenate(o_inter, axis=0)
        og = p_ref[:, C_OG + h * HG_D:C_OG + (h + 1) * HG_D]
        mix_ref[:, sl] = _hgrn_head_out(o, hgn_ref[:, sl], og).astype(BF16)

    convs = []
    for b in range(nb):
        base = 16 * b
        xpad_ref[base + 5:base + 8, :] = sconv_ref[b]
        xpad_ref[base + 8:base + 16, :] = p_ref[b * sl_len:(b + 1) * sl_len, C_XBC:C_XBC + CONV_DIM]
        cv = cb_ref[...] + cw_ref[0:1, :] * xpad_ref[base + 5:base + 13, :]
        for tap in range(1, CONV_W):
            cv = cv + cw_ref[tap:tap + 1, :] * xpad_ref[base + 5 + tap:base + 13 + tap, :]
        convs.append(cv)
        oconv_ref[b] = xpad_ref[base + 13:base + 16, :]
    conv = jnp.concatenate(convs, axis=0)

    xs, bm, cm, dtf, a = _ssd_inputs(conv, p_ref[:, C_DT:C_DT + LANES], dtb_ref[...], an_ref[...])
    causf = causal.astype(F32)
    acum = _mask_dot(causf, a)
    a_t = a.T
    acum_t = _mask_dot_nt(a_t, causf)
    atot_t = _mask_dot_nt(a_t, same.astype(F32))
    wall_t = jnp.exp(atot_t - acum_t)
    eatot_t = jnp.exp(atot_t)
    eall = jnp.exp(acum)
    lo = _iota2((rows, LANES), 1) < SSD_P
    top = _iota2((LANES, rows), 0) < SSD_P
    top_sq = _iota2((LANES, SSD_N), 0) < SSD_P
    colseq = _iota2((LANES, rows), 1) // sl_len
    ys = []
    for pair in range(SSD_PAIRS):
        grp = pair // 2
        cg = cm[:, grp * SSD_N:(grp + 1) * SSD_N]
        bg = bm[:, grp * SSD_N:(grp + 1) * SSD_N]
        cb = _dot_nt(cg, bg)
        psl = slice(pair * LANES, (pair + 1) * LANES)
        r0, r1 = 2 * pair, 2 * pair + 1
        xp = xs[:, psl] * jnp.where(lo, _lane_bcast(dtf, r0, LANES), _lane_bcast(dtf, r1, LANES))
        xw_t = xp.T * jnp.where(top, wall_t[r0:r0 + 1, :], wall_t[r1:r1 + 1, :])
        intra = []
        for r in (r0, r1):
            col = _lane_bcast(acum, r, rows)
            row = jnp.broadcast_to(acum_t[r:r + 1, :], (rows, rows))
            dec = jnp.exp(jnp.where(causal, col - row, -jnp.inf))
            intra.append(_dot(cb * dec, xp))
        y_intra = jnp.where(lo, intra[0], intra[1])
        y_inter = []
        for b in range(nb):
            tr = slice(b * sl_len, (b + 1) * sl_len)
            h_old = sssm_ref[b, pair]
            y_inter.append(_dot_nt(cg[tr, :], h_old))
            upd = _dot(jnp.where(colseq == b, xw_t, 0.0), bg)
            c0 = b * sl_len
            ea = jnp.where(top_sq, eatot_t[r0:r0 + 1, c0:c0 + 1], eatot_t[r1:r1 + 1, c0:c0 + 1])
            ossm_ref[n_prev, b, pair] = ea * h_old + upd
        e_pair = jnp.where(lo, _lane_bcast(eall, r0, LANES), _lane_bcast(eall, r1, LANES))
        ys.append(y_intra + e_pair * jnp.concatenate(y_inter, axis=0))
    y = jnp.concatenate(ys, axis=1)
    y = _ssd_finish(y, xs, p_ref[:, C_Z:C_Z + SSD_W], dsk_ref[...], ssn_ref[...])
    mix_ref[:, HG_W:] = y.astype(BF16)


def _scan_sample_call(l, proj, s_hg, s_ssm_t, s_conv, prm, prev):
    nb = SAMPLE_SEQS
    rows = nb * DEC_SEQ
    row0 = T_PROMPT // rows
    c0 = lambda i: (0, 0)
    rs = lambda w: pl.BlockSpec((1, w), c0)
    state_blk = (nb, HG_HEADS, HG_D, HG_D)
    in_specs = [pl.BlockSpec((rows, PROJ_W), lambda i: (row0 + i, 0)),
                pl.BlockSpec((None,) + state_blk, lambda i: (l, i, 0, 0, 0)),
                pl.BlockSpec((None,) + state_blk, lambda i: (l, i, 0, 0, 0)),
                pl.BlockSpec((None, nb, CONV_W - 1, CONV_DIM), lambda i: (l, i, 0, 0)),
                rs(HG_W), rs(HG_W), rs(HG_W), rs(HG_W),
                pl.BlockSpec((CONV_W, CONV_DIM), c0), rs(CONV_DIM),
                rs(LANES), rs(LANES), rs(SSD_W), rs(SSD_W)]
    in_specs += [pl.BlockSpec((l,) + state_blk, lambda i: (0, i, 0, 0, 0))] * (2 if l else 0)
    out_specs = [pl.BlockSpec((rows, D_MODEL), lambda i: (i, 0)),
                 pl.BlockSpec((l + 1,) + state_blk, lambda i: (0, i, 0, 0, 0)),
                 pl.BlockSpec((l + 1,) + state_blk, lambda i: (0, i, 0, 0, 0)),
                 pl.BlockSpec((nb, CONV_W - 1, CONV_DIM), lambda i: (i, 0, 0))]
    stacked = jax.ShapeDtypeStruct((l + 1, DEC_BATCH) + state_blk[1:], F32)
    out_shape = [jax.ShapeDtypeStruct((T_SAMPLE, D_MODEL), BF16), stacked, stacked,
                 jax.ShapeDtypeStruct((DEC_BATCH, CONV_W - 1, CONV_DIM), F32)]
    return pl.pallas_call(
        functools.partial(_scan_sample_body, l),
        grid=(DEC_BATCH // nb,),
        in_specs=in_specs, out_specs=out_specs, out_shape=out_shape,
        scratch_shapes=[pltpu.VMEM((16 * nb, CONV_DIM), F32), pltpu.VMEM((5, rows, HG_W), F32)],
        compiler_params=pltpu.CompilerParams(dimension_semantics=("arbitrary",),
                                             vmem_limit_bytes=VMEM_LIMIT),
        name="scan_sample",
    )(proj, s_hg, s_ssm_t, s_conv, prm["la"], prm["lc"], prm["omlb"], prm["hgn"], prm["cw"],
      prm["cb"], prm["dtb"], prm["an"], prm["dsk"], prm["ssn"], *(prev if l else ()))


def _out_ffn_body(n_h, *refs):
    h_refs, mix_refs = refs[:n_h], refs[n_h:n_h + 2]
    wo_ref, g_ref, wg_ref, wu_ref, wd_ref, o_ref, hn_ref = refs[n_h + 2:]
    i = pl.program_id(0)
    f = pl.program_id(1)

    @pl.when(f == 0)
    def _():
        h1 = _rows(i, h_refs) + _dot(_rows(i, mix_refs), wo_ref[...])
        o_ref[...] = h1
        hn_ref[...] = _rms(h1, g_ref[...]).astype(BF16)

    o_ref[...] += _swiglu_part(hn_ref[...], wg_ref[...], wu_ref[...], wd_ref[...])


def _out_ffn_call(h, mixed, wo_bf, g, wi_bf, wd_bf):
    tm, tf = 512, D_FF // 2
    nf = D_FF // tf
    return pl.pallas_call(
        functools.partial(_out_ffn_body, len(h)),
        grid=(T_ALL // tm, nf),
        in_specs=_row_specs(h, tm) + _row_specs(mixed, tm) + [
                  pl.BlockSpec((D_MODEL, D_MODEL), lambda i, f: (0, 0)),
                  pl.BlockSpec((1, D_MODEL), lambda i, f: (0, 0)),
                  pl.BlockSpec((D_MODEL, tf), lambda i, f: (0, f)),
                  pl.BlockSpec((D_MODEL, tf), lambda i, f: (0, nf + f)),
                  pl.BlockSpec((tf, D_MODEL), lambda i, f: (f, 0))],
        out_specs=pl.BlockSpec((tm, D_MODEL), lambda i, f: (i, 0)),
        out_shape=jax.ShapeDtypeStruct((T_ALL, D_MODEL), F32),
        scratch_shapes=[pltpu.VMEM((tm, D_MODEL), BF16)],
        compiler_params=pltpu.CompilerParams(dimension_semantics=("arbitrary", "arbitrary"),
                                             vmem_limit_bytes=VMEM_LIMIT),
        name="out_ffn",
    )(*h, *mixed, wo_bf, g, wi_bf, wi_bf, wd_bf)


def _out_router_body(h_ref, mixp_ref, mixs_ref, wo_ref, g_ref, wr_ref,
                     h1_ref, hn_ref, route_ref, seg_ref):
    h1 = h_ref[...] + _dot(_rows(pl.program_id(0), (mixp_ref, mixs_ref)), wo_ref[...])
    h1_ref[...] = h1
    hn = _rms(h1, g_ref[...])
    hn_ref[...] = hn.astype(BF16)
    logits = _dot_f32x3(hn, wr_ref[...])
    lane = _iota2(logits.shape, 1)
    lg = jnp.where(lane < N_EXPERTS, logits, -jnp.inf)
    m1 = jnp.max(lg, axis=1, keepdims=True)
    i1 = jnp.min(jnp.where(lg == m1, lane, LANES), axis=1, keepdims=True)
    lg2 = jnp.where(lane == i1, -jnp.inf, lg)
    m2 = jnp.max(lg2, axis=1, keepdims=True)
    i2 = jnp.min(jnp.where(lg2 == m2, lane, LANES), axis=1, keepdims=True)
    e2 = jnp.exp(m2 - m1)
    g1 = 1.0 / (1.0 + e2)
    g2 = e2 / (1.0 + e2)

    tm = logits.shape[0]
    chosen = jnp.where((lane == i1) | (lane == i2), 1.0, 0.0)
    earlier = (_iota2((tm, tm), 1) < _iota2((tm, tm), 0)).astype(F32)
    rank = _dot(earlier, chosen)
    cnt = jnp.sum(chosen, axis=0, keepdims=True)
    seg = jnp.floor((cnt + (SUBLANES - 1)) * (1.0 / SUBLANES)) * SUBLANES
    below = (_iota2((LANES, LANES), 0) < _iota2((LANES, LANES), 1)).astype(F32)
    loc = _dot(jnp.broadcast_to(seg, (SUBLANES, LANES)), below)[0:1, :]
    pos = loc + rank
    p1 = jnp.sum(jnp.where(lane == i1, pos, 0.0), axis=1, keepdims=True)
    p2 = jnp.sum(jnp.where(lane == i2, pos, 0.0), axis=1, keepdims=True)
    route_ref[...] = jnp.where(lane == 0, p1, jnp.where(lane == 1, p2, jnp.where(
        lane == 2, g1, jnp.where(lane == 3, g2, 0.0))))
    seg_ref[0] = jnp.broadcast_to(seg, (SUBLANES, LANES)).astype(jnp.int32)


def _out_router_call(h, mixed, wo_bf, g, wr_pad):
    tm = MOE_TOKENS
    return pl.pallas_call(
        _out_router_body,
        grid=(MOE_TILES,),
        in_specs=[pl.BlockSpec((tm, D_MODEL), lambda i: (i, 0))] + _row_specs(mixed, tm) + [
                  pl.BlockSpec((D_MODEL, D_MODEL), lambda i: (0, 0)),
                  pl.BlockSpec((1, D_MODEL), lambda i: (0, 0)),
                  pl.BlockSpec((D_MODEL, LANES), lambda i: (0, 0))],
        out_specs=[pl.BlockSpec((tm, D_MODEL), lambda i: (i, 0)),
                   pl.BlockSpec((tm, D_MODEL), lambda i: (i, 0)),
                   pl.BlockSpec((tm, LANES), lambda i: (i, 0)),
                   pl.BlockSpec((1, SUBLANES, LANES), lambda i: (i, 0, 0))],
        out_shape=[jax.ShapeDtypeStruct((T_ALL, D_MODEL), F32),
                   jax.ShapeDtypeStruct((T_ALL, D_MODEL), BF16),
                   jax.ShapeDtypeStruct((T_ALL, LANES), F32),
                   jax.ShapeDtypeStruct((MOE_TILES, SUBLANES, LANES), jnp.int32)],
        compiler_params=pltpu.CompilerParams(dimension_semantics=("arbitrary",),
                                             vmem_limit_bytes=VMEM_LIMIT),
        name="out_router",
    )(h, *mixed, wo_bf, g, wr_pad)


def _row_tile_copy(tile_ref, hbm_ref, tile_row, hbm_row, sem, to_hbm):
    t = tile_ref.at[pl.ds(pl.multiple_of(tile_row, SUBLANES), SUBLANES), :]
    g = hbm_ref.at[pl.ds(pl.multiple_of(hbm_row, SUBLANES), SUBLANES), :]
    return pltpu.make_async_copy(t, g, sem) if to_hbm else pltpu.make_async_copy(g, t, sem)


def _seg_copies(i, n_ref, loc_ref, start_ref, tile_ref, hbm_ref, sem, to_hbm):
    for e in range(N_EXPERTS):
        k = i * N_EXPERTS + e
        lo = loc_ref[k]
        st = start_ref[k]

        def body(c, carry, lo=lo, st=st):
            _row_tile_copy(tile_ref, hbm_ref, lo + c * SUBLANES, st + c * SUBLANES, sem,
                           to_hbm).start()
            return carry

        lax.fori_loop(0, n_ref[k], body, 0)


def _seg_wait(i, n_ref, tile_ref, hbm_ref, sem, to_hbm):
    total = n_ref[i * N_EXPERTS]
    for e in range(1, N_EXPERTS):
        total = total + n_ref[i * N_EXPERTS + e]

    def body(c, carry):
        _row_tile_copy(tile_ref, hbm_ref, 0, 0, sem, to_hbm).wait()
        return carry

    lax.fori_loop(0, total, body, 0)


def _dispatch_body(n_ref, loc_ref, start_ref, zn_ref, zstart_ref, zb_ref, zbn_ref,
                   hn_ref, route_ref, xs_ref, stage_ref, zero_ref, sem):
    i = pl.program_id(0)
    last = pl.num_programs(0) - 1
    slot = i % 2

    @pl.when(i >= 2)
    def _():
        _seg_wait(i - 2, n_ref, stage_ref.at[slot], xs_ref, sem.at[slot], True)

    rt = route_ref[...].T
    s = _iota2((MOE_SLOTS, MOE_TOKENS), 0).astype(F32)
    perm = jnp.where((s == rt[0:1, :]) | (s == rt[1:2, :]), 1.0, 0.0)
    stage_ref[slot] = _dot(perm, hn_ref[...])
    _seg_copies(i, n_ref, loc_ref, start_ref, stage_ref.at[slot], xs_ref, sem.at[slot], True)

    @pl.when(i == last)
    def _():
        zero_ref[...] = jnp.zeros_like(zero_ref)

        def blk_copy(b):
            return pltpu.make_async_copy(
                zero_ref, xs_ref.at[pl.ds(pl.multiple_of(b * GMM_SUB, GMM_SUB), GMM_SUB), :],
                sem.at[3])

        def bbody(b, carry):
            @pl.when(zb_ref[b] == 1)
            def _():
                blk_copy(b).start()
            return carry

        lax.fori_loop(0, GMM_TILES * GMM_ROWS // GMM_SUB, bbody, 0)
        nz = zn_ref[0]
        for e in range(N_EXPERTS):
            st = zstart_ref[e]

            def body(c, carry, st=st):
                _row_tile_copy(zero_ref, xs_ref, 0, st + c * SUBLANES, sem.at[2], True).start()
                return carry

            lax.fori_loop(0, zn_ref[e], body, 0)
            if e:
                nz = nz + zn_ref[e]

        @pl.when(last >= 1)
        def _():
            _seg_wait(i - 1, n_ref, stage_ref.at[1 - slot], xs_ref, sem.at[1 - slot], True)

        _seg_wait(i, n_ref, stage_ref.at[slot], xs_ref, sem.at[slot], True)

        def zbody(c, carry):
            _row_tile_copy(zero_ref, xs_ref, 0, 0, sem.at[2], True).wait()
            return carry

        lax.fori_loop(0, nz, zbody, 0)

        def bwait(c, carry):
            blk_copy(0).wait()
            return carry

        lax.fori_loop(0, zbn_ref[0], bwait, 0)


def _dispatch_call(sched, hn_bf, route):
    grid_spec = pltpu.PrefetchScalarGridSpec(
        num_scalar_prefetch=7,
        grid=(MOE_TILES,),
        in_specs=[pl.BlockSpec((MOE_TOKENS, D_MODEL), lambda i, *_: (i, 0)),
                  pl.BlockSpec((MOE_TOKENS, LANES), lambda i, *_: (i, 0))],
        out_specs=pl.BlockSpec(memory_space=pl.ANY),
        scratch_shapes=[pltpu.VMEM((2, MOE_SLOTS, D_MODEL), F32),
                        pltpu.VMEM((GMM_SUB, D_MODEL), F32),
                        pltpu.SemaphoreType.DMA((4,))])
    return pl.pallas_call(
        _dispatch_body,
        grid_spec=grid_spec,
        out_shape=jax.ShapeDtypeStruct((GMM_TILES * GMM_ROWS, D_MODEL), F32),
        compiler_params=pltpu.CompilerParams(dimension_semantics=("arbitrary",),
                                             vmem_limit_bytes=VMEM_LIMIT),
        name="moe_dispatch",
    )(sched["nch"], sched["loc"], sched["start"], sched["zn"], sched["zstart"], sched["zb"],
      sched["zbn"], hn_bf, route)


def _swiglu_part(x, wg, wu, wd):
    gate = jnp.dot(x, wg, preferred_element_type=F32)
    up = jnp.dot(x, wu, preferred_element_type=F32)
    return jnp.dot((_silu(gate) * up).astype(BF16), wd, preferred_element_type=F32)


def _experts_body(te_ref, tv_ref, nu_ref, x_ref, wg_ref, wu_ref, wd_ref, o_ref,
                  wgb_ref, wub_ref, wdb_ref):
    del te_ref, nu_ref
    i = pl.program_id(0)
    f = pl.program_id(1)
    nv = tv_ref[i]

    @pl.when(f == 0)
    def _():
        o_ref[...] = jnp.zeros_like(o_ref)

    @pl.when(nv == GMM_ROWS)
    def _():
        o_ref[...] += _swiglu_part(x_ref[...].astype(BF16), wg_ref[0].astype(BF16),
                                   wu_ref[0].astype(BF16), wd_ref[0].astype(BF16))

    @pl.when((nv > 0) & (nv < GMM_ROWS))
    def _():
        wgb_ref[...] = wg_ref[0].astype(BF16)
        wub_ref[...] = wu_ref[0].astype(BF16)
        wdb_ref[...] = wd_ref[0].astype(BF16)
        for sub in range(GMM_ROWS // GMM_SUB):
            rows = slice(sub * GMM_SUB, (sub + 1) * GMM_SUB)

            @pl.when(sub * GMM_SUB < nv)
            def _(rows=rows):
                o_ref[rows, :] += _swiglu_part(x_ref[rows, :].astype(BF16), wgb_ref[...],
                                               wub_ref[...], wdb_ref[...])


def _experts_call(sched, x_sorted, w_in_e, w_out_e):
    tf = 512
    nf = D_FF_EXPERT // tf

    def used(i, nu):
        return jnp.maximum(jnp.minimum(i, nu[0] - 1), 0)

    def fidx(i, f, nu):
        return jnp.where(i < nu[0], f, nf - 1)

    grid_spec = pltpu.PrefetchScalarGridSpec(
        num_scalar_prefetch=3,
        grid=(GMM_TILES, nf),
        in_specs=[pl.BlockSpec((GMM_ROWS, D_MODEL), lambda i, f, te, tv, nu: (used(i, nu), 0)),
                  pl.BlockSpec((1, D_MODEL, tf), lambda i, f, te, tv, nu: (te[i], 0, fidx(i, f, nu))),
                  pl.BlockSpec((1, D_MODEL, tf),
                               lambda i, f, te, tv, nu: (te[i], 0, nf + fidx(i, f, nu))),
                  pl.BlockSpec((1, tf, D_MODEL), lambda i, f, te, tv, nu: (te[i], fidx(i, f, nu), 0))],
        out_specs=pl.BlockSpec((GMM_ROWS, D_MODEL), lambda i, f, te, tv, nu: (i, 0)),
        scratch_shapes=[pltpu.VMEM((D_MODEL, tf), BF16), pltpu.VMEM((D_MODEL, tf), BF16),
                        pltpu.VMEM((tf, D_MODEL), BF16)])
    return pl.pallas_call(
        _experts_body,
        grid_spec=grid_spec,
        out_shape=jax.ShapeDtypeStruct((GMM_TILES * GMM_ROWS, D_MODEL), F32),
        compiler_params=pltpu.CompilerParams(dimension_semantics=("arbitrary", "arbitrary"),
                                             vmem_limit_bytes=VMEM_LIMIT),
        name="moe_experts",
    )(sched["te"], sched["tv"], sched["nu"], x_sorted, w_in_e, w_in_e, w_out_e)


def _combine_body(n_ref, loc_ref, start_ref, h1_ref, route_ref, gf_ref, ys_ref, op_ref, os_ref,
                  buf_ref, sem):
    i = pl.program_id(0)
    last = pl.num_programs(0) - 1
    slot = i % 2

    @pl.when(i == 0)
    def _():
        buf_ref[...] = jnp.zeros_like(buf_ref)
        _seg_copies(0, n_ref, loc_ref, start_ref, buf_ref.at[0], ys_ref, sem.at[0], False)

    @pl.when(i < last)
    def _():
        _seg_copies(i + 1, n_ref, loc_ref, start_ref, buf_ref.at[1 - slot], ys_ref,
                    sem.at[1 - slot], False)

    _seg_wait(i, n_ref, buf_ref.at[slot], ys_ref, sem.at[slot], False)

    route = route_ref[...]
    s = _iota2((MOE_TOKENS, MOE_SLOTS), 1).astype(F32)
    rows = buf_ref[slot]
    y1 = _dot(jnp.where(s == route[:, 0:1], 1.0, 0.0), rows)
    y2 = _dot(jnp.where(s == route[:, 1:2], 1.0, 0.0), rows)
    y = _rms(h1_ref[...] + route[:, 2:3] * y1 + route[:, 3:4] * y2, gf_ref[...])

    @pl.when(i < T_PROMPT // MOE_TOKENS)
    def _():
        op_ref[...] = y

    @pl.when(i >= T_PROMPT // MOE_TOKENS)
    def _():
        os_ref[...] = y


def _combine_call(sched, h1, route, gfin, y_sorted):
    npt = T_PROMPT // MOE_TOKENS
    grid_spec = pltpu.PrefetchScalarGridSpec(
        num_scalar_prefetch=3,
        grid=(MOE_TILES,),
        in_specs=[pl.BlockSpec((MOE_TOKENS, D_MODEL), lambda i, *_: (i, 0)),
                  pl.BlockSpec((MOE_TOKENS, LANES), lambda i, *_: (i, 0)),
                  pl.BlockSpec((1, D_MODEL), lambda i, *_: (0, 0)),
                  pl.BlockSpec(memory_space=pl.ANY)],
        out_specs=[pl.BlockSpec((MOE_TOKENS, D_MODEL), lambda i, *_: (jnp.minimum(i, npt - 1), 0)),
                   pl.BlockSpec((MOE_TOKENS, D_MODEL), lambda i, *_: (jnp.maximum(i - npt, 0), 0))],
        scratch_shapes=[pltpu.VMEM((2, MOE_SLOTS, D_MODEL), F32),
                        pltpu.SemaphoreType.DMA((2,))])
    return pl.pallas_call(
        _combine_body,
        grid_spec=grid_spec,
        out_shape=[jax.ShapeDtypeStruct((T_PROMPT, D_MODEL), F32),
                   jax.ShapeDtypeStruct((T_SAMPLE, D_MODEL), F32)],
        compiler_params=pltpu.CompilerParams(dimension_semantics=("arbitrary",),
                                             vmem_limit_bytes=VMEM_LIMIT),
        name="moe_combine",
    )(sched["nch"], sched["loc"], sched["start"], h1, route, gfin, y_sorted)


def _moe_schedule(seg):
    tot = jnp.sum(seg, axis=0)
    region = (tot + GMM_ROWS - 1) // GMM_ROWS * GMM_ROWS
    base = jnp.cumsum(region) - region
    start = base[None, :] + jnp.cumsum(seg, axis=0) - seg
    loc = jnp.cumsum(seg, axis=1) - seg
    ntile = region // GMM_ROWS
    cum = jnp.cumsum(ntile)
    nu = cum[-1]
    i = jnp.arange(GMM_TILES, dtype=jnp.int32)
    te = jnp.minimum(jnp.sum(i[:, None] >= cum[None, :], axis=1), N_EXPERTS - 1).astype(jnp.int32)
    tv = jnp.clip(tot[te] - (i - (cum - ntile)[te]) * GMM_ROWS, 0, GMM_ROWS)
    tv = jnp.where(i < nu, tv, 0)
    te = jnp.where(i < nu, te, te[jnp.maximum(nu - 1, 0)])
    bstart = jnp.arange(GMM_TILES * GMM_ROWS // GMM_SUB, dtype=jnp.int32)[:, None] * GMM_SUB
    used_end = (base + (tot + GMM_SUB - 1) // GMM_SUB * GMM_SUB)[None, :]
    zb = jnp.any((bstart >= used_end) & (bstart < (base + region)[None, :]), axis=1)
    zb = (zb | (bstart[:, 0] >= jnp.sum(region))).astype(jnp.int32)
    return dict(nch=(seg // SUBLANES).reshape(-1), loc=loc.reshape(-1), start=start.reshape(-1),
                zn=((-tot) % GMM_SUB) // SUBLANES, zstart=base + tot,
                zb=zb, zbn=jnp.sum(zb).reshape(1),
                te=te, tv=tv.astype(jnp.int32), nu=nu.reshape(1).astype(jnp.int32))


def _row(x, width=None):
    x = x.astype(F32).reshape(1, -1)
    if width is not None and x.shape[1] < width:
        x = jnp.pad(x, ((0, 0), (0, width - x.shape[1])))
    return x


def _layer_params(l, lb_p, conv_w, conv_b, a_log, dt_bias, d_skip, hg_norm, ssd_norm):
    lb = jnp.sum(lb_p[1:l + 1], axis=0)
    return dict(
        la=_row(jnp.log(lb)), lc=_row(jnp.log1p(-lb)), omlb=_row(1.0 - lb),
        hgn=_row(hg_norm[l]), cw=conv_w[l].astype(F32), cb=_row(conv_b[l]),
        dtb=_row(dt_bias[l], LANES), an=_row(-jnp.exp(a_log[l].astype(F32)), LANES),
        dsk=_row(jnp.repeat(d_skip[l].astype(F32), SSD_P)), ssn=_row(ssd_norm[l]))


def kernel(x_prompt, x_sample, state_hgrn, state_ssm, state_conv, norm_mix, w_in, conv_w, conv_b,
           a_log, dt_bias, d_skip, lb_param, hg_norm, ssd_norm, w_out, norm_ffn, w_ffn_in,
           w_ffn_out, w_router, w_exp_in, w_exp_out, norm_final):
    h = (x_prompt.reshape(T_PROMPT, D_MODEL), x_sample.reshape(T_SAMPLE, D_MODEL))
    lb_p = jax.nn.softmax(lb_param.astype(F32), axis=0)
    packed = (SSD_PAIRS, LANES, SSD_N)
    s_ssm_t = jnp.swapaxes(state_ssm, 3, 4).reshape((DEPTH, DEC_BATCH) + packed)
    outs = {k: [] for k in ("hg_p", "ssm_p", "conv_p", "conv_s")}
    sample_states = None
    for l in range(DEPTH):
        prm = _layer_params(l, lb_p, conv_w, conv_b, a_log, dt_bias, d_skip, hg_norm, ssd_norm)
        w_in_bf = jnp.pad(w_in[l], ((0, 0), (0, PROJ_W - w_in.shape[2]))).astype(BF16)
        proj = _proj_call(h, _row(norm_mix[l]), w_in_bf)
        mixed_p, hg_p, ssm_p, conv_p = _scan_prompt_call(proj, prm)
        mixed_s, hg_s, ssm_s, conv_s = _scan_sample_call(l, proj, state_hgrn, s_ssm_t, state_conv,
                                                         prm, sample_states)
        sample_states = (hg_s, ssm_s)
        mixed = (mixed_p, mixed_s)
        for k, val in zip(outs, (hg_p, ssm_p, conv_p, conv_s)):
            outs[k].append(val)
        wo_bf = w_out[l].astype(BF16)
        if l % 2 == 0:
            h = (_out_ffn_call(h, mixed, wo_bf, _row(norm_ffn[l]), w_ffn_in[l // 2].astype(BF16),
                               w_ffn_out[l // 2].astype(BF16)),)
        else:
            wr_pad = jnp.pad(w_router[l // 2].astype(F32), ((0, 0), (0, LANES - N_EXPERTS)))
            h1, hn_bf, route, seg = _out_router_call(h[0], mixed, wo_bf, _row(norm_ffn[l]), wr_pad)
            sched = _moe_schedule(seg[:, 0, :N_EXPERTS])
            x_sorted = _dispatch_call(sched, hn_bf, route)
            y_sorted = _experts_call(sched, x_sorted, w_exp_in[l // 2], w_exp_out[l // 2])
            h = _combine_call(sched, h1, route, _row(norm_final), y_sorted)
    y_prompt = h[0].reshape(BATCH, SEQ, D_MODEL)
    y_sample = h[1].reshape(DEC_BATCH, DEC_SEQ, D_MODEL)
    def unpack_ssm(s_t, batch):
        return jnp.swapaxes(s_t.reshape(DEPTH, batch, SSD_HEADS, SSD_P, SSD_N), 3, 4)

    return (y_prompt, y_sample, jnp.stack(outs["hg_p"]), unpack_ssm(jnp.stack(outs["ssm_p"]), BATCH),
            jnp.stack(outs["conv_p"]), sample_states[0], unpack_ssm(sample_states[1], DEC_BATCH),
            jnp.stack(outs["conv_s"]))
```

```python
import functools

import jax
import jax.numpy as jnp
import numpy as np
from jax import lax
from jax.experimental import pallas as pl
from jax.experimental.pallas import tpu as pltpu

F32 = jnp.float32
BF16 = jnp.bfloat16

D_MODEL = 1024
BATCH = 8
SEQ = 2048
DEPTH = 2
DEC_BATCH = 128
DEC_SEQ = 8
HG_HEADS = 4
HG_D = 128
HG_W = HG_HEADS * HG_D
SSD_HEADS = 8
SSD_P = 64
SSD_N = 128
SSD_W = SSD_HEADS * SSD_P
SSD_PAIRS = SSD_HEADS // 2
CONV_W = 4
CONV_DIM = 1024
D_FF = 2816
N_EXPERTS = 8
D_FF_EXPERT = 3584
EPS = 1e-6

LANES = 128
C_Q, C_F, C_I, C_OG, C_Z, C_XBC, C_DT = 0, 512, 1024, 1536, 2048, 2560, 3584
PROJ_W = C_DT + LANES
HG_CHUNK = 64
SSD_CHUNK = 128
SAMPLE_SEQS = 8
PROMPT_SEQS = 4
PROJ_PIECE = 256
TILE_STAGES = 2 * HG_HEADS + 1 + SSD_PAIRS
HG_SAFE_RANGE = 80.0

T_PROMPT = BATCH * SEQ
T_SAMPLE = DEC_BATCH * DEC_SEQ
T_ALL = T_PROMPT + T_SAMPLE

SUBLANES = 8
TOP_K = 2
MOE_TOKENS = 256
MOE_TILES = T_ALL // MOE_TOKENS
MOE_SLOTS = TOP_K * MOE_TOKENS + N_EXPERTS * SUBLANES
SEG_COPY_ROWS = (64, 32, 16, 8)
GMM_ROWS = 1024
GMM_SUB = 256
GMM_TILES = (TOP_K * T_ALL + MOE_TILES * N_EXPERTS * (SUBLANES - 1)
             + N_EXPERTS * (GMM_ROWS - 1)) // GMM_ROWS + 1

VMEM_LIMIT = 60000 * 1024


def _sigmoid(x):
    return 0.5 * jnp.tanh(0.5 * x) + 0.5


def _silu(x):
    return x * _sigmoid(x)


def _softplus(x):
    return jnp.maximum(x, 0.0) + jnp.log(1.0 + jnp.exp(-jnp.abs(x)))


def _rms(x, g):
    return x * lax.rsqrt(jnp.mean(x * x, axis=-1, keepdims=True) + EPS) * g


def _dot(a, b):
    return jnp.dot(a.astype(BF16), b.astype(BF16), preferred_element_type=F32)


def _dot_nt(a, b):
    return lax.dot_general(a.astype(BF16), b.astype(BF16), (((1,), (1,)), ((), ())),
                           preferred_element_type=F32)


def _split3(x):
    x1 = x.astype(BF16)
    r1 = x - x1.astype(F32)
    x2 = r1.astype(BF16)
    x3 = (r1 - x2.astype(F32)).astype(BF16)
    return x1, x2, x3


def _mask_dot(m, x):
    mb = m.astype(BF16)
    x1, x2, x3 = _split3(x)
    return (jnp.dot(mb, x1, preferred_element_type=F32) + jnp.dot(mb, x2, preferred_element_type=F32)
            + jnp.dot(mb, x3, preferred_element_type=F32))


def _mask_dot_nt(x, m):
    mb = m.astype(BF16)
    dn = (((1,), (1,)), ((), ()))
    x1, x2, x3 = _split3(x)
    return (lax.dot_general(x1, mb, dn, preferred_element_type=F32)
            + lax.dot_general(x2, mb, dn, preferred_element_type=F32)
            + lax.dot_general(x3, mb, dn, preferred_element_type=F32))


def _dot_f32x3(a, b):
    a1, a2, _ = _split3(a)
    b1, b2, _ = _split3(b)
    return (jnp.dot(a1, b1, preferred_element_type=F32) + jnp.dot(a1, b2, preferred_element_type=F32)
            + jnp.dot(a2, b1, preferred_element_type=F32))


def _iota2(shape, dim):
    return lax.broadcasted_iota(jnp.int32, shape, dim)


def _seq_masks(rows, seq_len):
    t = _iota2((rows, rows), 0)
    s = _iota2((rows, rows), 1)
    same = (t // seq_len) == (s // seq_len)
    causal = same & (s <= t)
    ref = same & ((s % seq_len) < seq_len // 2)
    return causal, ref, same


def _row_specs(arrs, tm):
    if len(arrs) == 1:
        return [pl.BlockSpec((tm, arrs[0].shape[1]), lambda i, *_: (i, 0))]
    npt = T_PROMPT // tm
    return [pl.BlockSpec((tm, arrs[0].shape[1]), lambda i, *_: (jnp.minimum(i, npt - 1), 0)),
            pl.BlockSpec((tm, arrs[1].shape[1]), lambda i, *_: (jnp.maximum(i - npt, 0), 0),
                         pipeline_mode=pl.Buffered(1))]


def _rows(i, refs):
    if len(refs) == 1:
        return refs[0][...]
    return jnp.where(i < T_PROMPT // refs[0].shape[0], refs[0][...], refs[1][...])


def _in_proj(h_tiles, g, w):
    hn = jnp.concatenate([_rms(x, g) for x in h_tiles], axis=0)
    return _dot(hn, w)


def _proj_body(h_ref, g_ref, w_ref, o_ref):
    o_ref[...] = _in_proj([h_ref[...]], g_ref[...], w_ref[...])


def _proj_sample_call(h, g, w_bf):
    tm = 512
    row0 = h.shape[0] // tm - T_SAMPLE // tm
    return pl.pallas_call(
        _proj_body,
        grid=(T_SAMPLE // tm,),
        in_specs=[pl.BlockSpec((tm, D_MODEL), lambda i: (row0 + i, 0)),
                  pl.BlockSpec((1, D_MODEL), lambda i: (0, 0)),
                  pl.BlockSpec((D_MODEL, PROJ_W), lambda i: (0, 0))],
        out_specs=pl.BlockSpec((tm, PROJ_W), lambda i: (i, 0)),
        out_shape=jax.ShapeDtypeStruct((T_SAMPLE, PROJ_W), F32),
        compiler_params=pltpu.CompilerParams(dimension_semantics=("arbitrary",),
                                             vmem_limit_bytes=VMEM_LIMIT),
        name="in_proj_sample",
    )(h, g, w_bf)


def _hgrn_gates(p_q, p_f, la, lc, omlb):
    q = _silu(p_q)
    e = jnp.exp(-jnp.abs(p_f))
    b = lc + jnp.minimum(p_f, 0.0) - jnp.log(1.0 + e)
    logf = jnp.maximum(la, b) + jnp.log(1.0 + jnp.exp(-jnp.abs(la - b)))
    r = 1.0 / (1.0 + e)
    kin = omlb * jnp.where(p_f >= 0.0, e * r, r)
    return q, logf, kin


def _hgrn_decays(logf, causal, ref, same):
    rows = logf.shape[0]
    if ref is None:
        g = _mask_dot(causal, logf)
        return g, g[rows // 2 - 1:rows // 2, :], g[rows - 1:rows, :]
    m = jnp.concatenate([causal.astype(F32), ref.astype(F32), same.astype(F32)], axis=0)
    g3 = _mask_dot(m, logf)
    return g3[:rows], g3[rows:2 * rows], g3[2 * rows:]


def _hgrn_exact_intra(ex_ref, q, kin, g, v, seq_len):
    rows = q.shape[0]
    ex_ref[0] = g
    ex_ref[1] = q
    ex_ref[2] = kin
    ex_ref[3] = v
    s_idx = _iota2((rows, HG_W), 0)

    def body(t, carry):
        gt = ex_ref[0, pl.ds(t, 1), :]
        qt = ex_ref[1, pl.ds(t, 1), :]
        live = (s_idx <= t) & (s_idx // seq_len == t // seq_len)
        w = jnp.where(live, qt * ex_ref[2] * jnp.exp(jnp.minimum(gt - ex_ref[0], 0.0)), 0.0)
        outs = []
        for h in range(HG_HEADS):
            sl = slice(h * HG_D, (h + 1) * HG_D)
            score = jnp.sum(w[:, sl], axis=1, keepdims=True)
            outs.append(jnp.sum(score * ex_ref[3, :, sl], axis=0, keepdims=True))
        ex_ref[4, pl.ds(t, 1), :] = jnp.concatenate(outs, axis=1)
        return carry

    lax.fori_loop(0, rows, body, 0)
    return ex_ref[4]


def _hgrn_head_out(o, hgn, og):
    return _rms(o, hgn) * _silu(og)


def _ssd_inputs(conv, p_dt, dtb, a_neg):
    conv = _silu(conv)
    xs = conv[:, :SSD_W]
    bm = conv[:, SSD_W:SSD_W + 2 * SSD_N]
    cm = conv[:, SSD_W + 2 * SSD_N:]
    dtf = _softplus(p_dt + dtb)
    a = dtf * a_neg
    return xs, bm, cm, dtf, a


def _lane_bcast(x, lane, width):
    return jnp.broadcast_to(x[:, lane:lane + 1], (x.shape[0], width))


def _ssd_finish(y, xs, z, dsk, ssn):
    y = (y + dsk * xs) * _silu(z)
    half = SSD_W // 2
    return jnp.concatenate([_rms(y[:, :half], ssn[:, :half]), _rms(y[:, half:], ssn[:, half:])],
                           axis=1)


def _scan_prompt_body(*refs):
    ns = PROMPT_SEQS
    hnext_refs, hcur_refs, refs = refs[:ns], refs[ns:2 * ns], refs[2 * ns:]
    (gin_ref, win_ref, la_ref, lc_ref, omlb_ref), params = refs[:5], refs[5:12]
    (mix_ref, ohg_ref, ossm_ref, oconv_ref, st_ref, hp_ref, xpad_ref, ex_ref, proj_ref,
     pnext_ref) = refs[12:]
    j = pl.program_id(1)
    tl = SSD_CHUNK
    step = pl.program_id(0) * pl.num_programs(1) + j

    @pl.when(j == 0)
    def _():
        st_ref[...] = jnp.zeros_like(st_ref)
        hp_ref[...] = jnp.zeros_like(hp_ref)
        xpad_ref[:, 0:8, :] = jnp.zeros((ns, 8, CONV_DIM), F32)

    @pl.when(step == 0)
    def _():
        proj_ref[...] = _in_proj([r[...] for r in hcur_refs], gin_ref[...], win_ref[...])

    p_refs = [proj_ref.at[s * tl:(s + 1) * tl, :] for s in range(ns)]

    gates, worst = [], None
    for s in range(ns):
        gates.append([])
        for c in range(tl // HG_CHUNK):
            rows = slice(c * HG_CHUNK, (c + 1) * HG_CHUNK)
            gt = _hgrn_gates(p_refs[s][rows, C_Q:C_Q + HG_W], p_refs[s][rows, C_F:C_F + HG_W],
                             la_ref[...], lc_ref[...], omlb_ref[...])
            gates[s].append(gt)
            half = HG_CHUNK // 2
            for part in (gt[1][:half], gt[1][half:]):
                tot = jnp.sum(part, axis=0, keepdims=True)
                worst = tot if worst is None else jnp.minimum(worst, tot)
    safe = jnp.min(worst) >= -HG_SAFE_RANGE

    for exact in (False, True):
        @pl.when(safe != exact)
        def _(exact=exact):
            hn = jnp.concatenate([_rms(r[...], gin_ref[...]) for r in hnext_refs],
                                 axis=0).astype(BF16)
            cuts = [min(c, PROJ_W) for c in range(0, PROJ_W + PROJ_PIECE, PROJ_PIECE)]
            pieces = list(zip(cuts[:-1], cuts[1:]))
            points = ns * TILE_STAGES
            seen = [0]

            def stage_done():
                k = seen[0]
                seen[0] += 1
                for c0, c1 in pieces[k * len(pieces) // points:(k + 1) * len(pieces) // points]:
                    pnext_ref[:, c0:c1] = jnp.dot(hn, win_ref[:, c0:c1], preferred_element_type=F32)

            for s in range(ns):
                _scan_prompt_tile(gates[s], exact, ex_ref, stage_done, p_refs[s], *params,
                                  mix_ref.at[s], st_ref.at[s], hp_ref.at[s], xpad_ref.at[s])
            assert seen[0] == points

    proj_ref[...] = pnext_ref[...]

    @pl.when(j == pl.num_programs(1) - 1)
    def _():
        for s in range(ns):
            oconv_ref[s] = xpad_ref[s, 5:8, :]
            for h in range(HG_HEADS):
                ohg_ref[s, h] = st_ref[s, h].T
        ossm_ref[...] = hp_ref[...]


def _scan_prompt_tile(gates, exact, ex_ref, stage_done, p_ref, hgn_ref, cw_ref, cb_ref, dtb_ref, an_ref,
                      dsk_ref, ssn_ref, mix_ref, st_ref, hp_ref, xpad_ref):
    tl = SSD_CHUNK
    causal, _, _ = _seq_masks(HG_CHUNK, HG_CHUNK)
    for c in range(tl // HG_CHUNK):
        r0 = c * HG_CHUNK
        rows = slice(r0, r0 + HG_CHUNK)
        q, logf, kin = gates[c]
        v = p_ref[rows, C_I:C_I + HG_W]
        g, gmid, glast = _hgrn_decays(logf, causal, None, None)
        if exact:
            qh = q * jnp.exp(g)
            kd = kin * jnp.exp(glast - g)
            o_intra = _hgrn_exact_intra(ex_ref, q, kin, g, v, HG_CHUNK)
        else:
            qt = q * jnp.exp(g - gmid)
            kt = kin * jnp.exp(gmid - g)
            qh = qt * jnp.exp(gmid)
            kd = kt * jnp.exp(glast - gmid)
        ds = jnp.exp(glast[0:1, :])
        for h in range(HG_HEADS):
            sl = slice(h * HG_D, (h + 1) * HG_D)
            st = st_ref[h]
            if exact:
                o = o_intra[:, sl] + _dot_nt(qh[:, sl], st)
            else:
                sc = jnp.where(causal, _dot_nt(qt[:, sl], kt[:, sl]), 0.0)
                o = _dot(sc, v[:, sl]) + _dot_nt(qh[:, sl], st)
            st_ref[h] = st * ds[:, sl] + _dot(v[:, sl].T, kd[:, sl])
            og = p_ref[rows, C_OG + h * HG_D:C_OG + (h + 1) * HG_D]
            mix_ref[rows, sl] = _hgrn_head_out(o, hgn_ref[:, sl], og).astype(BF16)
            stage_done()

    xpad_ref[8:8 + tl, :] = p_ref[:, C_XBC:C_XBC + CONV_DIM]
    conv = cb_ref[...] + cw_ref[0:1, :] * xpad_ref[5:5 + tl, :]
    for tap in range(1, CONV_W):
        conv = conv + cw_ref[tap:tap + 1, :] * xpad_ref[5 + tap:5 + tap + tl, :]
    xpad_ref[0:8, :] = xpad_ref[tl:tl + 8, :]
    stage_done()

    xs, bm, cm, dtf, a = _ssd_inputs(conv, p_ref[:, C_DT:C_DT + LANES], dtb_ref[...], an_ref[...])
    tri, _, _ = _seq_masks(tl, tl)
    trif = tri.astype(F32)
    acum = _mask_dot(trif, a)
    acum_t = _mask_dot_nt(a.T, trif)
    lo = _iota2((tl, LANES), 1) < SSD_P
    top = _iota2((LANES, tl), 0) < SSD_P
    ys = []
    for pair in range(SSD_PAIRS):
        grp = pair // 2
        cg = cm[:, grp * SSD_N:(grp + 1) * SSD_N]
        bg = bm[:, grp * SSD_N:(grp + 1) * SSD_N]
        cb = _dot_nt(cg, bg)
        psl = slice(pair * LANES, (pair + 1) * LANES)
        r0, r1 = 2 * pair, 2 * pair + 1
        xp = xs[:, psl] * jnp.where(lo, _lane_bcast(dtf, r0, LANES), _lane_bcast(dtf, r1, LANES))
        intra, einter, wrow, ea = [], [], [], []
        for r in (r0, r1):
            col = _lane_bcast(acum, r, tl)
            row = jnp.broadcast_to(acum_t[r:r + 1, :], (tl, tl))
            dec = jnp.exp(jnp.where(tri, col - row, -jnp.inf))
            intra.append(_dot(cb * dec, xp))
            einter.append(jnp.exp(col))
            alast = acum_t[r:r + 1, tl - 1:tl]
            wrow.append(jnp.exp(alast - acum_t[r:r + 1, :]))
            ea.append(jnp.exp(alast))
        hp = hp_ref[pair]
        y = jnp.where(lo, intra[0], intra[1]) + jnp.where(lo, einter[0], einter[1]) * _dot_nt(cg, hp)
        xw_t = xp.T * jnp.where(top, wrow[0], wrow[1])
        hp_ref[pair] = jnp.where(top, ea[0], ea[1]) * hp + _dot(xw_t, bg)
        ys.append(y)
        stage_done()
    y = jnp.concatenate(ys, axis=1)
    y = _ssd_finish(y, xs, p_ref[:, C_Z:C_Z + SSD_W], dsk_ref[...], ssn_ref[...])
    mix_ref[:, HG_W:] = y.astype(BF16)


def _row_spec(width):
    return pl.BlockSpec((1, width), lambda b, j: (0, 0))


def _scan_prompt_call(h, g_in, w_in_bf, prm):
    tl = SSD_CHUNK
    nt = SEQ // tl
    ns = PROMPT_SEQS
    steps = BATCH // ns * nt
    state_blk = (ns, HG_HEADS, HG_D, HG_D)

    def tile_of(step, s):
        return ((step // nt) * ns + s) * nt + step % nt

    in_specs = [pl.BlockSpec((tl, D_MODEL),
                             lambda b, j, s=s: (tile_of(jnp.minimum(b * nt + j + 1, steps - 1), s), 0))
                for s in range(ns)]
    in_specs += [pl.BlockSpec((tl, D_MODEL), lambda b, j, s=s: (tile_of(b * nt + j, s), 0))
                 for s in range(ns)]
    in_specs += [_row_spec(D_MODEL), pl.BlockSpec((D_MODEL, PROJ_W), lambda b, j: (0, 0))]
    in_specs += [_row_spec(HG_W), _row_spec(HG_W), _row_spec(HG_W), _row_spec(HG_W),
                 pl.BlockSpec((CONV_W, CONV_DIM), lambda b, j: (0, 0)), _row_spec(CONV_DIM),
                 _row_spec(LANES), _row_spec(LANES), _row_spec(SSD_W), _row_spec(SSD_W)]
    out_specs = [pl.BlockSpec((ns, tl, D_MODEL), lambda b, j: (b, j, 0)),
                 pl.BlockSpec(state_blk, lambda b, j: (b, 0, 0, 0)),
                 pl.BlockSpec(state_blk, lambda b, j: (b, 0, 0, 0)),
                 pl.BlockSpec((ns, CONV_W - 1, CONV_DIM), lambda b, j: (b, 0, 0))]
    out_shape = [jax.ShapeDtypeStruct((BATCH, SEQ, D_MODEL), BF16),
                 jax.ShapeDtypeStruct((BATCH,) + state_blk[1:], F32),
                 jax.ShapeDtypeStruct((BATCH,) + state_blk[1:], F32),
                 jax.ShapeDtypeStruct((BATCH, CONV_W - 1, CONV_DIM), F32)]
    mixed, hg, ssm_t, conv = pl.pallas_call(
        _scan_prompt_body,
        grid=(BATCH // ns, nt),
        in_specs=in_specs, out_specs=out_specs, out_shape=out_shape,
        scratch_shapes=[pltpu.VMEM(state_blk, F32), pltpu.VMEM(state_blk, F32),
                        pltpu.VMEM((ns, tl + 8, CONV_DIM), F32),
                        pltpu.VMEM((5, HG_CHUNK, HG_W), F32),
                        pltpu.VMEM((ns * tl, PROJ_W), F32), pltpu.VMEM((ns * tl, PROJ_W), F32)],
        compiler_params=pltpu.CompilerParams(dimension_semantics=("arbitrary", "arbitrary"),
                                             vmem_limit_bytes=VMEM_LIMIT),
        name="scan_prompt",
    )(*([h] * (2 * ns)), g_in, w_in_bf, prm["la"], prm["lc"], prm["omlb"], prm["hgn"], prm["cw"],
      prm["cb"], prm["dtb"], prm["an"], prm["dsk"], prm["ssn"])
    return mixed.reshape(T_PROMPT, D_MODEL), hg, ssm_t, conv


def _scan_sample_body(n_prev, p_ref, shg_ref, sssm_ref, sconv_ref, la_ref, lc_ref, omlb_ref,
                      hgn_ref, cw_ref, cb_ref, dtb_ref, an_ref, dsk_ref, ssn_ref, *refs):
    if n_prev:
        phg_ref, pssm_ref = refs[:2]
        refs = refs[2:]
    mix_ref, ohg_ref, ossm_ref, oconv_ref, xpad_ref, ex_ref = refs
    if n_prev:
        ohg_ref[0:n_prev] = phg_ref[...]
        ossm_ref[0:n_prev] = pssm_ref[...]
    nb, sl_len = SAMPLE_SEQS, DEC_SEQ
    rows = nb * sl_len
    causal, ref, same = _seq_masks(rows, sl_len)
    rowseq = _iota2((rows, LANES), 0) // sl_len

    q, logf, kin = _hgrn_gates(p_ref[:, C_Q:C_Q + HG_W], p_ref[:, C_F:C_F + HG_W],
                               la_ref[...], lc_ref[...], omlb_ref[...])
    v = p_ref[:, C_I:C_I + HG_W]
    g, gmid, glast = _hgrn_decays(logf, causal, ref, same)
    safe = jnp.max(jnp.abs(g - gmid)) <= HG_SAFE_RANGE

    @pl.when(safe)
    def _():
        qt = q * jnp.exp(g - gmid)
        kt = kin * jnp.exp(gmid - g)
        for h in range(HG_HEADS):
            sl = slice(h * HG_D, (h + 1) * HG_D)
            sc = jnp.where(causal, _dot_nt(qt[:, sl], kt[:, sl]), 0.0)
            ex_ref[4, :, sl] = _dot(sc, v[:, sl])

    @pl.when(jnp.logical_not(safe))
    def _():
        _hgrn_exact_intra(ex_ref, q, kin, g, v, sl_len)

    qh = q * jnp.exp(g)
    kd = kin * jnp.exp(glast - g)
    ds_t = jnp.exp(glast).T
    kd_t = kd.T
    for h in range(HG_HEADS):
        sl = slice(h * HG_D, (h + 1) * HG_D)
        o_intra = ex_ref[4, :, sl]
        o_inter = []
        for b in range(nb):
            s_old = shg_ref[b, h]
            o_inter.append(_dot(qh[b * sl_len:(b + 1) * sl_len, sl], s_old))
            vb = jnp.where(rowseq == b, v[:, sl], 0.0)
            dcol = _lane_bcast(ds_t[sl, :], b * sl_len, HG_D)
            ohg_ref[n_prev, b, h] = dcol * s_old + _dot(kd_t[sl, :], vb)
        o = o_intra + jnp.concatenate(o_inter, axis=0)
        og = p_ref[:, C_OG + h * HG_D:C_OG + (h + 1) * HG_D]
        mix_ref[:, sl] = _hgrn_head_out(o, hgn_ref[:, sl], og).astype(BF16)

    convs = []
    for b in range(nb):
        base = 16 * b
        xpad_ref[base + 5:base + 8, :] = sconv_ref[b]
        xpad_ref[base + 8:base + 16, :] = p_ref[b * sl_len:(b + 1) * sl_len, C_XBC:C_XBC + CONV_DIM]
        cv = cb_ref[...] + cw_ref[0:1, :] * xpad_ref[base + 5:base + 13, :]
        for tap in range(1, CONV_W):
            cv = cv + cw_ref[tap:tap + 1, :] * xpad_ref[base + 5 + tap:base + 13 + tap, :]
        convs.append(cv)
        oconv_ref[b] = xpad_ref[base + 13:base + 16, :]
    conv = jnp.concatenate(convs, axis=0)

    xs, bm, cm, dtf, a = _ssd_inputs(conv, p_ref[:, C_DT:C_DT + LANES], dtb_ref[...], an_ref[...])
    causf = causal.astype(F32)
    acum = _mask_dot(causf, a)
    a_t = a.T
    acum_t = _mask_dot_nt(a_t, causf)
    atot_t = _mask_dot_nt(a_t, same.astype(F32))
    wall_t = jnp.exp(atot_t - acum_t)
    eatot_t = jnp.exp(atot_t)
    eall = jnp.exp(acum)
    lo = _iota2((rows, LANES), 1) < SSD_P
    top = _iota2((LANES, rows), 0) < SSD_P
    top_sq = _iota2((LANES, SSD_N), 0) < SSD_P
    colseq = _iota2((LANES, rows), 1) // sl_len
    ys = []
    for pair in range(SSD_PAIRS):
        grp = pair // 2
        cg = cm[:, grp * SSD_N:(grp + 1) * SSD_N]
        bg = bm[:, grp * SSD_N:(grp + 1) * SSD_N]
        cb = _dot_nt(cg, bg)
        psl = slice(pair * LANES, (pair + 1) * LANES)
        r0, r1 = 2 * pair, 2 * pair + 1
        xp = xs[:, psl] * jnp.where(lo, _lane_bcast(dtf, r0, LANES), _lane_bcast(dtf, r1, LANES))
        xw_t = xp.T * jnp.where(top, wall_t[r0:r0 + 1, :], wall_t[r1:r1 + 1, :])
        intra = []
        for r in (r0, r1):
            col = _lane_bcast(acum, r, rows)
            row = jnp.broadcast_to(acum_t[r:r + 1, :], (rows, rows))
            dec = jnp.exp(jnp.where(causal, col - row, -jnp.inf))
            intra.append(_dot(cb * dec, xp))
        y_intra = jnp.where(lo, intra[0], intra[1])
        y_inter = []
        for b in range(nb):
            tr = slice(b * sl_len, (b + 1) * sl_len)
            h_old = sssm_ref[b, pair]
            y_inter.append(_dot_nt(cg[tr, :], h_old))
            upd = _dot(jnp.where(colseq == b, xw_t, 0.0), bg)
            c0 = b * sl_len
            ea = jnp.where(top_sq, eatot_t[r0:r0 + 1, c0:c0 + 1], eatot_t[r1:r1 + 1, c0:c0 + 1])
            ossm_ref[n_prev, b, pair] = ea * h_old + upd
        e_pair = jnp.where(lo, _lane_bcast(eall, r0, LANES), _lane_bcast(eall, r1, LANES))
        ys.append(y_intra + e_pair * jnp.concatenate(y_inter, axis=0))
    y = jnp.concatenate(ys, axis=1)
    y = _ssd_finish(y, xs, p_ref[:, C_Z:C_Z + SSD_W], dsk_ref[...], ssn_ref[...])
    mix_ref[:, HG_W:] = y.astype(BF16)


def _scan_sample_call(l, proj, s_hg, s_ssm_t, s_conv, prm, prev):
    nb = SAMPLE_SEQS
    rows = nb * DEC_SEQ
    c0 = lambda i: (0, 0)
    rs = lambda w: pl.BlockSpec((1, w), c0)
    state_blk = (nb, HG_HEADS, HG_D, HG_D)
    in_specs = [pl.BlockSpec((rows, PROJ_W), lambda i: (i, 0)),
                pl.BlockSpec((None,) + state_blk, lambda i: (l, i, 0, 0, 0)),
                pl.BlockSpec((None,) + state_blk, lambda i: (l, i, 0, 0, 0)),
                pl.BlockSpec((None, nb, CONV_W - 1, CONV_DIM), lambda i: (l, i, 0, 0)),
                rs(HG_W), rs(HG_W), rs(HG_W), rs(HG_W),
                pl.BlockSpec((CONV_W, CONV_DIM), c0), rs(CONV_DIM),
                rs(LANES), rs(LANES), rs(SSD_W), rs(SSD_W)]
    in_specs += [pl.BlockSpec((l,) + state_blk, lambda i: (0, i, 0, 0, 0))] * (2 if l else 0)
    out_specs = [pl.BlockSpec((rows, D_MODEL), lambda i: (i, 0)),
                 pl.BlockSpec((l + 1,) + state_blk, lambda i: (0, i, 0, 0, 0)),
                 pl.BlockSpec((l + 1,) + state_blk, lambda i: (0, i, 0, 0, 0)),
                 pl.BlockSpec((nb, CONV_W - 1, CONV_DIM), lambda i: (i, 0, 0))]
    stacked = jax.ShapeDtypeStruct((l + 1, DEC_BATCH) + state_blk[1:], F32)
    out_shape = [jax.ShapeDtypeStruct((T_SAMPLE, D_MODEL), BF16), stacked, stacked,
                 jax.ShapeDtypeStruct((DEC_BATCH, CONV_W - 1, CONV_DIM), F32)]
    return pl.pallas_call(
        functools.partial(_scan_sample_body, l),
        grid=(DEC_BATCH // nb,),
        in_specs=in_specs, out_specs=out_specs, out_shape=out_shape,
        scratch_shapes=[pltpu.VMEM((16 * nb, CONV_DIM), F32), pltpu.VMEM((5, rows, HG_W), F32)],
        compiler_params=pltpu.CompilerParams(dimension_semantics=("arbitrary",),
                                             vmem_limit_bytes=VMEM_LIMIT),
        name="scan_sample",
    )(proj, s_hg, s_ssm_t, s_conv, prm["la"], prm["lc"], prm["omlb"], prm["hgn"], prm["cw"],
      prm["cb"], prm["dtb"], prm["an"], prm["dsk"], prm["ssn"], *(prev if l else ()))


def _out_ffn_body(n_h, *refs):
    h_refs, mix_refs = refs[:n_h], refs[n_h:n_h + 2]
    wo_ref, g_ref, wg_ref, wu_ref, wd_ref, o_ref, hn_ref = refs[n_h + 2:]
    i = pl.program_id(0)
    f = pl.program_id(1)

    @pl.when(f == 0)
    def _():
        h1 = _rows(i, h_refs) + _dot(_rows(i, mix_refs), wo_ref[...])
        o_ref[...] = h1
        hn_ref[...] = _rms(h1, g_ref[...]).astype(BF16)

    o_ref[...] += _swiglu_part(hn_ref[...], wg_ref[...], wu_ref[...], wd_ref[...])


def _out_ffn_call(h, mixed, wo_bf, g, wi_bf, wd_bf):
    tm, tf = 1024, D_FF // 2
    nf = D_FF // tf
    return pl.pallas_call(
        functools.partial(_out_ffn_body, len(h)),
        grid=(T_ALL // tm, nf),
        in_specs=_row_specs(h, tm) + _row_specs(mixed, tm) + [
                  pl.BlockSpec((D_MODEL, D_MODEL), lambda i, f: (0, 0)),
                  pl.BlockSpec((1, D_MODEL), lambda i, f: (0, 0)),
                  pl.BlockSpec((D_MODEL, tf), lambda i, f: (0, f)),
                  pl.BlockSpec((D_MODEL, tf), lambda i, f: (0, nf + f)),
                  pl.BlockSpec((tf, D_MODEL), lambda i, f: (f, 0))],
        out_specs=pl.BlockSpec((tm, D_MODEL), lambda i, f: (i, 0)),
        out_shape=jax.ShapeDtypeStruct((T_ALL, D_MODEL), F32),
        scratch_shapes=[pltpu.VMEM((tm, D_MODEL), BF16)],
        compiler_params=pltpu.CompilerParams(dimension_semantics=("arbitrary", "arbitrary"),
                                             vmem_limit_bytes=VMEM_LIMIT),
        name="out_ffn",
    )(*h, *mixed, wo_bf, g, wi_bf, wi_bf, wd_bf)


def _out_router_body(h_ref, mixp_ref, mixs_ref, wo_ref, g_ref, wr_ref,
                     h1_ref, hn_ref, route_ref, seg_ref):
    h1 = h_ref[...] + _dot(_rows(pl.program_id(0), (mixp_ref, mixs_ref)), wo_ref[...])
    h1_ref[...] = h1
    hn = _rms(h1, g_ref[...])
    hn_ref[...] = hn.astype(BF16)
    logits = _dot_f32x3(hn, wr_ref[...])
    lane = _iota2(logits.shape, 1)
    lg = jnp.where(lane < N_EXPERTS, logits, -jnp.inf)
    m1 = jnp.max(lg, axis=1, keepdims=True)
    i1 = jnp.min(jnp.where(lg == m1, lane, LANES), axis=1, keepdims=True)
    lg2 = jnp.where(lane == i1, -jnp.inf, lg)
    m2 = jnp.max(lg2, axis=1, keepdims=True)
    i2 = jnp.min(jnp.where(lg2 == m2, lane, LANES), axis=1, keepdims=True)
    e2 = jnp.exp(m2 - m1)
    g1 = 1.0 / (1.0 + e2)
    g2 = e2 / (1.0 + e2)

    tm = logits.shape[0]
    chosen = jnp.where((lane == i1) | (lane == i2), 1.0, 0.0)
    earlier = (_iota2((tm, tm), 1) < _iota2((tm, tm), 0)).astype(F32)
    rank = _dot(earlier, chosen)
    cnt = jnp.sum(chosen, axis=0, keepdims=True)
    seg = jnp.floor((cnt + (SUBLANES - 1)) * (1.0 / SUBLANES)) * SUBLANES
    below = (_iota2((LANES, LANES), 0) < _iota2((LANES, LANES), 1)).astype(F32)
    loc = _dot(jnp.broadcast_to(seg, (SUBLANES, LANES)), below)[0:1, :]
    pos = loc + rank
    p1 = jnp.sum(jnp.where(lane == i1, pos, 0.0), axis=1, keepdims=True)
    p2 = jnp.sum(jnp.where(lane == i2, pos, 0.0), axis=1, keepdims=True)
    route_ref[...] = jnp.where(lane == 0, p1, jnp.where(lane == 1, p2, jnp.where(
        lane == 2, g1, jnp.where(lane == 3, g2, 0.0))))
    seg_ref[0] = jnp.broadcast_to(seg, (SUBLANES, LANES)).astype(jnp.int32)


def _out_router_call(h, mixed, wo_bf, g, wr_pad):
    tm = MOE_TOKENS
    return pl.pallas_call(
        _out_router_body,
        grid=(MOE_TILES,),
        in_specs=[pl.BlockSpec((tm, D_MODEL), lambda i: (i, 0))] + _row_specs(mixed, tm) + [
                  pl.BlockSpec((D_MODEL, D_MODEL), lambda i: (0, 0)),
                  pl.BlockSpec((1, D_MODEL), lambda i: (0, 0)),
                  pl.BlockSpec((D_MODEL, LANES), lambda i: (0, 0))],
        out_specs=[pl.BlockSpec((tm, D_MODEL), lambda i: (i, 0)),
                   pl.BlockSpec((tm, D_MODEL), lambda i: (i, 0)),
                   pl.BlockSpec((tm, LANES), lambda i: (i, 0)),
                   pl.BlockSpec((1, SUBLANES, LANES), lambda i: (i, 0, 0))],
        out_shape=[jax.ShapeDtypeStruct((T_ALL, D_MODEL), F32),
                   jax.ShapeDtypeStruct((T_ALL, D_MODEL), BF16),
                   jax.ShapeDtypeStruct((T_ALL, LANES), F32),
                   jax.ShapeDtypeStruct((MOE_TILES, SUBLANES, LANES), jnp.int32)],
        compiler_params=pltpu.CompilerParams(dimension_semantics=("arbitrary",),
                                             vmem_limit_bytes=VMEM_LIMIT),
        name="out_router",
    )(h, *mixed, wo_bf, g, wr_pad)


def _row_tile_copy(tile_ref, hbm_ref, tile_row, hbm_row, sem, to_hbm, rows=SUBLANES):
    t = tile_ref.at[pl.ds(pl.multiple_of(tile_row, SUBLANES), rows), :]
    g = hbm_ref.at[pl.ds(pl.multiple_of(hbm_row, SUBLANES), rows), :]
    return pltpu.make_async_copy(t, g, sem) if to_hbm else pltpu.make_async_copy(g, t, sem)


def _seg_plan(n):
    counts = [lax.shift_right_logical(n, 3)]
    for shift in (2, 1, 0):
        counts.append(lax.shift_right_logical(n, shift) & 1)
    return counts


def _seg_copies(i, n_ref, loc_ref, start_ref, tile_ref, hbm_ref, sem, to_hbm):
    for e in range(N_EXPERTS):
        k = i * N_EXPERTS + e
        lo = loc_ref[k]
        st = start_ref[k]
        counts = _seg_plan(n_ref[k])
        big = SEG_COPY_ROWS[0]

        def body(c, carry, lo=lo, st=st):
            _row_tile_copy(tile_ref, hbm_ref, lo + c * big, st + c * big, sem, to_hbm, big).start()
            return carry

        lax.fori_loop(0, counts[0], body, 0)
        done = counts[0] * big
        for rows, cnt in zip(SEG_COPY_ROWS[1:], counts[1:]):
            @pl.when(cnt == 1)
            def _(rows=rows, done=done, lo=lo, st=st):
                _row_tile_copy(tile_ref, hbm_ref, lo + done, st + done, sem, to_hbm, rows).start()

            done = done + cnt * rows


def _seg_wait(i, n_ref, tile_ref, hbm_ref, sem, to_hbm):
    totals = None
    for e in range(N_EXPERTS):
        counts = _seg_plan(n_ref[i * N_EXPERTS + e])
        totals = counts if totals is None else [a + b for a, b in zip(totals, counts)]
    for rows, total in zip(SEG_COPY_ROWS, totals):
        def body(c, carry, rows=rows):
            _row_tile_copy(tile_ref, hbm_ref, 0, 0, sem, to_hbm, rows).wait()
            return carry

        lax.fori_loop(0, total, body, 0)


def _dispatch_body(n_ref, loc_ref, start_ref, zn_ref, zstart_ref, zb_ref, zbn_ref,
                   hn_ref, route_ref, xs_ref, stage_ref, zero_ref, sem):
    i = pl.program_id(0)
    last = pl.num_programs(0) - 1
    slot = i % 2

    @pl.when(i >= 2)
    def _():
        _seg_wait(i - 2, n_ref, stage_ref.at[slot], xs_ref, sem.at[slot], True)

    rt = route_ref[...].T
    s = _iota2((MOE_SLOTS, MOE_TOKENS), 0).astype(F32)
    perm = jnp.where((s == rt[0:1, :]) | (s == rt[1:2, :]), 1.0, 0.0)
    stage_ref[slot] = _dot(perm, hn_ref[...])
    _seg_copies(i, n_ref, loc_ref, start_ref, stage_ref.at[slot], xs_ref, sem.at[slot], True)

    @pl.when(i == last)
    def _():
        zero_ref[...] = jnp.zeros_like(zero_ref)

        def blk_copy(b):
            return pltpu.make_async_copy(
                zero_ref, xs_ref.at[pl.ds(pl.multiple_of(b * GMM_SUB, GMM_SUB), GMM_SUB), :],
                sem.at[3])

        def bbody(b, carry):
            @pl.when(zb_ref[b] == 1)
            def _():
                blk_copy(b).start()
            return carry

        lax.fori_loop(0, GMM_TILES * GMM_ROWS // GMM_SUB, bbody, 0)
        nz = zn_ref[0]
        for e in range(N_EXPERTS):
            st = zstart_ref[e]

            def body(c, carry, st=st):
                _row_tile_copy(zero_ref, xs_ref, 0, st + c * SUBLANES, sem.at[2], True).start()
                return carry

            lax.fori_loop(0, zn_ref[e], body, 0)
            if e:
                nz = nz + zn_ref[e]

        @pl.when(last >= 1)
        def _():
            _seg_wait(i - 1, n_ref, stage_ref.at[1 - slot], xs_ref, sem.at[1 - slot], True)

        _seg_wait(i, n_ref, stage_ref.at[slot], xs_ref, sem.at[slot], True)

        def zbody(c, carry):
            _row_tile_copy(zero_ref, xs_ref, 0, 0, sem.at[2], True).wait()
            return carry

        lax.fori_loop(0, nz, zbody, 0)

        def bwait(c, carry):
            blk_copy(0).wait()
            return carry

        lax.fori_loop(0, zbn_ref[0], bwait, 0)


def _dispatch_call(sched, hn_bf, route):
    grid_spec = pltpu.PrefetchScalarGridSpec(
        num_scalar_prefetch=7,
        grid=(MOE_TILES,),
        in_specs=[pl.BlockSpec((MOE_TOKENS, D_MODEL), lambda i, *_: (i, 0)),
                  pl.BlockSpec((MOE_TOKENS, LANES), lambda i, *_: (i, 0))],
        out_specs=pl.BlockSpec(memory_space=pl.ANY),
        scratch_shapes=[pltpu.VMEM((2, MOE_SLOTS, D_MODEL), F32),
                        pltpu.VMEM((GMM_SUB, D_MODEL), F32),
                        pltpu.SemaphoreType.DMA((4,))])
    return pl.pallas_call(
        _dispatch_body,
        grid_spec=grid_spec,
        out_shape=jax.ShapeDtypeStruct((GMM_TILES * GMM_ROWS, D_MODEL), F32),
        compiler_params=pltpu.CompilerParams(dimension_semantics=("arbitrary",),
                                             vmem_limit_bytes=VMEM_LIMIT),
        name="moe_dispatch",
    )(sched["nch"], sched["loc"], sched["start"], sched["zn"], sched["zstart"], sched["zb"],
      sched["zbn"], hn_bf, route)


def _swiglu_part(x, wg, wu, wd):
    gate = jnp.dot(x, wg, preferred_element_type=F32)
    up = jnp.dot(x, wu, preferred_element_type=F32)
    return jnp.dot((_silu(gate) * up).astype(BF16), wd, preferred_element_type=F32)


def _experts_body(te_ref, tv_ref, nu_ref, x_ref, wg_ref, wu_ref, wd_ref, o_ref,
                  wgb_ref, wub_ref, wdb_ref):
    del te_ref, nu_ref
    i = pl.program_id(0)
    f = pl.program_id(1)
    nv = tv_ref[i]

    @pl.when(f == 0)
    def _():
        o_ref[...] = jnp.zeros_like(o_ref)

    @pl.when(nv == GMM_ROWS)
    def _():
        o_ref[...] += _swiglu_part(x_ref[...].astype(BF16), wg_ref[0].astype(BF16),
                                   wu_ref[0].astype(BF16), wd_ref[0].astype(BF16))

    @pl.when((nv > 0) & (nv < GMM_ROWS))
    def _():
        wgb_ref[...] = wg_ref[0].astype(BF16)
        wub_ref[...] = wu_ref[0].astype(BF16)
        wdb_ref[...] = wd_ref[0].astype(BF16)
        for sub in range(GMM_ROWS // GMM_SUB):
            rows = slice(sub * GMM_SUB, (sub + 1) * GMM_SUB)

            @pl.when(sub * GMM_SUB < nv)
            def _(rows=rows):
                o_ref[rows, :] += _swiglu_part(x_ref[rows, :].astype(BF16), wgb_ref[...],
                                               wub_ref[...], wdb_ref[...])


def _experts_call(sched, x_sorted, w_in_e, w_out_e):
    tf = 512
    nf = D_FF_EXPERT // tf

    def used(i, nu):
        return jnp.maximum(jnp.minimum(i, nu[0] - 1), 0)

    def fidx(i, f, nu):
        return jnp.where(i < nu[0], f, nf - 1)

    grid_spec = pltpu.PrefetchScalarGridSpec(
        num_scalar_prefetch=3,
        grid=(GMM_TILES, nf),
        in_specs=[pl.BlockSpec((GMM_ROWS, D_MODEL), lambda i, f, te, tv, nu: (used(i, nu), 0)),
                  pl.BlockSpec((1, D_MODEL, tf), lambda i, f, te, tv, nu: (te[i], 0, fidx(i, f, nu))),
                  pl.BlockSpec((1, D_MODEL, tf),
                               lambda i, f, te, tv, nu: (te[i], 0, nf + fidx(i, f, nu))),
                  pl.BlockSpec((1, tf, D_MODEL), lambda i, f, te, tv, nu: (te[i], fidx(i, f, nu), 0))],
        out_specs=pl.BlockSpec((GMM_ROWS, D_MODEL), lambda i, f, te, tv, nu: (i, 0)),
        scratch_shapes=[pltpu.VMEM((D_MODEL, tf), BF16), pltpu.VMEM((D_MODEL, tf), BF16),
                        pltpu.VMEM((tf, D_MODEL), BF16)])
    return pl.pallas_call(
        _experts_body,
        grid_spec=grid_spec,
        out_shape=jax.ShapeDtypeStruct((GMM_TILES * GMM_ROWS, D_MODEL), F32),
        compiler_params=pltpu.CompilerParams(dimension_semantics=("arbitrary", "arbitrary"),
                                             vmem_limit_bytes=VMEM_LIMIT),
        name="moe_experts",
    )(sched["te"], sched["tv"], sched["nu"], x_sorted, w_in_e, w_in_e, w_out_e)


def _combine_body(n_ref, loc_ref, start_ref, h1_ref, route_ref, gf_ref, ys_ref, op_ref, os_ref,
                  buf_ref, sem):
    i = pl.program_id(0)
    last = pl.num_programs(0) - 1
    slot = i % 2

    @pl.when(i == 0)
    def _():
        buf_ref[...] = jnp.zeros_like(buf_ref)
        _seg_copies(0, n_ref, loc_ref, start_ref, buf_ref.at[0], ys_ref, sem.at[0], False)

    @pl.when(i < last)
    def _():
        _seg_copies(i + 1, n_ref, loc_ref, start_ref, buf_ref.at[1 - slot], ys_ref,
                    sem.at[1 - slot], False)

    _seg_wait(i, n_ref, buf_ref.at[slot], ys_ref, sem.at[slot], False)

    route = route_ref[...]
    s = _iota2((MOE_TOKENS, MOE_SLOTS), 1).astype(F32)
    rows = buf_ref[slot]
    y1 = _dot(jnp.where(s == route[:, 0:1], 1.0, 0.0), rows)
    y2 = _dot(jnp.where(s == route[:, 1:2], 1.0, 0.0), rows)
    y = _rms(h1_ref[...] + route[:, 2:3] * y1 + route[:, 3:4] * y2, gf_ref[...])

    @pl.when(i < T_PROMPT // MOE_TOKENS)
    def _():
        op_ref[...] = y

    @pl.when(i >= T_PROMPT // MOE_TOKENS)
    def _():
        os_ref[...] = y


def _combine_call(sched, h1, route, gfin, y_sorted):
    npt = T_PROMPT // MOE_TOKENS
    grid_spec = pltpu.PrefetchScalarGridSpec(
        num_scalar_prefetch=3,
        grid=(MOE_TILES,),
        in_specs=[pl.BlockSpec((MOE_TOKENS, D_MODEL), lambda i, *_: (i, 0)),
                  pl.BlockSpec((MOE_TOKENS, LANES), lambda i, *_: (i, 0)),
                  pl.BlockSpec((1, D_MODEL), lambda i, *_: (0, 0)),
                  pl.BlockSpec(memory_space=pl.ANY)],
        out_specs=[pl.BlockSpec((MOE_TOKENS, D_MODEL), lambda i, *_: (jnp.minimum(i, npt - 1), 0)),
                   pl.BlockSpec((MOE_TOKENS, D_MODEL), lambda i, *_: (jnp.maximum(i - npt, 0), 0))],
        scratch_shapes=[pltpu.VMEM((2, MOE_SLOTS, D_MODEL), F32),
                        pltpu.SemaphoreType.DMA((2,))])
    return pl.pallas_call(
        _combine_body,
        grid_spec=grid_spec,
        out_shape=[jax.ShapeDtypeStruct((T_PROMPT, D_MODEL), F32),
                   jax.ShapeDtypeStruct((T_SAMPLE, D_MODEL), F32)],
        compiler_params=pltpu.CompilerParams(dimension_semantics=("arbitrary",),
                                             vmem_limit_bytes=VMEM_LIMIT),
        name="moe_combine",
    )(sched["nch"], sched["loc"], sched["start"], h1, route, gfin, y_sorted)


def _moe_schedule(seg):
    tot = jnp.sum(seg, axis=0)
    region = (tot + GMM_ROWS - 1) // GMM_ROWS * GMM_ROWS
    base = jnp.cumsum(region) - region
    start = base[None, :] + jnp.cumsum(seg, axis=0) - seg
    loc = jnp.cumsum(seg, axis=1) - seg
    ntile = region // GMM_ROWS
    cum = jnp.cumsum(ntile)
    nu = cum[-1]
    i = jnp.arange(GMM_TILES, dtype=jnp.int32)
    te = jnp.minimum(jnp.sum(i[:, None] >= cum[None, :], axis=1), N_EXPERTS - 1).astype(jnp.int32)
    tv = jnp.clip(tot[te] - (i - (cum - ntile)[te]) * GMM_ROWS, 0, GMM_ROWS)
    tv = jnp.where(i < nu, tv, 0)
    te = jnp.where(i < nu, te, te[jnp.maximum(nu - 1, 0)])
    bstart = jnp.arange(GMM_TILES * GMM_ROWS // GMM_SUB, dtype=jnp.int32)[:, None] * GMM_SUB
    used_end = (base + (tot + GMM_SUB - 1) // GMM_SUB * GMM_SUB)[None, :]
    zb = jnp.any((bstart >= used_end) & (bstart < (base + region)[None, :]), axis=1)
    zb = (zb | (bstart[:, 0] >= jnp.sum(region))).astype(jnp.int32)
    return dict(nch=(seg // SUBLANES).reshape(-1), loc=loc.reshape(-1), start=start.reshape(-1),
                zn=((-tot) % GMM_SUB) // SUBLANES, zstart=base + tot,
                zb=zb, zbn=jnp.sum(zb).reshape(1),
                te=te, tv=tv.astype(jnp.int32), nu=nu.reshape(1).astype(jnp.int32))


def _row(x, width=None):
    x = x.astype(F32).reshape(1, -1)
    if width is not None and x.shape[1] < width:
        x = jnp.pad(x, ((0, 0), (0, width - x.shape[1])))
    return x


def _layer_params(l, lb_p, conv_w, conv_b, a_log, dt_bias, d_skip, hg_norm, ssd_norm):
    lb = jnp.sum(lb_p[1:l + 1], axis=0)
    return dict(
        la=_row(jnp.log(lb)), lc=_row(jnp.log1p(-lb)), omlb=_row(1.0 - lb),
        hgn=_row(hg_norm[l]), cw=conv_w[l].astype(F32), cb=_row(conv_b[l]),
        dtb=_row(dt_bias[l], LANES), an=_row(-jnp.exp(a_log[l].astype(F32)), LANES),
        dsk=_row(jnp.repeat(d_skip[l].astype(F32), SSD_P)), ssn=_row(ssd_norm[l]))


def kernel(x_prompt, x_sample, state_hgrn, state_ssm, state_conv, norm_mix, w_in, conv_w, conv_b,
           a_log, dt_bias, d_skip, lb_param, hg_norm, ssd_norm, w_out, norm_ffn, w_ffn_in,
           w_ffn_out, w_router, w_exp_in, w_exp_out, norm_final):
    h = (x_prompt.reshape(T_PROMPT, D_MODEL), x_sample.reshape(T_SAMPLE, D_MODEL))
    lb_p = jax.nn.softmax(lb_param.astype(F32), axis=0)
    packed = (SSD_PAIRS, LANES, SSD_N)
    s_ssm_t = jnp.swapaxes(state_ssm, 3, 4).reshape((DEPTH, DEC_BATCH) + packed)
    outs = {k: [] for k in ("hg_p", "ssm_p", "conv_p", "conv_s")}
    sample_states = None
    for l in range(DEPTH):
        prm = _layer_params(l, lb_p, conv_w, conv_b, a_log, dt_bias, d_skip, hg_norm, ssd_norm)
        w_in_bf = jnp.pad(w_in[l], ((0, 0), (0, PROJ_W - w_in.shape[2]))).astype(BF16)
        g_in = _row(norm_mix[l])
        mixed_p, hg_p, ssm_p, conv_p = _scan_prompt_call(h[0], g_in, w_in_bf, prm)
        proj_s = _proj_sample_call(h[-1], g_in, w_in_bf)
        mixed_s, hg_s, ssm_s, conv_s = _scan_sample_call(l, proj_s, state_hgrn, s_ssm_t,
                                                         state_conv, prm, sample_states)
        sample_states = (hg_s, ssm_s)
        mixed = (mixed_p, mixed_s)
        for k, val in zip(outs, (hg_p, ssm_p, conv_p, conv_s)):
            outs[k].append(val)
        wo_bf = w_out[l].astype(BF16)
        if l % 2 == 0:
            h = (_out_ffn_call(h, mixed, wo_bf, _row(norm_ffn[l]), w_ffn_in[l // 2].astype(BF16),
                               w_ffn_out[l // 2].astype(BF16)),)
        else:
            wr_pad = jnp.pad(w_router[l // 2].astype(F32), ((0, 0), (0, LANES - N_EXPERTS)))
            h1, hn_bf, route, seg = _out_router_call(h[0], mixed, wo_bf, _row(norm_ffn[l]), wr_pad)
            sched = _moe_schedule(seg[:, 0, :N_EXPERTS])
            x_sorted = _dispatch_call(sched, hn_bf, route)
            y_sorted = _experts_call(sched, x_sorted, w_exp_in[l // 2], w_exp_out[l // 2])
            h = _combine_call(sched, h1, route, _row(norm_final), y_sorted)
    y_prompt = h[0].reshape(BATCH, SEQ, D_MODEL)
    y_sample = h[1].reshape(DEC_BATCH, DEC_SEQ, D_MODEL)
    def unpack_ssm(s_t, batch):
        return jnp.swapaxes(s_t.reshape(DEPTH, batch, SSD_HEADS, SSD_P, SSD_N), 3, 4)

    return (y_prompt, y_sample, jnp.stack(outs["hg_p"]), unpack_ssm(jnp.stack(outs["ssm_p"]), BATCH),
            jnp.stack(outs["conv_p"]), sample_states[0], unpack_ssm(sample_states[1], DEC_BATCH),
            jnp.stack(outs["conv_s"]))
```

```python
import functools

import jax
import jax.numpy as jnp
import numpy as np
from jax import lax
from jax.experimental import pallas as pl
from jax.experimental.pallas import tpu as pltpu

F32 = jnp.float32
BF16 = jnp.bfloat16

D_MODEL = 1024
BATCH = 8
SEQ = 2048
DEPTH = 2
DEC_BATCH = 128
DEC_SEQ = 8
HG_HEADS = 4
HG_D = 128
HG_W = HG_HEADS * HG_D
SSD_HEADS = 8
SSD_P = 64
SSD_N = 128
SSD_W = SSD_HEADS * SSD_P
SSD_PAIRS = SSD_HEADS // 2
CONV_W = 4
CONV_DIM = 1024
D_FF = 2816
N_EXPERTS = 8
D_FF_EXPERT = 3584
EPS = 1e-6

LANES = 128
C_Q, C_F, C_I, C_OG, C_Z, C_XBC, C_DT = 0, 512, 1024, 1536, 2048, 2560, 3584
PROJ_W = C_DT + LANES
HG_CHUNK = 64
SSD_CHUNK = 128
SAMPLE_SEQS = 8
PROMPT_SEQS = 4
PROJ_SECTION = 256
HG_SAFE_RANGE = 80.0

T_PROMPT = BATCH * SEQ
T_SAMPLE = DEC_BATCH * DEC_SEQ
T_ALL = T_PROMPT + T_SAMPLE

SUBLANES = 8
TOP_K = 2
MOE_TOKENS = 256
MOE_TILES = T_ALL // MOE_TOKENS
MOE_SLOTS = TOP_K * MOE_TOKENS + N_EXPERTS * SUBLANES
SEG_COPY_ROWS = (64, 32, 16, 8)
GMM_ROWS = 1024
GMM_SUB = 256
GMM_TILES = (TOP_K * T_ALL + MOE_TILES * N_EXPERTS * (SUBLANES - 1)
             + N_EXPERTS * (GMM_ROWS - 1)) // GMM_ROWS + 1

VMEM_LIMIT = 60000 * 1024


def _sigmoid(x):
    return 0.5 * jnp.tanh(0.5 * x) + 0.5


def _silu(x):
    return x * _sigmoid(x)


def _softplus(x):
    return jnp.maximum(x, 0.0) + jnp.log(1.0 + jnp.exp(-jnp.abs(x)))


def _rms(x, g):
    return x * lax.rsqrt(jnp.mean(x * x, axis=-1, keepdims=True) + EPS) * g


def _dot(a, b):
    return jnp.dot(a.astype(BF16), b.astype(BF16), preferred_element_type=F32)


def _dot_nt(a, b):
    return lax.dot_general(a.astype(BF16), b.astype(BF16), (((1,), (1,)), ((), ())),
                           preferred_element_type=F32)


def _split3(x):
    x1 = x.astype(BF16)
    r1 = x - x1.astype(F32)
    x2 = r1.astype(BF16)
    x3 = (r1 - x2.astype(F32)).astype(BF16)
    return x1, x2, x3


def _mask_dot(m, x):
    mb = m.astype(BF16)
    x1, x2, x3 = _split3(x)
    return (jnp.dot(mb, x1, preferred_element_type=F32) + jnp.dot(mb, x2, preferred_element_type=F32)
            + jnp.dot(mb, x3, preferred_element_type=F32))


def _mask_dot_nt(x, m):
    mb = m.astype(BF16)
    dn = (((1,), (1,)), ((), ()))
    x1, x2, x3 = _split3(x)
    return (lax.dot_general(x1, mb, dn, preferred_element_type=F32)
            + lax.dot_general(x2, mb, dn, preferred_element_type=F32)
            + lax.dot_general(x3, mb, dn, preferred_element_type=F32))


def _dot_f32x3(a, b):
    a1, a2, _ = _split3(a)
    b1, b2, _ = _split3(b)
    return (jnp.dot(a1, b1, preferred_element_type=F32) + jnp.dot(a1, b2, preferred_element_type=F32)
            + jnp.dot(a2, b1, preferred_element_type=F32))


def _iota2(shape, dim):
    return lax.broadcasted_iota(jnp.int32, shape, dim)


def _seq_masks(rows, seq_len):
    t = _iota2((rows, rows), 0)
    s = _iota2((rows, rows), 1)
    same = (t // seq_len) == (s // seq_len)
    causal = same & (s <= t)
    ref = same & ((s % seq_len) < seq_len // 2)
    return causal, ref, same


def _row_specs(arrs, tm):
    if len(arrs) == 1:
        return [pl.BlockSpec((tm, arrs[0].shape[1]), lambda i, *_: (i, 0))]
    npt = T_PROMPT // tm
    return [pl.BlockSpec((tm, arrs[0].shape[1]), lambda i, *_: (jnp.minimum(i, npt - 1), 0)),
            pl.BlockSpec((tm, arrs[1].shape[1]), lambda i, *_: (jnp.maximum(i - npt, 0), 0),
                         pipeline_mode=pl.Buffered(1))]


def _rows(i, refs):
    if len(refs) == 1:
        return refs[0][...]
    return jnp.where(i < T_PROMPT // refs[0].shape[0], refs[0][...], refs[1][...])


def _log_f(p_f, la, lc):
    b = lc + jnp.minimum(p_f, 0.0) - jnp.log(1.0 + jnp.exp(-jnp.abs(p_f)))
    return jnp.maximum(la, b) + jnp.log(1.0 + jnp.exp(-jnp.abs(la - b)))


def _proj_body(n_h, *refs):
    h_refs, refs = refs[:n_h], refs[n_h:]
    g_ref, w_ref, la_ref, lc_ref, cw_ref, cb_ref, dtb_ref, o_ref, oconv_ref, xpad_ref = refs
    i = pl.program_id(0)
    tm = o_ref.shape[0]
    tiles_per_seq = SEQ // tm
    is_prompt = i < T_PROMPT // tm
    hn = _rms(_rows(i, h_refs), g_ref[...]).astype(BF16)

    @pl.when(i == 0)
    def _():
        xpad_ref[...] = jnp.zeros_like(xpad_ref)

    def section(c0, width):
        return jnp.dot(hn, w_ref[:, c0:c0 + width], preferred_element_type=F32)

    sw = PROJ_SECTION
    for k in range(CONV_DIM // sw):
        c0 = C_XBC + k * sw
        cols = slice(k * sw, (k + 1) * sw)
        xbc = section(c0, sw)
        xpad_ref[0:8, cols] = jnp.where(i % tiles_per_seq == 0, 0.0, xpad_ref[tm:tm + 8, cols])
        xpad_ref[8:8 + tm, cols] = xbc
        conv = cb_ref[:, cols] + cw_ref[0:1, cols] * xpad_ref[5:5 + tm, cols]
        for tap in range(1, CONV_W):
            conv = conv + cw_ref[tap:tap + 1, cols] * xpad_ref[5 + tap:5 + tap + tm, cols]
        o_ref[:, c0:c0 + sw] = jnp.where(is_prompt, _silu(conv), xbc)
    for k in range(HG_W // sw):
        cols = slice(k * sw, (k + 1) * sw)
        for base in (C_Q, C_OG, C_Z):
            o_ref[:, base + k * sw:base + (k + 1) * sw] = _silu(section(base + k * sw, sw))
        o_ref[:, C_F + k * sw:C_F + (k + 1) * sw] = _log_f(section(C_F + k * sw, sw),
                                                            la_ref[:, cols], lc_ref[:, cols])
        o_ref[:, C_I + k * sw:C_I + (k + 1) * sw] = section(C_I + k * sw, sw)
    o_ref[:, C_DT:] = _softplus(section(C_DT, LANES) + dtb_ref[...])

    @pl.when(is_prompt & (i % tiles_per_seq == tiles_per_seq - 1))
    def _():
        oconv_ref[0] = xpad_ref[5 + tm:8 + tm, :]


def _proj_call(h, g, w_bf, prm):
    tm = 512
    tiles_per_seq = SEQ // tm
    last_prompt_tile = T_PROMPT // tm - 1
    row = lambda w: pl.BlockSpec((1, w), lambda i: (0, 0))
    return pl.pallas_call(
        functools.partial(_proj_body, len(h)),
        grid=(T_ALL // tm,),
        in_specs=_row_specs(h, tm) + [row(D_MODEL), pl.BlockSpec((D_MODEL, PROJ_W), lambda i: (0, 0)),
                                      row(HG_W), row(HG_W),
                                      pl.BlockSpec((CONV_W, CONV_DIM), lambda i: (0, 0)),
                                      row(CONV_DIM), row(LANES)],
        out_specs=[pl.BlockSpec((tm, PROJ_W), lambda i: (i, 0)),
                   pl.BlockSpec((1, CONV_W - 1, CONV_DIM),
                                lambda i: (jnp.minimum(i, last_prompt_tile) // tiles_per_seq, 0, 0))],
        out_shape=[jax.ShapeDtypeStruct((T_ALL, PROJ_W), F32),
                   jax.ShapeDtypeStruct((BATCH, CONV_W - 1, CONV_DIM), F32)],
        scratch_shapes=[pltpu.VMEM((tm + 8, CONV_DIM), F32)],
        compiler_params=pltpu.CompilerParams(dimension_semantics=("arbitrary",),
                                             vmem_limit_bytes=VMEM_LIMIT),
        name="in_proj",
    )(*h, g, w_bf, prm["la"], prm["lc"], prm["cw"], prm["cb"], prm["dtb"])


def _hgrn_decays(logf, causal, ref, same):
    rows = logf.shape[0]
    if ref is None:
        g = _mask_dot(causal, logf)
        return g, g[rows // 2 - 1:rows // 2, :], g[rows - 1:rows, :]
    m = jnp.concatenate([causal.astype(F32), ref.astype(F32), same.astype(F32)], axis=0)
    g3 = _mask_dot(m, logf)
    return g3[:rows], g3[rows:2 * rows], g3[2 * rows:]


def _hgrn_exact_intra(ex_ref, q, kin, g, v, seq_len):
    rows = q.shape[0]
    ex_ref[0] = g
    ex_ref[1] = q
    ex_ref[2] = kin
    ex_ref[3] = v
    s_idx = _iota2((rows, HG_W), 0)

    def body(t, carry):
        gt = ex_ref[0, pl.ds(t, 1), :]
        qt = ex_ref[1, pl.ds(t, 1), :]
        live = (s_idx <= t) & (s_idx // seq_len == t // seq_len)
        w = jnp.where(live, qt * ex_ref[2] * jnp.exp(jnp.minimum(gt - ex_ref[0], 0.0)), 0.0)
        outs = []
        for h in range(HG_HEADS):
            sl = slice(h * HG_D, (h + 1) * HG_D)
            score = jnp.sum(w[:, sl], axis=1, keepdims=True)
            outs.append(jnp.sum(score * ex_ref[3, :, sl], axis=0, keepdims=True))
        ex_ref[4, pl.ds(t, 1), :] = jnp.concatenate(outs, axis=1)
        return carry

    lax.fori_loop(0, rows, body, 0)
    return ex_ref[4]


def _hgrn_qk(p_ref, rows):
    logf = p_ref[rows, C_F:C_F + HG_W]
    return p_ref[rows, C_Q:C_Q + HG_W], logf, 1.0 - jnp.exp(logf)


def _hgrn_head_out(o, hgn, gate):
    return _rms(o, hgn) * gate


def _ssd_inputs(conv, dtf, a_neg):
    xs = conv[:, :SSD_W]
    bm = conv[:, SSD_W:SSD_W + 2 * SSD_N]
    cm = conv[:, SSD_W + 2 * SSD_N:]
    return xs, bm, cm, dtf * a_neg


def _lane_bcast(x, lane, width):
    return jnp.broadcast_to(x[:, lane:lane + 1], (x.shape[0], width))


def _ssd_finish(y, xs, gate, dsk, ssn):
    y = (y + dsk * xs) * gate
    half = SSD_W // 2
    return jnp.concatenate([_rms(y[:, :half], ssn[:, :half]), _rms(y[:, half:], ssn[:, half:])],
                           axis=1)


def _scan_prompt_body(*refs):
    ns = PROMPT_SEQS
    p_refs, params = refs[:ns], refs[ns:ns + 4]
    mix_ref, ohg_ref, ossm_ref, st_ref, hp_ref, ex_ref = refs[ns + 4:]
    j = pl.program_id(1)
    tl = SSD_CHUNK

    @pl.when(j == 0)
    def _():
        st_ref[...] = jnp.zeros_like(st_ref)
        hp_ref[...] = jnp.zeros_like(hp_ref)

    worst = None
    half = HG_CHUNK // 2
    for s in range(ns):
        for r0 in range(0, tl, half):
            tot = jnp.sum(p_refs[s][r0:r0 + half, C_F:C_F + HG_W], axis=0, keepdims=True)
            worst = tot if worst is None else jnp.minimum(worst, tot)
    safe = jnp.min(worst) >= -HG_SAFE_RANGE

    for exact in (False, True):
        @pl.when(safe != exact)
        def _(exact=exact):
            for s in range(ns):
                _scan_prompt_tile(exact, ex_ref, p_refs[s], *params, mix_ref.at[s], st_ref.at[s],
                                  hp_ref.at[s])

    @pl.when(j == pl.num_programs(1) - 1)
    def _():
        for s in range(ns):
            for h in range(HG_HEADS):
                ohg_ref[s, h] = st_ref[s, h].T
        ossm_ref[...] = hp_ref[...]


def _scan_prompt_tile(exact, ex_ref, p_ref, hgn_ref, an_ref, dsk_ref, ssn_ref, mix_ref, st_ref,
                      hp_ref):
    tl = SSD_CHUNK
    causal, _, _ = _seq_masks(HG_CHUNK, HG_CHUNK)
    for c in range(tl // HG_CHUNK):
        r0 = c * HG_CHUNK
        rows = slice(r0, r0 + HG_CHUNK)
        q, logf, kin = _hgrn_qk(p_ref, rows)
        v = p_ref[rows, C_I:C_I + HG_W]
        g, gmid, glast = _hgrn_decays(logf, causal, None, None)
        if exact:
            qh = q * jnp.exp(g)
            kd = kin * jnp.exp(glast - g)
            o_intra = _hgrn_exact_intra(ex_ref, q, kin, g, v, HG_CHUNK)
        else:
            qt = q * jnp.exp(g - gmid)
            kt = kin * jnp.exp(gmid - g)
            qh = qt * jnp.exp(gmid)
            kd = kt * jnp.exp(glast - gmid)
        ds = jnp.exp(glast[0:1, :])
        for h in range(HG_HEADS):
            sl = slice(h * HG_D, (h + 1) * HG_D)
            st = st_ref[h]
            if exact:
                o = o_intra[:, sl] + _dot_nt(qh[:, sl], st)
            else:
                sc = jnp.where(causal, _dot_nt(qt[:, sl], kt[:, sl]), 0.0)
                o = _dot(sc, v[:, sl]) + _dot_nt(qh[:, sl], st)
            st_ref[h] = st * ds[:, sl] + _dot(v[:, sl].T, kd[:, sl])
            gate = p_ref[rows, C_OG + h * HG_D:C_OG + (h + 1) * HG_D]
            mix_ref[rows, sl] = _hgrn_head_out(o, hgn_ref[:, sl], gate).astype(BF16)

    dtf = p_ref[:, C_DT:C_DT + LANES]
    xs, bm, cm, a = _ssd_inputs(p_ref[:, C_XBC:C_XBC + CONV_DIM], dtf, an_ref[...])
    tri, _, _ = _seq_masks(tl, tl)
    trif = tri.astype(F32)
    acum = _mask_dot(trif, a)
    acum_t = _mask_dot_nt(a.T, trif)
    lo = _iota2((tl, LANES), 1) < SSD_P
    top = _iota2((LANES, tl), 0) < SSD_P
    ys = []
    for pair in range(SSD_PAIRS):
        grp = pair // 2
        cg = cm[:, grp * SSD_N:(grp + 1) * SSD_N]
        bg = bm[:, grp * SSD_N:(grp + 1) * SSD_N]
        cb = _dot_nt(cg, bg)
        psl = slice(pair * LANES, (pair + 1) * LANES)
        r0, r1 = 2 * pair, 2 * pair + 1
        xp = xs[:, psl] * jnp.where(lo, _lane_bcast(dtf, r0, LANES), _lane_bcast(dtf, r1, LANES))
        intra, einter, wrow, ea = [], [], [], []
        for r in (r0, r1):
            col = _lane_bcast(acum, r, tl)
            row = jnp.broadcast_to(acum_t[r:r + 1, :], (tl, tl))
            dec = jnp.exp(jnp.where(tri, col - row, -jnp.inf))
            intra.append(_dot(cb * dec, xp))
            einter.append(jnp.exp(col))
            alast = acum_t[r:r + 1, tl - 1:tl]
            wrow.append(jnp.exp(alast - acum_t[r:r + 1, :]))
            ea.append(jnp.exp(alast))
        hp = hp_ref[pair]
        y = jnp.where(lo, intra[0], intra[1]) + jnp.where(lo, einter[0], einter[1]) * _dot_nt(cg, hp)
        xw_t = xp.T * jnp.where(top, wrow[0], wrow[1])
        hp_ref[pair] = jnp.where(top, ea[0], ea[1]) * hp + _dot(xw_t, bg)
        ys.append(y)
    y = jnp.concatenate(ys, axis=1)
    y = _ssd_finish(y, xs, p_ref[:, C_Z:C_Z + SSD_W], dsk_ref[...], ssn_ref[...])
    mix_ref[:, HG_W:] = y.astype(BF16)


def _row_spec(width):
    return pl.BlockSpec((1, width), lambda b, j: (0, 0))


def _scan_prompt_call(proj, prm):
    tl = SSD_CHUNK
    nt = SEQ // tl
    ns = PROMPT_SEQS
    state_blk = (ns, HG_HEADS, HG_D, HG_D)
    in_specs = [pl.BlockSpec((tl, PROJ_W), lambda b, j, s=s: ((b * ns + s) * nt + j, 0))
                for s in range(ns)]
    in_specs += [_row_spec(HG_W), _row_spec(LANES), _row_spec(SSD_W), _row_spec(SSD_W)]
    out_specs = [pl.BlockSpec((ns, tl, D_MODEL), lambda b, j: (b, j, 0)),
                 pl.BlockSpec(state_blk, lambda b, j: (b, 0, 0, 0)),
                 pl.BlockSpec(state_blk, lambda b, j: (b, 0, 0, 0))]
    out_shape = [jax.ShapeDtypeStruct((BATCH, SEQ, D_MODEL), BF16),
                 jax.ShapeDtypeStruct((BATCH,) + state_blk[1:], F32),
                 jax.ShapeDtypeStruct((BATCH,) + state_blk[1:], F32)]
    mixed, hg, ssm_t = pl.pallas_call(
        _scan_prompt_body,
        grid=(BATCH // ns, nt),
        in_specs=in_specs, out_specs=out_specs, out_shape=out_shape,
        scratch_shapes=[pltpu.VMEM(state_blk, F32), pltpu.VMEM(state_blk, F32),
                        pltpu.VMEM((5, HG_CHUNK, HG_W), F32)],
        compiler_params=pltpu.CompilerParams(dimension_semantics=("arbitrary", "arbitrary"),
                                             vmem_limit_bytes=VMEM_LIMIT),
        name="scan_prompt",
    )(*([proj] * ns), prm["hgn"], prm["an"], prm["dsk"], prm["ssn"])
    return mixed.reshape(T_PROMPT, D_MODEL), hg, ssm_t


def _scan_sample_body(n_prev, p_ref, shg_ref, sssm_ref, sconv_ref, hgn_ref, cw_ref, cb_ref,
                      an_ref, dsk_ref, ssn_ref, *refs):
    if n_prev:
        phg_ref, pssm_ref = refs[:2]
        refs = refs[2:]
    mix_ref, ohg_ref, ossm_ref, oconv_ref, xpad_ref, ex_ref = refs
    if n_prev:
        ohg_ref[0:n_prev] = phg_ref[...]
        ossm_ref[0:n_prev] = pssm_ref[...]
    nb, sl_len = SAMPLE_SEQS, DEC_SEQ
    rows = nb * sl_len
    causal, ref, same = _seq_masks(rows, sl_len)
    rowseq = _iota2((rows, LANES), 0) // sl_len

    q, logf, kin = _hgrn_qk(p_ref, slice(None))
    v = p_ref[:, C_I:C_I + HG_W]
    g, gmid, glast = _hgrn_decays(logf, causal, ref, same)
    safe = jnp.max(jnp.abs(g - gmid)) <= HG_SAFE_RANGE

    @pl.when(safe)
    def _():
        qt = q * jnp.exp(g - gmid)
        kt = kin * jnp.exp(gmid - g)
        for h in range(HG_HEADS):
            sl = slice(h * HG_D, (h + 1) * HG_D)
            sc = jnp.where(causal, _dot_nt(qt[:, sl], kt[:, sl]), 0.0)
            ex_ref[4, :, sl] = _dot(sc, v[:, sl])

    @pl.when(jnp.logical_not(safe))
    def _():
        _hgrn_exact_intra(ex_ref, q, kin, g, v, sl_len)

    qh = q * jnp.exp(g)
    kd = kin * jnp.exp(glast - g)
    ds_t = jnp.exp(glast).T
    kd_t = kd.T
    for h in range(HG_HEADS):
        sl = slice(h * HG_D, (h + 1) * HG_D)
        o_intra = ex_ref[4, :, sl]
        o_inter = []
        for b in range(nb):
            s_old = shg_ref[b, h]
            o_inter.append(_dot(qh[b * sl_len:(b + 1) * sl_len, sl], s_old))
            vb = jnp.where(rowseq == b, v[:, sl], 0.0)
            dcol = _lane_bcast(ds_t[sl, :], b * sl_len, HG_D)
            ohg_ref[n_prev, b, h] = dcol * s_old + _dot(kd_t[sl, :], vb)
        o = o_intra + jnp.concatenate(o_inter, axis=0)
        gate = p_ref[:, C_OG + h * HG_D:C_OG + (h + 1) * HG_D]
        mix_ref[:, sl] = _hgrn_head_out(o, hgn_ref[:, sl], gate).astype(BF16)

    convs = []
    for b in range(nb):
        base = 16 * b
        xpad_ref[base + 5:base + 8, :] = sconv_ref[b]
        xpad_ref[base + 8:base + 16, :] = p_ref[b * sl_len:(b + 1) * sl_len, C_XBC:C_XBC + CONV_DIM]
        cv = cb_ref[...] + cw_ref[0:1, :] * xpad_ref[base + 5:base + 13, :]
        for tap in range(1, CONV_W):
            cv = cv + cw_ref[tap:tap + 1, :] * xpad_ref[base + 5 + tap:base + 13 + tap, :]
        convs.append(cv)
        oconv_ref[b] = xpad_ref[base + 13:base + 16, :]
    conv = _silu(jnp.concatenate(convs, axis=0))

    dtf = p_ref[:, C_DT:C_DT + LANES]
    xs, bm, cm, a = _ssd_inputs(conv, dtf, an_ref[...])
    causf = causal.astype(F32)
    acum = _mask_dot(causf, a)
    a_t = a.T
    acum_t = _mask_dot_nt(a_t, causf)
    atot_t = _mask_dot_nt(a_t, same.astype(F32))
    wall_t = jnp.exp(atot_t - acum_t)
    eatot_t = jnp.exp(atot_t)
    eall = jnp.exp(acum)
    lo = _iota2((rows, LANES), 1) < SSD_P
    top = _iota2((LANES, rows), 0) < SSD_P
    top_sq = _iota2((LANES, SSD_N), 0) < SSD_P
    colseq = _iota2((LANES, rows), 1) // sl_len
    ys = []
    for pair in range(SSD_PAIRS):
        grp = pair // 2
        cg = cm[:, grp * SSD_N:(grp + 1) * SSD_N]
        bg = bm[:, grp * SSD_N:(grp + 1) * SSD_N]
        cb = _dot_nt(cg, bg)
        psl = slice(pair * LANES, (pair + 1) * LANES)
        r0, r1 = 2 * pair, 2 * pair + 1
        xp = xs[:, psl] * jnp.where(lo, _lane_bcast(dtf, r0, LANES), _lane_bcast(dtf, r1, LANES))
        xw_t = xp.T * jnp.where(top, wall_t[r0:r0 + 1, :], wall_t[r1:r1 + 1, :])
        intra = []
        for r in (r0, r1):
            col = _lane_bcast(acum, r, rows)
            row = jnp.broadcast_to(acum_t[r:r + 1, :], (rows, rows))
            dec = jnp.exp(jnp.where(causal, col - row, -jnp.inf))
            intra.append(_dot(cb * dec, xp))
        y_intra = jnp.where(lo, intra[0], intra[1])
        y_inter = []
        for b in range(nb):
            tr = slice(b * sl_len, (b + 1) * sl_len)
            h_old = sssm_ref[b, pair]
            y_inter.append(_dot_nt(cg[tr, :], h_old))
            upd = _dot(jnp.where(colseq == b, xw_t, 0.0), bg)
            c0 = b * sl_len
            ea = jnp.where(top_sq, eatot_t[r0:r0 + 1, c0:c0 + 1], eatot_t[r1:r1 + 1, c0:c0 + 1])
            ossm_ref[n_prev, b, pair] = ea * h_old + upd
        e_pair = jnp.where(lo, _lane_bcast(eall, r0, LANES), _lane_bcast(eall, r1, LANES))
        ys.append(y_intra + e_pair * jnp.concatenate(y_inter, axis=0))
    y = jnp.concatenate(ys, axis=1)
    y = _ssd_finish(y, xs, p_ref[:, C_Z:C_Z + SSD_W], dsk_ref[...], ssn_ref[...])
    mix_ref[:, HG_W:] = y.astype(BF16)


def _scan_sample_call(l, proj, s_hg, s_ssm_t, s_conv, prm, prev):
    nb = SAMPLE_SEQS
    rows = nb * DEC_SEQ
    row0 = T_PROMPT // rows
    c0 = lambda i: (0, 0)
    rs = lambda w: pl.BlockSpec((1, w), c0)
    state_blk = (nb, HG_HEADS, HG_D, HG_D)
    in_specs = [pl.BlockSpec((rows, PROJ_W), lambda i: (row0 + i, 0)),
                pl.BlockSpec((None,) + state_blk, lambda i: (l, i, 0, 0, 0)),
                pl.BlockSpec((None,) + state_blk, lambda i: (l, i, 0, 0, 0)),
                pl.BlockSpec((None, nb, CONV_W - 1, CONV_DIM), lambda i: (l, i, 0, 0)),
                rs(HG_W), pl.BlockSpec((CONV_W, CONV_DIM), c0), rs(CONV_DIM),
                rs(LANES), rs(SSD_W), rs(SSD_W)]
    in_specs += [pl.BlockSpec((l,) + state_blk, lambda i: (0, i, 0, 0, 0))] * (2 if l else 0)
    out_specs = [pl.BlockSpec((rows, D_MODEL), lambda i: (i, 0)),
                 pl.BlockSpec((l + 1,) + state_blk, lambda i: (0, i, 0, 0, 0)),
                 pl.BlockSpec((l + 1,) + state_blk, lambda i: (0, i, 0, 0, 0)),
                 pl.BlockSpec((nb, CONV_W - 1, CONV_DIM), lambda i: (i, 0, 0))]
    stacked = jax.ShapeDtypeStruct((l + 1, DEC_BATCH) + state_blk[1:], F32)
    out_shape = [jax.ShapeDtypeStruct((T_SAMPLE, D_MODEL), BF16), stacked, stacked,
                 jax.ShapeDtypeStruct((DEC_BATCH, CONV_W - 1, CONV_DIM), F32)]
    return pl.pallas_call(
        functools.partial(_scan_sample_body, l),
        grid=(DEC_BATCH // nb,),
        in_specs=in_specs, out_specs=out_specs, out_shape=out_shape,
        scratch_shapes=[pltpu.VMEM((16 * nb, CONV_DIM), F32), pltpu.VMEM((5, rows, HG_W), F32)],
        compiler_params=pltpu.CompilerParams(dimension_semantics=("arbitrary",),
                                             vmem_limit_bytes=VMEM_LIMIT),
        name="scan_sample",
    )(proj, s_hg, s_ssm_t, s_conv, prm["hgn"], prm["cw"], prm["cb"], prm["an"], prm["dsk"],
      prm["ssn"], *(prev if l else ()))


def _out_ffn_body(n_h, *refs):
    h_refs, mix_refs = refs[:n_h], refs[n_h:n_h + 2]
    wo_ref, g_ref, wg_ref, wu_ref, wd_ref, o_ref, hn_ref = refs[n_h + 2:]
    i = pl.program_id(0)
    f = pl.program_id(1)

    @pl.when(f == 0)
    def _():
        h1 = _rows(i, h_refs) + _dot(_rows(i, mix_refs), wo_ref[...])
        o_ref[...] = h1
        hn_ref[...] = _rms(h1, g_ref[...]).astype(BF16)

    o_ref[...] += _swiglu_part(hn_ref[...], wg_ref[...], wu_ref[...], wd_ref[...])


def _out_ffn_call(h, mixed, wo_bf, g, wi_bf, wd_bf):
    tm, tf = 1024, D_FF // 2
    nf = D_FF // tf
    return pl.pallas_call(
        functools.partial(_out_ffn_body, len(h)),
        grid=(T_ALL // tm, nf),
        in_specs=_row_specs(h, tm) + _row_specs(mixed, tm) + [
                  pl.BlockSpec((D_MODEL, D_MODEL), lambda i, f: (0, 0)),
                  pl.BlockSpec((1, D_MODEL), lambda i, f: (0, 0)),
                  pl.BlockSpec((D_MODEL, tf), lambda i, f: (0, f)),
                  pl.BlockSpec((D_MODEL, tf), lambda i, f: (0, nf + f)),
                  pl.BlockSpec((tf, D_MODEL), lambda i, f: (f, 0))],
        out_specs=pl.BlockSpec((tm, D_MODEL), lambda i, f: (i, 0)),
        out_shape=jax.ShapeDtypeStruct((T_ALL, D_MODEL), F32),
        scratch_shapes=[pltpu.VMEM((tm, D_MODEL), BF16)],
        compiler_params=pltpu.CompilerParams(dimension_semantics=("arbitrary", "arbitrary"),
                                             vmem_limit_bytes=VMEM_LIMIT),
        name="out_ffn",
    )(*h, *mixed, wo_bf, g, wi_bf, wi_bf, wd_bf)


def _out_router_body(h_ref, mixp_ref, mixs_ref, wo_ref, g_ref, wr_ref,
                     h1_ref, hn_ref, route_ref, seg_ref):
    h1 = h_ref[...] + _dot(_rows(pl.program_id(0), (mixp_ref, mixs_ref)), wo_ref[...])
    h1_ref[...] = h1
    hn = _rms(h1, g_ref[...])
    hn_ref[...] = hn.astype(BF16)
    logits = _dot_f32x3(hn, wr_ref[...])
    lane = _iota2(logits.shape, 1)
    lg = jnp.where(lane < N_EXPERTS, logits, -jnp.inf)
    m1 = jnp.max(lg, axis=1, keepdims=True)
    i1 = jnp.min(jnp.where(lg == m1, lane, LANES), axis=1, keepdims=True)
    lg2 = jnp.where(lane == i1, -jnp.inf, lg)
    m2 = jnp.max(lg2, axis=1, keepdims=True)
    i2 = jnp.min(jnp.where(lg2 == m2, lane, LANES), axis=1, keepdims=True)
    e2 = jnp.exp(m2 - m1)
    g1 = 1.0 / (1.0 + e2)
    g2 = e2 / (1.0 + e2)

    tm = logits.shape[0]
    chosen = jnp.where((lane == i1) | (lane == i2), 1.0, 0.0)
    earlier = (_iota2((tm, tm), 1) < _iota2((tm, tm), 0)).astype(F32)
    rank = _dot(earlier, chosen)
    cnt = jnp.sum(chosen, axis=0, keepdims=True)
    seg = jnp.floor((cnt + (SUBLANES - 1)) * (1.0 / SUBLANES)) * SUBLANES
    below = (_iota2((LANES, LANES), 0) < _iota2((LANES, LANES), 1)).astype(F32)
    loc = _dot(jnp.broadcast_to(seg, (SUBLANES, LANES)), below)[0:1, :]
    pos = loc + rank
    p1 = jnp.sum(jnp.where(lane == i1, pos, 0.0), axis=1, keepdims=True)
    p2 = jnp.sum(jnp.where(lane == i2, pos, 0.0), axis=1, keepdims=True)
    route_ref[...] = jnp.where(lane == 0, p1, jnp.where(lane == 1, p2, jnp.where(
        lane == 2, g1, jnp.where(lane == 3, g2, 0.0))))
    seg_ref[0] = jnp.broadcast_to(seg, (SUBLANES, LANES)).astype(jnp.int32)


def _out_router_call(h, mixed, wo_bf, g, wr_pad):
    tm = MOE_TOKENS
    return pl.pallas_call(
        _out_router_body,
        grid=(MOE_TILES,),
        in_specs=[pl.BlockSpec((tm, D_MODEL), lambda i: (i, 0))] + _row_specs(mixed, tm) + [
                  pl.BlockSpec((D_MODEL, D_MODEL), lambda i: (0, 0)),
                  pl.BlockSpec((1, D_MODEL), lambda i: (0, 0)),
                  pl.BlockSpec((D_MODEL, LANES), lambda i: (0, 0))],
        out_specs=[pl.BlockSpec((tm, D_MODEL), lambda i: (i, 0)),
                   pl.BlockSpec((tm, D_MODEL), lambda i: (i, 0)),
                   pl.BlockSpec((tm, LANES), lambda i: (i, 0)),
                   pl.BlockSpec((1, SUBLANES, LANES), lambda i: (i, 0, 0))],
        out_shape=[jax.ShapeDtypeStruct((T_ALL, D_MODEL), F32),
                   jax.ShapeDtypeStruct((T_ALL, D_MODEL), BF16),
                   jax.ShapeDtypeStruct((T_ALL, LANES), F32),
                   jax.ShapeDtypeStruct((MOE_TILES, SUBLANES, LANES), jnp.int32)],
        compiler_params=pltpu.CompilerParams(dimension_semantics=("arbitrary",),
                                             vmem_limit_bytes=VMEM_LIMIT),
        name="out_router",
    )(h, *mixed, wo_bf, g, wr_pad)


def _row_tile_copy(tile_ref, hbm_ref, tile_row, hbm_row, sem, to_hbm, rows=SUBLANES):
    t = tile_ref.at[pl.ds(pl.multiple_of(tile_row, SUBLANES), rows), :]
    g = hbm_ref.at[pl.ds(pl.multiple_of(hbm_row, SUBLANES), rows), :]
    return pltpu.make_async_copy(t, g, sem) if to_hbm else pltpu.make_async_copy(g, t, sem)


def _seg_plan(n):
    counts = [lax.shift_right_logical(n, 3)]
    for shift in (2, 1, 0):
        counts.append(lax.shift_right_logical(n, shift) & 1)
    return counts


def _seg_copies(i, n_ref, loc_ref, start_ref, tile_ref, hbm_ref, sem, to_hbm):
    for e in range(N_EXPERTS):
        k = i * N_EXPERTS + e
        lo = loc_ref[k]
        st = start_ref[k]
        counts = _seg_plan(n_ref[k])
        big = SEG_COPY_ROWS[0]

        def body(c, carry, lo=lo, st=st):
            _row_tile_copy(tile_ref, hbm_ref, lo + c * big, st + c * big, sem, to_hbm, big).start()
            return carry

        lax.fori_loop(0, counts[0], body, 0)
        done = counts[0] * big
        for rows, cnt in zip(SEG_COPY_ROWS[1:], counts[1:]):
            @pl.when(cnt == 1)
            def _(rows=rows, done=done, lo=lo, st=st):
                _row_tile_copy(tile_ref, hbm_ref, lo + done, st + done, sem, to_hbm, rows).start()

            done = done + cnt * rows


def _seg_wait(i, n_ref, tile_ref, hbm_ref, sem, to_hbm):
    totals = None
    for e in range(N_EXPERTS):
        counts = _seg_plan(n_ref[i * N_EXPERTS + e])
        totals = counts if totals is None else [a + b for a, b in zip(totals, counts)]
    for rows, total in zip(SEG_COPY_ROWS, totals):
        def body(c, carry, rows=rows):
            _row_tile_copy(tile_ref, hbm_ref, 0, 0, sem, to_hbm, rows).wait()
            return carry

        lax.fori_loop(0, total, body, 0)


def _dispatch_body(n_ref, loc_ref, start_ref, zn_ref, zstart_ref, zb_ref, zbn_ref,
                   hn_ref, route_ref, xs_ref, stage_ref, zero_ref, sem):
    i = pl.program_id(0)
    last = pl.num_programs(0) - 1
    slot = i % 2

    @pl.when(i >= 2)
    def _():
        _seg_wait(i - 2, n_ref, stage_ref.at[slot], xs_ref, sem.at[slot], True)

    rt = route_ref[...].T
    s = _iota2((MOE_SLOTS, MOE_TOKENS), 0).astype(F32)
    perm = jnp.where((s == rt[0:1, :]) | (s == rt[1:2, :]), 1.0, 0.0)
    stage_ref[slot] = _dot(perm, hn_ref[...])
    _seg_copies(i, n_ref, loc_ref, start_ref, stage_ref.at[slot], xs_ref, sem.at[slot], True)

    @pl.when(i == last)
    def _():
        zero_ref[...] = jnp.zeros_like(zero_ref)

        def blk_copy(b):
            return pltpu.make_async_copy(
                zero_ref, xs_ref.at[pl.ds(pl.multiple_of(b * GMM_SUB, GMM_SUB), GMM_SUB), :],
                sem.at[3])

        def bbody(b, carry):
            @pl.when(zb_ref[b] == 1)
            def _():
                blk_copy(b).start()
            return carry

        lax.fori_loop(0, GMM_TILES * GMM_ROWS // GMM_SUB, bbody, 0)
        nz = zn_ref[0]
        for e in range(N_EXPERTS):
            st = zstart_ref[e]

            def body(c, carry, st=st):
                _row_tile_copy(zero_ref, xs_ref, 0, st + c * SUBLANES, sem.at[2], True).start()
                return carry

            lax.fori_loop(0, zn_ref[e], body, 0)
            if e:
                nz = nz + zn_ref[e]

        @pl.when(last >= 1)
        def _():
            _seg_wait(i - 1, n_ref, stage_ref.at[1 - slot], xs_ref, sem.at[1 - slot], True)

        _seg_wait(i, n_ref, stage_ref.at[slot], xs_ref, sem.at[slot], True)

        def zbody(c, carry):
            _row_tile_copy(zero_ref, xs_ref, 0, 0, sem.at[2], True).wait()
            return carry

        lax.fori_loop(0, nz, zbody, 0)

        def bwait(c, carry):
            blk_copy(0).wait()
            return carry

        lax.fori_loop(0, zbn_ref[0], bwait, 0)


def _dispatch_call(sched, hn_bf, route):
    grid_spec = pltpu.PrefetchScalarGridSpec(
        num_scalar_prefetch=7,
        grid=(MOE_TILES,),
        in_specs=[pl.BlockSpec((MOE_TOKENS, D_MODEL), lambda i, *_: (i, 0)),
                  pl.BlockSpec((MOE_TOKENS, LANES), lambda i, *_: (i, 0))],
        out_specs=pl.BlockSpec(memory_space=pl.ANY),
        scratch_shapes=[pltpu.VMEM((2, MOE_SLOTS, D_MODEL), F32),
                        pltpu.VMEM((GMM_SUB, D_MODEL), F32),
                        pltpu.SemaphoreType.DMA((4,))])
    return pl.pallas_call(
        _dispatch_body,
        grid_spec=grid_spec,
        out_shape=jax.ShapeDtypeStruct((GMM_TILES * GMM_ROWS, D_MODEL), F32),
        compiler_params=pltpu.CompilerParams(dimension_semantics=("arbitrary",),
                                             vmem_limit_bytes=VMEM_LIMIT),
        name="moe_dispatch",
    )(sched["nch"], sched["loc"], sched["start"], sched["zn"], sched["zstart"], sched["zb"],
      sched["zbn"], hn_bf, route)


def _swiglu_part(x, wg, wu, wd):
    gate = jnp.dot(x, wg, preferred_element_type=F32)
    up = jnp.dot(x, wu, preferred_element_type=F32)
    return jnp.dot((_silu(gate) * up).astype(BF16), wd, preferred_element_type=F32)


def _experts_body(te_ref, tv_ref, nu_ref, x_ref, wg_ref, wu_ref, wd_ref, o_ref,
                  wgb_ref, wub_ref, wdb_ref):
    del te_ref, nu_ref
    i = pl.program_id(0)
    f = pl.program_id(1)
    nv = tv_ref[i]

    @pl.when(f == 0)
    def _():
        o_ref[...] = jnp.zeros_like(o_ref)

    @pl.when(nv == GMM_ROWS)
    def _():
        o_ref[...] += _swiglu_part(x_ref[...].astype(BF16), wg_ref[0].astype(BF16),
                                   wu_ref[0].astype(BF16), wd_ref[0].astype(BF16))

    @pl.when((nv > 0) & (nv < GMM_ROWS))
    def _():
        wgb_ref[...] = wg_ref[0].astype(BF16)
        wub_ref[...] = wu_ref[0].astype(BF16)
        wdb_ref[...] = wd_ref[0].astype(BF16)
        for sub in range(GMM_ROWS // GMM_SUB):
            rows = slice(sub * GMM_SUB, (sub + 1) * GMM_SUB)

            @pl.when(sub * GMM_SUB < nv)
            def _(rows=rows):
                o_ref[rows, :] += _swiglu_part(x_ref[rows, :].astype(BF16), wgb_ref[...],
                                               wub_ref[...], wdb_ref[...])


def _experts_call(sched, x_sorted, w_in_e, w_out_e):
    tf = 512
    nf = D_FF_EXPERT // tf

    def used(i, nu):
        return jnp.maximum(jnp.minimum(i, nu[0] - 1), 0)

    def fidx(i, f, nu):
        return jnp.where(i < nu[0], f, nf - 1)

    grid_spec = pltpu.PrefetchScalarGridSpec(
        num_scalar_prefetch=3,
        grid=(GMM_TILES, nf),
        in_specs=[pl.BlockSpec((GMM_ROWS, D_MODEL), lambda i, f, te, tv, nu: (used(i, nu), 0)),
                  pl.BlockSpec((1, D_MODEL, tf), lambda i, f, te, tv, nu: (te[i], 0, fidx(i, f, nu))),
                  pl.BlockSpec((1, D_MODEL, tf),
                               lambda i, f, te, tv, nu: (te[i], 0, nf + fidx(i, f, nu))),
                  pl.BlockSpec((1, tf, D_MODEL), lambda i, f, te, tv, nu: (te[i], fidx(i, f, nu), 0))],
        out_specs=pl.BlockSpec((GMM_ROWS, D_MODEL), lambda i, f, te, tv, nu: (i, 0)),
        scratch_shapes=[pltpu.VMEM((D_MODEL, tf), BF16), pltpu.VMEM((D_MODEL, tf), BF16),
                        pltpu.VMEM((tf, D_MODEL), BF16)])
    return pl.pallas_call(
        _experts_body,
        grid_spec=grid_spec,
        out_shape=jax.ShapeDtypeStruct((GMM_TILES * GMM_ROWS, D_MODEL), F32),
        compiler_params=pltpu.CompilerParams(dimension_semantics=("arbitrary", "arbitrary"),
                                             vmem_limit_bytes=VMEM_LIMIT),
        name="moe_experts",
    )(sched["te"], sched["tv"], sched["nu"], x_sorted, w_in_e, w_in_e, w_out_e)


def _combine_body(n_ref, loc_ref, start_ref, h1_ref, route_ref, gf_ref, ys_ref, op_ref, os_ref,
                  buf_ref, sem):
    i = pl.program_id(0)
    last = pl.num_programs(0) - 1
    slot = i % 2

    @pl.when(i == 0)
    def _():
        buf_ref[...] = jnp.zeros_like(buf_ref)
        _seg_copies(0, n_ref, loc_ref, start_ref, buf_ref.at[0], ys_ref, sem.at[0], False)

    @pl.when(i < last)
    def _():
        _seg_copies(i + 1, n_ref, loc_ref, start_ref, buf_ref.at[1 - slot], ys_ref,
                    sem.at[1 - slot], False)

    _seg_wait(i, n_ref, buf_ref.at[slot], ys_ref, sem.at[slot], False)

    route = route_ref[...]
    s = _iota2((MOE_TOKENS, MOE_SLOTS), 1).astype(F32)
    rows = buf_ref[slot]
    y1 = _dot(jnp.where(s == route[:, 0:1], 1.0, 0.0), rows)
    y2 = _dot(jnp.where(s == route[:, 1:2], 1.0, 0.0), rows)
    y = _rms(h1_ref[...] + route[:, 2:3] * y1 + route[:, 3:4] * y2, gf_ref[...])

    @pl.when(i < T_PROMPT // MOE_TOKENS)
    def _():
        op_ref[...] = y

    @pl.when(i >= T_PROMPT // MOE_TOKENS)
    def _():
        os_ref[...] = y


def _combine_call(sched, h1, route, gfin, y_sorted):
    npt = T_PROMPT // MOE_TOKENS
    grid_spec = pltpu.PrefetchScalarGridSpec(
        num_scalar_prefetch=3,
        grid=(MOE_TILES,),
        in_specs=[pl.BlockSpec((MOE_TOKENS, D_MODEL), lambda i, *_: (i, 0)),
                  pl.BlockSpec((MOE_TOKENS, LANES), lambda i, *_: (i, 0)),
                  pl.BlockSpec((1, D_MODEL), lambda i, *_: (0, 0)),
                  pl.BlockSpec(memory_space=pl.ANY)],
        out_specs=[pl.BlockSpec((MOE_TOKENS, D_MODEL), lambda i, *_: (jnp.minimum(i, npt - 1), 0)),
                   pl.BlockSpec((MOE_TOKENS, D_MODEL), lambda i, *_: (jnp.maximum(i - npt, 0), 0))],
        scratch_shapes=[pltpu.VMEM((2, MOE_SLOTS, D_MODEL), F32),
                        pltpu.SemaphoreType.DMA((2,))])
    return pl.pallas_call(
        _combine_body,
        grid_spec=grid_spec,
        out_shape=[jax.ShapeDtypeStruct((T_PROMPT, D_MODEL), F32),
                   jax.ShapeDtypeStruct((T_SAMPLE, D_MODEL), F32)],
        compiler_params=pltpu.CompilerParams(dimension_semantics=("arbitrary",),
                                             vmem_limit_bytes=VMEM_LIMIT),
        name="moe_combine",
    )(sched["nch"], sched["loc"], sched["start"], h1, route, gfin, y_sorted)


def _moe_schedule(seg):
    tot = jnp.sum(seg, axis=0)
    region = (tot + GMM_ROWS - 1) // GMM_ROWS * GMM_ROWS
    base = jnp.cumsum(region) - region
    start = base[None, :] + jnp.cumsum(seg, axis=0) - seg
    loc = jnp.cumsum(seg, axis=1) - seg
    ntile = region // GMM_ROWS
    cum = jnp.cumsum(ntile)
    nu = cum[-1]
    i = jnp.arange(GMM_TILES, dtype=jnp.int32)
    te = jnp.minimum(jnp.sum(i[:, None] >= cum[None, :], axis=1), N_EXPERTS - 1).astype(jnp.int32)
    tv = jnp.clip(tot[te] - (i - (cum - ntile)[te]) * GMM_ROWS, 0, GMM_ROWS)
    tv = jnp.where(i < nu, tv, 0)
    te = jnp.where(i < nu, te, te[jnp.maximum(nu - 1, 0)])
    bstart = jnp.arange(GMM_TILES * GMM_ROWS // GMM_SUB, dtype=jnp.int32)[:, None] * GMM_SUB
    used_end = (base + (tot + GMM_SUB - 1) // GMM_SUB * GMM_SUB)[None, :]
    zb = jnp.any((bstart >= used_end) & (bstart < (base + region)[None, :]), axis=1)
    zb = (zb | (bstart[:, 0] >= jnp.sum(region))).astype(jnp.int32)
    return dict(nch=(seg // SUBLANES).reshape(-1), loc=loc.reshape(-1), start=start.reshape(-1),
                zn=((-tot) % GMM_SUB) // SUBLANES, zstart=base + tot,
                zb=zb, zbn=jnp.sum(zb).reshape(1),
                te=te, tv=tv.astype(jnp.int32), nu=nu.reshape(1).astype(jnp.int32))


def _row(x, width=None):
    x = x.astype(F32).reshape(1, -1)
    if width is not None and x.shape[1] < width:
        x = jnp.pad(x, ((0, 0), (0, width - x.shape[1])))
    return x


def _layer_params(l, lb_p, conv_w, conv_b, a_log, dt_bias, d_skip, hg_norm, ssd_norm):
    lb = jnp.sum(lb_p[1:l + 1], axis=0)
    return dict(
        la=_row(jnp.log(lb)), lc=_row(jnp.log1p(-lb)),
        hgn=_row(hg_norm[l]), cw=conv_w[l].astype(F32), cb=_row(conv_b[l]),
        dtb=_row(dt_bias[l], LANES), an=_row(-jnp.exp(a_log[l].astype(F32)), LANES),
        dsk=_row(jnp.repeat(d_skip[l].astype(F32), SSD_P)), ssn=_row(ssd_norm[l]))


def kernel(x_prompt, x_sample, state_hgrn, state_ssm, state_conv, norm_mix, w_in, conv_w, conv_b,
           a_log, dt_bias, d_skip, lb_param, hg_norm, ssd_norm, w_out, norm_ffn, w_ffn_in,
           w_ffn_out, w_router, w_exp_in, w_exp_out, norm_final):
    h = (x_prompt.reshape(T_PROMPT, D_MODEL), x_sample.reshape(T_SAMPLE, D_MODEL))
    lb_p = jax.nn.softmax(lb_param.astype(F32), axis=0)
    packed = (SSD_PAIRS, LANES, SSD_N)
    s_ssm_t = jnp.swapaxes(state_ssm, 3, 4).reshape((DEPTH, DEC_BATCH) + packed)
    outs = {k: [] for k in ("hg_p", "ssm_p", "conv_p", "conv_s")}
    sample_states = None
    for l in range(DEPTH):
        prm = _layer_params(l, lb_p, conv_w, conv_b, a_log, dt_bias, d_skip, hg_norm, ssd_norm)
        w_in_bf = jnp.pad(w_in[l], ((0, 0), (0, PROJ_W - w_in.shape[2]))).astype(BF16)
        proj, conv_p = _proj_call(h, _row(norm_mix[l]), w_in_bf, prm)
        mixed_p, hg_p, ssm_p = _scan_prompt_call(proj, prm)
        mixed_s, hg_s, ssm_s, conv_s = _scan_sample_call(l, proj, state_hgrn, s_ssm_t, state_conv,
                                                         prm, sample_states)
        sample_states = (hg_s, ssm_s)
        mixed = (mixed_p, mixed_s)
        for k, val in zip(outs, (hg_p, ssm_p, conv_p, conv_s)):
            outs[k].append(val)
        wo_bf = w_out[l].astype(BF16)
        if l % 2 == 0:
            h = (_out_ffn_call(h, mixed, wo_bf, _row(norm_ffn[l]), w_ffn_in[l // 2].astype(BF16),
                               w_ffn_out[l // 2].astype(BF16)),)
        else:
            wr_pad = jnp.pad(w_router[l // 2].astype(F32), ((0, 0), (0, LANES - N_EXPERTS)))
            h1, hn_bf, route, seg = _out_router_call(h[0], mixed, wo_bf, _row(norm_ffn[l]), wr_pad)
            sched = _moe_schedule(seg[:, 0, :N_EXPERTS])
            x_sorted = _dispatch_call(sched, hn_bf, route)
            y_sorted = _experts_call(sched, x_sorted, w_exp_in[l // 2], w_exp_out[l // 2])
            h = _combine_call(sched, h1, route, _row(norm_final), y_sorted)
    y_prompt = h[0].reshape(BATCH, SEQ, D_MODEL)
    y_sample = h[1].reshape(DEC_BATCH, DEC_SEQ, D_MODEL)
    def unpack_ssm(s_t, batch):
        return jnp.swapaxes(s_t.reshape(DEPTH, batch, SSD_HEADS, SSD_P, SSD_N), 3, 4)

    return (y_prompt, y_sample, jnp.stack(outs["hg_p"]), unpack_ssm(jnp.stack(outs["ssm_p"]), BATCH),
            jnp.stack(outs["conv_p"]), sample_states[0], unpack_ssm(sample_states[1], DEC_BATCH),
            jnp.stack(outs["conv_s"]))
```

```python
import functools

import jax
import jax.numpy as jnp
import numpy as np
from jax import lax
from jax.experimental import pallas as pl
from jax.experimental.pallas import tpu as pltpu

F32 = jnp.float32
BF16 = jnp.bfloat16

D_MODEL = 1024
BATCH = 8
SEQ = 2048
DEPTH = 2
DEC_BATCH = 128
DEC_SEQ = 8
HG_HEADS = 4
HG_D = 128
HG_W = HG_HEADS * HG_D
SSD_HEADS = 8
SSD_P = 64
SSD_N = 128
SSD_W = SSD_HEADS * SSD_P
SSD_PAIRS = SSD_HEADS // 2
CONV_W = 4
CONV_DIM = 1024
D_FF = 2816
N_EXPERTS = 8
D_FF_EXPERT = 3584
EPS = 1e-6

LANES = 128
C_Q, C_F, C_I, C_OG, C_Z, C_XBC, C_DT = 0, 512, 1024, 1536, 2048, 2560, 3584
PROJ_W = C_DT + LANES
HG_CHUNK = 64
SSD_CHUNK = 128
SAMPLE_SEQS = 8
PROMPT_SEQS = 4
HG_SAFE_RANGE = 80.0

T_PROMPT = BATCH * SEQ
T_SAMPLE = DEC_BATCH * DEC_SEQ
T_ALL = T_PROMPT + T_SAMPLE

SUBLANES = 8
TOP_K = 2
MOE_TOKENS = 256
MOE_TILES = T_ALL // MOE_TOKENS
ROUTER_TILES = 1
MOE_SLOTS = TOP_K * MOE_TOKENS + N_EXPERTS * SUBLANES
SEG_COPY_ROWS = (64, 32, 16, 8)
GMM_ROWS = 1024
GMM_SUB = 256
GMM_TILES = (TOP_K * T_ALL + MOE_TILES * N_EXPERTS * (SUBLANES - 1)
             + N_EXPERTS * (GMM_ROWS - 1)) // GMM_ROWS + 1

VMEM_LIMIT = 60000 * 1024


def _sigmoid(x):
    return 0.5 * jnp.tanh(0.5 * x) + 0.5


def _silu(x):
    return x * _sigmoid(x)


def _softplus(x):
    return jnp.maximum(x, 0.0) + jnp.log(1.0 + jnp.exp(-jnp.abs(x)))


def _rms(x, g):
    return x * lax.rsqrt(jnp.mean(x * x, axis=-1, keepdims=True) + EPS) * g


def _dot(a, b):
    return jnp.dot(a.astype(BF16), b.astype(BF16), preferred_element_type=F32)


def _dot_nt(a, b):
    return lax.dot_general(a.astype(BF16), b.astype(BF16), (((1,), (1,)), ((), ())),
                           preferred_element_type=F32)


def _split3(x):
    x1 = x.astype(BF16)
    r1 = x - x1.astype(F32)
    x2 = r1.astype(BF16)
    x3 = (r1 - x2.astype(F32)).astype(BF16)
    return x1, x2, x3


def _mask_dot(m, x):
    mb = m.astype(BF16)
    x1, x2, x3 = _split3(x)
    return (jnp.dot(mb, x1, preferred_element_type=F32) + jnp.dot(mb, x2, preferred_element_type=F32)
            + jnp.dot(mb, x3, preferred_element_type=F32))


def _mask_dot_nt(x, m):
    mb = m.astype(BF16)
    dn = (((1,), (1,)), ((), ()))
    x1, x2, x3 = _split3(x)
    return (lax.dot_general(x1, mb, dn, preferred_element_type=F32)
            + lax.dot_general(x2, mb, dn, preferred_element_type=F32)
            + lax.dot_general(x3, mb, dn, preferred_element_type=F32))


def _dot_f32x3(a, b):
    a1, a2, _ = _split3(a)
    b1, b2, _ = _split3(b)
    return (jnp.dot(a1, b1, preferred_element_type=F32) + jnp.dot(a1, b2, preferred_element_type=F32)
            + jnp.dot(a2, b1, preferred_element_type=F32))


def _iota2(shape, dim):
    return lax.broadcasted_iota(jnp.int32, shape, dim)


def _seq_masks(rows, seq_len):
    t = _iota2((rows, rows), 0)
    s = _iota2((rows, rows), 1)
    same = (t // seq_len) == (s // seq_len)
    causal = same & (s <= t)
    ref = same & ((s % seq_len) < seq_len // 2)
    return causal, ref, same


def _row_specs(arrs, tm):
    if len(arrs) == 1:
        return [pl.BlockSpec((tm, arrs[0].shape[1]), lambda i, *_: (i, 0))]
    npt = T_PROMPT // tm
    return [pl.BlockSpec((tm, arrs[0].shape[1]), lambda i, *_: (jnp.minimum(i, npt - 1), 0)),
            pl.BlockSpec((tm, arrs[1].shape[1]), lambda i, *_: (jnp.maximum(i - npt, 0), 0),
                         pipeline_mode=pl.Buffered(1))]


def _rows(i, refs):
    if len(refs) == 1:
        return refs[0][...]
    return jnp.where(i < T_PROMPT // refs[0].shape[0], refs[0][...], refs[1][...])


def _proj_body(n_h, *refs):
    h_refs, (g_ref, w_ref, o_ref, wb_ref) = refs[:n_h], refs[n_h:]
    i = pl.program_id(0)

    @pl.when(i == 0)
    def _():
        wb_ref[:, :C_DT] = w_ref[:, :C_DT].astype(BF16)
        tail = w_ref[:, C_DT:]
        pad = jnp.zeros((D_MODEL, LANES - tail.shape[1]), F32)
        wb_ref[:, C_DT:] = jnp.concatenate([tail, pad], axis=1).astype(BF16)

    hn = _rms(_rows(i, h_refs), g_ref[...])
    o_ref[...] = _dot(hn, wb_ref[...])


def _proj_call(l, h, g, w_in):
    tm = 512
    return pl.pallas_call(
        functools.partial(_proj_body, len(h)),
        grid=(T_ALL // tm,),
        in_specs=_row_specs(h, tm) + [pl.BlockSpec((1, D_MODEL), lambda i: (0, 0)),
                                      pl.BlockSpec((None,) + w_in.shape[1:], lambda i: (l, 0, 0),
                                                   pipeline_mode=pl.Buffered(1))],
        out_specs=pl.BlockSpec((tm, PROJ_W), lambda i: (i, 0)),
        out_shape=jax.ShapeDtypeStruct((T_ALL, PROJ_W), F32),
        scratch_shapes=[pltpu.VMEM((D_MODEL, PROJ_W), BF16)],
        compiler_params=pltpu.CompilerParams(dimension_semantics=("arbitrary",),
                                             vmem_limit_bytes=VMEM_LIMIT),
        name="in_proj",
    )(*h, g, w_in)


def _hgrn_gates(p_q, p_f, la, lc, omlb):
    q = _silu(p_q)
    e = jnp.exp(-jnp.abs(p_f))
    b = lc + jnp.minimum(p_f, 0.0) - jnp.log(1.0 + e)
    logf = jnp.maximum(la, b) + jnp.log(1.0 + jnp.exp(-jnp.abs(la - b)))
    r = 1.0 / (1.0 + e)
    kin = omlb * jnp.where(p_f >= 0.0, e * r, r)
    return q, logf, kin


def _hgrn_decays(logf, causal, ref, same):
    rows = logf.shape[0]
    if ref is None:
        g = _mask_dot(causal, logf)
        return g, g[rows // 2 - 1:rows // 2, :], g[rows - 1:rows, :]
    m = jnp.concatenate([causal.astype(F32), ref.astype(F32), same.astype(F32)], axis=0)
    g3 = _mask_dot(m, logf)
    return g3[:rows], g3[rows:2 * rows], g3[2 * rows:]


def _hgrn_exact_intra(ex_ref, q, kin, g, v, seq_len):
    rows = q.shape[0]
    ex_ref[0] = g
    ex_ref[1] = q
    ex_ref[2] = kin
    ex_ref[3] = v
    s_idx = _iota2((rows, HG_W), 0)

    def body(t, carry):
        gt = ex_ref[0, pl.ds(t, 1), :]
        qt = ex_ref[1, pl.ds(t, 1), :]
        live = (s_idx <= t) & (s_idx // seq_len == t // seq_len)
        w = jnp.where(live, qt * ex_ref[2] * jnp.exp(jnp.minimum(gt - ex_ref[0], 0.0)), 0.0)
        outs = []
        for h in range(HG_HEADS):
            sl = slice(h * HG_D, (h + 1) * HG_D)
            score = jnp.sum(w[:, sl], axis=1, keepdims=True)
            outs.append(jnp.sum(score * ex_ref[3, :, sl], axis=0, keepdims=True))
        ex_ref[4, pl.ds(t, 1), :] = jnp.concatenate(outs, axis=1)
        return carry

    lax.fori_loop(0, rows, body, 0)
    return ex_ref[4]


def _hgrn_head_out(o, hgn, og):
    return _rms(o, hgn) * _silu(og)


def _ssd_inputs(conv, p_dt, dtb, a_neg):
    conv = _silu(conv)
    xs = conv[:, :SSD_W]
    bm = conv[:, SSD_W:SSD_W + 2 * SSD_N]
    cm = conv[:, SSD_W + 2 * SSD_N:]
    dtf = _softplus(p_dt + dtb)
    a = dtf * a_neg
    return xs, bm, cm, dtf, a


def _lane_bcast(x, lane, width):
    return jnp.broadcast_to(x[:, lane:lane + 1], (x.shape[0], width))


def _ssd_finish(y, xs, z, dsk, ssn):
    y = (y + dsk * xs) * _silu(z)
    half = SSD_W // 2
    return jnp.concatenate([_rms(y[:, :half], ssn[:, :half]), _rms(y[:, half:], ssn[:, half:])],
                           axis=1)


def _scan_prompt_body(*refs):
    ns = PROMPT_SEQS
    p_refs, refs = refs[:ns], refs[ns:]
    (la_ref, lc_ref, omlb_ref), params = refs[:3], refs[3:10]
    mix_ref, ohg_ref, ossm_ref, oconv_ref, st_ref, hp_ref, xpad_ref, ex_ref = refs[10:]
    j = pl.program_id(1)
    tl = SSD_CHUNK

    @pl.when(j == 0)
    def _():
        st_ref[...] = jnp.zeros_like(st_ref)
        hp_ref[...] = jnp.zeros_like(hp_ref)
        xpad_ref[:, 0:8, :] = jnp.zeros((ns, 8, CONV_DIM), F32)

    gates, worst = [], None
    for s in range(ns):
        gates.append([])
        for c in range(tl // HG_CHUNK):
            rows = slice(c * HG_CHUNK, (c + 1) * HG_CHUNK)
            gt = _hgrn_gates(p_refs[s][rows, C_Q:C_Q + HG_W], p_refs[s][rows, C_F:C_F + HG_W],
                             la_ref[...], lc_ref[...], omlb_ref[...])
            gates[s].append(gt)
            half = HG_CHUNK // 2
            for part in (gt[1][:half], gt[1][half:]):
                tot = jnp.sum(part, axis=0, keepdims=True)
                worst = tot if worst is None else jnp.minimum(worst, tot)
    safe = jnp.min(worst) >= -HG_SAFE_RANGE

    for exact in (False, True):
        @pl.when(safe != exact)
        def _(exact=exact):
            for s in range(ns):
                _scan_prompt_tile(gates[s], exact, ex_ref, p_refs[s], *params, mix_ref.at[s],
                                  st_ref.at[s], hp_ref.at[s], xpad_ref.at[s])

    @pl.when(j == pl.num_programs(1) - 1)
    def _():
        for s in range(ns):
            oconv_ref[s] = xpad_ref[s, 5:8, :]
            for h in range(HG_HEADS):
                ohg_ref[s, h] = st_ref[s, h].T
        ossm_ref[...] = hp_ref[...]


def _scan_prompt_tile(gates, exact, ex_ref, p_ref, hgn_ref, cw_ref, cb_ref, dtb_ref, an_ref,
                      dsk_ref, ssn_ref, mix_ref, st_ref, hp_ref, xpad_ref):
    tl = SSD_CHUNK
    causal, _, _ = _seq_masks(HG_CHUNK, HG_CHUNK)
    for c in range(tl // HG_CHUNK):
        r0 = c * HG_CHUNK
        rows = slice(r0, r0 + HG_CHUNK)
        q, logf, kin = gates[c]
        v = p_ref[rows, C_I:C_I + HG_W]
        g, gmid, glast = _hgrn_decays(logf, causal, None, None)
        if exact:
            qh = q * jnp.exp(g)
            kd = kin * jnp.exp(glast - g)
            o_intra = _hgrn_exact_intra(ex_ref, q, kin, g, v, HG_CHUNK)
        else:
            qt = q * jnp.exp(g - gmid)
            kt = kin * jnp.exp(gmid - g)
            qh = qt * jnp.exp(gmid)
            kd = kt * jnp.exp(glast - gmid)
        ds = jnp.exp(glast[0:1, :])
        for h in range(HG_HEADS):
            sl = slice(h * HG_D, (h + 1) * HG_D)
            st = st_ref[h]
            if exact:
                o = o_intra[:, sl] + _dot_nt(qh[:, sl], st)
            else:
                sc = jnp.where(causal, _dot_nt(qt[:, sl], kt[:, sl]), 0.0)
                o = _dot(sc, v[:, sl]) + _dot_nt(qh[:, sl], st)
            st_ref[h] = st * ds[:, sl] + _dot(v[:, sl].T, kd[:, sl])
            og = p_ref[rows, C_OG + h * HG_D:C_OG + (h + 1) * HG_D]
            mix_ref[rows, sl] = _hgrn_head_out(o, hgn_ref[:, sl], og).astype(BF16)

    xpad_ref[8:8 + tl, :] = p_ref[:, C_XBC:C_XBC + CONV_DIM]
    conv = cb_ref[...] + cw_ref[0:1, :] * xpad_ref[5:5 + tl, :]
    for tap in range(1, CONV_W):
        conv = conv + cw_ref[tap:tap + 1, :] * xpad_ref[5 + tap:5 + tap + tl, :]
    xpad_ref[0:8, :] = xpad_ref[tl:tl + 8, :]

    xs, bm, cm, dtf, a = _ssd_inputs(conv, p_ref[:, C_DT:C_DT + LANES], dtb_ref[...], an_ref[...])
    tri, _, _ = _seq_masks(tl, tl)
    trif = tri.astype(F32)
    acum = _mask_dot(trif, a)
    acum_t = _mask_dot_nt(a.T, trif)
    lo = _iota2((tl, LANES), 1) < SSD_P
    top = _iota2((LANES, tl), 0) < SSD_P
    ys = []
    for pair in range(SSD_PAIRS):
        grp = pair // 2
        cg = cm[:, grp * SSD_N:(grp + 1) * SSD_N]
        bg = bm[:, grp * SSD_N:(grp + 1) * SSD_N]
        cb = _dot_nt(cg, bg)
        psl = slice(pair * LANES, (pair + 1) * LANES)
        r0, r1 = 2 * pair, 2 * pair + 1
        xp = xs[:, psl] * jnp.where(lo, _lane_bcast(dtf, r0, LANES), _lane_bcast(dtf, r1, LANES))
        intra, einter, wrow, ea = [], [], [], []
        for r in (r0, r1):
            col = _lane_bcast(acum, r, tl)
            row = jnp.broadcast_to(acum_t[r:r + 1, :], (tl, tl))
            dec = jnp.exp(jnp.where(tri, col - row, -jnp.inf))
            intra.append(_dot(cb * dec, xp))
            einter.append(jnp.exp(col))
            alast = acum_t[r:r + 1, tl - 1:tl]
            wrow.append(jnp.exp(alast - acum_t[r:r + 1, :]))
            ea.append(jnp.exp(alast))
        hp = hp_ref[pair]
        y = jnp.where(lo, intra[0], intra[1]) + jnp.where(lo, einter[0], einter[1]) * _dot_nt(cg, hp)
        xw_t = xp.T * jnp.where(top, wrow[0], wrow[1])
        hp_ref[pair] = jnp.where(top, ea[0], ea[1]) * hp + _dot(xw_t, bg)
        ys.append(y)
    y = jnp.concatenate(ys, axis=1)
    y = _ssd_finish(y, xs, p_ref[:, C_Z:C_Z + SSD_W], dsk_ref[...], ssn_ref[...])
    mix_ref[:, HG_W:] = y.astype(BF16)


def _row_spec(width):
    return pl.BlockSpec((1, width), lambda b, j: (0, 0))


def _scan_prompt_call(proj, prm):
    tl = SSD_CHUNK
    nt = SEQ // tl
    ns = PROMPT_SEQS
    state_blk = (ns, HG_HEADS, HG_D, HG_D)
    in_specs = [pl.BlockSpec((tl, PROJ_W), lambda b, j, s=s: ((b * ns + s) * nt + j, 0))
                for s in range(ns)]
    in_specs += [_row_spec(HG_W), _row_spec(HG_W), _row_spec(HG_W), _row_spec(HG_W),
                 pl.BlockSpec((CONV_W, CONV_DIM), lambda b, j: (0, 0)), _row_spec(CONV_DIM),
                 _row_spec(LANES), _row_spec(LANES), _row_spec(SSD_W), _row_spec(SSD_W)]
    out_specs = [pl.BlockSpec((ns, tl, D_MODEL), lambda b, j: (b, j, 0)),
                 pl.BlockSpec(state_blk, lambda b, j: (b, 0, 0, 0)),
                 pl.BlockSpec(state_blk, lambda b, j: (b, 0, 0, 0)),
                 pl.BlockSpec((ns, CONV_W - 1, CONV_DIM), lambda b, j: (b, 0, 0))]
    out_shape = [jax.ShapeDtypeStruct((BATCH, SEQ, D_MODEL), BF16),
                 jax.ShapeDtypeStruct((BATCH,) + state_blk[1:], F32),
                 jax.ShapeDtypeStruct((BATCH,) + state_blk[1:], F32),
                 jax.ShapeDtypeStruct((BATCH, CONV_W - 1, CONV_DIM), F32)]
    mixed, hg, ssm_t, conv = pl.pallas_call(
        _scan_prompt_body,
        grid=(BATCH // ns, nt),
        in_specs=in_specs, out_specs=out_specs, out_shape=out_shape,
        scratch_shapes=[pltpu.VMEM(state_blk, F32), pltpu.VMEM(state_blk, F32),
                        pltpu.VMEM((ns, tl + 8, CONV_DIM), F32),
                        pltpu.VMEM((5, HG_CHUNK, HG_W), F32)],
        compiler_params=pltpu.CompilerParams(dimension_semantics=("arbitrary", "arbitrary"),
                                             vmem_limit_bytes=VMEM_LIMIT),
        name="scan_prompt",
    )(*([proj] * ns), prm["la"], prm["lc"], prm["omlb"], prm["hgn"], prm["cw"], prm["cb"],
      prm["dtb"], prm["an"], prm["dsk"], prm["ssn"])
    return mixed.reshape(T_PROMPT, D_MODEL), hg, ssm_t, conv


def _scan_sample_body(n_prev, p_ref, shg_ref, sssm_ref, sconv_ref, la_ref, lc_ref, omlb_ref,
                      hgn_ref, cw_ref, cb_ref, dtb_ref, an_ref, dsk_ref, ssn_ref, *refs):
    if n_prev:
        phg_ref, pssm_ref = refs[:2]
        refs = refs[2:]
    mix_ref, ohg_ref, ossm_ref, oconv_ref, xpad_ref, ex_ref = refs
    if n_prev:
        ohg_ref[0:n_prev] = phg_ref[...]
        ossm_ref[0:n_prev] = pssm_ref[...]
    nb, sl_len = SAMPLE_SEQS, DEC_SEQ
    rows = nb * sl_len
    causal, ref, same = _seq_masks(rows, sl_len)
    rowseq = _iota2((rows, LANES), 0) // sl_len

    q, logf, kin = _hgrn_gates(p_ref[:, C_Q:C_Q + HG_W], p_ref[:, C_F:C_F + HG_W],
                               la_ref[...], lc_ref[...], omlb_ref[...])
    v = p_ref[:, C_I:C_I + HG_W]
    g, gmid, glast = _hgrn_decays(logf, causal, ref, same)
    safe = jnp.max(jnp.abs(g - gmid)) <= HG_SAFE_RANGE

    @pl.when(safe)
    def _():
        qt = q * jnp.exp(g - gmid)
        kt = kin * jnp.exp(gmid - g)
        for h in range(HG_HEADS):
            sl = slice(h * HG_D, (h + 1) * HG_D)
            sc = jnp.where(causal, _dot_nt(qt[:, sl], kt[:, sl]), 0.0)
            ex_ref[4, :, sl] = _dot(sc, v[:, sl])

    @pl.when(jnp.logical_not(safe))
    def _():
        _hgrn_exact_intra(ex_ref, q, kin, g, v, sl_len)

    qh = q * jnp.exp(g)
    kd = kin * jnp.exp(glast - g)
    ds_t = jnp.exp(glast).T
    kd_t = kd.T
    for h in range(HG_HEADS):
        sl = slice(h * HG_D, (h + 1) * HG_D)
        o_intra = ex_ref[4, :, sl]
        o_inter = []
        for b in range(nb):
            s_old = shg_ref[b, h]
            o_inter.append(_dot(qh[b * sl_len:(b + 1) * sl_len, sl], s_old))
            vb = jnp.where(rowseq == b, v[:, sl], 0.0)
            dcol = _lane_bcast(ds_t[sl, :], b * sl_len, HG_D)
            ohg_ref[n_prev, b, h] = dcol * s_old + _dot(kd_t[sl, :], vb)
        o = o_intra + jnp.concatenate(o_inter, axis=0)
        og = p_ref[:, C_OG + h * HG_D:C_OG + (h + 1) * HG_D]
        mix_ref[:, sl] = _hgrn_head_out(o, hgn_ref[:, sl], og).astype(BF16)

    convs = []
    for b in range(nb):
        base = 16 * b
        xpad_ref[base + 5:base + 8, :] = sconv_ref[b]
        xpad_ref[base + 8:base + 16, :] = p_ref[b * sl_len:(b + 1) * sl_len, C_XBC:C_XBC + CONV_DIM]
        cv = cb_ref[...] + cw_ref[0:1, :] * xpad_ref[base + 5:base + 13, :]
        for tap in range(1, CONV_W):
            cv = cv + cw_ref[tap:tap + 1, :] * xpad_ref[base + 5 + tap:base + 13 + tap, :]
        convs.append(cv)
        oconv_ref[b] = xpad_ref[base + 13:base + 16, :]
    conv = jnp.concatenate(convs, axis=0)

    xs, bm, cm, dtf, a = _ssd_inputs(conv, p_ref[:, C_DT:C_DT + LANES], dtb_ref[...], an_ref[...])
    causf = causal.astype(F32)
    acum = _mask_dot(causf, a)
    a_t = a.T
    acum_t = _mask_dot_nt(a_t, causf)
    atot_t = _mask_dot_nt(a_t, same.astype(F32))
    wall_t = jnp.exp(atot_t - acum_t)
    eatot_t = jnp.exp(atot_t)
    eall = jnp.exp(acum)
    lo = _iota2((rows, LANES), 1) < SSD_P
    top = _iota2((LANES, rows), 0) < SSD_P
    top_sq = _iota2((LANES, SSD_N), 0) < SSD_P
    colseq = _iota2((LANES, rows), 1) // sl_len
    ys = []
    for pair in range(SSD_PAIRS):
        grp = pair // 2
        cg = cm[:, grp * SSD_N:(grp + 1) * SSD_N]
        bg = bm[:, grp * SSD_N:(grp + 1) * SSD_N]
        cb = _dot_nt(cg, bg)
        psl = slice(pair * LANES, (pair + 1) * LANES)
        r0, r1 = 2 * pair, 2 * pair + 1
        xp = xs[:, psl] * jnp.where(lo, _lane_bcast(dtf, r0, LANES), _lane_bcast(dtf, r1, LANES))
        xw_t = xp.T * jnp.where(top, wall_t[r0:r0 + 1, :], wall_t[r1:r1 + 1, :])
        intra = []
        for r in (r0, r1):
            col = _lane_bcast(acum, r, rows)
            row = jnp.broadcast_to(acum_t[r:r + 1, :], (rows, rows))
            dec = jnp.exp(jnp.where(causal, col - row, -jnp.inf))
            intra.append(_dot(cb * dec, xp))
        y_intra = jnp.where(lo, intra[0], intra[1])
        y_inter = []
        for b in range(nb):
            tr = slice(b * sl_len, (b + 1) * sl_len)
            h_old = sssm_ref[b, pair]
            y_inter.append(_dot_nt(cg[tr, :], h_old))
            upd = _dot(jnp.where(colseq == b, xw_t, 0.0), bg)
            c0 = b * sl_len
            ea = jnp.where(top_sq, eatot_t[r0:r0 + 1, c0:c0 + 1], eatot_t[r1:r1 + 1, c0:c0 + 1])
            ossm_ref[n_prev, b, pair] = ea * h_old + upd
        e_pair = jnp.where(lo, _lane_bcast(eall, r0, LANES), _lane_bcast(eall, r1, LANES))
        ys.append(y_intra + e_pair * jnp.concatenate(y_inter, axis=0))
    y = jnp.concatenate(ys, axis=1)
    y = _ssd_finish(y, xs, p_ref[:, C_Z:C_Z + SSD_W], dsk_ref[...], ssn_ref[...])
    mix_ref[:, HG_W:] = y.astype(BF16)


def _scan_sample_call(l, proj, s_hg, s_ssm_t, s_conv, prm, prev):
    nb = SAMPLE_SEQS
    rows = nb * DEC_SEQ
    row0 = T_PROMPT // rows
    c0 = lambda i: (0, 0)
    rs = lambda w: pl.BlockSpec((1, w), c0)
    state_blk = (nb, HG_HEADS, HG_D, HG_D)
    in_specs = [pl.BlockSpec((rows, PROJ_W), lambda i: (row0 + i, 0)),
                pl.BlockSpec((None,) + state_blk, lambda i: (l, i, 0, 0, 0)),
                pl.BlockSpec((None,) + state_blk, lambda i: (l, i, 0, 0, 0)),
                pl.BlockSpec((None, nb, CONV_W - 1, CONV_DIM), lambda i: (l, i, 0, 0)),
                rs(HG_W), rs(HG_W), rs(HG_W), rs(HG_W),
                pl.BlockSpec((CONV_W, CONV_DIM), c0), rs(CONV_DIM),
                rs(LANES), rs(LANES), rs(SSD_W), rs(SSD_W)]
    in_specs += [pl.BlockSpec((l,) + state_blk, lambda i: (0, i, 0, 0, 0))] * (2 if l else 0)
    out_specs = [pl.BlockSpec((rows, D_MODEL), lambda i: (i, 0)),
                 pl.BlockSpec((l + 1,) + state_blk, lambda i: (0, i, 0, 0, 0)),
                 pl.BlockSpec((l + 1,) + state_blk, lambda i: (0, i, 0, 0, 0)),
                 pl.BlockSpec((nb, CONV_W - 1, CONV_DIM), lambda i: (i, 0, 0))]
    stacked = jax.ShapeDtypeStruct((l + 1, DEC_BATCH) + state_blk[1:], F32)
    out_shape = [jax.ShapeDtypeStruct((T_SAMPLE, D_MODEL), BF16), stacked, stacked,
                 jax.ShapeDtypeStruct((DEC_BATCH, CONV_W - 1, CONV_DIM), F32)]
    return pl.pallas_call(
        functools.partial(_scan_sample_body, l),
        grid=(DEC_BATCH // nb,),
        in_specs=in_specs, out_specs=out_specs, out_shape=out_shape,
        scratch_shapes=[pltpu.VMEM((16 * nb, CONV_DIM), F32), pltpu.VMEM((5, rows, HG_W), F32)],
        compiler_params=pltpu.CompilerParams(dimension_semantics=("arbitrary",),
                                             vmem_limit_bytes=VMEM_LIMIT),
        name="scan_sample",
    )(proj, s_hg, s_ssm_t, s_conv, prm["la"], prm["lc"], prm["omlb"], prm["hgn"], prm["cw"],
      prm["cb"], prm["dtb"], prm["an"], prm["dsk"], prm["ssn"], *(prev if l else ()))


def _out_ffn_body(n_h, *refs):
    h_refs, mix_refs = refs[:n_h], refs[n_h:n_h + 2]
    wo_ref, g_ref, wg_ref, wu_ref, wd_ref, o_ref, hn_ref = refs[n_h + 2:]
    i = pl.program_id(0)
    f = pl.program_id(1)

    @pl.when(f == 0)
    def _():
        h1 = _rows(i, h_refs) + _dot(_rows(i, mix_refs), wo_ref[...])
        o_ref[...] = h1
        hn_ref[...] = _rms(h1, g_ref[...]).astype(BF16)

    o_ref[...] += _swiglu_part(hn_ref[...], wg_ref[...], wu_ref[...], wd_ref[...])


def _out_ffn_call(h, mixed, wo_bf, g, wi_bf, wd_bf):
    tm, tf = 1024, D_FF // 2
    nf = D_FF // tf
    return pl.pallas_call(
        functools.partial(_out_ffn_body, len(h)),
        grid=(T_ALL // tm, nf),
        in_specs=_row_specs(h, tm) + _row_specs(mixed, tm) + [
                  pl.BlockSpec((D_MODEL, D_MODEL), lambda i, f: (0, 0)),
                  pl.BlockSpec((1, D_MODEL), lambda i, f: (0, 0)),
                  pl.BlockSpec((D_MODEL, tf), lambda i, f: (0, f)),
                  pl.BlockSpec((D_MODEL, tf), lambda i, f: (0, nf + f)),
                  pl.BlockSpec((tf, D_MODEL), lambda i, f: (f, 0))],
        out_specs=pl.BlockSpec((tm, D_MODEL), lambda i, f: (i, 0)),
        out_shape=jax.ShapeDtypeStruct((T_ALL, D_MODEL), F32),
        scratch_shapes=[pltpu.VMEM((tm, D_MODEL), BF16)],
        compiler_params=pltpu.CompilerParams(dimension_semantics=("arbitrary", "arbitrary"),
                                             vmem_limit_bytes=VMEM_LIMIT),
        name="out_ffn",
    )(*h, *mixed, wo_bf, g, wi_bf, wi_bf, wd_bf)


def _out_router_body(h_ref, mixp_ref, mixs_ref, wo_ref, g_ref, wr_ref,
                     h1_ref, hn_ref, route_ref, seg_ref):
    h1 = h_ref[...] + _dot(_rows(pl.program_id(0), (mixp_ref, mixs_ref)), wo_ref[...])
    h1_ref[...] = h1
    hn = _rms(h1, g_ref[...])
    hn_ref[...] = hn.astype(BF16)
    logits = _dot_f32x3(hn, wr_ref[...])
    lane = _iota2(logits.shape, 1)
    lg = jnp.where(lane < N_EXPERTS, logits, -jnp.inf)
    m1 = jnp.max(lg, axis=1, keepdims=True)
    i1 = jnp.min(jnp.where(lg == m1, lane, LANES), axis=1, keepdims=True)
    lg2 = jnp.where(lane == i1, -jnp.inf, lg)
    m2 = jnp.max(lg2, axis=1, keepdims=True)
    i2 = jnp.min(jnp.where(lg2 == m2, lane, LANES), axis=1, keepdims=True)
    e2 = jnp.exp(m2 - m1)
    g1 = 1.0 / (1.0 + e2)
    g2 = e2 / (1.0 + e2)

    tm = logits.shape[0]
    chosen = jnp.where((lane == i1) | (lane == i2), 1.0, 0.0)
    t_row, t_col = _iota2((tm, tm), 0), _iota2((tm, tm), 1)
    earlier = ((t_col < t_row) & (t_col // MOE_TOKENS == t_row // MOE_TOKENS)).astype(F32)
    rank = _dot(earlier, chosen)
    below = (_iota2((LANES, LANES), 0) < _iota2((LANES, LANES), 1)).astype(F32)
    pos = []
    for k in range(tm // MOE_TOKENS):
        rows = slice(k * MOE_TOKENS, (k + 1) * MOE_TOKENS)
        cnt = jnp.sum(chosen[rows], axis=0, keepdims=True)
        seg = jnp.floor((cnt + (SUBLANES - 1)) * (1.0 / SUBLANES)) * SUBLANES
        seg8 = jnp.broadcast_to(seg, (SUBLANES, LANES))
        pos.append(_dot(seg8, below)[0:1, :] + rank[rows])
        seg_ref[k] = seg8.astype(jnp.int32)
    pos = jnp.concatenate(pos, axis=0)
    p1 = jnp.sum(jnp.where(lane == i1, pos, 0.0), axis=1, keepdims=True)
    p2 = jnp.sum(jnp.where(lane == i2, pos, 0.0), axis=1, keepdims=True)
    route_ref[...] = jnp.where(lane == 0, p1, jnp.where(lane == 1, p2, jnp.where(
        lane == 2, g1, jnp.where(lane == 3, g2, 0.0))))


def _out_router_call(h, mixed, wo_bf, g, wr_pad):
    tm = ROUTER_TILES * MOE_TOKENS
    return pl.pallas_call(
        _out_router_body,
        grid=(T_ALL // tm,),
        in_specs=[pl.BlockSpec((tm, D_MODEL), lambda i: (i, 0))] + _row_specs(mixed, tm) + [
                  pl.BlockSpec((D_MODEL, D_MODEL), lambda i: (0, 0)),
                  pl.BlockSpec((1, D_MODEL), lambda i: (0, 0)),
                  pl.BlockSpec((D_MODEL, LANES), lambda i: (0, 0))],
        out_specs=[pl.BlockSpec((tm, D_MODEL), lambda i: (i, 0)),
                   pl.BlockSpec((tm, D_MODEL), lambda i: (i, 0)),
                   pl.BlockSpec((tm, LANES), lambda i: (i, 0)),
                   pl.BlockSpec((ROUTER_TILES, SUBLANES, LANES), lambda i: (i, 0, 0))],
        out_shape=[jax.ShapeDtypeStruct((T_ALL, D_MODEL), F32),
                   jax.ShapeDtypeStruct((T_ALL, D_MODEL), BF16),
                   jax.ShapeDtypeStruct((T_ALL, LANES), F32),
                   jax.ShapeDtypeStruct((MOE_TILES, SUBLANES, LANES), jnp.int32)],
        compiler_params=pltpu.CompilerParams(dimension_semantics=("arbitrary",),
                                             vmem_limit_bytes=VMEM_LIMIT),
        name="out_router",
    )(h, *mixed, wo_bf, g, wr_pad)


def _row_tile_copy(tile_ref, hbm_ref, tile_row, hbm_row, sem, to_hbm, rows=SUBLANES):
    t = tile_ref.at[pl.ds(pl.multiple_of(tile_row, SUBLANES), rows), :]
    g = hbm_ref.at[pl.ds(pl.multiple_of(hbm_row, SUBLANES), rows), :]
    return pltpu.make_async_copy(t, g, sem) if to_hbm else pltpu.make_async_copy(g, t, sem)


def _seg_plan(n):
    counts = [lax.shift_right_logical(n, 3)]
    for shift in (2, 1, 0):
        counts.append(lax.shift_right_logical(n, shift) & 1)
    return counts


def _seg_copies(i, n_ref, loc_ref, start_ref, tile_ref, hbm_ref, sem, to_hbm):
    for e in range(N_EXPERTS):
        k = i * N_EXPERTS + e
        lo = loc_ref[k]
        st = start_ref[k]
        counts = _seg_plan(n_ref[k])
        big = SEG_COPY_ROWS[0]

        def body(c, carry, lo=lo, st=st):
            _row_tile_copy(tile_ref, hbm_ref, lo + c * big, st + c * big, sem, to_hbm, big).start()
            return carry

        lax.fori_loop(0, counts[0], body, 0)
        done = counts[0] * big
        for rows, cnt in zip(SEG_COPY_ROWS[1:], counts[1:]):
            @pl.when(cnt == 1)
            def _(rows=rows, done=done, lo=lo, st=st):
                _row_tile_copy(tile_ref, hbm_ref, lo + done, st + done, sem, to_hbm, rows).start()

            done = done + cnt * rows


def _seg_wait(i, n_ref, tile_ref, hbm_ref, sem, to_hbm):
    totals = None
    for e in range(N_EXPERTS):
        counts = _seg_plan(n_ref[i * N_EXPERTS + e])
        totals = counts if totals is None else [a + b for a, b in zip(totals, counts)]
    for rows, total in zip(SEG_COPY_ROWS, totals):
        def body(c, carry, rows=rows):
            _row_tile_copy(tile_ref, hbm_ref, 0, 0, sem, to_hbm, rows).wait()
            return carry

        lax.fori_loop(0, total, body, 0)


def _dispatch_body(n_ref, loc_ref, start_ref, zn_ref, zstart_ref, zb_ref, zbn_ref,
                   hn_ref, route_ref, xs_ref, stage_ref, zero_ref, sem):
    i = pl.program_id(0)
    last = pl.num_programs(0) - 1
    slot = i % 2

    @pl.when(i >= 2)
    def _():
        _seg_wait(i - 2, n_ref, stage_ref.at[slot], xs_ref, sem.at[slot], True)

    rt = route_ref[...].T
    s = _iota2((MOE_SLOTS, MOE_TOKENS), 0).astype(F32)
    perm = jnp.where((s == rt[0:1, :]) | (s == rt[1:2, :]), 1.0, 0.0)
    stage_ref[slot] = _dot(perm, hn_ref[...])
    _seg_copies(i, n_ref, loc_ref, start_ref, stage_ref.at[slot], xs_ref, sem.at[slot], True)

    @pl.when(i == last)
    def _():
        zero_ref[...] = jnp.zeros_like(zero_ref)

        def blk_copy(b):
            return pltpu.make_async_copy(
                zero_ref, xs_ref.at[pl.ds(pl.multiple_of(b * GMM_SUB, GMM_SUB), GMM_SUB), :],
                sem.at[3])

        def bbody(b, carry):
            @pl.when(zb_ref[b] == 1)
            def _():
                blk_copy(b).start()
            return carry

        lax.fori_loop(0, GMM_TILES * GMM_ROWS // GMM_SUB, bbody, 0)
        nz = zn_ref[0]
        for e in range(N_EXPERTS):
            st = zstart_ref[e]

            def body(c, carry, st=st):
                _row_tile_copy(zero_ref, xs_ref, 0, st + c * SUBLANES, sem.at[2], True).start()
                return carry

            lax.fori_loop(0, zn_ref[e], body, 0)
            if e:
                nz = nz + zn_ref[e]

        @pl.when(last >= 1)
        def _():
            _seg_wait(i - 1, n_ref, stage_ref.at[1 - slot], xs_ref, sem.at[1 - slot], True)

        _seg_wait(i, n_ref, stage_ref.at[slot], xs_ref, sem.at[slot], True)

        def zbody(c, carry):
            _row_tile_copy(zero_ref, xs_ref, 0, 0, sem.at[2], True).wait()
            return carry

        lax.fori_loop(0, nz, zbody, 0)

        def bwait(c, carry):
            blk_copy(0).wait()
            return carry

        lax.fori_loop(0, zbn_ref[0], bwait, 0)


def _dispatch_call(sched, hn_bf, route):
    grid_spec = pltpu.PrefetchScalarGridSpec(
        num_scalar_prefetch=7,
        grid=(MOE_TILES,),
        in_specs=[pl.BlockSpec((MOE_TOKENS, D_MODEL), lambda i, *_: (i, 0)),
                  pl.BlockSpec((MOE_TOKENS, LANES), lambda i, *_: (i, 0))],
        out_specs=pl.BlockSpec(memory_space=pl.ANY),
        scratch_shapes=[pltpu.VMEM((2, MOE_SLOTS, D_MODEL), F32),
                        pltpu.VMEM((GMM_SUB, D_MODEL), F32),
                        pltpu.SemaphoreType.DMA((4,))])
    return pl.pallas_call(
        _dispatch_body,
        grid_spec=grid_spec,
        out_shape=jax.ShapeDtypeStruct((GMM_TILES * GMM_ROWS, D_MODEL), F32),
        compiler_params=pltpu.CompilerParams(dimension_semantics=("arbitrary",),
                                             vmem_limit_bytes=VMEM_LIMIT),
        name="moe_dispatch",
    )(sched["nch"], sched["loc"], sched["start"], sched["zn"], sched["zstart"], sched["zb"],
      sched["zbn"], hn_bf, route)


def _swiglu_part(x, wg, wu, wd):
    gate = jnp.dot(x, wg, preferred_element_type=F32)
    up = jnp.dot(x, wu, preferred_element_type=F32)
    return jnp.dot((_silu(gate) * up).astype(BF16), wd, preferred_element_type=F32)


def _experts_body(te_ref, tv_ref, nu_ref, x_ref, wg_ref, wu_ref, wd_ref, o_ref,
                  wgb_ref, wub_ref, wdb_ref):
    del te_ref, nu_ref
    i = pl.program_id(0)
    f = pl.program_id(1)
    nv = tv_ref[i]

    @pl.when(f == 0)
    def _():
        o_ref[...] = jnp.zeros_like(o_ref)

    @pl.when(nv == GMM_ROWS)
    def _():
        o_ref[...] += _swiglu_part(x_ref[...].astype(BF16), wg_ref[0].astype(BF16),
                                   wu_ref[0].astype(BF16), wd_ref[0].astype(BF16))

    @pl.when((nv > 0) & (nv < GMM_ROWS))
    def _():
        wgb_ref[...] = wg_ref[0].astype(BF16)
        wub_ref[...] = wu_ref[0].astype(BF16)
        wdb_ref[...] = wd_ref[0].astype(BF16)
        for sub in range(GMM_ROWS // GMM_SUB):
            rows = slice(sub * GMM_SUB, (sub + 1) * GMM_SUB)

            @pl.when(sub * GMM_SUB < nv)
            def _(rows=rows):
                o_ref[rows, :] += _swiglu_part(x_ref[rows, :].astype(BF16), wgb_ref[...],
                                               wub_ref[...], wdb_ref[...])


def _experts_call(sched, x_sorted, w_in_e, w_out_e):
    tf = 512
    nf = D_FF_EXPERT // tf

    def used(i, nu):
        return jnp.maximum(jnp.minimum(i, nu[0] - 1), 0)

    def fidx(i, f, nu):
        return jnp.where(i < nu[0], f, nf - 1)

    grid_spec = pltpu.PrefetchScalarGridSpec(
        num_scalar_prefetch=3,
        grid=(GMM_TILES, nf),
        in_specs=[pl.BlockSpec((GMM_ROWS, D_MODEL), lambda i, f, te, tv, nu: (used(i, nu), 0)),
                  pl.BlockSpec((1, D_MODEL, tf), lambda i, f, te, tv, nu: (te[i], 0, fidx(i, f, nu))),
                  pl.BlockSpec((1, D_MODEL, tf),
                               lambda i, f, te, tv, nu: (te[i], 0, nf + fidx(i, f, nu))),
                  pl.BlockSpec((1, tf, D_MODEL), lambda i, f, te, tv, nu: (te[i], fidx(i, f, nu), 0))],
        out_specs=pl.BlockSpec((GMM_ROWS, D_MODEL), lambda i, f, te, tv, nu: (i, 0)),
        scratch_shapes=[pltpu.VMEM((D_MODEL, tf), BF16), pltpu.VMEM((D_MODEL, tf), BF16),
                        pltpu.VMEM((tf, D_MODEL), BF16)])
    return pl.pallas_call(
        _experts_body,
        grid_spec=grid_spec,
        out_shape=jax.ShapeDtypeStruct((GMM_TILES * GMM_ROWS, D_MODEL), F32),
        compiler_params=pltpu.CompilerParams(dimension_semantics=("arbitrary", "arbitrary"),
                                             vmem_limit_bytes=VMEM_LIMIT),
        name="moe_experts",
    )(sched["te"], sched["tv"], sched["nu"], x_sorted, w_in_e, w_in_e, w_out_e)


def _combine_body(n_ref, loc_ref, start_ref, h1_ref, route_ref, gf_ref, ys_ref, op_ref, os_ref,
                  buf_ref, sem):
    i = pl.program_id(0)
    last = pl.num_programs(0) - 1
    slot = i % 2

    @pl.when(i == 0)
    def _():
        buf_ref[...] = jnp.zeros_like(buf_ref)
        _seg_copies(0, n_ref, loc_ref, start_ref, buf_ref.at[0], ys_ref, sem.at[0], False)

    @pl.when(i < last)
    def _():
        _seg_copies(i + 1, n_ref, loc_ref, start_ref, buf_ref.at[1 - slot], ys_ref,
                    sem.at[1 - slot], False)

    _seg_wait(i, n_ref, buf_ref.at[slot], ys_ref, sem.at[slot], False)

    route = route_ref[...]
    s = _iota2((MOE_TOKENS, MOE_SLOTS), 1).astype(F32)
    rows = buf_ref[slot]
    y1 = _dot(jnp.where(s == route[:, 0:1], 1.0, 0.0), rows)
    y2 = _dot(jnp.where(s == route[:, 1:2], 1.0, 0.0), rows)
    y = _rms(h1_ref[...] + route[:, 2:3] * y1 + route[:, 3:4] * y2, gf_ref[...])

    @pl.when(i < T_PROMPT // MOE_TOKENS)
    def _():
        op_ref[...] = y

    @pl.when(i >= T_PROMPT // MOE_TOKENS)
    def _():
        os_ref[...] = y


def _combine_call(sched, h1, route, gfin, y_sorted):
    npt = T_PROMPT // MOE_TOKENS
    grid_spec = pltpu.PrefetchScalarGridSpec(
        num_scalar_prefetch=3,
        grid=(MOE_TILES,),
        in_specs=[pl.BlockSpec((MOE_TOKENS, D_MODEL), lambda i, *_: (i, 0)),
                  pl.BlockSpec((MOE_TOKENS, LANES), lambda i, *_: (i, 0)),
                  pl.BlockSpec((1, D_MODEL), lambda i, *_: (0, 0)),
                  pl.BlockSpec(memory_space=pl.ANY)],
        out_specs=[pl.BlockSpec((MOE_TOKENS, D_MODEL), lambda i, *_: (jnp.minimum(i, npt - 1), 0)),
                   pl.BlockSpec((MOE_TOKENS, D_MODEL), lambda i, *_: (jnp.maximum(i - npt, 0), 0))],
        scratch_shapes=[pltpu.VMEM((2, MOE_SLOTS, D_MODEL), F32),
                        pltpu.SemaphoreType.DMA((2,))])
    return pl.pallas_call(
        _combine_body,
        grid_spec=grid_spec,
        out_shape=[jax.ShapeDtypeStruct((T_PROMPT, D_MODEL), F32),
                   jax.ShapeDtypeStruct((T_SAMPLE, D_MODEL), F32)],
        compiler_params=pltpu.CompilerParams(dimension_semantics=("arbitrary",),
                                             vmem_limit_bytes=VMEM_LIMIT),
        name="moe_combine",
    )(sched["nch"], sched["loc"], sched["start"], h1, route, gfin, y_sorted)


def _moe_schedule(seg):
    tot = jnp.sum(seg, axis=0)
    region = (tot + GMM_ROWS - 1) // GMM_ROWS * GMM_ROWS
    base = jnp.cumsum(region) - region
    start = base[None, :] + jnp.cumsum(seg, axis=0) - seg
    loc = jnp.cumsum(seg, axis=1) - seg
    ntile = region // GMM_ROWS
    cum = jnp.cumsum(ntile)
    nu = cum[-1]
    i = jnp.arange(GMM_TILES, dtype=jnp.int32)[:, None]
    first = (cum - ntile)[None, :]
    mine = (i >= first) & (i < cum[None, :])
    experts = jnp.arange(N_EXPERTS, dtype=jnp.int32)
    te = jnp.sum(jnp.where(mine, experts[None, :], 0), axis=1)
    tv = jnp.sum(jnp.where(mine, jnp.clip(tot[None, :] - (i - first) * GMM_ROWS, 0, GMM_ROWS), 0),
                 axis=1)
    te = jnp.where(i[:, 0] < nu, te, jnp.max(jnp.where(ntile > 0, experts, 0)))
    bstart = jnp.arange(GMM_TILES * GMM_ROWS // GMM_SUB, dtype=jnp.int32)[:, None] * GMM_SUB
    used_end = (base + (tot + GMM_SUB - 1) // GMM_SUB * GMM_SUB)[None, :]
    zb = jnp.any((bstart >= used_end) & (bstart < (base + region)[None, :]), axis=1)
    zb = (zb | (bstart[:, 0] >= jnp.sum(region))).astype(jnp.int32)
    return dict(nch=(seg // SUBLANES).reshape(-1), loc=loc.reshape(-1), start=start.reshape(-1),
                zn=((-tot) % GMM_SUB) // SUBLANES, zstart=base + tot,
                zb=zb, zbn=jnp.sum(zb).reshape(1),
                te=te, tv=tv.astype(jnp.int32), nu=nu.reshape(1).astype(jnp.int32))


def _row(x, width=None):
    x = x.astype(F32).reshape(1, -1)
    if width is not None and x.shape[1] < width:
        x = jnp.pad(x, ((0, 0), (0, width - x.shape[1])))
    return x


def _layer_params(l, lb_p, conv_w, conv_b, a_log, dt_bias, d_skip, hg_norm, ssd_norm):
    lb = jnp.sum(lb_p[1:l + 1], axis=0)
    return dict(
        la=_row(jnp.log(lb)), lc=_row(jnp.log1p(-lb)), omlb=_row(1.0 - lb),
        hgn=_row(hg_norm[l]), cw=conv_w[l].astype(F32), cb=_row(conv_b[l]),
        dtb=_row(dt_bias[l], LANES), an=_row(-jnp.exp(a_log[l].astype(F32)), LANES),
        dsk=_row(jnp.repeat(d_skip[l].astype(F32), SSD_P)), ssn=_row(ssd_norm[l]))


def kernel(x_prompt, x_sample, state_hgrn, state_ssm, state_conv, norm_mix, w_in, conv_w, conv_b,
           a_log, dt_bias, d_skip, lb_param, hg_norm, ssd_norm, w_out, norm_ffn, w_ffn_in,
           w_ffn_out, w_router, w_exp_in, w_exp_out, norm_final):
    h = (x_prompt.reshape(T_PROMPT, D_MODEL), x_sample.reshape(T_SAMPLE, D_MODEL))
    lb_p = jax.nn.softmax(lb_param.astype(F32), axis=0)
    packed = (SSD_PAIRS, LANES, SSD_N)
    s_ssm_t = jnp.swapaxes(state_ssm, 3, 4).reshape((DEPTH, DEC_BATCH) + packed)
    outs = {k: [] for k in ("hg_p", "ssm_p", "conv_p", "conv_s")}
    sample_states = None
    for l in range(DEPTH):
        prm = _layer_params(l, lb_p, conv_w, conv_b, a_log, dt_bias, d_skip, hg_norm, ssd_norm)
        proj = _proj_call(l, h, _row(norm_mix[l]), w_in)
        mixed_p, hg_p, ssm_p, conv_p = _scan_prompt_call(proj, prm)
        mixed_s, hg_s, ssm_s, conv_s = _scan_sample_call(l, proj, state_hgrn, s_ssm_t, state_conv,
                                                         prm, sample_states)
        sample_states = (hg_s, ssm_s)
        mixed = (mixed_p, mixed_s)
        for k, val in zip(outs, (hg_p, ssm_p, conv_p, conv_s)):
            outs[k].append(val)
        wo_bf = w_out[l].astype(BF16)
        if l % 2 == 0:
            h = (_out_ffn_call(h, mixed, wo_bf, _row(norm_ffn[l]), w_ffn_in[l // 2].astype(BF16),
                               w_ffn_out[l // 2].astype(BF16)),)
        else:
            wr_pad = jnp.pad(w_router[l // 2].astype(F32), ((0, 0), (0, LANES - N_EXPERTS)))
            h1, hn_bf, route, seg = _out_router_call(h[0], mixed, wo_bf, _row(norm_ffn[l]), wr_pad)
            sched = _moe_schedule(seg[:, 0, :N_EXPERTS])
            x_sorted = _dispatch_call(sched, hn_bf, route)
            y_sorted = _experts_call(sched, x_sorted, w_exp_in[l // 2], w_exp_out[l // 2])
            h = _combine_call(sched, h1, route, _row(norm_final), y_sorted)
    y_prompt = h[0].reshape(BATCH, SEQ, D_MODEL)
    y_sample = h[1].reshape(DEC_BATCH, DEC_SEQ, D_MODEL)
    def unpack_ssm(s_t, batch):
        return jnp.swapaxes(s_t.reshape(DEPTH, batch, SSD_HEADS, SSD_P, SSD_N), 3, 4)

    return (y_prompt, y_sample, jnp.stack(outs["hg_p"]), unpack_ssm(jnp.stack(outs["ssm_p"]), BATCH),
            jnp.stack(outs["conv_p"]), sample_states[0], unpack_ssm(sample_states[1], DEC_BATCH),
            jnp.stack(outs["conv_s"]))
```

```python
import functools

import jax
import jax.numpy as jnp
import numpy as np
from jax import lax
from jax.experimental import pallas as pl
from jax.experimental.pallas import tpu as pltpu

F32 = jnp.float32
BF16 = jnp.bfloat16

D_MODEL = 1024
BATCH = 8
SEQ = 2048
DEPTH = 2
DEC_BATCH = 128
DEC_SEQ = 8
HG_HEADS = 4
HG_D = 128
HG_W = HG_HEADS * HG_D
SSD_HEADS = 8
SSD_P = 64
SSD_N = 128
SSD_W = SSD_HEADS * SSD_P
SSD_PAIRS = SSD_HEADS // 2
CONV_W = 4
CONV_DIM = 1024
D_FF = 2816
N_EXPERTS = 8
D_FF_EXPERT = 3584
EPS = 1e-6

LANES = 128
C_Q, C_F, C_I, C_OG, C_Z, C_XBC, C_DT = 0, 512, 1024, 1536, 2048, 2560, 3584
PROJ_W = C_DT + LANES
HG_CHUNK = 64
SSD_CHUNK = 128
SAMPLE_SEQS = 8
PROMPT_SEQS = 4
HG_SAFE_RANGE = 80.0

T_PROMPT = BATCH * SEQ
T_SAMPLE = DEC_BATCH * DEC_SEQ
T_ALL = T_PROMPT + T_SAMPLE

SUBLANES = 8
TOP_K = 2
MOE_TOKENS = 256
MOE_TILES = T_ALL // MOE_TOKENS
ROUTER_TILES = 1
MOE_SLOTS = TOP_K * MOE_TOKENS + N_EXPERTS * SUBLANES
SEG_COPY_ROWS = (64, 32, 16, 8)
GMM_ROWS = 1024
GMM_SUB = 256
GMM_TILES = (TOP_K * T_ALL + MOE_TILES * N_EXPERTS * (SUBLANES - 1)
             + N_EXPERTS * (GMM_ROWS - 1)) // GMM_ROWS + 1

VMEM_LIMIT = 60000 * 1024


def _sigmoid(x):
    return 0.5 * jnp.tanh(0.5 * x) + 0.5


def _silu(x):
    return x * _sigmoid(x)


def _softplus(x):
    return jnp.maximum(x, 0.0) + jnp.log(1.0 + jnp.exp(-jnp.abs(x)))


def _rms(x, g):
    return x * lax.rsqrt(jnp.mean(x * x, axis=-1, keepdims=True) + EPS) * g


def _dot(a, b):
    return jnp.dot(a.astype(BF16), b.astype(BF16), preferred_element_type=F32)


def _dot_nt(a, b):
    return lax.dot_general(a.astype(BF16), b.astype(BF16), (((1,), (1,)), ((), ())),
                           preferred_element_type=F32)


def _split3(x):
    x1 = x.astype(BF16)
    r1 = x - x1.astype(F32)
    x2 = r1.astype(BF16)
    x3 = (r1 - x2.astype(F32)).astype(BF16)
    return x1, x2, x3


def _mask_dot(m, x):
    mb = m.astype(BF16)
    x1, x2, x3 = _split3(x)
    return (jnp.dot(mb, x1, preferred_element_type=F32) + jnp.dot(mb, x2, preferred_element_type=F32)
            + jnp.dot(mb, x3, preferred_element_type=F32))


def _mask_dot_nt(x, m):
    mb = m.astype(BF16)
    dn = (((1,), (1,)), ((), ()))
    x1, x2, x3 = _split3(x)
    return (lax.dot_general(x1, mb, dn, preferred_element_type=F32)
            + lax.dot_general(x2, mb, dn, preferred_element_type=F32)
            + lax.dot_general(x3, mb, dn, preferred_element_type=F32))


def _dot_f32x3(a, b):
    a1, a2, _ = _split3(a)
    b1, b2, _ = _split3(b)
    return (jnp.dot(a1, b1, preferred_element_type=F32) + jnp.dot(a1, b2, preferred_element_type=F32)
            + jnp.dot(a2, b1, preferred_element_type=F32))


def _iota2(shape, dim):
    return lax.broadcasted_iota(jnp.int32, shape, dim)


def _seq_masks(rows, seq_len):
    t = _iota2((rows, rows), 0)
    s = _iota2((rows, rows), 1)
    same = (t // seq_len) == (s // seq_len)
    causal = same & (s <= t)
    ref = same & ((s % seq_len) < seq_len // 2)
    return causal, ref, same


def _row_specs(arrs, tm):
    if len(arrs) == 1:
        return [pl.BlockSpec((tm, arrs[0].shape[1]), lambda i, *_: (i, 0))]
    npt = T_PROMPT // tm
    return [pl.BlockSpec((tm, arrs[0].shape[1]), lambda i, *_: (jnp.minimum(i, npt - 1), 0)),
            pl.BlockSpec((tm, arrs[1].shape[1]), lambda i, *_: (jnp.maximum(i - npt, 0), 0),
                         pipeline_mode=pl.Buffered(1))]


def _rows(i, refs):
    if len(refs) == 1:
        return refs[0][...]
    return jnp.where(i < T_PROMPT // refs[0].shape[0], refs[0][...], refs[1][...])


def _proj_body(n_h, *refs):
    h_refs, (g_ref, w_ref, o_ref, wb_ref) = refs[:n_h], refs[n_h:]
    i = pl.program_id(0)

    @pl.when(i == 0)
    def _():
        for r0 in range(0, C_DT, HG_W):
            wb_ref[r0:r0 + HG_W, :] = w_ref[r0:r0 + HG_W, :].astype(BF16)
        tail = w_ref[C_DT:, :]
        pad = jnp.zeros((LANES - tail.shape[0], D_MODEL), F32)
        wb_ref[C_DT:, :] = jnp.concatenate([tail, pad], axis=0).astype(BF16)

    hn = _rms(_rows(i, h_refs), g_ref[...])
    o_ref[...] = _dot_nt(hn, wb_ref[...])


def _proj_call(l, h, g, w_in):
    tm = 512
    w_t = jnp.swapaxes(w_in, 1, 2)
    return pl.pallas_call(
        functools.partial(_proj_body, len(h)),
        grid=(T_ALL // tm,),
        in_specs=_row_specs(h, tm) + [pl.BlockSpec((1, D_MODEL), lambda i: (0, 0)),
                                      pl.BlockSpec((None,) + w_t.shape[1:], lambda i: (l, 0, 0),
                                                   pipeline_mode=pl.Buffered(1))],
        out_specs=pl.BlockSpec((tm, PROJ_W), lambda i: (i, 0)),
        out_shape=jax.ShapeDtypeStruct((T_ALL, PROJ_W), F32),
        scratch_shapes=[pltpu.VMEM((PROJ_W, D_MODEL), BF16)],
        compiler_params=pltpu.CompilerParams(dimension_semantics=("arbitrary",),
                                             vmem_limit_bytes=VMEM_LIMIT),
        name="in_proj",
    )(*h, g, w_t)


def _hgrn_gates(p_q, p_f, la, lc, omlb):
    q = _silu(p_q)
    e = jnp.exp(-jnp.abs(p_f))
    b = lc + jnp.minimum(p_f, 0.0) - jnp.log(1.0 + e)
    logf = jnp.maximum(la, b) + jnp.log(1.0 + jnp.exp(-jnp.abs(la - b)))
    r = 1.0 / (1.0 + e)
    kin = omlb * jnp.where(p_f >= 0.0, e * r, r)
    return q, logf, kin


def _hgrn_decays(logf, causal, ref, same):
    rows = logf.shape[0]
    if ref is None:
        g = _mask_dot(causal, logf)
        return g, g[rows // 2 - 1:rows // 2, :], g[rows - 1:rows, :]
    m = jnp.concatenate([causal.astype(F32), ref.astype(F32), same.astype(F32)], axis=0)
    g3 = _mask_dot(m, logf)
    return g3[:rows], g3[rows:2 * rows], g3[2 * rows:]


def _hgrn_exact_intra(ex_ref, q, kin, g, v, seq_len):
    rows = q.shape[0]
    ex_ref[0] = g
    ex_ref[1] = q
    ex_ref[2] = kin
    ex_ref[3] = v
    s_idx = _iota2((rows, HG_W), 0)

    def body(t, carry):
        gt = ex_ref[0, pl.ds(t, 1), :]
        qt = ex_ref[1, pl.ds(t, 1), :]
        live = (s_idx <= t) & (s_idx // seq_len == t // seq_len)
        w = jnp.where(live, qt * ex_ref[2] * jnp.exp(jnp.minimum(gt - ex_ref[0], 0.0)), 0.0)
        outs = []
        for h in range(HG_HEADS):
            sl = slice(h * HG_D, (h + 1) * HG_D)
            score = jnp.sum(w[:, sl], axis=1, keepdims=True)
            outs.append(jnp.sum(score * ex_ref[3, :, sl], axis=0, keepdims=True))
        ex_ref[4, pl.ds(t, 1), :] = jnp.concatenate(outs, axis=1)
        return carry

    lax.fori_loop(0, rows, body, 0)
    return ex_ref[4]


def _hgrn_head_out(o, hgn, og):
    return _rms(o, hgn) * _silu(og)


def _ssd_inputs(conv, p_dt, dtb, a_neg):
    conv = _silu(conv)
    xs = conv[:, :SSD_W]
    bm = conv[:, SSD_W:SSD_W + 2 * SSD_N]
    cm = conv[:, SSD_W + 2 * SSD_N:]
    dtf = _softplus(p_dt + dtb)
    a = dtf * a_neg
    return xs, bm, cm, dtf, a


def _lane_bcast(x, lane, width):
    return jnp.broadcast_to(x[:, lane:lane + 1], (x.shape[0], width))


def _ssd_finish(y, xs, z, dsk, ssn):
    y = (y + dsk * xs) * _silu(z)
    half = SSD_W // 2
    return jnp.concatenate([_rms(y[:, :half], ssn[:, :half]), _rms(y[:, half:], ssn[:, half:])],
                           axis=1)


def _scan_prompt_body(*refs):
    ns = PROMPT_SEQS
    p_refs, refs = refs[:ns], refs[ns:]
    (la_ref, lc_ref, omlb_ref), params = refs[:3], refs[3:10]
    mix_ref, ohg_ref, ossm_ref, oconv_ref, st_ref, hp_ref, xpad_ref, ex_ref = refs[10:]
    j = pl.program_id(1)
    tl = SSD_CHUNK

    @pl.when(j == 0)
    def _():
        st_ref[...] = jnp.zeros_like(st_ref)
        hp_ref[...] = jnp.zeros_like(hp_ref)
        xpad_ref[:, 0:8, :] = jnp.zeros((ns, 8, CONV_DIM), F32)

    gates, worst = [], None
    for s in range(ns):
        gates.append([])
        for c in range(tl // HG_CHUNK):
            rows = slice(c * HG_CHUNK, (c + 1) * HG_CHUNK)
            gt = _hgrn_gates(p_refs[s][rows, C_Q:C_Q + HG_W], p_refs[s][rows, C_F:C_F + HG_W],
                             la_ref[...], lc_ref[...], omlb_ref[...])
            gates[s].append(gt)
            half = HG_CHUNK // 2
            for part in (gt[1][:half], gt[1][half:]):
                tot = jnp.sum(part, axis=0, keepdims=True)
                worst = tot if worst is None else jnp.minimum(worst, tot)
    safe = jnp.min(worst) >= -HG_SAFE_RANGE

    for exact in (False, True):
        @pl.when(safe != exact)
        def _(exact=exact):
            for s in range(ns):
                _scan_prompt_tile(gates[s], exact, ex_ref, p_refs[s], *params, mix_ref.at[s],
                                  st_ref.at[s], hp_ref.at[s], xpad_ref.at[s])

    @pl.when(j == pl.num_programs(1) - 1)
    def _():
        for s in range(ns):
            oconv_ref[s] = xpad_ref[s, 5:8, :]
            for h in range(HG_HEADS):
                ohg_ref[s, h] = st_ref[s, h].T
        ossm_ref[...] = hp_ref[...]


def _scan_prompt_tile(gates, exact, ex_ref, p_ref, hgn_ref, cw_ref, cb_ref, dtb_ref, an_ref,
                      dsk_ref, ssn_ref, mix_ref, st_ref, hp_ref, xpad_ref):
    tl = SSD_CHUNK
    causal, _, _ = _seq_masks(HG_CHUNK, HG_CHUNK)
    for c in range(tl // HG_CHUNK):
        r0 = c * HG_CHUNK
        rows = slice(r0, r0 + HG_CHUNK)
        q, logf, kin = gates[c]
        v = p_ref[rows, C_I:C_I + HG_W]
        g, gmid, glast = _hgrn_decays(logf, causal, None, None)
        if exact:
            qh = q * jnp.exp(g)
            kd = kin * jnp.exp(glast - g)
            o_intra = _hgrn_exact_intra(ex_ref, q, kin, g, v, HG_CHUNK)
        else:
            qt = q * jnp.exp(g - gmid)
            kt = kin * jnp.exp(gmid - g)
            qh = qt * jnp.exp(gmid)
            kd = kt * jnp.exp(glast - gmid)
        ds = jnp.exp(glast[0:1, :])
        for h in range(HG_HEADS):
            sl = slice(h * HG_D, (h + 1) * HG_D)
            st = st_ref[h]
            if exact:
                o = o_intra[:, sl] + _dot_nt(qh[:, sl], st)
            else:
                sc = jnp.where(causal, _dot_nt(qt[:, sl], kt[:, sl]), 0.0)
                o = _dot(sc, v[:, sl]) + _dot_nt(qh[:, sl], st)
            st_ref[h] = st * ds[:, sl] + _dot(v[:, sl].T, kd[:, sl])
            og = p_ref[rows, C_OG + h * HG_D:C_OG + (h + 1) * HG_D]
            mix_ref[rows, sl] = _hgrn_head_out(o, hgn_ref[:, sl], og).astype(BF16)

    xpad_ref[8:8 + tl, :] = p_ref[:, C_XBC:C_XBC + CONV_DIM]
    conv = cb_ref[...] + cw_ref[0:1, :] * xpad_ref[5:5 + tl, :]
    for tap in range(1, CONV_W):
        conv = conv + cw_ref[tap:tap + 1, :] * xpad_ref[5 + tap:5 + tap + tl, :]
    xpad_ref[0:8, :] = xpad_ref[tl:tl + 8, :]

    xs, bm, cm, dtf, a = _ssd_inputs(conv, p_ref[:, C_DT:C_DT + LANES], dtb_ref[...], an_ref[...])
    tri, _, _ = _seq_masks(tl, tl)
    trif = tri.astype(F32)
    acum = _mask_dot(trif, a)
    acum_t = _mask_dot_nt(a.T, trif)
    lo = _iota2((tl, LANES), 1) < SSD_P
    top = _iota2((LANES, tl), 0) < SSD_P
    ys = []
    for pair in range(SSD_PAIRS):
        grp = pair // 2
        cg = cm[:, grp * SSD_N:(grp + 1) * SSD_N]
        bg = bm[:, grp * SSD_N:(grp + 1) * SSD_N]
        cb = _dot_nt(cg, bg)
        psl = slice(pair * LANES, (pair + 1) * LANES)
        r0, r1 = 2 * pair, 2 * pair + 1
        xp = xs[:, psl] * jnp.where(lo, _lane_bcast(dtf, r0, LANES), _lane_bcast(dtf, r1, LANES))
        intra, einter, wrow, ea = [], [], [], []
        for r in (r0, r1):
            col = _lane_bcast(acum, r, tl)
            row = jnp.broadcast_to(acum_t[r:r + 1, :], (tl, tl))
            dec = jnp.exp(jnp.where(tri, col - row, -jnp.inf))
            intra.append(_dot(cb * dec, xp))
            einter.append(jnp.exp(col))
            alast = acum_t[r:r + 1, tl - 1:tl]
            wrow.append(jnp.exp(alast - acum_t[r:r + 1, :]))
            ea.append(jnp.exp(alast))
        hp = hp_ref[pair]
        y = jnp.where(lo, intra[0], intra[1]) + jnp.where(lo, einter[0], einter[1]) * _dot_nt(cg, hp)
        xw_t = xp.T * jnp.where(top, wrow[0], wrow[1])
        hp_ref[pair] = jnp.where(top, ea[0], ea[1]) * hp + _dot(xw_t, bg)
        ys.append(y)
    y = jnp.concatenate(ys, axis=1)
    y = _ssd_finish(y, xs, p_ref[:, C_Z:C_Z + SSD_W], dsk_ref[...], ssn_ref[...])
    mix_ref[:, HG_W:] = y.astype(BF16)


def _row_spec(width):
    return pl.BlockSpec((1, width), lambda b, j: (0, 0))


def _scan_prompt_call(proj, prm):
    tl = SSD_CHUNK
    nt = SEQ // tl
    ns = PROMPT_SEQS
    state_blk = (ns, HG_HEADS, HG_D, HG_D)
    in_specs = [pl.BlockSpec((tl, PROJ_W), lambda b, j, s=s: ((b * ns + s) * nt + j, 0))
                for s in range(ns)]
    in_specs += [_row_spec(HG_W), _row_spec(HG_W), _row_spec(HG_W), _row_spec(HG_W),
                 pl.BlockSpec((CONV_W, CONV_DIM), lambda b, j: (0, 0)), _row_spec(CONV_DIM),
                 _row_spec(LANES), _row_spec(LANES), _row_spec(SSD_W), _row_spec(SSD_W)]
    out_specs = [pl.BlockSpec((ns, tl, D_MODEL), lambda b, j: (b, j, 0)),
                 pl.BlockSpec(state_blk, lambda b, j: (b, 0, 0, 0)),
                 pl.BlockSpec(state_blk, lambda b, j: (b, 0, 0, 0)),
                 pl.BlockSpec((ns, CONV_W - 1, CONV_DIM), lambda b, j: (b, 0, 0))]
    out_shape = [jax.ShapeDtypeStruct((BATCH, SEQ, D_MODEL), BF16),
                 jax.ShapeDtypeStruct((BATCH,) + state_blk[1:], F32),
                 jax.ShapeDtypeStruct((BATCH,) + state_blk[1:], F32),
                 jax.ShapeDtypeStruct((BATCH, CONV_W - 1, CONV_DIM), F32)]
    mixed, hg, ssm_t, conv = pl.pallas_call(
        _scan_prompt_body,
        grid=(BATCH // ns, nt),
        in_specs=in_specs, out_specs=out_specs, out_shape=out_shape,
        scratch_shapes=[pltpu.VMEM(state_blk, F32), pltpu.VMEM(state_blk, F32),
                        pltpu.VMEM((ns, tl + 8, CONV_DIM), F32),
                        pltpu.VMEM((5, HG_CHUNK, HG_W), F32)],
        compiler_params=pltpu.CompilerParams(dimension_semantics=("arbitrary", "arbitrary"),
                                             vmem_limit_bytes=VMEM_LIMIT),
        name="scan_prompt",
    )(*([proj] * ns), prm["la"], prm["lc"], prm["omlb"], prm["hgn"], prm["cw"], prm["cb"],
      prm["dtb"], prm["an"], prm["dsk"], prm["ssn"])
    return mixed.reshape(T_PROMPT, D_MODEL), hg, ssm_t, conv


def _scan_sample_body(n_prev, p_ref, shg_ref, sssm_ref, sconv_ref, la_ref, lc_ref, omlb_ref,
                      hgn_ref, cw_ref, cb_ref, dtb_ref, an_ref, dsk_ref, ssn_ref, *refs):
    if n_prev:
        phg_ref, pssm_ref = refs[:2]
        refs = refs[2:]
    mix_ref, ohg_ref, ossm_ref, oconv_ref, xpad_ref, ex_ref = refs
    if n_prev:
        ohg_ref[0:n_prev] = phg_ref[...]
        ossm_ref[0:n_prev] = pssm_ref[...]
    nb, sl_len = SAMPLE_SEQS, DEC_SEQ
    rows = nb * sl_len
    causal, ref, same = _seq_masks(rows, sl_len)
    rowseq = _iota2((rows, LANES), 0) // sl_len

    q, logf, kin = _hgrn_gates(p_ref[:, C_Q:C_Q + HG_W], p_ref[:, C_F:C_F + HG_W],
                               la_ref[...], lc_ref[...], omlb_ref[...])
    v = p_ref[:, C_I:C_I + HG_W]
    g, gmid, glast = _hgrn_decays(logf, causal, ref, same)
    safe = jnp.max(jnp.abs(g - gmid)) <= HG_SAFE_RANGE

    @pl.when(safe)
    def _():
        qt = q * jnp.exp(g - gmid)
        kt = kin * jnp.exp(gmid - g)
        for h in range(HG_HEADS):
            sl = slice(h * HG_D, (h + 1) * HG_D)
            sc = jnp.where(causal, _dot_nt(qt[:, sl], kt[:, sl]), 0.0)
            ex_ref[4, :, sl] = _dot(sc, v[:, sl])

    @pl.when(jnp.logical_not(safe))
    def _():
        _hgrn_exact_intra(ex_ref, q, kin, g, v, sl_len)

    qh = q * jnp.exp(g)
    kd = kin * jnp.exp(glast - g)
    ds_t = jnp.exp(glast).T
    kd_t = kd.T
    for h in range(HG_HEADS):
        sl = slice(h * HG_D, (h + 1) * HG_D)
        o_intra = ex_ref[4, :, sl]
        o_inter = []
        for b in range(nb):
            s_old = shg_ref[b, h]
            o_inter.append(_dot(qh[b * sl_len:(b + 1) * sl_len, sl], s_old))
            vb = jnp.where(rowseq == b, v[:, sl], 0.0)
            dcol = _lane_bcast(ds_t[sl, :], b * sl_len, HG_D)
            ohg_ref[n_prev, b, h] = dcol * s_old + _dot(kd_t[sl, :], vb)
        o = o_intra + jnp.concatenate(o_inter, axis=0)
        og = p_ref[:, C_OG + h * HG_D:C_OG + (h + 1) * HG_D]
        mix_ref[:, sl] = _hgrn_head_out(o, hgn_ref[:, sl], og).astype(BF16)

    convs = []
    for b in range(nb):
        base = 16 * b
        xpad_ref[base + 5:base + 8, :] = sconv_ref[:, b, :]
        xpad_ref[base + 8:base + 16, :] = p_ref[b * sl_len:(b + 1) * sl_len, C_XBC:C_XBC + CONV_DIM]
        cv = cb_ref[...] + cw_ref[0:1, :] * xpad_ref[base + 5:base + 13, :]
        for tap in range(1, CONV_W):
            cv = cv + cw_ref[tap:tap + 1, :] * xpad_ref[base + 5 + tap:base + 13 + tap, :]
        convs.append(cv)
        oconv_ref[:, b, :] = xpad_ref[base + 13:base + 16, :]
    conv = jnp.concatenate(convs, axis=0)

    xs, bm, cm, dtf, a = _ssd_inputs(conv, p_ref[:, C_DT:C_DT + LANES], dtb_ref[...], an_ref[...])
    causf = causal.astype(F32)
    acum = _mask_dot(causf, a)
    a_t = a.T
    acum_t = _mask_dot_nt(a_t, causf)
    atot_t = _mask_dot_nt(a_t, same.astype(F32))
    wall_t = jnp.exp(atot_t - acum_t)
    eatot_t = jnp.exp(atot_t)
    eall = jnp.exp(acum)
    lo = _iota2((rows, LANES), 1) < SSD_P
    top = _iota2((LANES, rows), 0) < SSD_P
    top_sq = _iota2((LANES, SSD_N), 0) < SSD_P
    colseq = _iota2((LANES, rows), 1) // sl_len
    ys = []
    for pair in range(SSD_PAIRS):
        grp = pair // 2
        cg = cm[:, grp * SSD_N:(grp + 1) * SSD_N]
        bg = bm[:, grp * SSD_N:(grp + 1) * SSD_N]
        cb = _dot_nt(cg, bg)
        psl = slice(pair * LANES, (pair + 1) * LANES)
        r0, r1 = 2 * pair, 2 * pair + 1
        xp = xs[:, psl] * jnp.where(lo, _lane_bcast(dtf, r0, LANES), _lane_bcast(dtf, r1, LANES))
        xw_t = xp.T * jnp.where(top, wall_t[r0:r0 + 1, :], wall_t[r1:r1 + 1, :])
        intra = []
        for r in (r0, r1):
            col = _lane_bcast(acum, r, rows)
            row = jnp.broadcast_to(acum_t[r:r + 1, :], (rows, rows))
            dec = jnp.exp(jnp.where(causal, col - row, -jnp.inf))
            intra.append(_dot(cb * dec, xp))
        y_intra = jnp.where(lo, intra[0], intra[1])
        y_inter = []
        for b in range(nb):
            tr = slice(b * sl_len, (b + 1) * sl_len)
            h_old = sssm_ref[b, pair]
            y_inter.append(_dot_nt(cg[tr, :], h_old))
            upd = _dot(jnp.where(colseq == b, xw_t, 0.0), bg)
            c0 = b * sl_len
            ea = jnp.where(top_sq, eatot_t[r0:r0 + 1, c0:c0 + 1], eatot_t[r1:r1 + 1, c0:c0 + 1])
            ossm_ref[n_prev, b, pair] = ea * h_old + upd
        e_pair = jnp.where(lo, _lane_bcast(eall, r0, LANES), _lane_bcast(eall, r1, LANES))
        ys.append(y_intra + e_pair * jnp.concatenate(y_inter, axis=0))
    y = jnp.concatenate(ys, axis=1)
    y = _ssd_finish(y, xs, p_ref[:, C_Z:C_Z + SSD_W], dsk_ref[...], ssn_ref[...])
    mix_ref[:, HG_W:] = y.astype(BF16)


def _scan_sample_call(l, proj, s_hg, s_ssm_t, s_conv, prm, prev):
    nb = SAMPLE_SEQS
    rows = nb * DEC_SEQ
    row0 = T_PROMPT // rows
    c0 = lambda i: (0, 0)
    rs = lambda w: pl.BlockSpec((1, w), c0)
    state_blk = (nb, HG_HEADS, HG_D, HG_D)
    in_specs = [pl.BlockSpec((rows, PROJ_W), lambda i: (row0 + i, 0)),
                pl.BlockSpec((None,) + state_blk, lambda i: (l, i, 0, 0, 0)),
                pl.BlockSpec((None,) + state_blk, lambda i: (l, i, 0, 0, 0)),
                pl.BlockSpec((None, CONV_W - 1, nb, CONV_DIM), lambda i: (l, 0, i, 0)),
                rs(HG_W), rs(HG_W), rs(HG_W), rs(HG_W),
                pl.BlockSpec((CONV_W, CONV_DIM), c0), rs(CONV_DIM),
                rs(LANES), rs(LANES), rs(SSD_W), rs(SSD_W)]
    in_specs += [pl.BlockSpec((l,) + state_blk, lambda i: (0, i, 0, 0, 0))] * (2 if l else 0)
    out_specs = [pl.BlockSpec((rows, D_MODEL), lambda i: (i, 0)),
                 pl.BlockSpec((l + 1,) + state_blk, lambda i: (0, i, 0, 0, 0)),
                 pl.BlockSpec((l + 1,) + state_blk, lambda i: (0, i, 0, 0, 0)),
                 pl.BlockSpec((CONV_W - 1, nb, CONV_DIM), lambda i: (0, i, 0))]
    stacked =jax.ShapeDtypeStruct((l + 1, DEC_BATCH) + state_blk[1:], F32)
    out_shape = [jax.ShapeDtypeStruct((T_SAMPLE, D_MODEL), BF16), stacked, stacked,
                 jax.ShapeDtypeStruct((CONV_W - 1, DEC_BATCH, CONV_DIM), F32)]
    return pl.pallas_call(
        functools.partial(_scan_sample_body, l),
        grid=(DEC_BATCH // nb,),
        in_specs=in_specs, out_specs=out_specs, out_shape=out_shape,
        scratch_shapes=[pltpu.VMEM((16 * nb, CONV_DIM), F32), pltpu.VMEM((5, rows, HG_W), F32)],
        compiler_params=pltpu.CompilerParams(dimension_semantics=("arbitrary",),
                                             vmem_limit_bytes=VMEM_LIMIT),
        name="scan_sample",
    )(proj, s_hg, s_ssm_t, s_conv, prm["la"], prm["lc"], prm["omlb"], prm["hgn"], prm["cw"],
      prm["cb"], prm["dtb"], prm["an"], prm["dsk"], prm["ssn"], *(prev if l else ()))


def _out_ffn_body(n_h, *refs):
    h_refs, mix_refs = refs[:n_h], refs[n_h:n_h + 2]
    wo_ref, g_ref, wg_ref, wu_ref, wd_ref, o_ref, hn_ref = refs[n_h + 2:]
    i = pl.program_id(0)
    f = pl.program_id(1)

    @pl.when(f == 0)
    def _():
        h1 = _rows(i, h_refs) + _dot(_rows(i, mix_refs), wo_ref[...])
        o_ref[...] = h1
        hn_ref[...] = _rms(h1, g_ref[...]).astype(BF16)

    o_ref[...] += _swiglu_part(hn_ref[...], wg_ref[...], wu_ref[...], wd_ref[...])


def _out_ffn_call(h, mixed, wo_bf, g, wi_bf, wd_bf):
    tm, tf = 1024, D_FF // 2
    nf = D_FF // tf
    return pl.pallas_call(
        functools.partial(_out_ffn_body, len(h)),
        grid=(T_ALL // tm, nf),
        in_specs=_row_specs(h, tm) + _row_specs(mixed, tm) + [
                  pl.BlockSpec((D_MODEL, D_MODEL), lambda i, f: (0, 0)),
                  pl.BlockSpec((1, D_MODEL), lambda i, f: (0, 0)),
                  pl.BlockSpec((D_MODEL, tf), lambda i, f: (0, f)),
                  pl.BlockSpec((D_MODEL, tf), lambda i, f: (0, nf + f)),
                  pl.BlockSpec((tf, D_MODEL), lambda i, f: (f, 0))],
        out_specs=pl.BlockSpec((tm, D_MODEL), lambda i, f: (i, 0)),
        out_shape=jax.ShapeDtypeStruct((T_ALL, D_MODEL), F32),
        scratch_shapes=[pltpu.VMEM((tm, D_MODEL), BF16)],
        compiler_params=pltpu.CompilerParams(dimension_semantics=("arbitrary", "arbitrary"),
                                             vmem_limit_bytes=VMEM_LIMIT),
        name="out_ffn",
    )(*h, *mixed, wo_bf, g, wi_bf, wi_bf, wd_bf)


def _out_router_body(h_ref, mixp_ref, mixs_ref, wo_ref, g_ref, wr_ref,
                     h1_ref, hn_ref, route_ref, seg_ref):
    h1 = h_ref[...] + _dot(_rows(pl.program_id(0), (mixp_ref, mixs_ref)), wo_ref[...])
    h1_ref[...] = h1
    hn = _rms(h1, g_ref[...])
    hn_ref[...] = hn.astype(BF16)
    logits = _dot_f32x3(hn, wr_ref[...])
    lane = _iota2(logits.shape, 1)
    lg = jnp.where(lane < N_EXPERTS, logits, -jnp.inf)
    m1 = jnp.max(lg, axis=1, keepdims=True)
    i1 = jnp.min(jnp.where(lg == m1, lane, LANES), axis=1, keepdims=True)
    lg2 = jnp.where(lane == i1, -jnp.inf, lg)
    m2 = jnp.max(lg2, axis=1, keepdims=True)
    i2 = jnp.min(jnp.where(lg2 == m2, lane, LANES), axis=1, keepdims=True)
    e2 = jnp.exp(m2 - m1)
    g1 = 1.0 / (1.0 + e2)
    g2 = e2 / (1.0 + e2)

    tm = logits.shape[0]
    chosen = jnp.where((lane == i1) | (lane == i2), 1.0, 0.0)
    t_row, t_col = _iota2((tm, tm), 0), _iota2((tm, tm), 1)
    earlier = ((t_col < t_row) & (t_col // MOE_TOKENS == t_row // MOE_TOKENS)).astype(F32)
    rank = _dot(earlier, chosen)
    below = (_iota2((LANES, LANES), 0) < _iota2((LANES, LANES), 1)).astype(F32)
    pos = []
    for k in range(tm // MOE_TOKENS):
        rows = slice(k * MOE_TOKENS, (k + 1) * MOE_TOKENS)
        cnt = jnp.sum(chosen[rows], axis=0, keepdims=True)
        seg = jnp.floor((cnt + (SUBLANES - 1)) * (1.0 / SUBLANES)) * SUBLANES
        seg8 = jnp.broadcast_to(seg, (SUBLANES, LANES))
        pos.append(_dot(seg8, below)[0:1, :] + rank[rows])
        seg_ref[k] = seg8.astype(jnp.int32)
    pos = jnp.concatenate(pos, axis=0)
    p1 = jnp.sum(jnp.where(lane == i1, pos, 0.0), axis=1, keepdims=True)
    p2 = jnp.sum(jnp.where(lane == i2, pos, 0.0), axis=1, keepdims=True)
    route_ref[...] = jnp.where(lane == 0, p1, jnp.where(lane == 1, p2, jnp.where(
        lane == 2, g1, jnp.where(lane == 3, g2, 0.0))))


def _out_router_call(h, mixed, wo_bf, g, wr_pad):
    tm = ROUTER_TILES * MOE_TOKENS
    return pl.pallas_call(
        _out_router_body,
        grid=(T_ALL // tm,),
        in_specs=[pl.BlockSpec((tm, D_MODEL), lambda i: (i, 0))] + _row_specs(mixed, tm) + [
                  pl.BlockSpec((D_MODEL, D_MODEL), lambda i: (0, 0)),
                  pl.BlockSpec((1, D_MODEL), lambda i: (0, 0)),
                  pl.BlockSpec((D_MODEL, LANES), lambda i: (0, 0))],
        out_specs=[pl.BlockSpec((tm, D_MODEL), lambda i: (i, 0)),
                   pl.BlockSpec((tm, D_MODEL), lambda i: (i, 0)),
                   pl.BlockSpec((tm, LANES), lambda i: (i, 0)),
                   pl.BlockSpec((ROUTER_TILES, SUBLANES, LANES), lambda i: (i, 0, 0))],
        out_shape=[jax.ShapeDtypeStruct((T_ALL, D_MODEL), F32),
                   jax.ShapeDtypeStruct((T_ALL, D_MODEL), BF16),
                   jax.ShapeDtypeStruct((T_ALL, LANES), F32),
                   jax.ShapeDtypeStruct((MOE_TILES, SUBLANES, LANES), jnp.int32)],
        compiler_params=pltpu.CompilerParams(dimension_semantics=("arbitrary",),
                                             vmem_limit_bytes=VMEM_LIMIT),
        name="out_router",
    )(h, *mixed, wo_bf, g, wr_pad)


def _row_tile_copy(tile_ref, hbm_ref, tile_row, hbm_row, sem, to_hbm, rows=SUBLANES):
    t = tile_ref.at[pl.ds(pl.multiple_of(tile_row, SUBLANES), rows), :]
    g = hbm_ref.at[pl.ds(pl.multiple_of(hbm_row, SUBLANES), rows), :]
    return pltpu.make_async_copy(t, g, sem) if to_hbm else pltpu.make_async_copy(g, t, sem)


def _seg_plan(n):
    counts = [lax.shift_right_logical(n, 3)]
    for shift in (2, 1, 0):
        counts.append(lax.shift_right_logical(n, shift) & 1)
    return counts


def _seg_copies(i, n_ref, loc_ref, start_ref, tile_ref, hbm_ref, sem, to_hbm):
    for e in range(N_EXPERTS):
        k = i * N_EXPERTS + e
        lo = loc_ref[k]
        st = start_ref[k]
        counts = _seg_plan(n_ref[k])
        big = SEG_COPY_ROWS[0]

        def body(c, carry, lo=lo, st=st):
            _row_tile_copy(tile_ref, hbm_ref, lo + c * big, st + c * big, sem, to_hbm, big).start()
            return carry

        lax.fori_loop(0, counts[0], body, 0)
        done = counts[0] * big
        for rows, cnt in zip(SEG_COPY_ROWS[1:], counts[1:]):
            @pl.when(cnt == 1)
            def _(rows=rows, done=done, lo=lo, st=st):
                _row_tile_copy(tile_ref, hbm_ref, lo + done, st + done, sem, to_hbm, rows).start()

            done = done + cnt * rows


def _seg_wait(i, n_ref, tile_ref, hbm_ref, sem, to_hbm):
    totals = None
    for e in range(N_EXPERTS):
        counts = _seg_plan(n_ref[i * N_EXPERTS + e])
        totals = counts if totals is None else [a + b for a, b in zip(totals, counts)]
    for rows, total in zip(SEG_COPY_ROWS, totals):
        def body(c, carry, rows=rows):
            _row_tile_copy(tile_ref, hbm_ref, 0, 0, sem, to_hbm, rows).wait()
            return carry

        lax.fori_loop(0, total, body, 0)


def _dispatch_body(n_ref, loc_ref, start_ref, zn_ref, zstart_ref, zb_ref, zbn_ref,
                   hn_ref, route_ref, xs_ref, stage_ref, zero_ref, sem):
    i = pl.program_id(0)
    last = pl.num_programs(0) - 1
    slot = i % 2

    @pl.when(i >= 2)
    def _():
        _seg_wait(i - 2, n_ref, stage_ref.at[slot], xs_ref, sem.at[slot], True)

    rt = route_ref[...].T
    s = _iota2((MOE_SLOTS, MOE_TOKENS), 0).astype(F32)
    perm = jnp.where((s == rt[0:1, :]) | (s == rt[1:2, :]), 1.0, 0.0)
    stage_ref[slot] = _dot(perm, hn_ref[...])
    _seg_copies(i, n_ref, loc_ref, start_ref, stage_ref.at[slot], xs_ref, sem.at[slot], True)

    @pl.when(i == last)
    def _():
        zero_ref[...] = jnp.zeros_like(zero_ref)

        def blk_copy(b):
            return pltpu.make_async_copy(
                zero_ref, xs_ref.at[pl.ds(pl.multiple_of(b * GMM_SUB, GMM_SUB), GMM_SUB), :],
                sem.at[3])

        def bbody(b, carry):
            @pl.when(zb_ref[b] == 1)
            def _():
                blk_copy(b).start()
            return carry

        lax.fori_loop(0, GMM_TILES * GMM_ROWS // GMM_SUB, bbody, 0)
        nz = zn_ref[0]
        for e in range(N_EXPERTS):
            st = zstart_ref[e]

            def body(c, carry, st=st):
                _row_tile_copy(zero_ref, xs_ref, 0, st + c * SUBLANES, sem.at[2], True).start()
                return carry

            lax.fori_loop(0, zn_ref[e], body, 0)
            if e:
                nz = nz + zn_ref[e]

        @pl.when(last >= 1)
        def _():
            _seg_wait(i - 1, n_ref, stage_ref.at[1 - slot], xs_ref, sem.at[1 - slot], True)

        _seg_wait(i, n_ref, stage_ref.at[slot], xs_ref, sem.at[slot], True)

        def zbody(c, carry):
            _row_tile_copy(zero_ref, xs_ref, 0, 0, sem.at[2], True).wait()
            return carry

        lax.fori_loop(0, nz, zbody, 0)

        def bwait(c, carry):
            blk_copy(0).wait()
            return carry

        lax.fori_loop(0, zbn_ref[0], bwait, 0)


def _dispatch_call(sched, hn_bf, route):
    grid_spec = pltpu.PrefetchScalarGridSpec(
        num_scalar_prefetch=7,
        grid=(MOE_TILES,),
        in_specs=[pl.BlockSpec((MOE_TOKENS, D_MODEL), lambda i, *_: (i, 0)),
                  pl.BlockSpec((MOE_TOKENS, LANES), lambda i, *_: (i, 0))],
        out_specs=pl.BlockSpec(memory_space=pl.ANY),
        scratch_shapes=[pltpu.VMEM((2, MOE_SLOTS, D_MODEL), F32),
                        pltpu.VMEM((GMM_SUB, D_MODEL), F32),
                        pltpu.SemaphoreType.DMA((4,))])
    return pl.pallas_call(
        _dispatch_body,
        grid_spec=grid_spec,
        out_shape=jax.ShapeDtypeStruct((GMM_TILES * GMM_ROWS, D_MODEL), F32),
        compiler_params=pltpu.CompilerParams(dimension_semantics=("arbitrary",),
                                             vmem_limit_bytes=VMEM_LIMIT),
        name="moe_dispatch",
    )(sched["nch"], sched["loc"], sched["start"], sched["zn"], sched["zstart"], sched["zb"],
      sched["zbn"], hn_bf, route)


def _swiglu_part(x, wg, wu, wd):
    gate = jnp.dot(x, wg, preferred_element_type=F32)
    up = jnp.dot(x, wu, preferred_element_type=F32)
    return jnp.dot((_silu(gate) * up).astype(BF16), wd, preferred_element_type=F32)


def _experts_body(te_ref, tv_ref, nu_ref, x_ref, wg_ref, wu_ref, wd_ref, o_ref,
                  wgb_ref, wub_ref, wdb_ref):
    del te_ref, nu_ref
    i = pl.program_id(0)
    f = pl.program_id(1)
    nv = tv_ref[i]

    @pl.when(f == 0)
    def _():
        o_ref[...] = jnp.zeros_like(o_ref)

    @pl.when(nv == GMM_ROWS)
    def _():
        o_ref[...] += _swiglu_part(x_ref[...].astype(BF16), wg_ref[0].astype(BF16),
                                   wu_ref[0].astype(BF16), wd_ref[0].astype(BF16))

    @pl.when((nv > 0) & (nv < GMM_ROWS))
    def _():
        wgb_ref[...] = wg_ref[0].astype(BF16)
        wub_ref[...] = wu_ref[0].astype(BF16)
        wdb_ref[...] = wd_ref[0].astype(BF16)
        for sub in range(GMM_ROWS // GMM_SUB):
            rows = slice(sub * GMM_SUB, (sub + 1) * GMM_SUB)

            @pl.when(sub * GMM_SUB < nv)
            def _(rows=rows):
                o_ref[rows, :] += _swiglu_part(x_ref[rows, :].astype(BF16), wgb_ref[...],
                                               wub_ref[...], wdb_ref[...])


def _experts_call(sched, x_sorted, w_in_e, w_out_e):
    tf = 512
    nf = D_FF_EXPERT // tf

    def used(i, nu):
        return jnp.maximum(jnp.minimum(i, nu[0] - 1), 0)

    def fidx(i, f, nu):
        return jnp.where(i < nu[0], f, nf - 1)

    grid_spec = pltpu.PrefetchScalarGridSpec(
        num_scalar_prefetch=3,
        grid=(GMM_TILES, nf),
        in_specs=[pl.BlockSpec((GMM_ROWS, D_MODEL), lambda i, f, te, tv, nu: (used(i, nu), 0)),
                  pl.BlockSpec((1, D_MODEL, tf), lambda i, f, te, tv, nu: (te[i], 0, fidx(i, f, nu))),
                  pl.BlockSpec((1, D_MODEL, tf),
                               lambda i, f, te, tv, nu: (te[i], 0, nf + fidx(i, f, nu))),
                  pl.BlockSpec((1, tf, D_MODEL), lambda i, f, te, tv, nu: (te[i], fidx(i, f, nu), 0))],
        out_specs=pl.BlockSpec((GMM_ROWS, D_MODEL), lambda i, f, te, tv, nu: (i, 0)),
        scratch_shapes=[pltpu.VMEM((D_MODEL, tf), BF16), pltpu.VMEM((D_MODEL, tf), BF16),
                        pltpu.VMEM((tf, D_MODEL), BF16)])
    return pl.pallas_call(
        _experts_body,
        grid_spec=grid_spec,
        out_shape=jax.ShapeDtypeStruct((GMM_TILES * GMM_ROWS, D_MODEL), F32),
        compiler_params=pltpu.CompilerParams(dimension_semantics=("arbitrary", "arbitrary"),
                                             vmem_limit_bytes=VMEM_LIMIT),
        name="moe_experts",
    )(sched["te"], sched["tv"], sched["nu"], x_sorted, w_in_e, w_in_e, w_out_e)


def _combine_body(n_ref, loc_ref, start_ref, h1_ref, route_ref, gf_ref, ys_ref, op_ref, os_ref,
                  buf_ref, sem):
    i = pl.program_id(0)
    last = pl.num_programs(0) - 1
    slot = i % 2

    @pl.when(i == 0)
    def _():
        buf_ref[...] = jnp.zeros_like(buf_ref)
        _seg_copies(0, n_ref, loc_ref, start_ref, buf_ref.at[0], ys_ref, sem.at[0], False)

    @pl.when(i < last)
    def _():
        _seg_copies(i + 1, n_ref, loc_ref, start_ref, buf_ref.at[1 - slot], ys_ref,
                    sem.at[1 - slot], False)

    _seg_wait(i, n_ref, buf_ref.at[slot], ys_ref, sem.at[slot], False)

    route = route_ref[...]
    s = _iota2((MOE_TOKENS, MOE_SLOTS), 1).astype(F32)
    rows = buf_ref[slot]
    y1 = _dot(jnp.where(s == route[:, 0:1], 1.0, 0.0), rows)
    y2 = _dot(jnp.where(s == route[:, 1:2], 1.0, 0.0), rows)
    y = _rms(h1_ref[...] + route[:, 2:3] * y1 + route[:, 3:4] * y2, gf_ref[...])

    @pl.when(i < T_PROMPT // MOE_TOKENS)
    def _():
        op_ref[...] = y

    @pl.when(i >= T_PROMPT // MOE_TOKENS)
    def _():
        os_ref[...] = y


def _combine_call(sched, h1, route, gfin, y_sorted):
    npt = T_PROMPT // MOE_TOKENS
    grid_spec = pltpu.PrefetchScalarGridSpec(
        num_scalar_prefetch=3,
        grid=(MOE_TILES,),
        in_specs=[pl.BlockSpec((MOE_TOKENS, D_MODEL), lambda i, *_: (i, 0)),
                  pl.BlockSpec((MOE_TOKENS, LANES), lambda i, *_: (i, 0)),
                  pl.BlockSpec((1, D_MODEL), lambda i, *_: (0, 0)),
                  pl.BlockSpec(memory_space=pl.ANY)],
        out_specs=[pl.BlockSpec((MOE_TOKENS, D_MODEL), lambda i, *_: (jnp.minimum(i, npt - 1), 0)),
                   pl.BlockSpec((MOE_TOKENS, D_MODEL), lambda i, *_: (jnp.maximum(i - npt, 0), 0))],
        scratch_shapes=[pltpu.VMEM((2, MOE_SLOTS, D_MODEL), F32),
                        pltpu.SemaphoreType.DMA((2,))])
    return pl.pallas_call(
        _combine_body,
        grid_spec=grid_spec,
        out_shape=[jax.ShapeDtypeStruct((T_PROMPT, D_MODEL), F32),
                   jax.ShapeDtypeStruct((T_SAMPLE, D_MODEL), F32)],
        compiler_params=pltpu.CompilerParams(dimension_semantics=("arbitrary",),
                                             vmem_limit_bytes=VMEM_LIMIT),
        name="moe_combine",
    )(sched["nch"], sched["loc"], sched["start"], h1, route, gfin, y_sorted)


def _moe_schedule(seg):
    tot = jnp.sum(seg, axis=0)
    region = (tot + GMM_ROWS - 1) // GMM_ROWS * GMM_ROWS
    base = jnp.cumsum(region) - region
    start = base[None, :] + jnp.cumsum(seg, axis=0) - seg
    loc = jnp.cumsum(seg, axis=1) - seg
    ntile = region // GMM_ROWS
    cum = jnp.cumsum(ntile)
    nu = cum[-1]
    i = jnp.arange(GMM_TILES, dtype=jnp.int32)[:, None]
    first = (cum - ntile)[None, :]
    mine = (i >= first) & (i < cum[None, :])
    experts = jnp.arange(N_EXPERTS, dtype=jnp.int32)
    te = jnp.sum(jnp.where(mine, experts[None, :], 0), axis=1)
    tv = jnp.sum(jnp.where(mine, jnp.clip(tot[None, :] - (i - first) * GMM_ROWS, 0, GMM_ROWS), 0),
                 axis=1)
    te = jnp.where(i[:, 0] < nu, te, jnp.max(jnp.where(ntile > 0, experts, 0)))
    bstart = jnp.arange(GMM_TILES * GMM_ROWS // GMM_SUB, dtype=jnp.int32)[:, None] * GMM_SUB
    used_end = (base + (tot + GMM_SUB - 1) // GMM_SUB * GMM_SUB)[None, :]
    zb = jnp.any((bstart >= used_end) & (bstart < (base + region)[None, :]), axis=1)
    zb = (zb | (bstart[:, 0] >= jnp.sum(region))).astype(jnp.int32)
    return dict(nch=(seg // SUBLANES).reshape(-1), loc=loc.reshape(-1), start=start.reshape(-1),
                zn=((-tot) % GMM_SUB) // SUBLANES, zstart=base + tot,
                zb=zb, zbn=jnp.sum(zb).reshape(1),
                te=te, tv=tv.astype(jnp.int32), nu=nu.reshape(1).astype(jnp.int32))


def _row(x, width=None):
    x = x.astype(F32).reshape(1, -1)
    if width is not None and x.shape[1] < width:
        x = jnp.pad(x, ((0, 0), (0, width - x.shape[1])))
    return x


def _layer_params(l, lb_p, conv_w, conv_b, a_log, dt_bias, d_skip, hg_norm, ssd_norm):
    lb = jnp.sum(lb_p[1:l + 1], axis=0)
    return dict(
        la=_row(jnp.log(lb)), lc=_row(jnp.log1p(-lb)), omlb=_row(1.0 - lb),
        hgn=_row(hg_norm[l]), cw=conv_w[l].astype(F32), cb=_row(conv_b[l]),
        dtb=_row(dt_bias[l], LANES), an=_row(-jnp.exp(a_log[l].astype(F32)), LANES),
        dsk=_row(jnp.repeat(d_skip[l].astype(F32), SSD_P)), ssn=_row(ssd_norm[l]))


def kernel(x_prompt, x_sample, state_hgrn, state_ssm, state_conv, norm_mix, w_in, conv_w, conv_b,
           a_log, dt_bias, d_skip, lb_param, hg_norm, ssd_norm, w_out, norm_ffn, w_ffn_in,
           w_ffn_out, w_router, w_exp_in, w_exp_out, norm_final):
    h = (x_prompt.reshape(T_PROMPT, D_MODEL), x_sample.reshape(T_SAMPLE, D_MODEL))
    lb_p = jax.nn.softmax(lb_param.astype(F32), axis=0)
    packed = (SSD_PAIRS, LANES, SSD_N)
    s_ssm_t = jnp.swapaxes(state_ssm, 3, 4).reshape((DEPTH, DEC_BATCH) + packed)
    s_conv_t = jnp.swapaxes(state_conv, 1, 2)
    outs = {k: [] for k in ("hg_p", "ssm_p", "conv_p", "conv_s")}
    sample_states = None
    for l in range(DEPTH):
        prm = _layer_params(l, lb_p, conv_w, conv_b, a_log, dt_bias, d_skip, hg_norm, ssd_norm)
        proj = _proj_call(l, h, _row(norm_mix[l]), w_in)
        mixed_p, hg_p, ssm_p, conv_p = _scan_prompt_call(proj, prm)
        mixed_s, hg_s, ssm_s, conv_s = _scan_sample_call(l, proj, state_hgrn, s_ssm_t, s_conv_t,
                                                         prm, sample_states)
        sample_states = (hg_s, ssm_s)
        mixed = (mixed_p, mixed_s)
        for k, val in zip(outs, (hg_p, ssm_p, conv_p, conv_s)):
            outs[k].append(val)
        wo_bf = w_out[l].astype(BF16)
        if l % 2 == 0:
            h = (_out_ffn_call(h, mixed, wo_bf, _row(norm_ffn[l]), w_ffn_in[l // 2].astype(BF16),
                               w_ffn_out[l // 2].astype(BF16)),)
        else:
            wr_pad = jnp.pad(w_router[l // 2].astype(F32), ((0, 0), (0, LANES - N_EXPERTS)))
            h1, hn_bf, route, seg = _out_router_call(h[0], mixed, wo_bf, _row(norm_ffn[l]), wr_pad)
            sched = _moe_schedule(seg[:, 0, :N_EXPERTS])
            x_sorted = _dispatch_call(sched, hn_bf, route)
            y_sorted = _experts_call(sched, x_sorted, w_exp_in[l // 2], w_exp_out[l // 2])
            h = _combine_call(sched, h1, route, _row(norm_final), y_sorted)
    y_prompt = h[0].reshape(BATCH, SEQ, D_MODEL)
    y_sample = h[1].reshape(DEC_BATCH, DEC_SEQ, D_MODEL)
    def unpack_ssm(s_t, batch):
        return jnp.swapaxes(s_t.reshape(DEPTH, batch, SSD_HEADS, SSD_P, SSD_N), 3, 4)

    return (y_prompt, y_sample, jnp.stack(outs["hg_p"]), unpack_ssm(jnp.stack(outs["ssm_p"]), BATCH),
            jnp.stack(outs["conv_p"]), sample_states[0], unpack_ssm(sample_states[1], DEC_BATCH),
            jnp.swapaxes(jnp.stack(outs["conv_s"]), 1, 2))
```

```python
import functools

import jax
import jax.numpy as jnp
import numpy as np
from jax import lax
from jax.experimental import pallas as pl
from jax.experimental.pallas import tpu as pltpu

F32 = jnp.float32
BF16 = jnp.bfloat16

D_MODEL = 1024
BATCH = 8
SEQ = 2048
DEPTH = 2
DEC_BATCH = 128
DEC_SEQ = 8
HG_HEADS = 4
HG_D = 128
HG_W = HG_HEADS * HG_D
SSD_HEADS = 8
SSD_P = 64
SSD_N = 128
SSD_W = SSD_HEADS * SSD_P
SSD_PAIRS = SSD_HEADS // 2
CONV_W = 4
CONV_DIM = 1024
D_FF = 2816
N_EXPERTS = 8
D_FF_EXPERT = 3584
EPS = 1e-6

LANES = 128
C_Q, C_F, C_I, C_OG, C_Z, C_XBC, C_DT = 0, 512, 1024, 1536, 2048, 2560, 3584
PROJ_W = C_DT + LANES
HG_CHUNK = 64
SSD_CHUNK = 128
SAMPLE_SEQS = 4
PROMPT_SEQS = 4
HG_SAFE_RANGE = 80.0

T_PROMPT = BATCH * SEQ
T_SAMPLE = DEC_BATCH * DEC_SEQ
T_ALL = T_PROMPT + T_SAMPLE

SUBLANES = 8
TOP_K = 2
MOE_TOKENS = 256
MOE_TILES = T_ALL // MOE_TOKENS
ROUTER_TILES = 1
MOE_SLOTS = TOP_K * MOE_TOKENS + N_EXPERTS * SUBLANES
SEG_COPY_ROWS = (64, 32, 16, 8)
GMM_ROWS = 1024
GMM_SUB = 256
GMM_TILES = (TOP_K * T_ALL + MOE_TILES * N_EXPERTS * (SUBLANES - 1)
             + N_EXPERTS * (GMM_ROWS - 1)) // GMM_ROWS + 1

VMEM_LIMIT = 60000 * 1024


def _sigmoid(x):
    return 0.5 * jnp.tanh(0.5 * x) + 0.5


def _silu(x):
    return x * _sigmoid(x)


def _softplus(x):
    return jnp.maximum(x, 0.0) + jnp.log(1.0 + jnp.exp(-jnp.abs(x)))


def _rms(x, g):
    return x * lax.rsqrt(jnp.mean(x * x, axis=-1, keepdims=True) + EPS) * g


def _dot(a, b):
    return jnp.dot(a.astype(BF16), b.astype(BF16), preferred_element_type=F32)


def _dot_nt(a, b):
    return lax.dot_general(a.astype(BF16), b.astype(BF16), (((1,), (1,)), ((), ())),
                           preferred_element_type=F32)


def _split3(x):
    x1 = x.astype(BF16)
    r1 = x - x1.astype(F32)
    x2 = r1.astype(BF16)
    x3 = (r1 - x2.astype(F32)).astype(BF16)
    return x1, x2, x3


def _mask_dot(m, x):
    mb = m.astype(BF16)
    x1, x2, x3 = _split3(x)
    return (jnp.dot(mb, x1, preferred_element_type=F32) + jnp.dot(mb, x2, preferred_element_type=F32)
            + jnp.dot(mb, x3, preferred_element_type=F32))


def _mask_dot_nt(x, m):
    mb = m.astype(BF16)
    dn = (((1,), (1,)), ((), ()))
    x1, x2, x3 = _split3(x)
    return (lax.dot_general(x1, mb, dn, preferred_element_type=F32)
            + lax.dot_general(x2, mb, dn, preferred_element_type=F32)
            + lax.dot_general(x3, mb, dn, preferred_element_type=F32))


def _dot_f32x3(a, b):
    a1, a2, _ = _split3(a)
    b1, b2, _ = _split3(b)
    return (jnp.dot(a1, b1, preferred_element_type=F32) + jnp.dot(a1, b2, preferred_element_type=F32)
            + jnp.dot(a2, b1, preferred_element_type=F32))


def _iota2(shape, dim):
    return lax.broadcasted_iota(jnp.int32, shape, dim)


def _seq_masks(rows, seq_len):
    t = _iota2((rows, rows), 0)
    s = _iota2((rows, rows), 1)
    same = (t // seq_len) == (s // seq_len)
    causal = same & (s <= t)
    ref = same & ((s % seq_len) < seq_len // 2)
    return causal, ref, same


def _row_specs(arrs, tm):
    if len(arrs) == 1:
        return [pl.BlockSpec((tm, arrs[0].shape[1]), lambda i, *_: (i, 0))]
    npt = T_PROMPT // tm
    return [pl.BlockSpec((tm, arrs[0].shape[1]), lambda i, *_: (jnp.minimum(i, npt - 1), 0)),
            pl.BlockSpec((tm, arrs[1].shape[1]), lambda i, *_: (jnp.maximum(i - npt, 0), 0),
                         pipeline_mode=pl.Buffered(1))]


def _rows(i, refs):
    if len(refs) == 1:
        return refs[0][...]
    return jnp.where(i < T_PROMPT // refs[0].shape[0], refs[0][...], refs[1][...])


def _proj_body(n_h, *refs):
    h_refs, (g_ref, w_ref, o_ref, wb_ref) = refs[:n_h], refs[n_h:]
    i = pl.program_id(0)

    @pl.when(i == 0)
    def _():
        for r0 in range(0, C_DT, HG_W):
            wb_ref[r0:r0 + HG_W, :] = w_ref[r0:r0 + HG_W, :].astype(BF16)
        tail = w_ref[C_DT:, :]
        pad = jnp.zeros((LANES - tail.shape[0], D_MODEL), F32)
        wb_ref[C_DT:, :] = jnp.concatenate([tail, pad], axis=0).astype(BF16)

    hn = _rms(_rows(i, h_refs), g_ref[...])
    o_ref[...] = _dot_nt(hn, wb_ref[...])


def _proj_call(l, h, g, w_in):
    tm = 512
    w_t = jnp.swapaxes(w_in, 1, 2)
    return pl.pallas_call(
        functools.partial(_proj_body, len(h)),
        grid=(T_ALL // tm,),
        in_specs=_row_specs(h, tm) + [pl.BlockSpec((1, D_MODEL), lambda i: (0, 0)),
                                      pl.BlockSpec((None,) + w_t.shape[1:], lambda i: (l, 0, 0),
                                                   pipeline_mode=pl.Buffered(1))],
        out_specs=pl.BlockSpec((tm, PROJ_W), lambda i: (i, 0)),
        out_shape=jax.ShapeDtypeStruct((T_ALL, PROJ_W), F32),
        scratch_shapes=[pltpu.VMEM((PROJ_W, D_MODEL), BF16)],
        compiler_params=pltpu.CompilerParams(dimension_semantics=("arbitrary",),
                                             vmem_limit_bytes=VMEM_LIMIT),
        name="in_proj",
    )(*h, g, w_t)


def _hgrn_gates(p_q, p_f, la, lc, omlb):
    q = _silu(p_q)
    e = jnp.exp(-jnp.abs(p_f))
    b = lc + jnp.minimum(p_f, 0.0) - jnp.log(1.0 + e)
    logf = jnp.maximum(la, b) + jnp.log(1.0 + jnp.exp(-jnp.abs(la - b)))
    r = 1.0 / (1.0 + e)
    kin = omlb * jnp.where(p_f >= 0.0, e * r, r)
    return q, logf, kin


def _hgrn_decays(logf, causal, ref, same):
    rows = logf.shape[0]
    if ref is None:
        g = _mask_dot(causal, logf)
        return g, g[rows // 2 - 1:rows // 2, :], g[rows - 1:rows, :]
    m = jnp.concatenate([causal.astype(F32), ref.astype(F32), same.astype(F32)], axis=0)
    g3 = _mask_dot(m, logf)
    return g3[:rows], g3[rows:2 * rows], g3[2 * rows:]


def _hgrn_exact_intra(ex_ref, q, kin, g, v, seq_len):
    rows = q.shape[0]
    ex_ref[0] = g
    ex_ref[1] = q
    ex_ref[2] = kin
    ex_ref[3] = v
    s_idx = _iota2((rows, HG_W), 0)

    def body(t, carry):
        gt = ex_ref[0, pl.ds(t, 1), :]
        qt = ex_ref[1, pl.ds(t, 1), :]
        live = (s_idx <= t) & (s_idx // seq_len == t // seq_len)
        w = jnp.where(live, qt * ex_ref[2] * jnp.exp(jnp.minimum(gt - ex_ref[0], 0.0)), 0.0)
        outs = []
        for h in range(HG_HEADS):
            sl = slice(h * HG_D, (h + 1) * HG_D)
            score = jnp.sum(w[:, sl], axis=1, keepdims=True)
            outs.append(jnp.sum(score * ex_ref[3, :, sl], axis=0, keepdims=True))
        ex_ref[4, pl.ds(t, 1), :] = jnp.concatenate(outs, axis=1)
        return carry

    lax.fori_loop(0, rows, body, 0)
    return ex_ref[4]


def _hgrn_head_out(o, hgn, og):
    return _rms(o, hgn) * _silu(og)


def _ssd_inputs(conv, p_dt, dtb, a_neg):
    conv = _silu(conv)
    xs = conv[:, :SSD_W]
    bm = conv[:, SSD_W:SSD_W + 2 * SSD_N]
    cm = conv[:, SSD_W + 2 * SSD_N:]
    dtf = _softplus(p_dt + dtb)
    a = dtf * a_neg
    return xs, bm, cm, dtf, a


def _lane_bcast(x, lane, width):
    return jnp.broadcast_to(x[:, lane:lane + 1], (x.shape[0], width))


def _ssd_finish(y, xs, z, dsk, ssn):
    y = (y + dsk * xs) * _silu(z)
    half = SSD_W // 2
    return jnp.concatenate([_rms(y[:, :half], ssn[:, :half]), _rms(y[:, half:], ssn[:, half:])],
                           axis=1)


def _scan_body(n_prev, *refs):
    ns = PROMPT_SEQS
    p_refs, refs = refs[:ns], refs[ns:]
    (ps_ref, shg_ref, sssm_ref, sconv_ref), refs = refs[:4], refs[4:]
    (la_ref, lc_ref, omlb_ref), params, refs = refs[:3], refs[3:10], refs[10:]
    prev_refs, refs = refs[:2 * bool(n_prev)], refs[2 * bool(n_prev):]
    (mix_ref, ohg_ref, ossm_ref, oconv_ref, mixs_ref, ohgs_ref, ossms_ref, oconvs_ref,
     st_ref, hp_ref, xpad_ref, ex_ref, xpads_ref, exs_ref) = refs
    j = pl.program_id(1)
    tl = SSD_CHUNK

    @pl.when(j == 0)
    def _():
        st_ref[...] = jnp.zeros_like(st_ref)
        hp_ref[...] = jnp.zeros_like(hp_ref)
        xpad_ref[:, 0:8, :] = jnp.zeros((ns, 8, CONV_DIM), F32)

    gates, worst = [], None
    for s in range(ns):
        gates.append([])
        for c in range(tl // HG_CHUNK):
            rows = slice(c * HG_CHUNK, (c + 1) * HG_CHUNK)
            gt = _hgrn_gates(p_refs[s][rows, C_Q:C_Q + HG_W], p_refs[s][rows, C_F:C_F + HG_W],
                             la_ref[...], lc_ref[...], omlb_ref[...])
            gates[s].append(gt)
            half = HG_CHUNK // 2
            for part in (gt[1][:half], gt[1][half:]):
                tot = jnp.sum(part, axis=0, keepdims=True)
                worst = tot if worst is None else jnp.minimum(worst, tot)
    gates_s = _hgrn_gates(ps_ref[:, C_Q:C_Q + HG_W], ps_ref[:, C_F:C_F + HG_W],
                          la_ref[...], lc_ref[...], omlb_ref[...])
    for b in range(SAMPLE_SEQS):
        tot = jnp.sum(gates_s[1][b * DEC_SEQ:(b + 1) * DEC_SEQ], axis=0, keepdims=True)
        worst = jnp.minimum(worst, tot)
    safe = jnp.min(worst) >= -HG_SAFE_RANGE

    for exact in (False, True):
        @pl.when(safe != exact)
        def _(exact=exact):
            for s in range(ns):
                _scan_prompt_tile(gates[s], exact, ex_ref, p_refs[s], *params, mix_ref.at[s],
                                  st_ref.at[s], hp_ref.at[s], xpad_ref.at[s])
            seq0 = (pl.program_id(0) * pl.num_programs(1) + j) % (SUBLANES // SAMPLE_SEQS) * SAMPLE_SEQS
            _scan_sample_tile(n_prev, gates_s, exact, seq0, ps_ref, shg_ref, sssm_ref, sconv_ref,
                              prev_refs, *params, mixs_ref, ohgs_ref, ossms_ref, oconvs_ref,
                              xpads_ref, exs_ref)

    @pl.when(j == pl.num_programs(1) - 1)
    def _():
        for s in range(ns):
            oconv_ref[s] = xpad_ref[s, 5:8, :]
            for h in range(HG_HEADS):
                ohg_ref[s, h] = st_ref[s, h].T
        ossm_ref[...] = hp_ref[...]


def _scan_prompt_tile(gates, exact, ex_ref, p_ref, hgn_ref, cw_ref, cb_ref, dtb_ref, an_ref,
                      dsk_ref, ssn_ref, mix_ref, st_ref, hp_ref, xpad_ref):
    tl = SSD_CHUNK
    causal, _, _ = _seq_masks(HG_CHUNK, HG_CHUNK)
    for c in range(tl // HG_CHUNK):
        r0 = c * HG_CHUNK
        rows = slice(r0, r0 + HG_CHUNK)
        q, logf, kin = gates[c]
        v = p_ref[rows, C_I:C_I + HG_W]
        g, gmid, glast = _hgrn_decays(logf, causal, None, None)
        if exact:
            qh = q * jnp.exp(g)
            kd = kin * jnp.exp(glast - g)
            o_intra = _hgrn_exact_intra(ex_ref, q, kin, g, v, HG_CHUNK)
        else:
            qt = q * jnp.exp(g - gmid)
            kt = kin * jnp.exp(gmid - g)
            qh = qt * jnp.exp(gmid)
            kd = kt * jnp.exp(glast - gmid)
        ds = jnp.exp(glast[0:1, :])
        for h in range(HG_HEADS):
            sl = slice(h * HG_D, (h + 1) * HG_D)
            st = st_ref[h]
            if exact:
                o = o_intra[:, sl] + _dot_nt(qh[:, sl], st)
            else:
                sc = jnp.where(causal, _dot_nt(qt[:, sl], kt[:, sl]), 0.0)
                o = _dot(sc, v[:, sl]) + _dot_nt(qh[:, sl], st)
            st_ref[h] = st * ds[:, sl] + _dot(v[:, sl].T, kd[:, sl])
            og = p_ref[rows, C_OG + h * HG_D:C_OG + (h + 1) * HG_D]
            mix_ref[rows, sl] = _hgrn_head_out(o, hgn_ref[:, sl], og).astype(BF16)

    xpad_ref[8:8 + tl, :] = p_ref[:, C_XBC:C_XBC + CONV_DIM]
    conv = cb_ref[...] + cw_ref[0:1, :] * xpad_ref[5:5 + tl, :]
    for tap in range(1, CONV_W):
        conv = conv + cw_ref[tap:tap + 1, :] * xpad_ref[5 + tap:5 + tap + tl, :]
    xpad_ref[0:8, :] = xpad_ref[tl:tl + 8, :]

    xs, bm, cm, dtf, a = _ssd_inputs(conv, p_ref[:, C_DT:C_DT + LANES], dtb_ref[...], an_ref[...])
    tri, _, _ = _seq_masks(tl, tl)
    trif = tri.astype(F32)
    acum = _mask_dot(trif, a)
    acum_t = _mask_dot_nt(a.T, trif)
    lo = _iota2((tl, LANES), 1) < SSD_P
    top = _iota2((LANES, tl), 0) < SSD_P
    ys = []
    for pair in range(SSD_PAIRS):
        grp = pair // 2
        cg = cm[:, grp * SSD_N:(grp + 1) * SSD_N]
        bg = bm[:, grp * SSD_N:(grp + 1) * SSD_N]
        cb = _dot_nt(cg, bg)
        psl = slice(pair * LANES, (pair + 1) * LANES)
        r0, r1 = 2 * pair, 2 * pair + 1
        xp = xs[:, psl] * jnp.where(lo, _lane_bcast(dtf, r0, LANES), _lane_bcast(dtf, r1, LANES))
        intra, einter, wrow, ea = [], [], [], []
        for r in (r0, r1):
            col = _lane_bcast(acum, r, tl)
            row = jnp.broadcast_to(acum_t[r:r + 1, :], (tl, tl))
            dec = jnp.exp(jnp.where(tri, col - row, -jnp.inf))
            intra.append(_dot(cb * dec, xp))
            einter.append(jnp.exp(col))
            alast = acum_t[r:r + 1, tl - 1:tl]
            wrow.append(jnp.exp(alast - acum_t[r:r + 1, :]))
            ea.append(jnp.exp(alast))
        hp = hp_ref[pair]
        y = jnp.where(lo, intra[0], intra[1]) + jnp.where(lo, einter[0], einter[1]) * _dot_nt(cg, hp)
        xw_t = xp.T * jnp.where(top, wrow[0], wrow[1])
        hp_ref[pair] = jnp.where(top, ea[0], ea[1]) * hp + _dot(xw_t, bg)
        ys.append(y)
    y = jnp.concatenate(ys, axis=1)
    y = _ssd_finish(y, xs, p_ref[:, C_Z:C_Z + SSD_W], dsk_ref[...], ssn_ref[...])
    mix_ref[:, HG_W:] = y.astype(BF16)


def _row_spec(width):
    return pl.BlockSpec((1, width), lambda b, j: (0, 0))


def _scan_call(l, proj, s_hg, s_ssm_t, s_conv_t, prm, prev):
    tl = SSD_CHUNK
    nt = SEQ // tl
    ns, nb = PROMPT_SEQS, SAMPLE_SEQS
    assert BATCH // ns * nt * nb == DEC_BATCH
    srows = nb * DEC_SEQ
    srow0 = T_PROMPT // srows
    step = lambda b, j: b * nt + j
    pstate, sstate = (ns, HG_HEADS, HG_D, HG_D), (nb, HG_HEADS, HG_D, HG_D)
    in_specs = [pl.BlockSpec((tl, PROJ_W), lambda b, j, s=s: ((b * ns + s) * nt + j, 0))
                for s in range(ns)]
    in_specs += [pl.BlockSpec((srows, PROJ_W), lambda b, j: (srow0 + step(b, j), 0)),
                 pl.BlockSpec((None,) + sstate, lambda b, j: (l, step(b, j), 0, 0, 0)),
                 pl.BlockSpec((None,) + sstate, lambda b, j: (l, step(b, j), 0, 0, 0)),
                 pl.BlockSpec((None, CONV_W - 1, SUBLANES, CONV_DIM),
                              lambda b, j: (l, 0, step(b, j) // (SUBLANES // nb), 0))]
    in_specs += [_row_spec(HG_W), _row_spec(HG_W), _row_spec(HG_W), _row_spec(HG_W),
                 pl.BlockSpec((CONV_W, CONV_DIM), lambda b, j: (0, 0)), _row_spec(CONV_DIM),
                 _row_spec(LANES), _row_spec(LANES), _row_spec(SSD_W), _row_spec(SSD_W)]
    in_specs += [pl.BlockSpec((l,) + sstate, lambda b, j: (0, step(b, j), 0, 0, 0))] * (2 if l else 0)
    out_specs = [pl.BlockSpec((ns, tl, D_MODEL), lambda b, j: (b, j, 0)),
                 pl.BlockSpec(pstate, lambda b, j: (b, 0, 0, 0)),
                 pl.BlockSpec(pstate, lambda b, j: (b, 0, 0, 0)),
                 pl.BlockSpec((ns, CONV_W - 1, CONV_DIM), lambda b, j: (b, 0, 0)),
                 pl.BlockSpec((srows, D_MODEL), lambda b, j: (step(b, j), 0)),
                 pl.BlockSpec((l + 1,) + sstate, lambda b, j: (0, step(b, j), 0, 0, 0)),
                 pl.BlockSpec((l + 1,) + sstate, lambda b, j: (0, step(b, j), 0, 0, 0)),
                 pl.BlockSpec((CONV_W - 1, SUBLANES, CONV_DIM),
                              lambda b, j: (0, step(b, j) // (SUBLANES // nb), 0))]
    stacked = jax.ShapeDtypeStruct((l + 1, DEC_BATCH) + sstate[1:], F32)
    out_shape = [jax.ShapeDtypeStruct((BATCH, SEQ, D_MODEL), BF16),
                 jax.ShapeDtypeStruct((BATCH,) + pstate[1:], F32),
                 jax.ShapeDtypeStruct((BATCH,) + pstate[1:], F32),
                 jax.ShapeDtypeStruct((BATCH, CONV_W - 1, CONV_DIM), F32),
                 jax.ShapeDtypeStruct((T_SAMPLE, D_MODEL), BF16), stacked, stacked,
                 jax.ShapeDtypeStruct((CONV_W - 1, DEC_BATCH, CONV_DIM), F32)]
    outs = pl.pallas_call(
        functools.partial(_scan_body, l),
        grid=(BATCH // ns, nt),
        in_specs=in_specs, out_specs=out_specs, out_shape=out_shape,
        scratch_shapes=[pltpu.VMEM(pstate, F32), pltpu.VMEM(pstate, F32),
                        pltpu.VMEM((ns, tl + 8, CONV_DIM), F32),
                        pltpu.VMEM((5, HG_CHUNK, HG_W), F32),
                        pltpu.VMEM((16 * nb, CONV_DIM), F32), pltpu.VMEM((5, srows, HG_W), F32)],
        compiler_params=pltpu.CompilerParams(dimension_semantics=("arbitrary", "arbitrary"),
                                             vmem_limit_bytes=VMEM_LIMIT),
        name="scan",
    )(*([proj] * (ns + 1)), s_hg, s_ssm_t, s_conv_t, prm["la"], prm["lc"], prm["omlb"], prm["hgn"],
      prm["cw"], prm["cb"], prm["dtb"], prm["an"], prm["dsk"], prm["ssn"], *(prev if l else ()))
    return (outs[0].reshape(T_PROMPT, D_MODEL),) + tuple(outs[1:])


def _scan_sample_tile(n_prev, gates, exact, seq0, p_ref, shg_ref, sssm_ref, sconv_ref, prev_refs,
                      hgn_ref, cw_ref, cb_ref, dtb_ref, an_ref, dsk_ref, ssn_ref,
                      mix_ref, ohg_ref, ossm_ref, oconv_ref, xpad_ref, ex_ref):
    if n_prev:
        ohg_ref[0:n_prev] = prev_refs[0][...]
        ossm_ref[0:n_prev] = prev_refs[1][...]
    nb, sl_len = SAMPLE_SEQS, DEC_SEQ
    rows = nb * sl_len
    causal, ref, same = _seq_masks(rows, sl_len)
    rowseq = _iota2((rows, LANES), 0) // sl_len

    q, logf, kin = gates
    v = p_ref[:, C_I:C_I + HG_W]
    g, gmid, glast = _hgrn_decays(logf, causal, ref, same)
    if exact:
        o_intra_all = _hgrn_exact_intra(ex_ref, q, kin, g, v, sl_len)
    else:
        qt = q * jnp.exp(g - gmid)
        kt = kin * jnp.exp(gmid - g)
    qh = q * jnp.exp(g)
    kd = kin * jnp.exp(glast - g)
    ds_t = jnp.exp(glast).T
    kd_t = kd.T
    for h in range(HG_HEADS):
        sl = slice(h * HG_D, (h + 1) * HG_D)
        if exact:
            o_intra = o_intra_all[:, sl]
        else:
            sc = jnp.where(causal, _dot_nt(qt[:, sl], kt[:, sl]), 0.0)
            o_intra = _dot(sc, v[:, sl])
        o_inter = []
        for b in range(nb):
            s_old = shg_ref[b, h]
            o_inter.append(_dot(qh[b * sl_len:(b + 1) * sl_len, sl], s_old))
            vb = jnp.where(rowseq == b, v[:, sl], 0.0)
            dcol = _lane_bcast(ds_t[sl, :], b * sl_len, HG_D)
            ohg_ref[n_prev, b, h] = dcol * s_old + _dot(kd_t[sl, :], vb)
        o = o_intra + jnp.concatenate(o_inter, axis=0)
        og = p_ref[:, C_OG + h * HG_D:C_OG + (h + 1) * HG_D]
        mix_ref[:, sl] = _hgrn_head_out(o, hgn_ref[:, sl], og).astype(BF16)

    convs = []
    for b in range(nb):
        base = 16 * b
        for r in range(CONV_W - 1):
            xpad_ref[base + 5 + r:base + 6 + r, :] = sconv_ref[r, pl.ds(seq0 + b, 1), :]
        xpad_ref[base + 8:base + 16, :] = p_ref[b * sl_len:(b + 1) * sl_len, C_XBC:C_XBC + CONV_DIM]
        cv = cb_ref[...] + cw_ref[0:1, :] * xpad_ref[base + 5:base + 13, :]
        for tap in range(1, CONV_W):
            cv = cv + cw_ref[tap:tap + 1, :] * xpad_ref[base + 5 + tap:base + 13 + tap, :]
        convs.append(cv)
        for r in range(CONV_W - 1):
            oconv_ref[r, pl.ds(seq0 + b, 1), :] = xpad_ref[base + 13 + r:base + 14 + r, :]
    conv = jnp.concatenate(convs, axis=0)

    xs, bm, cm, dtf, a = _ssd_inputs(conv, p_ref[:, C_DT:C_DT + LANES], dtb_ref[...], an_ref[...])
    causf = causal.astype(F32)
    acum = _mask_dot(causf, a)
    a_t = a.T
    acum_t = _mask_dot_nt(a_t, causf)
    atot_t = _mask_dot_nt(a_t, same.astype(F32))
    wall_t = jnp.exp(atot_t - acum_t)
    eatot_t = jnp.exp(atot_t)
    eall = jnp.exp(acum)
    lo = _iota2((rows, LANES), 1) < SSD_P
    top = _iota2((LANES, rows), 0) < SSD_P
    top_sq = _iota2((LANES, SSD_N), 0) < SSD_P
    colseq = _iota2((LANES, rows), 1) // sl_len
    ys = []
    for pair in range(SSD_PAIRS):
        grp = pair // 2
        cg = cm[:, grp * SSD_N:(grp + 1) * SSD_N]
        bg = bm[:, grp * SSD_N:(grp + 1) * SSD_N]
        cb = _dot_nt(cg, bg)
        psl = slice(pair * LANES, (pair + 1) * LANES)
        r0, r1 = 2 * pair, 2 * pair + 1
        xp = xs[:, psl] * jnp.where(lo, _lane_bcast(dtf, r0, LANES), _lane_bcast(dtf, r1, LANES))
        xw_t = xp.T * jnp.where(top, wall_t[r0:r0 + 1, :], wall_t[r1:r1 + 1, :])
        intra = []
        for r in (r0, r1):
            col = _lane_bcast(acum, r, rows)
            row = jnp.broadcast_to(acum_t[r:r + 1, :], (rows, rows))
            dec = jnp.exp(jnp.where(causal, col - row, -jnp.inf))
            intra.append(_dot(cb * dec, xp))
        y_intra = jnp.where(lo, intra[0], intra[1])
        y_inter = []
        for b in range(nb):
            tr = slice(b * sl_len, (b + 1) * sl_len)
            h_old = sssm_ref[b, pair]
            y_inter.append(_dot_nt(cg[tr, :], h_old))
            upd = _dot(jnp.where(colseq == b, xw_t, 0.0), bg)
            c0 = b * sl_len
            ea = jnp.where(top_sq, eatot_t[r0:r0 + 1, c0:c0 + 1], eatot_t[r1:r1 + 1, c0:c0 + 1])
            ossm_ref[n_prev, b, pair] = ea * h_old + upd
        e_pair = jnp.where(lo, _lane_bcast(eall, r0, LANES), _lane_bcast(eall, r1, LANES))
        ys.append(y_intra + e_pair * jnp.concatenate(y_inter, axis=0))
    y = jnp.concatenate(ys, axis=1)
    y = _ssd_finish(y, xs, p_ref[:, C_Z:C_Z + SSD_W], dsk_ref[...], ssn_ref[...])
    mix_ref[:, HG_W:] = y.astype(BF16)


def _out_ffn_body(n_h, *refs):
    h_refs, mix_refs = refs[:n_h], refs[n_h:n_h + 2]
    wo_ref, g_ref, wg_ref, wu_ref, wd_ref, o_ref, hn_ref = refs[n_h + 2:]
    i = pl.program_id(0)
    f = pl.program_id(1)

    @pl.when(f == 0)
    def _():
        h1 = _rows(i, h_refs) + _dot(_rows(i, mix_refs), wo_ref[...])
        o_ref[...] = h1
        hn_ref[...] = _rms(h1, g_ref[...]).astype(BF16)

    o_ref[...] += _swiglu_part(hn_ref[...], wg_ref[...], wu_ref[...], wd_ref[...])


def _out_ffn_call(h, mixed, wo_bf, g, wi_bf, wd_bf):
    tm, tf = 1024, D_FF // 2
    nf = D_FF // tf
    return pl.pallas_call(
        functools.partial(_out_ffn_body, len(h)),
        grid=(T_ALL // tm, nf),
        in_specs=_row_specs(h, tm) + _row_specs(mixed, tm) + [
                  pl.BlockSpec((D_MODEL, D_MODEL), lambda i, f: (0, 0)),
                  pl.BlockSpec((1, D_MODEL), lambda i, f: (0, 0)),
                  pl.BlockSpec((D_MODEL, tf), lambda i, f: (0, f)),
                  pl.BlockSpec((D_MODEL, tf), lambda i, f: (0, nf + f)),
                  pl.BlockSpec((tf, D_MODEL), lambda i, f: (f, 0))],
        out_specs=pl.BlockSpec((tm, D_MODEL), lambda i, f: (i, 0)),
        out_shape=jax.ShapeDtypeStruct((T_ALL, D_MODEL), F32),
        scratch_shapes=[pltpu.VMEM((tm, D_MODEL), BF16)],
        compiler_params=pltpu.CompilerParams(dimension_semantics=("arbitrary", "arbitrary"),
                                             vmem_limit_bytes=VMEM_LIMIT),
        name="out_ffn",
    )(*h, *mixed, wo_bf, g, wi_bf, wi_bf, wd_bf)


def _out_router_body(h_ref, mixp_ref, mixs_ref, wo_ref, g_ref, wr_ref,
                     h1_ref, hn_ref, route_ref, seg_ref):
    h1 = h_ref[...] + _dot(_rows(pl.program_id(0), (mixp_ref, mixs_ref)), wo_ref[...])
    h1_ref[...] = h1
    hn = _rms(h1, g_ref[...])
    hn_ref[...] = hn.astype(BF16)
    logits = _dot_f32x3(hn, wr_ref[...])
    lane = _iota2(logits.shape, 1)
    lg = jnp.where(lane < N_EXPERTS, logits, -jnp.inf)
    m1 = jnp.max(lg, axis=1, keepdims=True)
    i1 = jnp.min(jnp.where(lg == m1, lane, LANES), axis=1, keepdims=True)
    lg2 = jnp.where(lane == i1, -jnp.inf, lg)
    m2 = jnp.max(lg2, axis=1, keepdims=True)
    i2 = jnp.min(jnp.where(lg2 == m2, lane, LANES), axis=1, keepdims=True)
    e2 = jnp.exp(m2 - m1)
    g1 = 1.0 / (1.0 + e2)
    g2 = e2 / (1.0 + e2)

    tm = logits.shape[0]
    chosen = jnp.where((lane == i1) | (lane == i2), 1.0, 0.0)
    t_row, t_col = _iota2((tm, tm), 0), _iota2((tm, tm), 1)
    earlier = ((t_col < t_row) & (t_col // MOE_TOKENS == t_row // MOE_TOKENS)).astype(F32)
    rank = _dot(earlier, chosen)
    below = (_iota2((LANES, LANES), 0) < _iota2((LANES, LANES), 1)).astype(F32)
    pos = []
    for k in range(tm // MOE_TOKENS):
        rows = slice(k * MOE_TOKENS, (k + 1) * MOE_TOKENS)
        cnt = jnp.sum(chosen[rows], axis=0, keepdims=True)
        seg = jnp.floor((cnt + (SUBLANES - 1)) * (1.0 / SUBLANES)) * SUBLANES
        seg8 = jnp.broadcast_to(seg, (SUBLANES, LANES))
        pos.append(_dot(seg8, below)[0:1, :] + rank[rows])
        seg_ref[k] = seg8.astype(jnp.int32)
    pos = jnp.concatenate(pos, axis=0)
    p1 = jnp.sum(jnp.where(lane == i1, pos, 0.0), axis=1, keepdims=True)
    p2 = jnp.sum(jnp.where(lane == i2, pos, 0.0), axis=1, keepdims=True)
    route_ref[...] = jnp.where(lane == 0, p1, jnp.where(lane == 1, p2, jnp.where(
        lane == 2, g1, jnp.where(lane == 3, g2, 0.0))))


def _out_router_call(h, mixed, wo_bf, g, wr_pad):
    tm = ROUTER_TILES * MOE_TOKENS
    return pl.pallas_call(
        _out_router_body,
        grid=(T_ALL // tm,),
        in_specs=[pl.BlockSpec((tm, D_MODEL), lambda i: (i, 0))] + _row_specs(mixed, tm) + [
                  pl.BlockSpec((D_MODEL, D_MODEL), lambda i: (0, 0)),
                  pl.BlockSpec((1, D_MODEL), lambda i: (0, 0)),
                  pl.BlockSpec((D_MODEL, LANES), lambda i: (0, 0))],
        out_specs=[pl.BlockSpec((tm, D_MODEL), lambda i: (i, 0)),
                   pl.BlockSpec((tm, D_MODEL), lambda i: (i, 0)),
                   pl.BlockSpec((tm, LANES), lambda i: (i, 0)),
                   pl.BlockSpec((ROUTER_TILES, SUBLANES, LANES), lambda i: (i, 0, 0))],
        out_shape=[jax.ShapeDtypeStruct((T_ALL, D_MODEL), F32),
                   jax.ShapeDtypeStruct((T_ALL, D_MODEL), BF16),
                   jax.ShapeDtypeStruct((T_ALL, LANES), F32),
                   jax.ShapeDtypeStruct((MOE_TILES, SUBLANES, LANES), jnp.int32)],
        compiler_params=pltpu.CompilerParams(dimension_semantics=("arbitrary",),
                                             vmem_limit_bytes=VMEM_LIMIT),
        name="out_router",
    )(h, *mixed, wo_bf, g, wr_pad)


def _row_tile_copy(tile_ref, hbm_ref, tile_row, hbm_row, sem, to_hbm, rows=SUBLANES):
    t = tile_ref.at[pl.ds(pl.multiple_of(tile_row, SUBLANES), rows), :]
    g = hbm_ref.at[pl.ds(pl.multiple_of(hbm_row, SUBLANES), rows), :]
    return pltpu.make_async_copy(t, g, sem) if to_hbm else pltpu.make_async_copy(g, t, sem)


def _seg_plan(n):
    counts = [lax.shift_right_logical(n, 3)]
    for shift in (2, 1, 0):
        counts.append(lax.shift_right_logical(n, shift) & 1)
    return counts


def _seg_copies(i, n_ref, loc_ref, start_ref, tile_ref, hbm_ref, sem, to_hbm):
    for e in range(N_EXPERTS):
        k = i * N_EXPERTS + e
        lo = loc_ref[k]
        st = start_ref[k]
        counts = _seg_plan(n_ref[k])
        big = SEG_COPY_ROWS[0]

        def body(c, carry, lo=lo, st=st):
            _row_tile_copy(tile_ref, hbm_ref, lo + c * big, st + c * big, sem, to_hbm, big).start()
            return carry

        lax.fori_loop(0, counts[0], body, 0)
        done = counts[0] * big
        for rows, cnt in zip(SEG_COPY_ROWS[1:], counts[1:]):
            @pl.when(cnt == 1)
            def _(rows=rows, done=done, lo=lo, st=st):
                _row_tile_copy(tile_ref, hbm_ref, lo + done, st + done, sem, to_hbm, rows).start()

            done = done + cnt * rows


def _seg_wait(i, n_ref, tile_ref, hbm_ref, sem, to_hbm):
    totals = None
    for e in range(N_EXPERTS):
        counts = _seg_plan(n_ref[i * N_EXPERTS + e])
        totals = counts if totals is None else [a + b for a, b in zip(totals, counts)]
    for rows, total in zip(SEG_COPY_ROWS, totals):
        def body(c, carry, rows=rows):
            _row_tile_copy(tile_ref, hbm_ref, 0, 0, sem, to_hbm, rows).wait()
            return carry

        lax.fori_loop(0, total, body, 0)


def _dispatch_body(n_ref, loc_ref, start_ref, zn_ref, zstart_ref, zb_ref, zbn_ref,
                   hn_ref, route_ref, xs_ref, stage_ref, zero_ref, sem):
    i = pl.program_id(0)
    last = pl.num_programs(0) - 1
    slot = i % 2

    @pl.when(i >= 2)
    def _():
        _seg_wait(i - 2, n_ref, stage_ref.at[slot], xs_ref, sem.at[slot], True)

    rt = route_ref[...].T
    s = _iota2((MOE_SLOTS, MOE_TOKENS), 0).astype(F32)
    perm = jnp.where((s == rt[0:1, :]) | (s == rt[1:2, :]), 1.0, 0.0)
    stage_ref[slot] = _dot(perm, hn_ref[...])
    _seg_copies(i, n_ref, loc_ref, start_ref, stage_ref.at[slot], xs_ref, sem.at[slot], True)

    @pl.when(i == last)
    def _():
        zero_ref[...] = jnp.zeros_like(zero_ref)

        def blk_copy(b):
            return pltpu.make_async_copy(
                zero_ref, xs_ref.at[pl.ds(pl.multiple_of(b * GMM_SUB, GMM_SUB), GMM_SUB), :],
                sem.at[3])

        def bbody(b, carry):
            @pl.when(zb_ref[b] == 1)
            def _():
                blk_copy(b).start()
            return carry

        lax.fori_loop(0, GMM_TILES * GMM_ROWS // GMM_SUB, bbody, 0)
        nz = zn_ref[0]
        for e in range(N_EXPERTS):
            st = zstart_ref[e]

            def body(c, carry, st=st):
                _row_tile_copy(zero_ref, xs_ref, 0, st + c * SUBLANES, sem.at[2], True).start()
                return carry

            lax.fori_loop(0, zn_ref[e], body, 0)
            if e:
                nz = nz + zn_ref[e]

        @pl.when(last >= 1)
        def _():
            _seg_wait(i - 1, n_ref, stage_ref.at[1 - slot], xs_ref, sem.at[1 - slot], True)

        _seg_wait(i, n_ref, stage_ref.at[slot], xs_ref, sem.at[slot], True)

        def zbody(c, carry):
            _row_tile_copy(zero_ref, xs_ref, 0, 0, sem.at[2], True).wait()
            return carry

        lax.fori_loop(0, nz, zbody, 0)

        def bwait(c, carry):
            blk_copy(0).wait()
            return carry

        lax.fori_loop(0, zbn_ref[0], bwait, 0)


def _dispatch_call(sched, hn_bf, route):
    grid_spec = pltpu.PrefetchScalarGridSpec(
        num_scalar_prefetch=7,
        grid=(MOE_TILES,),
        in_specs=[pl.BlockSpec((MOE_TOKENS, D_MODEL), lambda i, *_: (i, 0)),
                  pl.BlockSpec((MOE_TOKENS, LANES), lambda i, *_: (i, 0))],
        out_specs=pl.BlockSpec(memory_space=pl.ANY),
        scratch_shapes=[pltpu.VMEM((2, MOE_SLOTS, D_MODEL), F32),
                        pltpu.VMEM((GMM_SUB, D_MODEL), F32),
                        pltpu.SemaphoreType.DMA((4,))])
    return pl.pallas_call(
        _dispatch_body,
        grid_spec=grid_spec,
        out_shape=jax.ShapeDtypeStruct((GMM_TILES * GMM_ROWS, D_MODEL), F32),
        compiler_params=pltpu.CompilerParams(dimension_semantics=("arbitrary",),
                                             vmem_limit_bytes=VMEM_LIMIT),
        name="moe_dispatch",
    )(sched["nch"], sched["loc"], sched["start"], sched["zn"], sched["zstart"], sched["zb"],
      sched["zbn"], hn_bf, route)


def _swiglu_part(x, wg, wu, wd):
    gate = jnp.dot(x, wg, preferred_element_type=F32)
    up = jnp.dot(x, wu, preferred_element_type=F32)
    return jnp.dot((_silu(gate) * up).astype(BF16), wd, preferred_element_type=F32)


def _experts_body(te_ref, tv_ref, nu_ref, x_ref, wg_ref, wu_ref, wd_ref, o_ref,
                  wgb_ref, wub_ref, wdb_ref):
    del te_ref, nu_ref
    i = pl.program_id(0)
    f = pl.program_id(1)
    nv = tv_ref[i]

    @pl.when(f == 0)
    def _():
        o_ref[...] = jnp.zeros_like(o_ref)

    @pl.when(nv == GMM_ROWS)
    def _():
        o_ref[...] += _swiglu_part(x_ref[...].astype(BF16), wg_ref[0].astype(BF16),
                                   wu_ref[0].astype(BF16), wd_ref[0].astype(BF16))

    @pl.when((nv > 0) & (nv < GMM_ROWS))
    def _():
        wgb_ref[...] = wg_ref[0].astype(BF16)
        wub_ref[...] = wu_ref[0].astype(BF16)
        wdb_ref[...] = wd_ref[0].astype(BF16)
        for sub in range(GMM_ROWS // GMM_SUB):
            rows = slice(sub * GMM_SUB, (sub + 1) * GMM_SUB)

            @pl.when(sub * GMM_SUB < nv)
            def _(rows=rows):
                o_ref[rows, :] += _swiglu_part(x_ref[rows, :].astype(BF16), wgb_ref[...],
                                               wub_ref[...], wdb_ref[...])


def _experts_call(sched, x_sorted, w_in_e, w_out_e):
    tf = 512
    nf = D_FF_EXPERT // tf

    def used(i, nu):
        return jnp.maximum(jnp.minimum(i, nu[0] - 1), 0)

    def fidx(i, f, nu):
        return jnp.where(i < nu[0], f, nf - 1)

    grid_spec = pltpu.PrefetchScalarGridSpec(
        num_scalar_prefetch=3,
        grid=(GMM_TILES, nf),
        in_specs=[pl.BlockSpec((GMM_ROWS, D_MODEL), lambda i, f, te, tv, nu: (used(i, nu), 0)),
                  pl.BlockSpec((1, D_MODEL, tf), lambda i, f, te, tv, nu: (te[i], 0, fidx(i, f, nu))),
                  pl.BlockSpec((1, D_MODEL, tf),
                               lambda i, f, te, tv, nu: (te[i], 0, nf + fidx(i, f, nu))),
                  pl.BlockSpec((1, tf, D_MODEL), lambda i, f, te, tv, nu: (te[i], fidx(i, f, nu), 0))],
        out_specs=pl.BlockSpec((GMM_ROWS, D_MODEL), lambda i, f, te, tv, nu: (i, 0)),
        scratch_shapes=[pltpu.VMEM((D_MODEL, tf), BF16), pltpu.VMEM((D_MODEL, tf), BF16),
                        pltpu.VMEM((tf, D_MODEL), BF16)])
    return pl.pallas_call(
        _experts_body,
        grid_spec=grid_spec,
        out_shape=jax.ShapeDtypeStruct((GMM_TILES * GMM_ROWS, D_MODEL), F32),
        compiler_params=pltpu.CompilerParams(dimension_semantics=("arbitrary", "arbitrary"),
                                             vmem_limit_bytes=VMEM_LIMIT),
        name="moe_experts",
    )(sched["te"], sched["tv"], sched["nu"], x_sorted, w_in_e, w_in_e, w_out_e)


def _combine_body(n_ref, loc_ref, start_ref, h1_ref, route_ref, gf_ref, ys_ref, op_ref, os_ref,
                  buf_ref, sem):
    i = pl.program_id(0)
    last = pl.num_programs(0) - 1
    slot = i % 2

    @pl.when(i == 0)
    def _():
        buf_ref[...] = jnp.zeros_like(buf_ref)
        _seg_copies(0, n_ref, loc_ref, start_ref, buf_ref.at[0], ys_ref, sem.at[0], False)

    @pl.when(i < last)
    def _():
        _seg_copies(i + 1, n_ref, loc_ref, start_ref, buf_ref.at[1 - slot], ys_ref,
                    sem.at[1 - slot], False)

    _seg_wait(i, n_ref, buf_ref.at[slot], ys_ref, sem.at[slot], False)

    route = route_ref[...]
    s = _iota2((MOE_TOKENS, MOE_SLOTS), 1).astype(F32)
    rows = buf_ref[slot]
    y1 = _dot(jnp.where(s == route[:, 0:1], 1.0, 0.0), rows)
    y2 = _dot(jnp.where(s == route[:, 1:2], 1.0, 0.0), rows)
    y = _rms(h1_ref[...] + route[:, 2:3] * y1 + route[:, 3:4] * y2, gf_ref[...])

    @pl.when(i < T_PROMPT // MOE_TOKENS)
    def _():
        op_ref[...] = y

    @pl.when(i >= T_PROMPT // MOE_TOKENS)
    def _():
        os_ref[...] = y


def _combine_call(sched, h1, route, gfin, y_sorted):
    npt = T_PROMPT // MOE_TOKENS
    grid_spec = pltpu.PrefetchScalarGridSpec(
        num_scalar_prefetch=3,
        grid=(MOE_TILES,),
        in_specs=[pl.BlockSpec((MOE_TOKENS, D_MODEL), lambda i, *_: (i, 0)),
                  pl.BlockSpec((MOE_TOKENS, LANES), lambda i, *_: (i, 0)),
                  pl.BlockSpec((1, D_MODEL), lambda i, *_: (0, 0)),
                  pl.BlockSpec(memory_space=pl.ANY)],
        out_specs=[pl.BlockSpec((MOE_TOKENS, D_MODEL), lambda i, *_: (jnp.minimum(i, npt - 1), 0)),
                   pl.BlockSpec((MOE_TOKENS, D_MODEL), lambda i, *_: (jnp.maximum(i - npt, 0), 0))],
        scratch_shapes=[pltpu.VMEM((2, MOE_SLOTS, D_MODEL), F32),
                        pltpu.SemaphoreType.DMA((2,))])
    return pl.pallas_call(
        _combine_body,
        grid_spec=grid_spec,
        out_shape=[jax.ShapeDtypeStruct((T_PROMPT, D_MODEL), F32),
                   jax.ShapeDtypeStruct((T_SAMPLE, D_MODEL), F32)],
        compiler_params=pltpu.CompilerParams(dimension_semantics=("arbitrary",),
                                             vmem_limit_bytes=VMEM_LIMIT),
        name="moe_combine",
    )(sched["nch"], sched["loc"], sched["start"], h1, route, gfin, y_sorted)


def _moe_schedule(seg):
    tot = jnp.sum(seg, axis=0)
    region = (tot + GMM_ROWS - 1) // GMM_ROWS * GMM_ROWS
    base = jnp.cumsum(region) - region
    start = base[None, :] + jnp.cumsum(seg, axis=0) - seg
    loc = jnp.cumsum(seg, axis=1) - seg
    ntile = region // GMM_ROWS
    cum = jnp.cumsum(ntile)
    nu = cum[-1]
    i = jnp.arange(GMM_TILES, dtype=jnp.int32)[:, None]
    first = (cum - ntile)[None, :]
    mine = (i >= first) & (i < cum[None, :])
    experts = jnp.arange(N_EXPERTS, dtype=jnp.int32)
    te = jnp.sum(jnp.where(mine, experts[None, :], 0), axis=1)
    tv = jnp.sum(jnp.where(mine, jnp.clip(tot[None, :] - (i - first) * GMM_ROWS, 0, GMM_ROWS), 0),
                 axis=1)
    te = jnp.where(i[:, 0] < nu, te, jnp.max(jnp.where(ntile > 0, experts, 0)))
    bstart = jnp.arange(GMM_TILES * GMM_ROWS // GMM_SUB, dtype=jnp.int32)[:, None] * GMM_SUB
    used_end = (base + (tot + GMM_SUB - 1) // GMM_SUB * GMM_SUB)[None, :]
    zb = jnp.any((bstart >= used_end) & (bstart < (base + region)[None, :]), axis=1)
    zb = (zb | (bstart[:, 0] >= jnp.sum(region))).astype(jnp.int32)
    return dict(nch=(seg // SUBLANES).reshape(-1), loc=loc.reshape(-1), start=start.reshape(-1),
                zn=((-tot) % GMM_SUB) // SUBLANES, zstart=base + tot,
                zb=zb, zbn=jnp.sum(zb).reshape(1),
                te=te, tv=tv.astype(jnp.int32), nu=nu.reshape(1).astype(jnp.int32))


def _row(x, width=None):
    x = x.astype(F32).reshape(1, -1)
    if width is not None and x.shape[1] < width:
        x = jnp.pad(x, ((0, 0), (0, width - x.shape[1])))
    return x


def _layer_params(l, lb_p, conv_w, conv_b, a_log, dt_bias, d_skip, hg_norm, ssd_norm):
    lb = jnp.sum(lb_p[1:l + 1], axis=0)
    return dict(
        la=_row(jnp.log(lb)), lc=_row(jnp.log1p(-lb)), omlb=_row(1.0 - lb),
        hgn=_row(hg_norm[l]), cw=conv_w[l].astype(F32), cb=_row(conv_b[l]),
        dtb=_row(dt_bias[l], LANES), an=_row(-jnp.exp(a_log[l].astype(F32)), LANES),
        dsk=_row(jnp.repeat(d_skip[l].astype(F32), SSD_P)), ssn=_row(ssd_norm[l]))


def kernel(x_prompt, x_sample, state_hgrn, state_ssm, state_conv, norm_mix, w_in, conv_w, conv_b,
           a_log, dt_bias, d_skip, lb_param, hg_norm, ssd_norm, w_out, norm_ffn, w_ffn_in,
           w_ffn_out, w_router, w_exp_in, w_exp_out, norm_final):
    h = (x_prompt.reshape(T_PROMPT, D_MODEL), x_sample.reshape(T_SAMPLE, D_MODEL))
    lb_p = jax.nn.softmax(lb_param.astype(F32), axis=0)
    packed = (SSD_PAIRS, LANES, SSD_N)
    s_ssm_t = jnp.swapaxes(state_ssm, 3, 4).reshape((DEPTH, DEC_BATCH) + packed)
    s_conv_t = jnp.swapaxes(state_conv, 1, 2)
    outs = {k: [] for k in ("hg_p", "ssm_p", "conv_p", "conv_s")}
    sample_states = None
    for l in range(DEPTH):
        prm = _layer_params(l, lb_p, conv_w, conv_b, a_log, dt_bias, d_skip, hg_norm, ssd_norm)
        proj = _proj_call(l, h, _row(norm_mix[l]), w_in)
        mixed_p, hg_p, ssm_p, conv_p, mixed_s, hg_s, ssm_s, conv_s = _scan_call(
            l, proj, state_hgrn, s_ssm_t, s_conv_t, prm, sample_states)
        sample_states = (hg_s, ssm_s)
        mixed = (mixed_p, mixed_s)
        for k, val in zip(outs, (hg_p, ssm_p, conv_p, conv_s)):
            outs[k].append(val)
        wo_bf = w_out[l].astype(BF16)
        if l % 2 == 0:
            h = (_out_ffn_call(h, mixed, wo_bf, _row(norm_ffn[l]), w_ffn_in[l // 2].astype(BF16),
                               w_ffn_out[l // 2].astype(BF16)),)
        else:
            wr_pad = jnp.pad(w_router[l // 2].astype(F32), ((0, 0), (0, LANES - N_EXPERTS)))
            h1, hn_bf, route, seg = _out_router_call(h[0], mixed, wo_bf, _row(norm_ffn[l]), wr_pad)
            sched = _moe_schedule(seg[:, 0, :N_EXPERTS])
            x_sorted = _dispatch_call(sched, hn_bf, route)
            y_sorted = _experts_call(sched, x_sorted, w_exp_in[l // 2], w_exp_out[l // 2])
            h = _combine_call(sched, h1, route, _row(norm_final), y_sorted)
    y_prompt = h[0].reshape(BATCH, SEQ, D_MODEL)
    y_sample = h[1].reshape(DEC_BATCH, DEC_SEQ, D_MODEL)
    def unpack_ssm(s_t, batch):
        return jnp.swapaxes(s_t.reshape(DEPTH, batch, SSD_HEADS, SSD_P, SSD_N), 3, 4)

    return (y_prompt, y_sample, jnp.stack(outs["hg_p"]), unpack_ssm(jnp.stack(outs["ssm_p"]), BATCH),
            jnp.stack(outs["conv_p"]), sample_states[0], unpack_ssm(sample_states[1], DEC_BATCH),
            jnp.swapaxes(jnp.stack(outs["conv_s"]), 1, 2))
```

```python
import functools

import jax
import jax.numpy as jnp
import numpy as np
from jax import lax
from jax.experimental import pallas as pl
from jax.experimental.pallas import tpu as pltpu

F32 = jnp.float32
BF16 = jnp.bfloat16

D_MODEL = 1024
BATCH = 8
SEQ = 2048
DEPTH = 2
DEC_BATCH = 128
DEC_SEQ = 8
HG_HEADS = 4
HG_D = 128
HG_W = HG_HEADS * HG_D
SSD_HEADS = 8
SSD_P = 64
SSD_N = 128
SSD_W = SSD_HEADS * SSD_P
SSD_PAIRS = SSD_HEADS // 2
CONV_W = 4
CONV_DIM = 1024
D_FF = 2816
N_EXPERTS = 8
D_FF_EXPERT = 3584
EPS = 1e-6

LANES = 128
C_Q, C_F, C_I, C_OG, C_Z, C_XBC, C_DT = 0, 512, 1024, 1536, 2048, 2560, 3584
PROJ_W = C_DT + LANES
HG_CHUNK = 64
SSD_CHUNK = 128
SAMPLE_SEQS = 4
PROMPT_SEQS = 4
HG_SAFE_RANGE = 80.0
LOG2 = 0.6931471805599453

T_PROMPT = BATCH * SEQ
T_SAMPLE = DEC_BATCH * DEC_SEQ
T_ALL = T_PROMPT + T_SAMPLE

SUBLANES = 8
TOP_K = 2
MOE_TOKENS = 256
MOE_TILES = T_ALL // MOE_TOKENS
ROUTER_TILES = 1
MOE_SLOTS = TOP_K * MOE_TOKENS + N_EXPERTS * SUBLANES
SEG_COPY_ROWS = (64, 32, 16, 8)
GMM_ROWS = 1024
GMM_SUB = 256
GMM_TILES = (TOP_K * T_ALL + MOE_TILES * N_EXPERTS * (SUBLANES - 1)
             + N_EXPERTS * (GMM_ROWS - 1)) // GMM_ROWS + 1

VMEM_LIMIT = 60000 * 1024


def _sigmoid(x):
    return 0.5 * jnp.tanh(0.5 * x) + 0.5


def _silu(x):
    return x * _sigmoid(x)


def _softplus(x):
    return jnp.maximum(x, 0.0) + jnp.log(1.0 + jnp.exp(-jnp.abs(x)))


def _rms(x, g):
    return x * lax.rsqrt(jnp.mean(x * x, axis=-1, keepdims=True) + EPS) * g


def _dot(a, b):
    return jnp.dot(a.astype(BF16), b.astype(BF16), preferred_element_type=F32)


def _dot_nt(a, b):
    return lax.dot_general(a.astype(BF16), b.astype(BF16), (((1,), (1,)), ((), ())),
                           preferred_element_type=F32)


def _split3(x):
    x1 = x.astype(BF16)
    r1 = x - x1.astype(F32)
    x2 = r1.astype(BF16)
    x3 = (r1 - x2.astype(F32)).astype(BF16)
    return x1, x2, x3


def _mask_dot(m, x):
    mb = m.astype(BF16)
    x1, x2, x3 = _split3(x)
    return (jnp.dot(mb, x1, preferred_element_type=F32) + jnp.dot(mb, x2, preferred_element_type=F32)
            + jnp.dot(mb, x3, preferred_element_type=F32))


def _mask_dot_nt(x, m):
    mb = m.astype(BF16)
    dn = (((1,), (1,)), ((), ()))
    x1, x2, x3 = _split3(x)
    return (lax.dot_general(x1, mb, dn, preferred_element_type=F32)
            + lax.dot_general(x2, mb, dn, preferred_element_type=F32)
            + lax.dot_general(x3, mb, dn, preferred_element_type=F32))


def _dot_f32x3(a, b):
    a1, a2, _ = _split3(a)
    b1, b2, _ = _split3(b)
    return (jnp.dot(a1, b1, preferred_element_type=F32) + jnp.dot(a1, b2, preferred_element_type=F32)
            + jnp.dot(a2, b1, preferred_element_type=F32))


def _iota2(shape, dim):
    return lax.broadcasted_iota(jnp.int32, shape, dim)


def _seq_masks(rows, seq_len):
    t = _iota2((rows, rows), 0)
    s = _iota2((rows, rows), 1)
    same = (t // seq_len) == (s // seq_len)
    causal = same & (s <= t)
    ref = same & ((s % seq_len) < seq_len // 2)
    return causal, ref, same


def _row_specs(arrs, tm):
    if len(arrs) == 1:
        return [pl.BlockSpec((tm, arrs[0].shape[1]), lambda i, *_: (i, 0))]
    npt = T_PROMPT // tm
    return [pl.BlockSpec((tm, arrs[0].shape[1]), lambda i, *_: (jnp.minimum(i, npt - 1), 0)),
            pl.BlockSpec((tm, arrs[1].shape[1]), lambda i, *_: (jnp.maximum(i - npt, 0), 0),
                         pipeline_mode=pl.Buffered(1))]


def _rows(i, refs):
    if len(refs) == 1:
        return refs[0][...]
    return jnp.where(i < T_PROMPT // refs[0].shape[0], refs[0][...], refs[1][...])


def _proj_body(n_h, *refs):
    h_refs, (g_ref, w_ref, o_ref, wb_ref) = refs[:n_h], refs[n_h:]
    i = pl.program_id(0)

    @pl.when(i == 0)
    def _():
        for r0 in range(0, C_DT, HG_W):
            wb_ref[r0:r0 + HG_W, :] = w_ref[r0:r0 + HG_W, :].astype(BF16)
        tail = w_ref[C_DT:, :]
        pad = jnp.zeros((LANES - tail.shape[0], D_MODEL), F32)
        wb_ref[C_DT:, :] = jnp.concatenate([tail, pad], axis=0).astype(BF16)

    hn = _rms(_rows(i, h_refs), g_ref[...])
    o_ref[...] = _dot_nt(hn, wb_ref[...])


def _proj_call(l, h, g, w_in):
    tm = 512
    w_t = jnp.swapaxes(w_in, 1, 2)
    return pl.pallas_call(
        functools.partial(_proj_body, len(h)),
        grid=(T_ALL // tm,),
        in_specs=_row_specs(h, tm) + [pl.BlockSpec((1, D_MODEL), lambda i: (0, 0)),
                                      pl.BlockSpec((None,) + w_t.shape[1:], lambda i: (l, 0, 0),
                                                   pipeline_mode=pl.Buffered(1))],
        out_specs=pl.BlockSpec((tm, PROJ_W), lambda i: (i, 0)),
        out_shape=jax.ShapeDtypeStruct((T_ALL, PROJ_W), F32),
        scratch_shapes=[pltpu.VMEM((PROJ_W, D_MODEL), BF16)],
        compiler_params=pltpu.CompilerParams(dimension_semantics=("arbitrary",),
                                             vmem_limit_bytes=VMEM_LIMIT),
        name="in_proj",
    )(*h, g, w_t)


def _hgrn_gates(p_q, p_f, la, lc, omlb):
    q = _silu(p_q)
    e = jnp.exp(-jnp.abs(p_f))
    b = lc + jnp.minimum(p_f, 0.0) - jnp.log(1.0 + e)
    logf = jnp.maximum(la, b) + jnp.log(1.0 + jnp.exp(-jnp.abs(la - b)))
    r = 1.0 / (1.0 + e)
    kin = omlb * jnp.where(p_f >= 0.0, e * r, r)
    return q, logf, kin


def _hgrn_decays(logf, causal, ref, same):
    rows = logf.shape[0]
    if ref is None:
        g = _mask_dot(causal, logf)
        return g, g[rows // 2 - 1:rows // 2, :], g[rows - 1:rows, :]
    m = jnp.concatenate([causal.astype(F32), ref.astype(F32), same.astype(F32)], axis=0)
    g3 = _mask_dot(m, logf)
    return g3[:rows], g3[rows:2 * rows], g3[2 * rows:]


def _hgrn_exact_intra(ex_ref, q, kin, g, v, seq_len):
    rows = q.shape[0]
    ex_ref[0] = g
    ex_ref[1] = q
    ex_ref[2] = kin
    ex_ref[3] = v
    s_idx = _iota2((rows, HG_W), 0)

    def body(t, carry):
        gt = ex_ref[0, pl.ds(t, 1), :]
        qt = ex_ref[1, pl.ds(t, 1), :]
        live = (s_idx <= t) & (s_idx // seq_len == t // seq_len)
        w = jnp.where(live, qt * ex_ref[2] * jnp.exp(jnp.minimum(gt - ex_ref[0], 0.0)), 0.0)
        outs = []
        for h in range(HG_HEADS):
            sl = slice(h * HG_D, (h + 1) * HG_D)
            score = jnp.sum(w[:, sl], axis=1, keepdims=True)
            outs.append(jnp.sum(score * ex_ref[3, :, sl], axis=0, keepdims=True))
        ex_ref[4, pl.ds(t, 1), :] = jnp.concatenate(outs, axis=1)
        return carry

    lax.fori_loop(0, rows, body, 0)
    return ex_ref[4]


def _hgrn_head_out(o, hgn, og):
    return _rms(o, hgn) * _silu(og)


def _ssd_inputs(conv, p_dt, dtb, a_neg):
    conv = _silu(conv)
    xs = conv[:, :SSD_W]
    bm = conv[:, SSD_W:SSD_W + 2 * SSD_N]
    cm = conv[:, SSD_W + 2 * SSD_N:]
    dtf = _softplus(p_dt + dtb)
    a = dtf * a_neg
    return xs, bm, cm, dtf, a


def _lane_bcast(x, lane, width):
    return jnp.broadcast_to(x[:, lane:lane + 1], (x.shape[0], width))


def _ssd_finish(y, xs, z, dsk, ssn):
    y = (y + dsk * xs) * _silu(z)
    half = SSD_W // 2
    return jnp.concatenate([_rms(y[:, :half], ssn[:, :half]), _rms(y[:, half:], ssn[:, half:])],
                           axis=1)


def _scan_body(n_prev, *refs):
    ns = PROMPT_SEQS
    p_refs, refs = refs[:ns], refs[ns:]
    (ps_ref, shg_ref, sssm_ref, sconv_ref), refs = refs[:4], refs[4:]
    prm_ref, refs = refs[0], refs[1:]
    la_ref, lc_ref, omlb_ref, hgn_ref = (prm_ref.at[r:r + 1, 0:HG_W] for r in range(4))
    params = (hgn_ref, prm_ref.at[4:4 + CONV_W, :], prm_ref.at[8:9, :], prm_ref.at[9:10, 0:LANES],
              prm_ref.at[10:11, 0:LANES], prm_ref.at[11:12, 0:SSD_W], prm_ref.at[12:13, 0:SSD_W])
    prev_refs, refs = refs[:2 * bool(n_prev)], refs[2 * bool(n_prev):]
    (mix_ref, ohg_ref, ossm_ref, oconv_ref, mixs_ref, ohgs_ref, ossms_ref, oconvs_ref,
     st_ref, hp_ref, xpad_ref, ex_ref, xpads_ref, exs_ref) = refs
    j = pl.program_id(1)
    tl = SSD_CHUNK

    @pl.when(j == 0)
    def _():
        st_ref[...] = jnp.zeros_like(st_ref)
        hp_ref[...] = jnp.zeros_like(hp_ref)
        xpad_ref[:, 0:8, :] = jnp.zeros((ns, 8, CONV_DIM), F32)

    def decay_bound(z):
        low = jnp.maximum(la_ref[...], lc_ref[...] + jnp.minimum(z, 0.0) - LOG2)
        return jnp.sum(low, axis=0, keepdims=True)

    bounds = [decay_bound(p_refs[s][r0:r0 + HG_CHUNK // 2, C_F:C_F + HG_W])
              for s in range(ns) for r0 in range(0, tl, HG_CHUNK // 2)]
    bounds += [decay_bound(ps_ref[b * DEC_SEQ:(b + 1) * DEC_SEQ, C_F:C_F + HG_W])
               for b in range(SAMPLE_SEQS)]
    safe = jnp.min(functools.reduce(jnp.minimum, bounds)) >= -HG_SAFE_RANGE
    gate_refs = (la_ref, lc_ref, omlb_ref)

    for exact in (False, True):
        @pl.when(safe != exact)
        def _(exact=exact):
            for s in range(ns):
                _scan_prompt_tile(exact, ex_ref, p_refs[s], gate_refs, *params, mix_ref.at[s],
                                  st_ref.at[s], hp_ref.at[s], xpad_ref.at[s])
            seq0 = (pl.program_id(0) * pl.num_programs(1) + j) % (SUBLANES // SAMPLE_SEQS) * SAMPLE_SEQS
            _scan_sample_tile(n_prev, exact, seq0, ps_ref, shg_ref, sssm_ref, sconv_ref, prev_refs,
                              gate_refs, *params, mixs_ref, ohgs_ref, ossms_ref, oconvs_ref,
                              xpads_ref, exs_ref)

    @pl.when(j == pl.num_programs(1) - 1)
    def _():
        for s in range(ns):
            oconv_ref[s] = xpad_ref[s, 5:8, :]
            for h in range(HG_HEADS):
                ohg_ref[s, h] = st_ref[s, h].T
        ossm_ref[...] = hp_ref[...]


def _scan_prompt_tile(exact, ex_ref, p_ref, gate_refs, hgn_ref, cw_ref, cb_ref, dtb_ref, an_ref,
                      dsk_ref, ssn_ref, mix_ref, st_ref, hp_ref, xpad_ref):
    tl = SSD_CHUNK
    causal, _, _ = _seq_masks(HG_CHUNK, HG_CHUNK)
    for c in range(tl // HG_CHUNK):
        r0 = c * HG_CHUNK
        rows = slice(r0, r0 + HG_CHUNK)
        q, logf, kin = _hgrn_gates(p_ref[rows, C_Q:C_Q + HG_W], p_ref[rows, C_F:C_F + HG_W],
                                   *(r[...] for r in gate_refs))
        v = p_ref[rows, C_I:C_I + HG_W]
        g, gmid, glast = _hgrn_decays(logf, causal, None, None)
        if exact:
            qh = q * jnp.exp(g)
            kd = kin * jnp.exp(glast - g)
            o_intra = _hgrn_exact_intra(ex_ref, q, kin, g, v, HG_CHUNK)
        else:
            qt = q * jnp.exp(g - gmid)
            kt = kin * jnp.exp(gmid - g)
            qh = qt * jnp.exp(gmid)
            kd = kt * jnp.exp(glast - gmid)
        ds = jnp.exp(glast[0:1, :])
        for h in range(HG_HEADS):
            sl = slice(h * HG_D, (h + 1) * HG_D)
            st = st_ref[h]
            if exact:
                o = o_intra[:, sl] + _dot_nt(qh[:, sl], st)
            else:
                sc = jnp.where(causal, _dot_nt(qt[:, sl], kt[:, sl]), 0.0)
                o = _dot(sc, v[:, sl]) + _dot_nt(qh[:, sl], st)
            st_ref[h] = st * ds[:, sl] + _dot(v[:, sl].T, kd[:, sl])
            og = p_ref[rows, C_OG + h * HG_D:C_OG + (h + 1) * HG_D]
            mix_ref[rows, sl] = _hgrn_head_out(o, hgn_ref[:, sl], og).astype(BF16)

    xpad_ref[8:8 + tl, :] = p_ref[:, C_XBC:C_XBC + CONV_DIM]
    conv = cb_ref[...] + cw_ref[0:1, :] * xpad_ref[5:5 + tl, :]
    for tap in range(1, CONV_W):
        conv = conv + cw_ref[tap:tap + 1, :] * xpad_ref[5 + tap:5 + tap + tl, :]
    xpad_ref[0:8, :] = xpad_ref[tl:tl + 8, :]

    xs, bm, cm, dtf, a = _ssd_inputs(conv, p_ref[:, C_DT:C_DT + LANES], dtb_ref[...], an_ref[...])
    tri, _, _ = _seq_masks(tl, tl)
    trif = tri.astype(F32)
    acum = _mask_dot(trif, a)
    acum_t = _mask_dot_nt(a.T, trif)
    lo = _iota2((tl, LANES), 1) < SSD_P
    top = _iota2((LANES, tl), 0) < SSD_P
    ys = []
    for pair in range(SSD_PAIRS):
        grp = pair // 2
        cg = cm[:, grp * SSD_N:(grp + 1) * SSD_N]
        bg = bm[:, grp * SSD_N:(grp + 1) * SSD_N]
        cb = _dot_nt(cg, bg)
        psl = slice(pair * LANES, (pair + 1) * LANES)
        r0, r1 = 2 * pair, 2 * pair + 1
        xp = xs[:, psl] * jnp.where(lo, _lane_bcast(dtf, r0, LANES), _lane_bcast(dtf, r1, LANES))
        intra, einter, wrow, ea = [], [], [], []
        for r in (r0, r1):
            col = _lane_bcast(acum, r, tl)
            row = jnp.broadcast_to(acum_t[r:r + 1, :], (tl, tl))
            dec = jnp.exp(jnp.where(tri, col - row, -jnp.inf))
            intra.append(_dot(cb * dec, xp))
            einter.append(jnp.exp(col))
            alast = acum_t[r:r + 1, tl - 1:tl]
            wrow.append(jnp.exp(alast - acum_t[r:r + 1, :]))
            ea.append(jnp.exp(alast))
        hp = hp_ref[pair]
        y = jnp.where(lo, intra[0], intra[1]) + jnp.where(lo, einter[0], einter[1]) * _dot_nt(cg, hp)
        xw_t = xp.T * jnp.where(top, wrow[0], wrow[1])
        hp_ref[pair] = jnp.where(top, ea[0], ea[1]) * hp + _dot(xw_t, bg)
        ys.append(y)
    y = jnp.concatenate(ys, axis=1)
    y = _ssd_finish(y, xs, p_ref[:, C_Z:C_Z + SSD_W], dsk_ref[...], ssn_ref[...])
    mix_ref[:, HG_W:] = y.astype(BF16)


def _row_spec(width):
    return pl.BlockSpec((1, width), lambda b, j: (0, 0))


def _scan_call(l, proj, s_hg, s_ssm_t, s_conv_t, prm, prev):
    tl = SSD_CHUNK
    nt = SEQ // tl
    ns, nb = PROMPT_SEQS, SAMPLE_SEQS
    assert BATCH // ns * nt * nb == DEC_BATCH
    srows = nb * DEC_SEQ
    srow0 = T_PROMPT // srows
    step = lambda b, j: b * nt + j
    pstate, sstate = (ns, HG_HEADS, HG_D, HG_D), (nb, HG_HEADS, HG_D, HG_D)
    in_specs = [pl.BlockSpec((tl, PROJ_W), lambda b, j, s=s: ((b * ns + s) * nt + j, 0))
                for s in range(ns)]
    in_specs += [pl.BlockSpec((srows, PROJ_W), lambda b, j: (srow0 + step(b, j), 0)),
                 pl.BlockSpec((None,) + sstate, lambda b, j: (l, step(b, j), 0, 0, 0)),
                 pl.BlockSpec((None,) + sstate, lambda b, j: (l, step(b, j), 0, 0, 0)),
                 pl.BlockSpec((None, CONV_W - 1, SUBLANES, CONV_DIM),
                              lambda b, j: (l, 0, step(b, j) // (SUBLANES // nb), 0))]
    in_specs += [pl.BlockSpec(prm.shape, lambda b, j: (0, 0))]
    in_specs += [pl.BlockSpec((l,) + sstate, lambda b, j: (0, step(b, j), 0, 0, 0))] * (2 if l else 0)
    out_specs = [pl.BlockSpec((ns, tl, D_MODEL), lambda b, j: (b, j, 0)),
                 pl.BlockSpec(pstate, lambda b, j: (b, 0, 0, 0)),
                 pl.BlockSpec(pstate, lambda b, j: (b, 0, 0, 0)),
                 pl.BlockSpec((ns, CONV_W - 1, CONV_DIM), lambda b, j: (b, 0, 0)),
                 pl.BlockSpec((srows, D_MODEL), lambda b, j: (step(b, j), 0)),
                 pl.BlockSpec((l + 1,) + sstate, lambda b, j: (0, step(b, j), 0, 0, 0)),
                 pl.BlockSpec((l + 1,) + sstate, lambda b, j: (0, step(b, j), 0, 0, 0)),
                 pl.BlockSpec((CONV_W - 1, SUBLANES, CONV_DIM),
                              lambda b, j: (0, step(b, j) // (SUBLANES // nb), 0))]
    stacked = jax.ShapeDtypeStruct((l + 1, DEC_BATCH) + sstate[1:], F32)
    out_shape = [jax.ShapeDtypeStruct((BATCH, SEQ, D_MODEL), BF16),
                 jax.ShapeDtypeStruct((BATCH,) + pstate[1:], F32),
                 jax.ShapeDtypeStruct((BATCH,) + pstate[1:], F32),
                 jax.ShapeDtypeStruct((BATCH, CONV_W - 1, CONV_DIM), F32),
                 jax.ShapeDtypeStruct((T_SAMPLE, D_MODEL), BF16), stacked, stacked,
                 jax.ShapeDtypeStruct((CONV_W - 1, DEC_BATCH, CONV_DIM), F32)]
    outs = pl.pallas_call(
        functools.partial(_scan_body, l),
        grid=(BATCH // ns, nt),
        in_specs=in_specs, out_specs=out_specs, out_shape=out_shape,
        scratch_shapes=[pltpu.VMEM(pstate, F32), pltpu.VMEM(pstate, F32),
                        pltpu.VMEM((ns, tl + 8, CONV_DIM), F32),
                        pltpu.VMEM((5, HG_CHUNK, HG_W), F32),
                        pltpu.VMEM((16 * nb, CONV_DIM), F32), pltpu.VMEM((5, srows, HG_W), F32)],
        compiler_params=pltpu.CompilerParams(dimension_semantics=("arbitrary", "arbitrary"),
                                             vmem_limit_bytes=VMEM_LIMIT),
        name="scan",
    )(*([proj] * (ns + 1)), s_hg, s_ssm_t, s_conv_t, prm, *(prev if l else ()))
    return (outs[0].reshape(T_PROMPT, D_MODEL),) + tuple(outs[1:])


def _scan_sample_tile(n_prev, exact, seq0, p_ref, shg_ref, sssm_ref, sconv_ref, prev_refs,
                      gate_refs, hgn_ref, cw_ref, cb_ref, dtb_ref, an_ref, dsk_ref, ssn_ref,
                      mix_ref, ohg_ref, ossm_ref, oconv_ref, xpad_ref, ex_ref):
    if n_prev:
        ohg_ref[0:n_prev] = prev_refs[0][...]
        ossm_ref[0:n_prev] = prev_refs[1][...]
    nb, sl_len = SAMPLE_SEQS, DEC_SEQ
    rows = nb * sl_len
    causal, ref, same = _seq_masks(rows, sl_len)
    rowseq = _iota2((rows, LANES), 0) // sl_len

    q, logf, kin = _hgrn_gates(p_ref[:, C_Q:C_Q + HG_W], p_ref[:, C_F:C_F + HG_W],
                               *(r[...] for r in gate_refs))
    v = p_ref[:, C_I:C_I + HG_W]
    g, gmid, glast = _hgrn_decays(logf, causal, ref, same)
    if exact:
        o_intra_all = _hgrn_exact_intra(ex_ref, q, kin, g, v, sl_len)
    else:
        qt = q * jnp.exp(g - gmid)
        kt = kin * jnp.exp(gmid - g)
    qh = q * jnp.exp(g)
    kd = kin * jnp.exp(glast - g)
    ds_t = jnp.exp(glast).T
    kd_t = kd.T
    for h in range(HG_HEADS):
        sl = slice(h * HG_D, (h + 1) * HG_D)
        if exact:
            o_intra = o_intra_all[:, sl]
        else:
            sc = jnp.where(causal, _dot_nt(qt[:, sl], kt[:, sl]), 0.0)
            o_intra = _dot(sc, v[:, sl])
        o_inter = []
        for b in range(nb):
            s_old = shg_ref[b, h]
            o_inter.append(_dot(qh[b * sl_len:(b + 1) * sl_len, sl], s_old))
            vb = jnp.where(rowseq == b, v[:, sl], 0.0)
            dcol = _lane_bcast(ds_t[sl, :], b * sl_len, HG_D)
            ohg_ref[n_prev, b, h] = dcol * s_old + _dot(kd_t[sl, :], vb)
        o = o_intra + jnp.concatenate(o_inter, axis=0)
        og = p_ref[:, C_OG + h * HG_D:C_OG + (h + 1) * HG_D]
        mix_ref[:, sl] = _hgrn_head_out(o, hgn_ref[:, sl], og).astype(BF16)

    convs = []
    for b in range(nb):
        base = 16 * b
        for r in range(CONV_W - 1):
            xpad_ref[base + 5 + r:base + 6 + r, :] = sconv_ref[r, pl.ds(seq0 + b, 1), :]
        xpad_ref[base + 8:base + 16, :] = p_ref[b * sl_len:(b + 1) * sl_len, C_XBC:C_XBC + CONV_DIM]
        cv = cb_ref[...] + cw_ref[0:1, :] * xpad_ref[base + 5:base + 13, :]
        for tap in range(1, CONV_W):
            cv = cv + cw_ref[tap:tap + 1, :] * xpad_ref[base + 5 + tap:base + 13 + tap, :]
        convs.append(cv)
        for r in range(CONV_W - 1):
            oconv_ref[r, pl.ds(seq0 + b, 1), :] = xpad_ref[base + 13 + r:base + 14 + r, :]
    conv = jnp.concatenate(convs, axis=0)

    xs, bm, cm, dtf, a = _ssd_inputs(conv, p_ref[:, C_DT:C_DT + LANES], dtb_ref[...], an_ref[...])
    causf = causal.astype(F32)
    acum = _mask_dot(causf, a)
    a_t = a.T
    acum_t = _mask_dot_nt(a_t, causf)
    atot_t = _mask_dot_nt(a_t, same.astype(F32))
    wall_t = jnp.exp(atot_t - acum_t)
    eatot_t = jnp.exp(atot_t)
    eall = jnp.exp(acum)
    lo = _iota2((rows, LANES), 1) < SSD_P
    top = _iota2((LANES, rows), 0) < SSD_P
    top_sq = _iota2((LANES, SSD_N), 0) < SSD_P
    colseq = _iota2((LANES, rows), 1) // sl_len
    ys = []
    for pair in range(SSD_PAIRS):
        grp = pair // 2
        cg = cm[:, grp * SSD_N:(grp + 1) * SSD_N]
        bg = bm[:, grp * SSD_N:(grp + 1) * SSD_N]
        cb = _dot_nt(cg, bg)
        psl = slice(pair * LANES, (pair + 1) * LANES)
        r0, r1 = 2 * pair, 2 * pair + 1
        xp = xs[:, psl] * jnp.where(lo, _lane_bcast(dtf, r0, LANES), _lane_bcast(dtf, r1, LANES))
        xw_t = xp.T * jnp.where(top, wall_t[r0:r0 + 1, :], wall_t[r1:r1 + 1, :])
        intra = []
        for r in (r0, r1):
            col = _lane_bcast(acum, r, rows)
            row = jnp.broadcast_to(acum_t[r:r + 1, :], (rows, rows))
            dec = jnp.exp(jnp.where(causal, col - row, -jnp.inf))
            intra.append(_dot(cb * dec, xp))
        y_intra = jnp.where(lo, intra[0], intra[1])
        y_inter = []
        for b in range(nb):
            tr = slice(b * sl_len, (b + 1) * sl_len)
            h_old = sssm_ref[b, pair]
            y_inter.append(_dot_nt(cg[tr, :], h_old))
            upd = _dot(jnp.where(colseq == b, xw_t, 0.0), bg)
            c0 = b * sl_len
            ea = jnp.where(top_sq, eatot_t[r0:r0 + 1, c0:c0 + 1], eatot_t[r1:r1 + 1, c0:c0 + 1])
            ossm_ref[n_prev, b, pair] = ea * h_old + upd
        e_pair = jnp.where(lo, _lane_bcast(eall, r0, LANES), _lane_bcast(eall, r1, LANES))
        ys.append(y_intra + e_pair * jnp.concatenate(y_inter, axis=0))
    y = jnp.concatenate(ys, axis=1)
    y = _ssd_finish(y, xs, p_ref[:, C_Z:C_Z + SSD_W], dsk_ref[...], ssn_ref[...])
    mix_ref[:, HG_W:] = y.astype(BF16)


def _out_ffn_body(n_h, *refs):
    h_refs, mix_refs = refs[:n_h], refs[n_h:n_h + 2]
    wo_ref, g_ref, wg_ref, wu_ref, wd_ref, o_ref, hn_ref = refs[n_h + 2:]
    i = pl.program_id(0)
    f = pl.program_id(1)

    @pl.when(f == 0)
    def _():
        h1 = _rows(i, h_refs) + _dot(_rows(i, mix_refs), wo_ref[...])
        o_ref[...] = h1
        hn_ref[...] = _rms(h1, g_ref[...]).astype(BF16)

    o_ref[...] += _swiglu_part(hn_ref[...], wg_ref[...], wu_ref[...], wd_ref[...])


def _out_ffn_call(h, mixed, wo_bf, g, wi_bf, wd_bf):
    tm, tf = 1024, D_FF // 2
    nf = D_FF // tf
    return pl.pallas_call(
        functools.partial(_out_ffn_body, len(h)),
        grid=(T_ALL // tm, nf),
        in_specs=_row_specs(h, tm) + _row_specs(mixed, tm) + [
                  pl.BlockSpec((D_MODEL, D_MODEL), lambda i, f: (0, 0)),
                  pl.BlockSpec((1, D_MODEL), lambda i, f: (0, 0)),
                  pl.BlockSpec((D_MODEL, tf), lambda i, f: (0, f)),
                  pl.BlockSpec((D_MODEL, tf), lambda i, f: (0, nf + f)),
                  pl.BlockSpec((tf, D_MODEL), lambda i, f: (f, 0))],
        out_specs=pl.BlockSpec((tm, D_MODEL), lambda i, f: (i, 0)),
        out_shape=jax.ShapeDtypeStruct((T_ALL, D_MODEL), F32),
        scratch_shapes=[pltpu.VMEM((tm, D_MODEL), BF16)],
        compiler_params=pltpu.CompilerParams(dimension_semantics=("arbitrary", "arbitrary"),
                                             vmem_limit_bytes=VMEM_LIMIT),
        name="out_ffn",
    )(*h, *mixed, wo_bf, g, wi_bf, wi_bf, wd_bf)


def _out_router_body(h_ref, mixp_ref, mixs_ref, wo_ref, g_ref, wr_ref,
                     h1_ref, hn_ref, route_ref, seg_ref):
    h1 = h_ref[...] + _dot(_rows(pl.program_id(0), (mixp_ref, mixs_ref)), wo_ref[...])
    h1_ref[...] = h1
    hn = _rms(h1, g_ref[...])
    hn_ref[...] = hn.astype(BF16)
    logits = _dot_f32x3(hn, wr_ref[...])
    lane = _iota2(logits.shape, 1)
    lg = jnp.where(lane < N_EXPERTS, logits, -jnp.inf)
    m1 = jnp.max(lg, axis=1, keepdims=True)
    i1 = jnp.min(jnp.where(lg == m1, lane, LANES), axis=1, keepdims=True)
    lg2 = jnp.where(lane == i1, -jnp.inf, lg)
    m2 = jnp.max(lg2, axis=1, keepdims=True)
    i2 = jnp.min(jnp.where(lg2 == m2, lane, LANES), axis=1, keepdims=True)
    e2 = jnp.exp(m2 - m1)
    g1 = 1.0 / (1.0 + e2)
    g2 = e2 / (1.0 + e2)

    tm = logits.shape[0]
    chosen = jnp.where((lane == i1) | (lane == i2), 1.0, 0.0)
    t_row, t_col = _iota2((tm, tm), 0), _iota2((tm, tm), 1)
    earlier = ((t_col < t_row) & (t_col // MOE_TOKENS == t_row // MOE_TOKENS)).astype(F32)
    rank = _dot(earlier, chosen)
    below = (_iota2((LANES, LANES), 0) < _iota2((LANES, LANES), 1)).astype(F32)
    pos = []
    for k in range(tm // MOE_TOKENS):
        rows = slice(k * MOE_TOKENS, (k + 1) * MOE_TOKENS)
        cnt = jnp.sum(chosen[rows], axis=0, keepdims=True)
        seg = jnp.floor((cnt + (SUBLANES - 1)) * (1.0 / SUBLANES)) * SUBLANES
        seg8 = jnp.broadcast_to(seg, (SUBLANES, LANES))
        pos.append(_dot(seg8, below)[0:1, :] + rank[rows])
        seg_ref[k] = seg8.astype(jnp.int32)
    pos = jnp.concatenate(pos, axis=0)
    p1 = jnp.sum(jnp.where(lane == i1, pos, 0.0), axis=1, keepdims=True)
    p2 = jnp.sum(jnp.where(lane == i2, pos, 0.0), axis=1, keepdims=True)
    route_ref[...] = jnp.where(lane == 0, p1, jnp.where(lane == 1, p2, jnp.where(
        lane == 2, g1, jnp.where(lane == 3, g2, 0.0))))


def _out_router_call(h, mixed, wo_bf, g, wr_pad):
    tm = ROUTER_TILES * MOE_TOKENS
    return pl.pallas_call(
        _out_router_body,
        grid=(T_ALL // tm,),
        in_specs=[pl.BlockSpec((tm, D_MODEL), lambda i: (i, 0))] + _row_specs(mixed, tm) + [
                  pl.BlockSpec((D_MODEL, D_MODEL), lambda i: (0, 0)),
                  pl.BlockSpec((1, D_MODEL), lambda i: (0, 0)),
                  pl.BlockSpec((D_MODEL, LANES), lambda i: (0, 0))],
        out_specs=[pl.BlockSpec((tm, D_MODEL), lambda i: (i, 0)),
                   pl.BlockSpec((tm, D_MODEL), lambda i: (i, 0)),
                   pl.BlockSpec((tm, LANES), lambda i: (i, 0)),
                   pl.BlockSpec((ROUTER_TILES, SUBLANES, LANES), lambda i: (i, 0, 0))],
        out_shape=[jax.ShapeDtypeStruct((T_ALL, D_MODEL), F32),
                   jax.ShapeDtypeStruct((T_ALL, D_MODEL), BF16),
                   jax.ShapeDtypeStruct((T_ALL, LANES), F32),
                   jax.ShapeDtypeStruct((MOE_TILES, SUBLANES, LANES), jnp.int32)],
        compiler_params=pltpu.CompilerParams(dimension_semantics=("arbitrary",),
                                             vmem_limit_bytes=VMEM_LIMIT),
        name="out_router",
    )(h, *mixed, wo_bf, g, wr_pad)


def _row_tile_copy(tile_ref, hbm_ref, tile_row, hbm_row, sem, to_hbm, rows=SUBLANES):
    t = tile_ref.at[pl.ds(pl.multiple_of(tile_row, SUBLANES), rows), :]
    g = hbm_ref.at[pl.ds(pl.multiple_of(hbm_row, SUBLANES), rows), :]
    return pltpu.make_async_copy(t, g, sem) if to_hbm else pltpu.make_async_copy(g, t, sem)


def _seg_plan(n):
    counts = [lax.shift_right_logical(n, 3)]
    for shift in (2, 1, 0):
        counts.append(lax.shift_right_logical(n, shift) & 1)
    return counts


def _seg_copies(i, n_ref, loc_ref, start_ref, tile_ref, hbm_ref, sem, to_hbm):
    for e in range(N_EXPERTS):
        k = i * N_EXPERTS + e
        lo = loc_ref[k]
        st = start_ref[k]
        counts = _seg_plan(n_ref[k])
        big = SEG_COPY_ROWS[0]

        def body(c, carry, lo=lo, st=st):
            _row_tile_copy(tile_ref, hbm_ref, lo + c * big, st + c * big, sem, to_hbm, big).start()
            return carry

        lax.fori_loop(0, counts[0], body, 0)
        done = counts[0] * big
        for rows, cnt in zip(SEG_COPY_ROWS[1:], counts[1:]):
            @pl.when(cnt == 1)
            def _(rows=rows, done=done, lo=lo, st=st):
                _row_tile_copy(tile_ref, hbm_ref, lo + done, st + done, sem, to_hbm, rows).start()

            done = done + cnt * rows


def _seg_wait(i, n_ref, tile_ref, hbm_ref, sem, to_hbm):
    totals = None
    for e in range(N_EXPERTS):
        counts = _seg_plan(n_ref[i * N_EXPERTS + e])
        totals = counts if totals is None else [a + b for a, b in zip(totals, counts)]
    for rows, total in zip(SEG_COPY_ROWS, totals):
        def body(c, carry, rows=rows):
            _row_tile_copy(tile_ref, hbm_ref, 0, 0, sem, to_hbm, rows).wait()
            return carry

        lax.fori_loop(0, total, body, 0)


def _dispatch_body(n_ref, loc_ref, start_ref, zn_ref, zstart_ref, zb_ref, zbn_ref,
                   hn_ref, route_ref, xs_ref, stage_ref, zero_ref, sem):
    i = pl.program_id(0)
    last = pl.num_programs(0) - 1
    slot = i % 2

    @pl.when(i >= 2)
    def _():
        _seg_wait(i - 2, n_ref, stage_ref.at[slot], xs_ref, sem.at[slot], True)

    rt = route_ref[...].T
    s = _iota2((MOE_SLOTS, MOE_TOKENS), 0).astype(F32)
    perm = jnp.where((s == rt[0:1, :]) | (s == rt[1:2, :]), 1.0, 0.0)
    stage_ref[slot] = _dot(perm, hn_ref[...])
    _seg_copies(i, n_ref, loc_ref, start_ref, stage_ref.at[slot], xs_ref, sem.at[slot], True)

    @pl.when(i == last)
    def _():
        zero_ref[...] = jnp.zeros_like(zero_ref)

        def blk_copy(b):
            return pltpu.make_async_copy(
                zero_ref, xs_ref.at[pl.ds(pl.multiple_of(b * GMM_SUB, GMM_SUB), GMM_SUB), :],
                sem.at[3])

        def bbody(b, carry):
            @pl.when(zb_ref[b] == 1)
            def _():
                blk_copy(b).start()
            return carry

        lax.fori_loop(0, GMM_TILES * GMM_ROWS // GMM_SUB, bbody, 0)
        nz = zn_ref[0]
        for e in range(N_EXPERTS):
            st = zstart_ref[e]

            def body(c, carry, st=st):
                _row_tile_copy(zero_ref, xs_ref, 0, st + c * SUBLANES, sem.at[2], True).start()
                return carry

            lax.fori_loop(0, zn_ref[e], body, 0)
            if e:
                nz = nz + zn_ref[e]

        @pl.when(last >= 1)
        def _():
            _seg_wait(i - 1, n_ref, stage_ref.at[1 - slot], xs_ref, sem.at[1 - slot], True)

        _seg_wait(i, n_ref, stage_ref.at[slot], xs_ref, sem.at[slot], True)

        def zbody(c, carry):
            _row_tile_copy(zero_ref, xs_ref, 0, 0, sem.at[2], True).wait()
            return carry

        lax.fori_loop(0, nz, zbody, 0)

        def bwait(c, carry):
            blk_copy(0).wait()
            return carry

        lax.fori_loop(0, zbn_ref[0], bwait, 0)


def _dispatch_call(sched, hn_bf, route):
    grid_spec = pltpu.PrefetchScalarGridSpec(
        num_scalar_prefetch=7,
        grid=(MOE_TILES,),
        in_specs=[pl.BlockSpec((MOE_TOKENS, D_MODEL), lambda i, *_: (i, 0)),
                  pl.BlockSpec((MOE_TOKENS, LANES), lambda i, *_: (i, 0))],
        out_specs=pl.BlockSpec(memory_space=pl.ANY),
        scratch_shapes=[pltpu.VMEM((2, MOE_SLOTS, D_MODEL), F32),
                        pltpu.VMEM((GMM_SUB, D_MODEL), F32),
                        pltpu.SemaphoreType.DMA((4,))])
    return pl.pallas_call(
        _dispatch_body,
        grid_spec=grid_spec,
        out_shape=jax.ShapeDtypeStruct((GMM_TILES * GMM_ROWS, D_MODEL), F32),
        compiler_params=pltpu.CompilerParams(dimension_semantics=("arbitrary",),
                                             vmem_limit_bytes=VMEM_LIMIT),
        name="moe_dispatch",
    )(sched["nch"], sched["loc"], sched["start"], sched["zn"], sched["zstart"], sched["zb"],
      sched["zbn"], hn_bf, route)


def _swiglu_part(x, wg, wu, wd):
    gate = jnp.dot(x, wg, preferred_element_type=F32)
    up = jnp.dot(x, wu, preferred_element_type=F32)
    return jnp.dot((_silu(gate) * up).astype(BF16), wd, preferred_element_type=F32)


def _experts_body(te_ref, tv_ref, nu_ref, x_ref, wg_ref, wu_ref, wd_ref, o_ref,
                  wgb_ref, wub_ref, wdb_ref):
    del te_ref, nu_ref
    i = pl.program_id(0)
    f = pl.program_id(1)
    nv = tv_ref[i]

    @pl.when((f == 0) & (nv < GMM_ROWS))
    def _():
        o_ref[...] = jnp.zeros_like(o_ref)

    def full_tile():
        return _swiglu_part(x_ref[...].astype(BF16), wg_ref[0].astype(BF16),
                            wu_ref[0].astype(BF16), wd_ref[0].astype(BF16))

    @pl.when((nv == GMM_ROWS) & (f == 0))
    def _():
        o_ref[...] = full_tile()

    @pl.when((nv == GMM_ROWS) & (f > 0))
    def _():
        o_ref[...] += full_tile()

    @pl.when((nv > 0) & (nv < GMM_ROWS))
    def _():
        wgb_ref[...] = wg_ref[0].astype(BF16)
        wub_ref[...] = wu_ref[0].astype(BF16)
        wdb_ref[...] = wd_ref[0].astype(BF16)
        for sub in range(GMM_ROWS // GMM_SUB):
            rows = slice(sub * GMM_SUB, (sub + 1) * GMM_SUB)

            @pl.when(sub * GMM_SUB < nv)
            def _(rows=rows):
                o_ref[rows, :] += _swiglu_part(x_ref[rows, :].astype(BF16), wgb_ref[...],
                                               wub_ref[...], wdb_ref[...])


def _experts_call(sched, x_sorted, w_in_e, w_out_e):
    tf = 512
    nf = D_FF_EXPERT // tf

    def used(i, nu):
        return jnp.maximum(jnp.minimum(i, nu[0] - 1), 0)

    def fidx(i, f, nu):
        return jnp.where(i < nu[0], f, nf - 1)

    grid_spec = pltpu.PrefetchScalarGridSpec(
        num_scalar_prefetch=3,
        grid=(GMM_TILES, nf),
        in_specs=[pl.BlockSpec((GMM_ROWS, D_MODEL), lambda i, f, te, tv, nu: (used(i, nu), 0)),
                  pl.BlockSpec((1, D_MODEL, tf), lambda i, f, te, tv, nu: (te[i], 0, fidx(i, f, nu))),
                  pl.BlockSpec((1, D_MODEL, tf),
                               lambda i, f, te, tv, nu: (te[i], 0, nf + fidx(i, f, nu))),
                  pl.BlockSpec((1, tf, D_MODEL), lambda i, f, te, tv, nu: (te[i], fidx(i, f, nu), 0))],
        out_specs=pl.BlockSpec((GMM_ROWS, D_MODEL), lambda i, f, te, tv, nu: (i, 0)),
        scratch_shapes=[pltpu.VMEM((D_MODEL, tf), BF16), pltpu.VMEM((D_MODEL, tf), BF16),
                        pltpu.VMEM((tf, D_MODEL), BF16)])
    return pl.pallas_call(
        _experts_body,
        grid_spec=grid_spec,
        out_shape=jax.ShapeDtypeStruct((GMM_TILES * GMM_ROWS, D_MODEL), F32),
        compiler_params=pltpu.CompilerParams(dimension_semantics=("arbitrary", "arbitrary"),
                                             vmem_limit_bytes=VMEM_LIMIT),
        name="moe_experts",
    )(sched["te"], sched["tv"], sched["nu"], x_sorted, w_in_e, w_in_e, w_out_e)


def _combine_body(n_ref, loc_ref, start_ref, h1_ref, route_ref, gf_ref, ys_ref, op_ref, os_ref,
                  buf_ref, sem):
    i = pl.program_id(0)
    last = pl.num_programs(0) - 1
    slot = i % 2

    @pl.when(i == 0)
    def _():
        buf_ref[...] = jnp.zeros_like(buf_ref)
        _seg_copies(0, n_ref, loc_ref, start_ref, buf_ref.at[0], ys_ref, sem.at[0], False)

    @pl.when(i < last)
    def _():
        _seg_copies(i + 1, n_ref, loc_ref, start_ref, buf_ref.at[1 - slot], ys_ref,
                    sem.at[1 - slot], False)

    _seg_wait(i, n_ref, buf_ref.at[slot], ys_ref, sem.at[slot], False)

    route = route_ref[...]
    s = _iota2((MOE_TOKENS, MOE_SLOTS), 1).astype(F32)
    rows = buf_ref[slot]
    y1 = _dot(jnp.where(s == route[:, 0:1], 1.0, 0.0), rows)
    y2 = _dot(jnp.where(s == route[:, 1:2], 1.0, 0.0), rows)
    y = _rms(h1_ref[...] + route[:, 2:3] * y1 + route[:, 3:4] * y2, gf_ref[...])

    @pl.when(i < T_PROMPT // MOE_TOKENS)
    def _():
        op_ref[...] = y

    @pl.when(i >= T_PROMPT // MOE_TOKENS)
    def _():
        os_ref[...] = y


def _combine_call(sched, h1, route, gfin, y_sorted):
    npt = T_PROMPT // MOE_TOKENS
    grid_spec = pltpu.PrefetchScalarGridSpec(
        num_scalar_prefetch=3,
        grid=(MOE_TILES,),
        in_specs=[pl.BlockSpec((MOE_TOKENS, D_MODEL), lambda i, *_: (i, 0)),
                  pl.BlockSpec((MOE_TOKENS, LANES), lambda i, *_: (i, 0)),
                  pl.BlockSpec((1, D_MODEL), lambda i, *_: (0, 0)),
                  pl.BlockSpec(memory_space=pl.ANY)],
        out_specs=[pl.BlockSpec((MOE_TOKENS, D_MODEL), lambda i, *_: (jnp.minimum(i, npt - 1), 0)),
                   pl.BlockSpec((MOE_TOKENS, D_MODEL), lambda i, *_: (jnp.maximum(i - npt, 0), 0))],
        scratch_shapes=[pltpu.VMEM((2, MOE_SLOTS, D_MODEL), F32),
                        pltpu.SemaphoreType.DMA((2,))])
    return pl.pallas_call(
        _combine_body,
        grid_spec=grid_spec,
        out_shape=[jax.ShapeDtypeStruct((T_PROMPT, D_MODEL), F32),
                   jax.ShapeDtypeStruct((T_SAMPLE, D_MODEL), F32)],
        compiler_params=pltpu.CompilerParams(dimension_semantics=("arbitrary",),
                                             vmem_limit_bytes=VMEM_LIMIT),
        name="moe_combine",
    )(sched["nch"], sched["loc"], sched["start"], h1, route, gfin, y_sorted)


def _moe_schedule(seg):
    tot = jnp.sum(seg, axis=0)
    region = (tot + GMM_ROWS - 1) // GMM_ROWS * GMM_ROWS
    base = jnp.cumsum(region) - region
    start = base[None, :] + jnp.cumsum(seg, axis=0) - seg
    loc = jnp.cumsum(seg, axis=1) - seg
    ntile = region // GMM_ROWS
    cum = jnp.cumsum(ntile)
    nu = cum[-1]
    i = jnp.arange(GMM_TILES, dtype=jnp.int32)[:, None]
    first = (cum - ntile)[None, :]
    mine = (i >= first) & (i < cum[None, :])
    experts = jnp.arange(N_EXPERTS, dtype=jnp.int32)
    te = jnp.sum(jnp.where(mine, experts[None, :], 0), axis=1)
    tv = jnp.sum(jnp.where(mine, jnp.clip(tot[None, :] - (i - first) * GMM_ROWS, 0, GMM_ROWS), 0),
                 axis=1)
    te = jnp.where(i[:, 0] < nu, te, jnp.max(jnp.where(ntile > 0, experts, 0)))
    bstart = jnp.arange(GMM_TILES * GMM_ROWS // GMM_SUB, dtype=jnp.int32)[:, None] * GMM_SUB
    used_end = (base + (tot + GMM_SUB - 1) // GMM_SUB * GMM_SUB)[None, :]
    zb = jnp.any((bstart >= used_end) & (bstart < (base + region)[None, :]), axis=1)
    zb = (zb | (bstart[:, 0] >= jnp.sum(region))).astype(jnp.int32)
    return dict(nch=(seg // SUBLANES).reshape(-1), loc=loc.reshape(-1), start=start.reshape(-1),
                zn=((-tot) % GMM_SUB) // SUBLANES, zstart=base + tot,
                zb=zb, zbn=jnp.sum(zb).reshape(1),
                te=te, tv=tv.astype(jnp.int32), nu=nu.reshape(1).astype(jnp.int32))


def _row(x, width=None):
    x = x.astype(F32).reshape(1, -1)
    if width is not None and x.shape[1] < width:
        x = jnp.pad(x, ((0, 0), (0, width - x.shape[1])))
    return x


def _layer_params(l, lb_p, conv_w, conv_b, a_log, dt_bias, d_skip, hg_norm, ssd_norm):
    lb = jnp.sum(lb_p[1:l + 1], axis=0)
    rows = [jnp.log(lb), jnp.log1p(-lb), 1.0 - lb, hg_norm[l], *conv_w[l], conv_b[l], dt_bias[l],
            -jnp.exp(a_log[l].astype(F32)), jnp.repeat(d_skip[l].astype(F32), SSD_P), ssd_norm[l]]
    rows = [jnp.pad(r.astype(F32), (0, CONV_DIM - r.shape[0])) for r in rows]
    return jnp.stack(rows + [jnp.zeros((CONV_DIM,), F32)] * (16 - len(rows)))


def kernel(x_prompt, x_sample, state_hgrn, state_ssm, state_conv, norm_mix, w_in, conv_w, conv_b,
           a_log, dt_bias, d_skip, lb_param, hg_norm, ssd_norm, w_out, norm_ffn, w_ffn_in,
           w_ffn_out, w_router, w_exp_in, w_exp_out, norm_final):
    h = (x_prompt.reshape(T_PROMPT, D_MODEL), x_sample.reshape(T_SAMPLE, D_MODEL))
    lb_p = jax.nn.softmax(lb_param.astype(F32), axis=0)
    packed = (SSD_PAIRS, LANES, SSD_N)
    s_ssm_t = jnp.swapaxes(state_ssm, 3, 4).reshape((DEPTH, DEC_BATCH) + packed)
    s_conv_t = jnp.swapaxes(state_conv, 1, 2)
    outs = {k: [] for k in ("hg_p", "ssm_p", "conv_p", "conv_s")}
    sample_states = None
    for l in range(DEPTH):
        prm = _layer_params(l, lb_p, conv_w, conv_b, a_log, dt_bias, d_skip, hg_norm, ssd_norm)
        proj = _proj_call(l, h, _row(norm_mix[l]), w_in)
        mixed_p, hg_p, ssm_p, conv_p, mixed_s, hg_s, ssm_s, conv_s = _scan_call(
            l, proj, state_hgrn, s_ssm_t, s_conv_t, prm, sample_states)
        sample_states = (hg_s, ssm_s)
        mixed = (mixed_p, mixed_s)
        for k, val in zip(outs, (hg_p, ssm_p, conv_p, conv_s)):
            outs[k].append(val)
        wo_bf = w_out[l].astype(BF16)
        if l % 2 == 0:
            h = (_out_ffn_call(h, mixed, wo_bf, _row(norm_ffn[l]), w_ffn_in[l // 2].astype(BF16),
                               w_ffn_out[l // 2].astype(BF16)),)
        else:
            wr_pad = jnp.pad(w_router[l // 2].astype(F32), ((0, 0), (0, LANES - N_EXPERTS)))
            h1, hn_bf, route, seg = _out_router_call(h[0], mixed, wo_bf, _row(norm_ffn[l]), wr_pad)
            sched = _moe_schedule(seg[:, 0, :N_EXPERTS])
            x_sorted = _dispatch_call(sched, hn_bf, route)
            y_sorted = _experts_call(sched, x_sorted, w_exp_in[l // 2], w_exp_out[l // 2])
            h = _combine_call(sched, h1, route, _row(norm_final), y_sorted)
    y_prompt = h[0].reshape(BATCH, SEQ, D_MODEL)
    y_sample = h[1].reshape(DEC_BATCH, DEC_SEQ, D_MODEL)
    def unpack_ssm(s_t, batch):
        return jnp.swapaxes(s_t.reshape(DEPTH, batch, SSD_HEADS, SSD_P, SSD_N), 3, 4)

    return (y_prompt, y_sample, jnp.stack(outs["hg_p"]), unpack_ssm(jnp.stack(outs["ssm_p"]), BATCH),
            jnp.stack(outs["conv_p"]), sample_states[0], unpack_ssm(sample_states[1], DEC_BATCH),
            jnp.swapaxes(jnp.stack(outs["conv_s"]), 1, 2))
```

```python
import functools

import jax
import jax.numpy as jnp
from jax import lax
from jax.experimental import pallas as pl
from jax.experimental.pallas import tpu as pltpu

F32 = jnp.float32
BF16 = jnp.bfloat16

D_MODEL = 1024
BATCH = 8
SEQ = 2048
DEPTH = 2
DEC_BATCH = 128
DEC_SEQ = 8
HG_HEADS = 4
HG_D = 128
HG_W = HG_HEADS * HG_D
SSD_HEADS = 8
SSD_P = 64
SSD_N = 128
SSD_W = SSD_HEADS * SSD_P
SSD_PAIRS = SSD_HEADS // 2
CONV_W = 4
CONV_DIM = 1024
D_FF = 2816
N_EXPERTS = 8
D_FF_EXPERT = 3584
EPS = 1e-6

LANES = 128
C_Q, C_F, C_I, C_OG, C_Z, C_XBC, C_DT = 0, 512, 1024, 1536, 2048, 2560, 3584
PROJ_W = C_DT + LANES
HG_CHUNK = 64
SSD_CHUNK = 128
SAMPLE_SEQS = 4
PROMPT_SEQS = 4
HG_SAFE_RANGE = 80.0
LOG2 = 0.6931471805599453
P_LA, P_LC, P_OMLB, P_HGN, P_CW, P_CB, P_DTB, P_AN, P_DSK, P_SSN, P_ROWS = 0, 1, 2, 3, 4, 8, 9, 10, 11, 12, 16

T_PROMPT = BATCH * SEQ
T_SAMPLE = DEC_BATCH * DEC_SEQ
T_ALL = T_PROMPT + T_SAMPLE

SUBLANES = 8
TOP_K = 2
MOE_TOKENS = 256
MOE_TILES = T_ALL // MOE_TOKENS
ROUTER_TILES = 1
MOE_SLOTS = TOP_K * MOE_TOKENS + N_EXPERTS * SUBLANES
SEG_COPY_ROWS = (64, 32, 16, 8)
GMM_ROWS = 1024
GMM_SUB = 256
GMM_TILES = (TOP_K * T_ALL + MOE_TILES * N_EXPERTS * (SUBLANES - 1)
             + N_EXPERTS * (GMM_ROWS - 1)) // GMM_ROWS + 1

VMEM_LIMIT = 60000 * 1024


def _sigmoid(x):
    return 0.5 * jnp.tanh(0.5 * x) + 0.5


def _silu(x):
    return x * _sigmoid(x)


def _softplus(x):
    return jnp.maximum(x, 0.0) + jnp.log(1.0 + jnp.exp(-jnp.abs(x)))


def _rms(x, g):
    return x * lax.rsqrt(jnp.mean(x * x, axis=-1, keepdims=True) + EPS) * g


def _dot(a, b):
    return jnp.dot(a.astype(BF16), b.astype(BF16), preferred_element_type=F32)


def _dot_nt(a, b):
    return lax.dot_general(a.astype(BF16), b.astype(BF16), (((1,), (1,)), ((), ())),
                           preferred_element_type=F32)


def _split3(x):
    x1 = x.astype(BF16)
    r1 = x - x1.astype(F32)
    x2 = r1.astype(BF16)
    x3 = (r1 - x2.astype(F32)).astype(BF16)
    return x1, x2, x3


def _mask_dot(m, x):
    mb = m.astype(BF16)
    x1, x2, x3 = _split3(x)
    return (jnp.dot(mb, x1, preferred_element_type=F32) + jnp.dot(mb, x2, preferred_element_type=F32)
            + jnp.dot(mb, x3, preferred_element_type=F32))


def _mask_dot_nt(x, m):
    mb = m.astype(BF16)
    dn = (((1,), (1,)), ((), ()))
    x1, x2, x3 = _split3(x)
    return (lax.dot_general(x1, mb, dn, preferred_element_type=F32)
            + lax.dot_general(x2, mb, dn, preferred_element_type=F32)
            + lax.dot_general(x3, mb, dn, preferred_element_type=F32))


def _dot_f32x3(a, b):
    a1, a2, _ = _split3(a)
    b1, b2, _ = _split3(b)
    return (jnp.dot(a1, b1, preferred_element_type=F32) + jnp.dot(a1, b2, preferred_element_type=F32)
            + jnp.dot(a2, b1, preferred_element_type=F32))


def _iota2(shape, dim):
    return lax.broadcasted_iota(jnp.int32, shape, dim)


def _seq_masks(rows, seq_len):
    t = _iota2((rows, rows), 0)
    s = _iota2((rows, rows), 1)
    same = (t // seq_len) == (s // seq_len)
    causal = same & (s <= t)
    ref = same & ((s % seq_len) < seq_len // 2)
    return causal, ref, same


def _row_specs(arrs, tm):
    if len(arrs) == 1:
        return [pl.BlockSpec((tm, arrs[0].shape[1]), lambda i, *_: (i, 0))]
    npt = T_PROMPT // tm
    return [pl.BlockSpec((tm, arrs[0].shape[1]), lambda i, *_: (jnp.minimum(i, npt - 1), 0)),
            pl.BlockSpec((tm, arrs[1].shape[1]), lambda i, *_: (jnp.maximum(i - npt, 0), 0),
                         pipeline_mode=pl.Buffered(1))]


def _rows(i, refs):
    if len(refs) == 1:
        return refs[0][...]
    return jnp.where(i < T_PROMPT // refs[0].shape[0], refs[0][...], refs[1][...])


def _proj_body(n_h, *refs):
    h_refs, (g_ref, w_ref, o_ref, wb_ref) = refs[:n_h], refs[n_h:]
    i = pl.program_id(0)

    @pl.when(i == 0)
    def _():
        for r0 in range(0, C_DT, HG_W):
            wb_ref[r0:r0 + HG_W, :] = w_ref[r0:r0 + HG_W, :].astype(BF16)
        tail = w_ref[C_DT:, :]
        pad = jnp.zeros((LANES - tail.shape[0], D_MODEL), F32)
        wb_ref[C_DT:, :] = jnp.concatenate([tail, pad], axis=0).astype(BF16)

    hn = _rms(_rows(i, h_refs), g_ref[...])
    o_ref[...] = _dot_nt(hn, wb_ref[...])


def _proj_call(l, h, g, w_in):
    tm = 512
    w_t = jnp.swapaxes(w_in, 1, 2)
    return pl.pallas_call(
        functools.partial(_proj_body, len(h)),
        grid=(T_ALL // tm,),
        in_specs=_row_specs(h, tm) + [pl.BlockSpec((1, D_MODEL), lambda i: (0, 0)),
                                      pl.BlockSpec((None,) + w_t.shape[1:], lambda i: (l, 0, 0),
                                                   pipeline_mode=pl.Buffered(1))],
        out_specs=pl.BlockSpec((tm, PROJ_W), lambda i: (i, 0)),
        out_shape=jax.ShapeDtypeStruct((T_ALL, PROJ_W), F32),
        scratch_shapes=[pltpu.VMEM((PROJ_W, D_MODEL), BF16)],
        compiler_params=pltpu.CompilerParams(dimension_semantics=("arbitrary",),
                                             vmem_limit_bytes=VMEM_LIMIT),
        name="in_proj",
    )(*h, g, w_t)


def _hgrn_gates(p_q, p_f, la, lc, omlb):
    q = _silu(p_q)
    e = jnp.exp(-jnp.abs(p_f))
    b = lc + jnp.minimum(p_f, 0.0) - jnp.log(1.0 + e)
    logf = jnp.maximum(la, b) + jnp.log(1.0 + jnp.exp(-jnp.abs(la - b)))
    r = 1.0 / (1.0 + e)
    kin = omlb * jnp.where(p_f >= 0.0, e * r, r)
    return q, logf, kin


def _hgrn_decays(logf, causal, ref, same):
    rows = logf.shape[0]
    if ref is None:
        g = _mask_dot(causal, logf)
        return g, g[rows // 2 - 1:rows // 2, :], g[rows - 1:rows, :]
    m = jnp.concatenate([causal.astype(F32), ref.astype(F32), same.astype(F32)], axis=0)
    g3 = _mask_dot(m, logf)
    return g3[:rows], g3[rows:2 * rows], g3[2 * rows:]


def _hgrn_exact_intra(ex_ref, q, kin, g, v, seq_len):
    rows = q.shape[0]
    ex_ref[0] = g
    ex_ref[1] = q
    ex_ref[2] = kin
    ex_ref[3] = v
    s_idx = _iota2((rows, HG_W), 0)

    def body(t, carry):
        gt = ex_ref[0, pl.ds(t, 1), :]
        qt = ex_ref[1, pl.ds(t, 1), :]
        live = (s_idx <= t) & (s_idx // seq_len == t // seq_len)
        w = jnp.where(live, qt * ex_ref[2] * jnp.exp(jnp.minimum(gt - ex_ref[0], 0.0)), 0.0)
        outs = []
        for h in range(HG_HEADS):
            sl = slice(h * HG_D, (h + 1) * HG_D)
            score = jnp.sum(w[:, sl], axis=1, keepdims=True)
            outs.append(jnp.sum(score * ex_ref[3, :, sl], axis=0, keepdims=True))
        ex_ref[4, pl.ds(t, 1), :] = jnp.concatenate(outs, axis=1)
        return carry

    lax.fori_loop(0, rows, body, 0)
    return ex_ref[4]


def _hgrn_head_out(o, hgn, og):
    return _rms(o, hgn) * _silu(og)


def _ssd_inputs(conv, p_dt, dtb, a_neg):
    conv = _silu(conv)
    xs = conv[:, :SSD_W]
    bm = conv[:, SSD_W:SSD_W + 2 * SSD_N]
    cm = conv[:, SSD_W + 2 * SSD_N:]
    dtf = _softplus(p_dt + dtb)
    a = dtf * a_neg
    return xs, bm, cm, dtf, a


def _lane_bcast(x, lane, width):
    return jnp.broadcast_to(x[:, lane:lane + 1], (x.shape[0], width))


def _ssd_finish(y, xs, z, dsk, ssn):
    y = (y + dsk * xs) * _silu(z)
    half = SSD_W // 2
    return jnp.concatenate([_rms(y[:, :half], ssn[:, :half]), _rms(y[:, half:], ssn[:, half:])],
                           axis=1)


def _scan_body(n_prev, *refs):
    ns = PROMPT_SEQS
    p_refs, refs = refs[:ns], refs[ns:]
    (ps_ref, shg_ref, sssm_ref, sconv_ref), refs = refs[:4], refs[4:]
    prm_ref, refs = refs[0], refs[1:]
    row = lambda r, width: prm_ref.at[r:r + 1, 0:width]
    la_ref, lc_ref, omlb_ref = row(P_LA, HG_W), row(P_LC, HG_W), row(P_OMLB, HG_W)
    params = (row(P_HGN, HG_W), prm_ref.at[P_CW:P_CW + CONV_W, :], row(P_CB, CONV_DIM),
              row(P_DTB, LANES), row(P_AN, LANES), row(P_DSK, SSD_W), row(P_SSN, SSD_W))
    prev_refs, refs = refs[:2 * bool(n_prev)], refs[2 * bool(n_prev):]
    (mix_ref, ohg_ref, ossm_ref, oconv_ref, mixs_ref, ohgs_ref, ossms_ref, oconvs_ref,
     st_ref, hp_ref, xpad_ref, ex_ref, xpads_ref, exs_ref) = refs
    j = pl.program_id(1)
    tl = SSD_CHUNK

    @pl.when(j == 0)
    def _():
        st_ref[...] = jnp.zeros_like(st_ref)
        hp_ref[...] = jnp.zeros_like(hp_ref)
        xpad_ref[:, 0:8, :] = jnp.zeros((ns, 8, CONV_DIM), F32)

    def decay_bound(z):
        low = jnp.maximum(la_ref[...], lc_ref[...] + jnp.minimum(z, 0.0) - LOG2)
        return jnp.sum(low, axis=0, keepdims=True)

    bounds = [decay_bound(p_refs[s][r0:r0 + HG_CHUNK // 2, C_F:C_F + HG_W])
              for s in range(ns) for r0 in range(0, tl, HG_CHUNK // 2)]
    bounds += [decay_bound(ps_ref[b * DEC_SEQ:(b + 1) * DEC_SEQ, C_F:C_F + HG_W])
               for b in range(SAMPLE_SEQS)]
    safe = jnp.min(functools.reduce(jnp.minimum, bounds)) >= -HG_SAFE_RANGE
    gate_refs = (la_ref, lc_ref, omlb_ref)

    for exact in (False, True):
        @pl.when(safe != exact)
        def _(exact=exact):
            for s in range(ns):
                _scan_prompt_tile(exact, ex_ref, p_refs[s], gate_refs, *params, mix_ref.at[s],
                                  st_ref.at[s], hp_ref.at[s], xpad_ref.at[s])
            seq0 = (pl.program_id(0) * pl.num_programs(1) + j) % (SUBLANES // SAMPLE_SEQS) * SAMPLE_SEQS
            _scan_sample_tile(n_prev, exact, seq0, ps_ref, shg_ref, sssm_ref, sconv_ref, prev_refs,
                              gate_refs, *params, mixs_ref, ohgs_ref, ossms_ref, oconvs_ref,
                              xpads_ref, exs_ref)

    @pl.when(j == pl.num_programs(1) - 1)
    def _():
        for s in range(ns):
            oconv_ref[s] = xpad_ref[s, 5:8, :]
            for h in range(HG_HEADS):
                ohg_ref[s, h] = st_ref[s, h].T
        ossm_ref[...] = hp_ref[...]


def _scan_prompt_tile(exact, ex_ref, p_ref, gate_refs, hgn_ref, cw_ref, cb_ref, dtb_ref, an_ref,
                      dsk_ref, ssn_ref, mix_ref, st_ref, hp_ref, xpad_ref):
    tl = SSD_CHUNK
    causal, _, _ = _seq_masks(HG_CHUNK, HG_CHUNK)
    for c in range(tl // HG_CHUNK):
        r0 = c * HG_CHUNK
        rows = slice(r0, r0 + HG_CHUNK)
        q, logf, kin = _hgrn_gates(p_ref[rows, C_Q:C_Q + HG_W], p_ref[rows, C_F:C_F + HG_W],
                                   *(r[...] for r in gate_refs))
        v = p_ref[rows, C_I:C_I + HG_W]
        g, gmid, glast = _hgrn_decays(logf, causal, None, None)
        if exact:
            qh = q * jnp.exp(g)
            kd = kin * jnp.exp(glast - g)
            o_intra = _hgrn_exact_intra(ex_ref, q, kin, g, v, HG_CHUNK)
        else:
            qt = q * jnp.exp(g - gmid)
            kt = kin * jnp.exp(gmid - g)
            qh = qt * jnp.exp(gmid)
            kd = kt * jnp.exp(glast - gmid)
        ds = jnp.exp(glast[0:1, :])
        for h in range(HG_HEADS):
            sl = slice(h * HG_D, (h + 1) * HG_D)
            st = st_ref[h]
            if exact:
                o = o_intra[:, sl] + _dot_nt(qh[:, sl], st)
            else:
                sc = jnp.where(causal, _dot_nt(qt[:, sl], kt[:, sl]), 0.0)
                o = _dot(sc, v[:, sl]) + _dot_nt(qh[:, sl], st)
            st_ref[h] = st * ds[:, sl] + _dot(v[:, sl].T, kd[:, sl])
            og = p_ref[rows, C_OG + h * HG_D:C_OG + (h + 1) * HG_D]
            mix_ref[rows, sl] = _hgrn_head_out(o, hgn_ref[:, sl], og).astype(BF16)

    xpad_ref[8:8 + tl, :] = p_ref[:, C_XBC:C_XBC + CONV_DIM]
    conv = cb_ref[...] + cw_ref[0:1, :] * xpad_ref[5:5 + tl, :]
    for tap in range(1, CONV_W):
        conv = conv + cw_ref[tap:tap + 1, :] * xpad_ref[5 + tap:5 + tap + tl, :]
    xpad_ref[0:8, :] = xpad_ref[tl:tl + 8, :]

    xs, bm, cm, dtf, a = _ssd_inputs(conv, p_ref[:, C_DT:C_DT + LANES], dtb_ref[...], an_ref[...])
    tri, _, _ = _seq_masks(tl, tl)
    trif = tri.astype(F32)
    acum = _mask_dot(trif, a)
    acum_t = _mask_dot_nt(a.T, trif)
    lo = _iota2((tl, LANES), 1) < SSD_P
    top = _iota2((LANES, tl), 0) < SSD_P
    ys = []
    for pair in range(SSD_PAIRS):
        grp = pair // 2
        cg = cm[:, grp * SSD_N:(grp + 1) * SSD_N]
        bg = bm[:, grp * SSD_N:(grp + 1) * SSD_N]
        cb = _dot_nt(cg, bg)
        psl = slice(pair * LANES, (pair + 1) * LANES)
        r0, r1 = 2 * pair, 2 * pair + 1
        xp = xs[:, psl] * jnp.where(lo, _lane_bcast(dtf, r0, LANES), _lane_bcast(dtf, r1, LANES))
        intra, einter, wrow, ea = [], [], [], []
        for r in (r0, r1):
            col = _lane_bcast(acum, r, tl)
            row = jnp.broadcast_to(acum_t[r:r + 1, :], (tl, tl))
            dec = jnp.exp(jnp.where(tri, col - row, -jnp.inf))
            intra.append(_dot(cb * dec, xp))
            einter.append(jnp.exp(col))
            alast = acum_t[r:r + 1, tl - 1:tl]
            wrow.append(jnp.exp(alast - acum_t[r:r + 1, :]))
            ea.append(jnp.exp(alast))
        hp = hp_ref[pair]
        y = jnp.where(lo, intra[0], intra[1]) + jnp.where(lo, einter[0], einter[1]) * _dot_nt(cg, hp)
        xw_t = xp.T * jnp.where(top, wrow[0], wrow[1])
        hp_ref[pair] = jnp.where(top, ea[0], ea[1]) * hp + _dot(xw_t, bg)
        ys.append(y)
    y = jnp.concatenate(ys, axis=1)
    y = _ssd_finish(y, xs, p_ref[:, C_Z:C_Z + SSD_W], dsk_ref[...], ssn_ref[...])
    mix_ref[:, HG_W:] = y.astype(BF16)


def _scan_call(l, proj, s_hg, s_ssm_t, s_conv_t, prm, prev):
    tl = SSD_CHUNK
    nt = SEQ // tl
    ns, nb = PROMPT_SEQS, SAMPLE_SEQS
    assert BATCH // ns * nt * nb == DEC_BATCH
    srows = nb * DEC_SEQ
    srow0 = T_PROMPT // srows
    step = lambda b, j: b * nt + j
    pstate, sstate = (ns, HG_HEADS, HG_D, HG_D), (nb, HG_HEADS, HG_D, HG_D)
    in_specs = [pl.BlockSpec((tl, PROJ_W), lambda b, j, s=s: ((b * ns + s) * nt + j, 0))
                for s in range(ns)]
    in_specs += [pl.BlockSpec((srows, PROJ_W), lambda b, j: (srow0 + step(b, j), 0)),
                 pl.BlockSpec((None,) + sstate, lambda b, j: (l, step(b, j), 0, 0, 0)),
                 pl.BlockSpec((None,) + sstate, lambda b, j: (l, step(b, j), 0, 0, 0)),
                 pl.BlockSpec((None, CONV_W - 1, SUBLANES, CONV_DIM),
                              lambda b, j: (l, 0, step(b, j) // (SUBLANES // nb), 0))]
    in_specs += [pl.BlockSpec(prm.shape, lambda b, j: (0, 0))]
    in_specs += [pl.BlockSpec((l,) + sstate, lambda b, j: (0, step(b, j), 0, 0, 0))] * (2 if l else 0)
    out_specs = [pl.BlockSpec((ns, tl, D_MODEL), lambda b, j: (b, j, 0)),
                 pl.BlockSpec(pstate, lambda b, j: (b, 0, 0, 0)),
                 pl.BlockSpec(pstate, lambda b, j: (b, 0, 0, 0)),
                 pl.BlockSpec((ns, CONV_W - 1, CONV_DIM), lambda b, j: (b, 0, 0)),
                 pl.BlockSpec((srows, D_MODEL), lambda b, j: (step(b, j), 0)),
                 pl.BlockSpec((l + 1,) + sstate, lambda b, j: (0, step(b, j), 0, 0, 0)),
                 pl.BlockSpec((l + 1,) + sstate, lambda b, j: (0, step(b, j), 0, 0, 0)),
                 pl.BlockSpec((CONV_W - 1, SUBLANES, CONV_DIM),
                              lambda b, j: (0, step(b, j) // (SUBLANES // nb), 0))]
    stacked = jax.ShapeDtypeStruct((l + 1, DEC_BATCH) + sstate[1:], F32)
    out_shape = [jax.ShapeDtypeStruct((BATCH, SEQ, D_MODEL), BF16),
                 jax.ShapeDtypeStruct((BATCH,) + pstate[1:], F32),
                 jax.ShapeDtypeStruct((BATCH,) + pstate[1:], F32),
                 jax.ShapeDtypeStruct((BATCH, CONV_W - 1, CONV_DIM), F32),
                 jax.ShapeDtypeStruct((T_SAMPLE, D_MODEL), BF16), stacked, stacked,
                 jax.ShapeDtypeStruct((CONV_W - 1, DEC_BATCH, CONV_DIM), F32)]
    outs = pl.pallas_call(
        functools.partial(_scan_body, l),
        grid=(BATCH // ns, nt),
        in_specs=in_specs, out_specs=out_specs, out_shape=out_shape,
        scratch_shapes=[pltpu.VMEM(pstate, F32), pltpu.VMEM(pstate, F32),
                        pltpu.VMEM((ns, tl + 8, CONV_DIM), F32),
                        pltpu.VMEM((5, HG_CHUNK, HG_W), F32),
                        pltpu.VMEM((16 * nb, CONV_DIM), F32), pltpu.VMEM((5, srows, HG_W), F32)],
        compiler_params=pltpu.CompilerParams(dimension_semantics=("arbitrary", "arbitrary"),
                                             vmem_limit_bytes=VMEM_LIMIT),
        name="scan",
    )(*([proj] * (ns + 1)), s_hg, s_ssm_t, s_conv_t, prm, *(prev if l else ()))
    return (outs[0].reshape(T_PROMPT, D_MODEL),) + tuple(outs[1:])


def _scan_sample_tile(n_prev, exact, seq0, p_ref, shg_ref, sssm_ref, sconv_ref, prev_refs,
                      gate_refs, hgn_ref, cw_ref, cb_ref, dtb_ref, an_ref, dsk_ref, ssn_ref,
                      mix_ref, ohg_ref, ossm_ref, oconv_ref, xpad_ref, ex_ref):
    if n_prev:
        ohg_ref[0:n_prev] = prev_refs[0][...]
        ossm_ref[0:n_prev] = prev_refs[1][...]
    nb, sl_len = SAMPLE_SEQS, DEC_SEQ
    rows = nb * sl_len
    causal, ref, same = _seq_masks(rows, sl_len)
    rowseq = _iota2((rows, LANES), 0) // sl_len

    q, logf, kin = _hgrn_gates(p_ref[:, C_Q:C_Q + HG_W], p_ref[:, C_F:C_F + HG_W],
                               *(r[...] for r in gate_refs))
    v = p_ref[:, C_I:C_I + HG_W]
    g, gmid, glast = _hgrn_decays(logf, causal, ref, same)
    if exact:
        o_intra_all = _hgrn_exact_intra(ex_ref, q, kin, g, v, sl_len)
    else:
        qt = q * jnp.exp(g - gmid)
        kt = kin * jnp.exp(gmid - g)
    qh = q * jnp.exp(g)
    kd = kin * jnp.exp(glast - g)
    ds_t = jnp.exp(glast).T
    kd_t = kd.T
    for h in range(HG_HEADS):
        sl = slice(h * HG_D, (h + 1) * HG_D)
        if exact:
            o_intra = o_intra_all[:, sl]
        else:
            sc = jnp.where(causal, _dot_nt(qt[:, sl], kt[:, sl]), 0.0)
            o_intra = _dot(sc, v[:, sl])
        o_inter = []
        for b in range(nb):
            s_old = shg_ref[b, h]
            o_inter.append(_dot(qh[b * sl_len:(b + 1) * sl_len, sl], s_old))
            vb = jnp.where(rowseq == b, v[:, sl], 0.0)
            dcol = _lane_bcast(ds_t[sl, :], b * sl_len, HG_D)
            ohg_ref[n_prev, b, h] = dcol * s_old + _dot(kd_t[sl, :], vb)
        o = o_intra + jnp.concatenate(o_inter, axis=0)
        og = p_ref[:, C_OG + h * HG_D:C_OG + (h + 1) * HG_D]
        mix_ref[:, sl] = _hgrn_head_out(o, hgn_ref[:, sl], og).astype(BF16)

    convs = []
    for b in range(nb):
        base = 16 * b
        for r in range(CONV_W - 1):
            xpad_ref[base + 5 + r:base + 6 + r, :] = sconv_ref[r, pl.ds(seq0 + b, 1), :]
        xpad_ref[base + 8:base + 16, :] = p_ref[b * sl_len:(b + 1) * sl_len, C_XBC:C_XBC + CONV_DIM]
        cv = cb_ref[...] + cw_ref[0:1, :] * xpad_ref[base + 5:base + 13, :]
        for tap in range(1, CONV_W):
            cv = cv + cw_ref[tap:tap + 1, :] * xpad_ref[base + 5 + tap:base + 13 + tap, :]
        convs.append(cv)
        for r in range(CONV_W - 1):
            oconv_ref[r, pl.ds(seq0 + b, 1), :] = xpad_ref[base + 13 + r:base + 14 + r, :]
    conv = jnp.concatenate(convs, axis=0)

    xs, bm, cm, dtf, a = _ssd_inputs(conv, p_ref[:, C_DT:C_DT + LANES], dtb_ref[...], an_ref[...])
    causf = causal.astype(F32)
    acum = _mask_dot(causf, a)
    a_t = a.T
    acum_t = _mask_dot_nt(a_t, causf)
    atot_t = _mask_dot_nt(a_t, same.astype(F32))
    wall_t = jnp.exp(atot_t - acum_t)
    eatot_t = jnp.exp(atot_t)
    eall = jnp.exp(acum)
    lo = _iota2((rows, LANES), 1) < SSD_P
    top = _iota2((LANES, rows), 0) < SSD_P
    top_sq = _iota2((LANES, SSD_N), 0) < SSD_P
    colseq = _iota2((LANES, rows), 1) // sl_len
    ys = []
    for pair in range(SSD_PAIRS):
        grp = pair // 2
        cg = cm[:, grp * SSD_N:(grp + 1) * SSD_N]
        bg = bm[:, grp * SSD_N:(grp + 1) * SSD_N]
        cb = _dot_nt(cg, bg)
        psl = slice(pair * LANES, (pair + 1) * LANES)
        r0, r1 = 2 * pair, 2 * pair + 1
        xp = xs[:, psl] * jnp.where(lo, _lane_bcast(dtf, r0, LANES), _lane_bcast(dtf, r1, LANES))
        xw_t = xp.T * jnp.where(top, wall_t[r0:r0 + 1, :], wall_t[r1:r1 + 1, :])
        intra = []
        for r in (r0, r1):
            col = _lane_bcast(acum, r, rows)
            row = jnp.broadcast_to(acum_t[r:r + 1, :], (rows, rows))
            dec = jnp.exp(jnp.where(causal, col - row, -jnp.inf))
            intra.append(_dot(cb * dec, xp))
        y_intra = jnp.where(lo, intra[0], intra[1])
        y_inter = []
        for b in range(nb):
            tr = slice(b * sl_len, (b + 1) * sl_len)
            h_old = sssm_ref[b, pair]
            y_inter.append(_dot_nt(cg[tr, :], h_old))
            upd = _dot(jnp.where(colseq == b, xw_t, 0.0), bg)
            c0 = b * sl_len
            ea = jnp.where(top_sq, eatot_t[r0:r0 + 1, c0:c0 + 1], eatot_t[r1:r1 + 1, c0:c0 + 1])
            ossm_ref[n_prev, b, pair] = ea * h_old + upd
        e_pair = jnp.where(lo, _lane_bcast(eall, r0, LANES), _lane_bcast(eall, r1, LANES))
        ys.append(y_intra + e_pair * jnp.concatenate(y_inter, axis=0))
    y = jnp.concatenate(ys, axis=1)
    y = _ssd_finish(y, xs, p_ref[:, C_Z:C_Z + SSD_W], dsk_ref[...], ssn_ref[...])
    mix_ref[:, HG_W:] = y.astype(BF16)


def _out_ffn_body(n_h, *refs):
    h_refs, mix_refs = refs[:n_h], refs[n_h:n_h + 2]
    wo_ref, g_ref, wg_ref, wu_ref, wd_ref, o_ref, hn_ref = refs[n_h + 2:]
    i = pl.program_id(0)
    f = pl.program_id(1)

    @pl.when(f == 0)
    def _():
        h1 = _rows(i, h_refs) + _dot(_rows(i, mix_refs), wo_ref[...])
        o_ref[...] = h1
        hn_ref[...] = _rms(h1, g_ref[...]).astype(BF16)

    o_ref[...] += _swiglu_part(hn_ref[...], wg_ref[...], wu_ref[...], wd_ref[...])


def _out_ffn_call(h, mixed, wo_bf, g, wi_bf, wd_bf):
    tm, tf = 1024, D_FF // 2
    nf = D_FF // tf
    return pl.pallas_call(
        functools.partial(_out_ffn_body, len(h)),
        grid=(T_ALL // tm, nf),
        in_specs=_row_specs(h, tm) + _row_specs(mixed, tm) + [
                  pl.BlockSpec((D_MODEL, D_MODEL), lambda i, f: (0, 0)),
                  pl.BlockSpec((1, D_MODEL), lambda i, f: (0, 0)),
                  pl.BlockSpec((D_MODEL, tf), lambda i, f: (0, f)),
                  pl.BlockSpec((D_MODEL, tf), lambda i, f: (0, nf + f)),
                  pl.BlockSpec((tf, D_MODEL), lambda i, f: (f, 0))],
        out_specs=pl.BlockSpec((tm, D_MODEL), lambda i, f: (i, 0)),
        out_shape=jax.ShapeDtypeStruct((T_ALL, D_MODEL), F32),
        scratch_shapes=[pltpu.VMEM((tm, D_MODEL), BF16)],
        compiler_params=pltpu.CompilerParams(dimension_semantics=("arbitrary", "arbitrary"),
                                             vmem_limit_bytes=VMEM_LIMIT),
        name="out_ffn",
    )(*h, *mixed, wo_bf, g, wi_bf, wi_bf, wd_bf)


def _out_router_body(h_ref, mixp_ref, mixs_ref, wo_ref, g_ref, wr_ref,
                     h1_ref, hn_ref, route_ref, seg_ref):
    h1 = h_ref[...] + _dot(_rows(pl.program_id(0), (mixp_ref, mixs_ref)), wo_ref[...])
    h1_ref[...] = h1
    hn = _rms(h1, g_ref[...])
    hn_ref[...] = hn.astype(BF16)
    logits = _dot_f32x3(hn, wr_ref[...])
    lane = _iota2(logits.shape, 1)
    lg = jnp.where(lane < N_EXPERTS, logits, -jnp.inf)
    m1 = jnp.max(lg, axis=1, keepdims=True)
    i1 = jnp.min(jnp.where(lg == m1, lane, LANES), axis=1, keepdims=True)
    lg2 = jnp.where(lane == i1, -jnp.inf, lg)
    m2 = jnp.max(lg2, axis=1, keepdims=True)
    i2 = jnp.min(jnp.where(lg2 == m2, lane, LANES), axis=1, keepdims=True)
    e2 = jnp.exp(m2 - m1)
    g1 = 1.0 / (1.0 + e2)
    g2 = e2 / (1.0 + e2)

    tm = logits.shape[0]
    chosen = jnp.where((lane == i1) | (lane == i2), 1.0, 0.0)
    t_row, t_col = _iota2((tm, tm), 0), _iota2((tm, tm), 1)
    earlier = ((t_col < t_row) & (t_col // MOE_TOKENS == t_row // MOE_TOKENS)).astype(F32)
    rank = _dot(earlier, chosen)
    below = (_iota2((LANES, LANES), 0) < _iota2((LANES, LANES), 1)).astype(F32)
    pos = []
    for k in range(tm // MOE_TOKENS):
        rows = slice(k * MOE_TOKENS, (k + 1) * MOE_TOKENS)
        cnt = jnp.sum(chosen[rows], axis=0, keepdims=True)
        seg = jnp.floor((cnt + (SUBLANES - 1)) * (1.0 / SUBLANES)) * SUBLANES
        seg8 = jnp.broadcast_to(seg, (SUBLANES, LANES))
        pos.append(_dot(seg8, below)[0:1, :] + rank[rows])
        seg_ref[k] = seg8.astype(jnp.int32)
    pos = jnp.concatenate(pos, axis=0)
    p1 = jnp.sum(jnp.where(lane == i1, pos, 0.0), axis=1, keepdims=True)
    p2 = jnp.sum(jnp.where(lane == i2, pos, 0.0), axis=1, keepdims=True)
    route_ref[...] = jnp.where(lane == 0, p1, jnp.where(lane == 1, p2, jnp.where(
        lane == 2, g1, jnp.where(lane == 3, g2, 0.0))))


def _out_router_call(h, mixed, wo_bf, g, wr_pad):
    tm = ROUTER_TILES * MOE_TOKENS
    return pl.pallas_call(
        _out_router_body,
        grid=(T_ALL // tm,),
        in_specs=[pl.BlockSpec((tm, D_MODEL), lambda i: (i, 0))] + _row_specs(mixed, tm) + [
                  pl.BlockSpec((D_MODEL, D_MODEL), lambda i: (0, 0)),
                  pl.BlockSpec((1, D_MODEL), lambda i: (0, 0)),
                  pl.BlockSpec((D_MODEL, LANES), lambda i: (0, 0))],
        out_specs=[pl.BlockSpec((tm, D_MODEL), lambda i: (i, 0)),
                   pl.BlockSpec((tm, D_MODEL), lambda i: (i, 0)),
                   pl.BlockSpec((tm, LANES), lambda i: (i, 0)),
                   pl.BlockSpec((ROUTER_TILES, SUBLANES, LANES), lambda i: (i, 0, 0))],
        out_shape=[jax.ShapeDtypeStruct((T_ALL, D_MODEL), F32),
                   jax.ShapeDtypeStruct((T_ALL, D_MODEL), BF16),
                   jax.ShapeDtypeStruct((T_ALL, LANES), F32),
                   jax.ShapeDtypeStruct((MOE_TILES, SUBLANES, LANES), jnp.int32)],
        compiler_params=pltpu.CompilerParams(dimension_semantics=("arbitrary",),
                                             vmem_limit_bytes=VMEM_LIMIT),
        name="out_router",
    )(h, *mixed, wo_bf, g, wr_pad)


def _row_tile_copy(tile_ref, hbm_ref, tile_row, hbm_row, sem, to_hbm, rows=SUBLANES):
    t = tile_ref.at[pl.ds(pl.multiple_of(tile_row, SUBLANES), rows), :]
    g = hbm_ref.at[pl.ds(pl.multiple_of(hbm_row, SUBLANES), rows), :]
    return pltpu.make_async_copy(t, g, sem) if to_hbm else pltpu.make_async_copy(g, t, sem)


def _seg_plan(n):
    counts = [lax.shift_right_logical(n, 3)]
    for shift in (2, 1, 0):
        counts.append(lax.shift_right_logical(n, shift) & 1)
    return counts


def _seg_copies(i, n_ref, loc_ref, start_ref, tile_ref, hbm_ref, sem, to_hbm):
    for e in range(N_EXPERTS):
        k = i * N_EXPERTS + e
        lo = loc_ref[k]
        st = start_ref[k]
        counts = _seg_plan(n_ref[k])
        big = SEG_COPY_ROWS[0]

        def body(c, carry, lo=lo, st=st):
            _row_tile_copy(tile_ref, hbm_ref, lo + c * big, st + c * big, sem, to_hbm, big).start()
            return carry

        lax.fori_loop(0, counts[0], body, 0)
        done = counts[0] * big
        for rows, cnt in zip(SEG_COPY_ROWS[1:], counts[1:]):
            @pl.when(cnt == 1)
            def _(rows=rows, done=done, lo=lo, st=st):
                _row_tile_copy(tile_ref, hbm_ref, lo + done, st + done, sem, to_hbm, rows).start()

            done = done + cnt * rows


def _seg_wait(i, n_ref, tile_ref, hbm_ref, sem, to_hbm):
    totals = None
    for e in range(N_EXPERTS):
        counts = _seg_plan(n_ref[i * N_EXPERTS + e])
        totals = counts if totals is None else [a + b for a, b in zip(totals, counts)]
    for rows, total in zip(SEG_COPY_ROWS, totals):
        def body(c, carry, rows=rows):
            _row_tile_copy(tile_ref, hbm_ref, 0, 0, sem, to_hbm, rows).wait()
            return carry

        lax.fori_loop(0, total, body, 0)


def _dispatch_body(n_ref, loc_ref, start_ref, zn_ref, zstart_ref, zb_ref, zbn_ref,
                   hn_ref, route_ref, xs_ref, stage_ref, zero_ref, sem):
    i = pl.program_id(0)
    last = pl.num_programs(0) - 1
    slot = i % 2

    def blk_copy(b):
        return pltpu.make_async_copy(
            zero_ref, xs_ref.at[pl.ds(pl.multiple_of(b * GMM_SUB, GMM_SUB), GMM_SUB), :], sem.at[3])

    @pl.when(i == 0)
    def _():
        zero_ref[...] = jnp.zeros_like(zero_ref)

        def bbody(b, carry):
            @pl.when(zb_ref[b] == 1)
            def _():
                blk_copy(b).start()
            return carry

        lax.fori_loop(0, GMM_TILES * GMM_ROWS // GMM_SUB, bbody, 0)
        for e in range(N_EXPERTS):
            st = zstart_ref[e]

            def body(c, carry, st=st):
                _row_tile_copy(zero_ref, xs_ref, 0, st + c * SUBLANES, sem.at[2], True).start()
                return carry

            lax.fori_loop(0, zn_ref[e], body, 0)

    @pl.when(i >= 2)
    def _():
        _seg_wait(i - 2, n_ref, stage_ref.at[slot], xs_ref, sem.at[slot], True)

    rt = route_ref[...].T
    s = _iota2((MOE_SLOTS, MOE_TOKENS), 0).astype(F32)
    perm = jnp.where((s == rt[0:1, :]) | (s == rt[1:2, :]), 1.0, 0.0)
    stage_ref[slot] = _dot(perm, hn_ref[...])
    _seg_copies(i, n_ref, loc_ref, start_ref, stage_ref.at[slot], xs_ref, sem.at[slot], True)

    @pl.when(i == last)
    def _():
        @pl.when(last >= 1)
        def _():
            _seg_wait(i - 1, n_ref, stage_ref.at[1 - slot], xs_ref, sem.at[1 - slot], True)

        _seg_wait(i, n_ref, stage_ref.at[slot], xs_ref, sem.at[slot], True)
        nz = zn_ref[0]
        for e in range(1, N_EXPERTS):
            nz = nz + zn_ref[e]

        def zbody(c, carry):
            _row_tile_copy(zero_ref, xs_ref, 0, 0, sem.at[2], True).wait()
            return carry

        lax.fori_loop(0, nz, zbody, 0)

        def bwait(c, carry):
            blk_copy(0).wait()
            return carry

        lax.fori_loop(0, zbn_ref[0], bwait, 0)


def _dispatch_call(sched, hn_bf, route):
    grid_spec = pltpu.PrefetchScalarGridSpec(
        num_scalar_prefetch=7,
        grid=(MOE_TILES,),
        in_specs=[pl.BlockSpec((MOE_TOKENS, D_MODEL), lambda i, *_: (i, 0)),
                  pl.BlockSpec((MOE_TOKENS, LANES), lambda i, *_: (i, 0))],
        out_specs=pl.BlockSpec(memory_space=pl.ANY),
        scratch_shapes=[pltpu.VMEM((2, MOE_SLOTS, D_MODEL), F32),
                        pltpu.VMEM((GMM_SUB, D_MODEL), F32),
                        pltpu.SemaphoreType.DMA((4,))])
    return pl.pallas_call(
        _dispatch_body,
        grid_spec=grid_spec,
        out_shape=jax.ShapeDtypeStruct((GMM_TILES * GMM_ROWS, D_MODEL), F32),
        compiler_params=pltpu.CompilerParams(dimension_semantics=("arbitrary",),
                                             vmem_limit_bytes=VMEM_LIMIT),
        name="moe_dispatch",
    )(sched["nch"], sched["loc"], sched["start"], sched["zn"], sched["zstart"], sched["zb"],
      sched["zbn"], hn_bf, route)


def _swiglu_part(x, wg, wu, wd):
    gate = jnp.dot(x, wg, preferred_element_type=F32)
    up = jnp.dot(x, wu, preferred_element_type=F32)
    return jnp.dot((_silu(gate) * up).astype(BF16), wd, preferred_element_type=F32)


def _experts_body(te_ref, tv_ref, nu_ref, x_ref, wg_ref, wu_ref, wd_ref, o_ref,
                  wgb_ref, wub_ref, wdb_ref):
    del te_ref, nu_ref
    i = pl.program_id(0)
    f = pl.program_id(1)
    nv = tv_ref[i]

    @pl.when(f == 0)
    def _():
        o_ref[...] = jnp.zeros_like(o_ref)

    @pl.when(nv == GMM_ROWS)
    def _():
        o_ref[...] += _swiglu_part(x_ref[...].astype(BF16), wg_ref[0].astype(BF16),
                                   wu_ref[0].astype(BF16), wd_ref[0].astype(BF16))

    @pl.when((nv > 0) & (nv < GMM_ROWS))
    def _():
        wgb_ref[...] = wg_ref[0].astype(BF16)
        wub_ref[...] = wu_ref[0].astype(BF16)
        wdb_ref[...] = wd_ref[0].astype(BF16)
        for sub in range(GMM_ROWS // GMM_SUB):
            rows = slice(sub * GMM_SUB, (sub + 1) * GMM_SUB)

            @pl.when(sub * GMM_SUB < nv)
            def _(rows=rows):
                o_ref[rows, :] += _swiglu_part(x_ref[rows, :].astype(BF16), wgb_ref[...],
                                               wub_ref[...], wdb_ref[...])


def _experts_call(sched, x_sorted, w_in_e, w_out_e):
    tf = 512
    nf = D_FF_EXPERT // tf

    def used(i, nu):
        return jnp.maximum(jnp.minimum(i, nu[0] - 1), 0)

    def fidx(i, f, nu):
        return jnp.where(i < nu[0], f, nf - 1)

    grid_spec = pltpu.PrefetchScalarGridSpec(
        num_scalar_prefetch=3,
        grid=(GMM_TILES, nf),
        in_specs=[pl.BlockSpec((GMM_ROWS, D_MODEL), lambda i, f, te, tv, nu: (used(i, nu), 0)),
                  pl.BlockSpec((1, D_MODEL, tf), lambda i, f, te, tv, nu: (te[i], 0, fidx(i, f, nu))),
                  pl.BlockSpec((1, D_MODEL, tf),
                               lambda i, f, te, tv, nu: (te[i], 0, nf + fidx(i, f, nu))),
                  pl.BlockSpec((1, tf, D_MODEL), lambda i, f, te, tv, nu: (te[i], fidx(i, f, nu), 0))],
        out_specs=pl.BlockSpec((GMM_ROWS, D_MODEL), lambda i, f, te, tv, nu: (i, 0)),
        scratch_shapes=[pltpu.VMEM((D_MODEL, tf), BF16), pltpu.VMEM((D_MODEL, tf), BF16),
                        pltpu.VMEM((tf, D_MODEL), BF16)])
    return pl.pallas_call(
        _experts_body,
        grid_spec=grid_spec,
        out_shape=jax.ShapeDtypeStruct((GMM_TILES * GMM_ROWS, D_MODEL), F32),
        compiler_params=pltpu.CompilerParams(dimension_semantics=("arbitrary", "arbitrary"),
                                             vmem_limit_bytes=VMEM_LIMIT),
        name="moe_experts",
    )(sched["te"], sched["tv"], sched["nu"], x_sorted, w_in_e, w_in_e, w_out_e)


def _combine_body(n_ref, loc_ref, start_ref, h1_ref, route_ref, gf_ref, ys_ref, op_ref, os_ref,
                  buf_ref, sem):
    i = pl.program_id(0)
    last = pl.num_programs(0) - 1
    slot = i % 2

    @pl.when(i == 0)
    def _():
        buf_ref[...] = jnp.zeros_like(buf_ref)
        _seg_copies(0, n_ref, loc_ref, start_ref, buf_ref.at[0], ys_ref, sem.at[0], False)

    @pl.when(i < last)
    def _():
        _seg_copies(i + 1, n_ref, loc_ref, start_ref, buf_ref.at[1 - slot], ys_ref,
                    sem.at[1 - slot], False)

    _seg_wait(i, n_ref, buf_ref.at[slot], ys_ref, sem.at[slot], False)

    route = route_ref[...]
    s = _iota2((MOE_TOKENS, MOE_SLOTS), 1).astype(F32)
    rows = buf_ref[slot]
    y1 = _dot(jnp.where(s == route[:, 0:1], 1.0, 0.0), rows)
    y2 = _dot(jnp.where(s == route[:, 1:2], 1.0, 0.0), rows)
    y = _rms(h1_ref[...] + route[:, 2:3] * y1 + route[:, 3:4] * y2, gf_ref[...])

    @pl.when(i < T_PROMPT // MOE_TOKENS)
    def _():
        op_ref[...] = y

    @pl.when(i >= T_PROMPT // MOE_TOKENS)
    def _():
        os_ref[...] = y


def _combine_call(sched, h1, route, gfin, y_sorted):
    npt = T_PROMPT // MOE_TOKENS
    grid_spec = pltpu.PrefetchScalarGridSpec(
        num_scalar_prefetch=3,
        grid=(MOE_TILES,),
        in_specs=[pl.BlockSpec((MOE_TOKENS, D_MODEL), lambda i, *_: (i, 0)),
                  pl.BlockSpec((MOE_TOKENS, LANES), lambda i, *_: (i, 0)),
                  pl.BlockSpec((1, D_MODEL), lambda i, *_: (0, 0)),
                  pl.BlockSpec(memory_space=pl.ANY)],
        out_specs=[pl.BlockSpec((MOE_TOKENS, D_MODEL), lambda i, *_: (jnp.minimum(i, npt - 1), 0)),
                   pl.BlockSpec((MOE_TOKENS, D_MODEL), lambda i, *_: (jnp.maximum(i - npt, 0), 0))],
        scratch_shapes=[pltpu.VMEM((2, MOE_SLOTS, D_MODEL), F32),
                        pltpu.SemaphoreType.DMA((2,))])
    return pl.pallas_call(
        _combine_body,
        grid_spec=grid_spec,
        out_shape=[jax.ShapeDtypeStruct((T_PROMPT, D_MODEL), F32),
                   jax.ShapeDtypeStruct((T_SAMPLE, D_MODEL), F32)],
        compiler_params=pltpu.CompilerParams(dimension_semantics=("arbitrary",),
                                             vmem_limit_bytes=VMEM_LIMIT),
        name="moe_combine",
    )(sched["nch"], sched["loc"], sched["start"], h1, route, gfin, y_sorted)


def _moe_schedule(seg):
    tot = jnp.sum(seg, axis=0)
    region = (tot + GMM_ROWS - 1) // GMM_ROWS * GMM_ROWS
    base = jnp.cumsum(region) - region
    start = base[None, :] + jnp.cumsum(seg, axis=0) - seg
    loc = jnp.cumsum(seg, axis=1) - seg
    ntile = region // GMM_ROWS
    cum = jnp.cumsum(ntile)
    nu = cum[-1]
    i = jnp.arange(GMM_TILES, dtype=jnp.int32)[:, None]
    first = (cum - ntile)[None, :]
    mine = (i >= first) & (i < cum[None, :])
    experts = jnp.arange(N_EXPERTS, dtype=jnp.int32)
    te = jnp.sum(jnp.where(mine, experts[None, :], 0), axis=1)
    tv = jnp.sum(jnp.where(mine, jnp.clip(tot[None, :] - (i - first) * GMM_ROWS, 0, GMM_ROWS), 0),
                 axis=1)
    te = jnp.where(i[:, 0] < nu, te, jnp.max(jnp.where(ntile > 0, experts, 0)))
    bstart = jnp.arange(GMM_TILES * GMM_ROWS // GMM_SUB, dtype=jnp.int32)[:, None] * GMM_SUB
    used_end = (base + (tot + GMM_SUB - 1) // GMM_SUB * GMM_SUB)[None, :]
    zb = jnp.any((bstart >= used_end) & (bstart < (base + region)[None, :]), axis=1)
    zb = (zb | (bstart[:, 0] >= jnp.sum(region))).astype(jnp.int32)
    return dict(nch=(seg // SUBLANES).reshape(-1), loc=loc.reshape(-1), start=start.reshape(-1),
                zn=((-tot) % GMM_SUB) // SUBLANES, zstart=base + tot,
                zb=zb, zbn=jnp.sum(zb).reshape(1),
                te=te, tv=tv.astype(jnp.int32), nu=nu.reshape(1).astype(jnp.int32))


def _row(x, width=None):
    x = x.astype(F32).reshape(1, -1)
    if width is not None and x.shape[1] < width:
        x = jnp.pad(x, ((0, 0), (0, width - x.shape[1])))
    return x


def _layer_params(l, lb_p, conv_w, conv_b, a_log, dt_bias, d_skip, hg_norm, ssd_norm):
    lb = jnp.sum(lb_p[1:l + 1], axis=0)
    rows = {P_LA: jnp.log(lb), P_LC: jnp.log1p(-lb), P_OMLB: 1.0 - lb, P_HGN: hg_norm[l],
            P_CB: conv_b[l], P_DTB: dt_bias[l], P_AN: -jnp.exp(a_log[l].astype(F32)),
            P_DSK: jnp.repeat(d_skip[l].astype(F32), SSD_P), P_SSN: ssd_norm[l]}
    rows.update({P_CW + tap: conv_w[l, tap] for tap in range(CONV_W)})
    zero = jnp.zeros((0,), F32)
    return jnp.stack([jnp.pad(rows.get(r, zero).astype(F32), (0, CONV_DIM - rows.get(r, zero).shape[0]))
                      for r in range(P_ROWS)])


def kernel(x_prompt, x_sample, state_hgrn, state_ssm, state_conv, norm_mix, w_in, conv_w, conv_b,
           a_log, dt_bias, d_skip, lb_param, hg_norm, ssd_norm, w_out, norm_ffn, w_ffn_in,
           w_ffn_out, w_router, w_exp_in, w_exp_out, norm_final):
    h = (x_prompt.reshape(T_PROMPT, D_MODEL), x_sample.reshape(T_SAMPLE, D_MODEL))
    lb_p = jax.nn.softmax(lb_param.astype(F32), axis=0)
    packed = (SSD_PAIRS, LANES, SSD_N)
    s_ssm_t = jnp.swapaxes(state_ssm, 3, 4).reshape((DEPTH, DEC_BATCH) + packed)
    s_conv_t = jnp.swapaxes(state_conv, 1, 2)
    outs = {k: [] for k in ("hg_p", "ssm_p", "conv_p", "conv_s")}
    sample_states = None
    for l in range(DEPTH):
        prm = _layer_params(l, lb_p, conv_w, conv_b, a_log, dt_bias, d_skip, hg_norm, ssd_norm)
        proj = _proj_call(l, h, _row(norm_mix[l]), w_in)
        mixed_p, hg_p, ssm_p, conv_p, mixed_s, hg_s, ssm_s, conv_s = _scan_call(
            l, proj, state_hgrn, s_ssm_t, s_conv_t, prm, sample_states)
        sample_states = (hg_s, ssm_s)
        mixed = (mixed_p, mixed_s)
        for k, val in zip(outs, (hg_p, ssm_p, conv_p, conv_s)):
            outs[k].append(val)
        wo_bf = w_out[l].astype(BF16)
        if l % 2 == 0:
            h = (_out_ffn_call(h, mixed, wo_bf, _row(norm_ffn[l]), w_ffn_in[l // 2].astype(BF16),
                               w_ffn_out[l // 2].astype(BF16)),)
        else:
            wr_pad = jnp.pad(w_router[l // 2].astype(F32), ((0, 0), (0, LANES - N_EXPERTS)))
            h1, hn_bf, route, seg = _out_router_call(h[0], mixed, wo_bf, _row(norm_ffn[l]), wr_pad)
            sched = _moe_schedule(seg[:, 0, :N_EXPERTS])
            x_sorted = _dispatch_call(sched, hn_bf, route)
            y_sorted = _experts_call(sched, x_sorted, w_exp_in[l // 2], w_exp_out[l // 2])
            h = _combine_call(sched, h1, route, _row(norm_final), y_sorted)
    y_prompt = h[0].reshape(BATCH, SEQ, D_MODEL)
    y_sample = h[1].reshape(DEC_BATCH, DEC_SEQ, D_MODEL)
    def unpack_ssm(s_t, batch):
        return jnp.swapaxes(s_t.reshape(DEPTH, batch, SSD_HEADS, SSD_P, SSD_N), 3, 4)

    return (y_prompt, y_sample, jnp.stack(outs["hg_p"]), unpack_ssm(jnp.stack(outs["ssm_p"]), BATCH),
            jnp.stack(outs["conv_p"]), sample_states[0], unpack_ssm(sample_states[1], DEC_BATCH),
            jnp.swapaxes(jnp.stack(outs["conv_s"]), 1, 2))
```

```python
import functools

import jax
import jax.numpy as jnp
from jax import lax
from jax.experimental import pallas as pl
from jax.experimental.pallas import tpu as pltpu

F32 = jnp.float32
BF16 = jnp.bfloat16

D_MODEL = 1024
BATCH = 8
SEQ = 2048
DEPTH = 2
DEC_BATCH = 128
DEC_SEQ = 8
HG_HEADS = 4
HG_D = 128
HG_W = HG_HEADS * HG_D
SSD_HEADS = 8
SSD_P = 64
SSD_N = 128
SSD_W = SSD_HEADS * SSD_P
SSD_PAIRS = SSD_HEADS // 2
CONV_W = 4
CONV_DIM = 1024
D_FF = 2816
N_EXPERTS = 8
D_FF_EXPERT = 3584
EPS = 1e-6

LANES = 128
C_Q, C_F, C_I, C_OG, C_Z, C_XBC, C_DT = 0, 512, 1024, 1536, 2048, 2560, 3584
PROJ_W = C_DT + LANES
HG_CHUNK = 64
SSD_CHUNK = 128
SAMPLE_SEQS = 4
PROMPT_SEQS = 4
HG_SAFE_RANGE = 80.0
LOG2 = 0.6931471805599453
P_LA, P_LC, P_OMLB, P_HGN, P_CW, P_CB, P_DTB, P_AN, P_DSK, P_SSN, P_ROWS = 0, 1, 2, 3, 4, 8, 9, 10, 11, 12, 16

T_PROMPT = BATCH * SEQ
T_SAMPLE = DEC_BATCH * DEC_SEQ
T_ALL = T_PROMPT + T_SAMPLE

SUBLANES = 8
TOP_K = 2
MOE_TOKENS = 256
MOE_TILES = T_ALL // MOE_TOKENS
ROUTER_TILES = 1
MOE_SLOTS = TOP_K * MOE_TOKENS + N_EXPERTS * SUBLANES
SEG_COPY_ROWS = (64, 32, 16, 8)
GMM_ROWS = 1024
GMM_SUB = 256
GMM_TILES = (TOP_K * T_ALL + MOE_TILES * N_EXPERTS * (SUBLANES - 1)
             + N_EXPERTS * (GMM_ROWS - 1)) // GMM_ROWS + 1

VMEM_LIMIT = 60000 * 1024


def _sigmoid(x):
    return 0.5 * jnp.tanh(0.5 * x) + 0.5


def _silu(x):
    return x * _sigmoid(x)


def _softplus(x):
    return jnp.maximum(x, 0.0) + jnp.log(1.0 + jnp.exp(-jnp.abs(x)))


def _rms(x, g):
    return x * lax.rsqrt(jnp.mean(x * x, axis=-1, keepdims=True) + EPS) * g


def _dot(a, b):
    return jnp.dot(a.astype(BF16), b.astype(BF16), preferred_element_type=F32)


def _dot_nt(a, b):
    return lax.dot_general(a.astype(BF16), b.astype(BF16), (((1,), (1,)), ((), ())),
                           preferred_element_type=F32)


def _split3(x):
    x1 = x.astype(BF16)
    r1 = x - x1.astype(F32)
    x2 = r1.astype(BF16)
    x3 = (r1 - x2.astype(F32)).astype(BF16)
    return x1, x2, x3


def _mask_dot(m, x):
    mb = m.astype(BF16)
    x1, x2, x3 = _split3(x)
    return (jnp.dot(mb, x1, preferred_element_type=F32) + jnp.dot(mb, x2, preferred_element_type=F32)
            + jnp.dot(mb, x3, preferred_element_type=F32))


def _mask_dot_nt(x, m):
    mb = m.astype(BF16)
    dn = (((1,), (1,)), ((), ()))
    x1, x2, x3 = _split3(x)
    return (lax.dot_general(x1, mb, dn, preferred_element_type=F32)
            + lax.dot_general(x2, mb, dn, preferred_element_type=F32)
            + lax.dot_general(x3, mb, dn, preferred_element_type=F32))


def _dot_f32x3(a, b):
    a1, a2, _ = _split3(a)
    b1, b2, _ = _split3(b)
    return (jnp.dot(a1, b1, preferred_element_type=F32) + jnp.dot(a1, b2, preferred_element_type=F32)
            + jnp.dot(a2, b1, preferred_element_type=F32))


def _iota2(shape, dim):
    return lax.broadcasted_iota(jnp.int32, shape, dim)


def _seq_masks(rows, seq_len):
    t = _iota2((rows, rows), 0)
    s = _iota2((rows, rows), 1)
    same = (t // seq_len) == (s // seq_len)
    causal = same & (s <= t)
    ref = same & ((s % seq_len) < seq_len // 2)
    return causal, ref, same


def _row_specs(arrs, tm):
    if len(arrs) == 1:
        return [pl.BlockSpec((tm, arrs[0].shape[1]), lambda i, *_: (i, 0))]
    npt = T_PROMPT // tm
    return [pl.BlockSpec((tm, arrs[0].shape[1]), lambda i, *_: (jnp.minimum(i, npt - 1), 0)),
            pl.BlockSpec((tm, arrs[1].shape[1]), lambda i, *_: (jnp.maximum(i - npt, 0), 0),
                         pipeline_mode=pl.Buffered(1))]


def _rows(i, refs):
    if len(refs) == 1:
        return refs[0][...]
    return jnp.where(i < T_PROMPT // refs[0].shape[0], refs[0][...], refs[1][...])


def _proj_body(n_h, *refs):
    h_refs, (g_ref, w_ref, o_ref, wb_ref) = refs[:n_h], refs[n_h:]
    i = pl.program_id(0)

    @pl.when(i == 0)
    def _():
        for r0 in range(0, C_DT, HG_W):
            wb_ref[r0:r0 + HG_W, :] = w_ref[r0:r0 + HG_W, :].astype(BF16)
        tail = w_ref[C_DT:, :]
        pad = jnp.zeros((LANES - tail.shape[0], D_MODEL), F32)
        wb_ref[C_DT:, :] = jnp.concatenate([tail, pad], axis=0).astype(BF16)

    hn = _rms(_rows(i, h_refs), g_ref[...])
    o_ref[...] = _dot_nt(hn, wb_ref[...])


def _proj_call(l, h, g, w_in):
    tm = 512
    w_t = jnp.swapaxes(w_in, 1, 2)
    return pl.pallas_call(
        functools.partial(_proj_body, len(h)),
        grid=(T_ALL // tm,),
        in_specs=_row_specs(h, tm) + [pl.BlockSpec((1, D_MODEL), lambda i: (0, 0)),
                                      pl.BlockSpec((None,) + w_t.shape[1:], lambda i: (l, 0, 0),
                                                   pipeline_mode=pl.Buffered(1))],
        out_specs=pl.BlockSpec((tm, PROJ_W), lambda i: (i, 0)),
        out_shape=jax.ShapeDtypeStruct((T_ALL, PROJ_W), F32),
        scratch_shapes=[pltpu.VMEM((PROJ_W, D_MODEL), BF16)],
        compiler_params=pltpu.CompilerParams(dimension_semantics=("arbitrary",),
                                             vmem_limit_bytes=VMEM_LIMIT),
        name="in_proj",
    )(*h, g, w_t)


def _hgrn_gates(p_q, p_f, la, lc, omlb):
    q = _silu(p_q)
    e = jnp.exp(-jnp.abs(p_f))
    b = lc + jnp.minimum(p_f, 0.0) - jnp.log(1.0 + e)
    logf = jnp.maximum(la, b) + jnp.log(1.0 + jnp.exp(-jnp.abs(la - b)))
    r = 1.0 / (1.0 + e)
    kin = omlb * jnp.where(p_f >= 0.0, e * r, r)
    return q, logf, kin


def _hgrn_decays(logf, causal, ref, same):
    rows = logf.shape[0]
    if ref is None:
        g = _mask_dot(causal, logf)
        return g, g[rows // 2 - 1:rows // 2, :], g[rows - 1:rows, :]
    m = jnp.concatenate([causal.astype(F32), ref.astype(F32), same.astype(F32)], axis=0)
    g3 = _mask_dot(m, logf)
    return g3[:rows], g3[rows:2 * rows], g3[2 * rows:]


def _hgrn_exact_intra(ex_ref, q, kin, g, v, seq_len):
    rows = q.shape[0]
    ex_ref[0] = g
    ex_ref[1] = q
    ex_ref[2] = kin
    ex_ref[3] = v
    s_idx = _iota2((rows, HG_W), 0)

    def body(t, carry):
        gt = ex_ref[0, pl.ds(t, 1), :]
        qt = ex_ref[1, pl.ds(t, 1), :]
        live = (s_idx <= t) & (s_idx // seq_len == t // seq_len)
        w = jnp.where(live, qt * ex_ref[2] * jnp.exp(jnp.minimum(gt - ex_ref[0], 0.0)), 0.0)
        outs = []
        for h in range(HG_HEADS):
            sl = slice(h * HG_D, (h + 1) * HG_D)
            score = jnp.sum(w[:, sl], axis=1, keepdims=True)
            outs.append(jnp.sum(score * ex_ref[3, :, sl], axis=0, keepdims=True))
        ex_ref[4, pl.ds(t, 1), :] = jnp.concatenate(outs, axis=1)
        return carry

    lax.fori_loop(0, rows, body, 0)
    return ex_ref[4]


def _hgrn_head_out(o, hgn, og):
    return _rms(o, hgn) * _silu(og)


def _ssd_inputs(conv, p_dt, dtb, a_neg):
    conv = _silu(conv)
    xs = conv[:, :SSD_W]
    bm = conv[:, SSD_W:SSD_W + 2 * SSD_N]
    cm = conv[:, SSD_W + 2 * SSD_N:]
    dtf = _softplus(p_dt + dtb)
    a = dtf * a_neg
    return xs, bm, cm, dtf, a


def _lane_bcast(x, lane, width):
    return jnp.broadcast_to(x[:, lane:lane + 1], (x.shape[0], width))


def _ssd_finish(y, xs, z, dsk, ssn):
    y = (y + dsk * xs) * _silu(z)
    half = SSD_W // 2
    return jnp.concatenate([_rms(y[:, :half], ssn[:, :half]), _rms(y[:, half:], ssn[:, half:])],
                           axis=1)


def _scan_body(n_prev, *refs):
    ns = PROMPT_SEQS
    p_refs, refs = refs[:ns], refs[ns:]
    (ps_ref, shg_ref, sssm_ref, sconv_ref), refs = refs[:4], refs[4:]
    prm_ref, refs = refs[0], refs[1:]
    row = lambda r, width: prm_ref.at[r:r + 1, 0:width]
    la_ref, lc_ref, omlb_ref = row(P_LA, HG_W), row(P_LC, HG_W), row(P_OMLB, HG_W)
    params = (row(P_HGN, HG_W), prm_ref.at[P_CW:P_CW + CONV_W, :], row(P_CB, CONV_DIM),
              row(P_DTB, LANES), row(P_AN, LANES), row(P_DSK, SSD_W), row(P_SSN, SSD_W))
    prev_refs, refs = refs[:2 * bool(n_prev)], refs[2 * bool(n_prev):]
    (mix_ref, ohg_ref, ossm_ref, oconv_ref, mixs_ref, ohgs_ref, ossms_ref, oconvs_ref,
     st_ref, hp_ref, xpad_ref, ex_ref, xpads_ref, exs_ref) = refs
    j = pl.program_id(1)
    tl = SSD_CHUNK

    @pl.when(j == 0)
    def _():
        st_ref[...] = jnp.zeros_like(st_ref)
        hp_ref[...] = jnp.zeros_like(hp_ref)
        xpad_ref[:, 0:8, :] = jnp.zeros((ns, 8, CONV_DIM), F32)

    def decay_bound(z):
        low = jnp.maximum(la_ref[...], lc_ref[...] + jnp.minimum(z, 0.0) - LOG2)
        return jnp.sum(low, axis=0, keepdims=True)

    bounds = [decay_bound(p_refs[s][r0:r0 + HG_CHUNK // 2, C_F:C_F + HG_W])
              for s in range(ns) for r0 in range(0, tl, HG_CHUNK // 2)]
    bounds += [decay_bound(ps_ref[b * DEC_SEQ:(b + 1) * DEC_SEQ, C_F:C_F + HG_W])
               for b in range(SAMPLE_SEQS)]
    safe = jnp.min(functools.reduce(jnp.minimum, bounds)) >= -HG_SAFE_RANGE
    gate_refs = (la_ref, lc_ref, omlb_ref)

    for exact in (False, True):
        @pl.when(safe != exact)
        def _(exact=exact):
            for s in range(ns):
                _scan_prompt_tile(exact, ex_ref, p_refs[s], gate_refs, *params, mix_ref.at[s],
                                  st_ref.at[s], hp_ref.at[s], xpad_ref.at[s])
            seq0 = (pl.program_id(0) * pl.num_programs(1) + j) % (SUBLANES // SAMPLE_SEQS) * SAMPLE_SEQS
            _scan_sample_tile(n_prev, exact, seq0, ps_ref, shg_ref, sssm_ref, sconv_ref, prev_refs,
                              gate_refs, *params, mixs_ref, ohgs_ref, ossms_ref, oconvs_ref,
                              xpads_ref, exs_ref)

    @pl.when(j == pl.num_programs(1) - 1)
    def _():
        for s in range(ns):
            oconv_ref[s] = xpad_ref[s, 5:8, :]
            for h in range(HG_HEADS):
                ohg_ref[s, h] = st_ref[s, h].T
        ossm_ref[...] = hp_ref[...]


def _scan_prompt_tile(exact, ex_ref, p_ref, gate_refs, hgn_ref, cw_ref, cb_ref, dtb_ref, an_ref,
                      dsk_ref, ssn_ref, mix_ref, st_ref, hp_ref, xpad_ref):
    tl = SSD_CHUNK
    causal, _, _ = _seq_masks(HG_CHUNK, HG_CHUNK)
    for c in range(tl // HG_CHUNK):
        r0 = c * HG_CHUNK
        rows = slice(r0, r0 + HG_CHUNK)
        q, logf, kin = _hgrn_gates(p_ref[rows, C_Q:C_Q + HG_W], p_ref[rows, C_F:C_F + HG_W],
                                   *(r[...] for r in gate_refs))
        v = p_ref[rows, C_I:C_I + HG_W]
        g, gmid, glast = _hgrn_decays(logf, causal, None, None)
        if exact:
            qh = q * jnp.exp(g)
            kd = kin * jnp.exp(glast - g)
            o_intra = _hgrn_exact_intra(ex_ref, q, kin, g, v, HG_CHUNK)
        else:
            qt = q * jnp.exp(g - gmid)
            kt = kin * jnp.exp(gmid - g)
            qh = qt * jnp.exp(gmid)
            kd = kt * jnp.exp(glast - gmid)
        ds = jnp.exp(glast[0:1, :])
        for h in range(HG_HEADS):
            sl = slice(h * HG_D, (h + 1) * HG_D)
            st = st_ref[h]
            if exact:
                o = o_intra[:, sl] + _dot_nt(qh[:, sl], st)
            else:
                sc = jnp.where(causal, _dot_nt(qt[:, sl], kt[:, sl]), 0.0)
                o = _dot(sc, v[:, sl]) + _dot_nt(qh[:, sl], st)
            st_ref[h] = st * ds[:, sl] + _dot(v[:, sl].T, kd[:, sl])
            og = p_ref[rows, C_OG + h * HG_D:C_OG + (h + 1) * HG_D]
            mix_ref[rows, sl] = _hgrn_head_out(o, hgn_ref[:, sl], og).astype(BF16)

    xpad_ref[8:8 + tl, :] = p_ref[:, C_XBC:C_XBC + CONV_DIM]
    conv = cb_ref[...] + cw_ref[0:1, :] * xpad_ref[5:5 + tl, :]
    for tap in range(1, CONV_W):
        conv = conv + cw_ref[tap:tap + 1, :] * xpad_ref[5 + tap:5 + tap + tl, :]
    xpad_ref[0:8, :] = xpad_ref[tl:tl + 8, :]

    xs, bm, cm, dtf, a = _ssd_inputs(conv, p_ref[:, C_DT:C_DT + LANES], dtb_ref[...], an_ref[...])
    tri, _, _ = _seq_masks(tl, tl)
    trif = tri.astype(F32)
    acum = _mask_dot(trif, a)
    acum_t = _mask_dot_nt(a.T, trif)
    lo = _iota2((tl, LANES), 1) < SSD_P
    top = _iota2((LANES, tl), 0) < SSD_P
    ys = []
    for pair in range(SSD_PAIRS):
        grp = pair // 2
        cg = cm[:, grp * SSD_N:(grp + 1) * SSD_N]
        bg = bm[:, grp * SSD_N:(grp + 1) * SSD_N]
        cb = _dot_nt(cg, bg)
        psl = slice(pair * LANES, (pair + 1) * LANES)
        r0, r1 = 2 * pair, 2 * pair + 1
        xp = xs[:, psl] * jnp.where(lo, _lane_bcast(dtf, r0, LANES), _lane_bcast(dtf, r1, LANES))
        intra, einter, wrow, ea = [], [], [], []
        for r in (r0, r1):
            col = _lane_bcast(acum, r, tl)
            row = jnp.broadcast_to(acum_t[r:r + 1, :], (tl, tl))
            dec = jnp.exp(jnp.where(tri, col - row, -jnp.inf))
            intra.append(_dot(cb * dec, xp))
            einter.append(jnp.exp(col))
            alast = acum_t[r:r + 1, tl - 1:tl]
            wrow.append(jnp.exp(alast - acum_t[r:r + 1, :]))
            ea.append(jnp.exp(alast))
        hp = hp_ref[pair]
        y = jnp.where(lo, intra[0], intra[1]) + jnp.where(lo, einter[0], einter[1]) * _dot_nt(cg, hp)
        xw_t = xp.T * jnp.where(top, wrow[0], wrow[1])
        hp_ref[pair] = jnp.where(top, ea[0], ea[1]) * hp + _dot(xw_t, bg)
        ys.append(y)
    y = jnp.concatenate(ys, axis=1)
    y = _ssd_finish(y, xs, p_ref[:, C_Z:C_Z + SSD_W], dsk_ref[...], ssn_ref[...])
    mix_ref[:, HG_W:] = y.astype(BF16)


def _scan_call(l, proj, s_hg, s_ssm_t, s_conv_t, prm, prev):
    tl = SSD_CHUNK
    nt = SEQ // tl
    ns, nb = PROMPT_SEQS, SAMPLE_SEQS
    assert BATCH // ns * nt * nb == DEC_BATCH
    srows = nb * DEC_SEQ
    srow0 = T_PROMPT // srows
    step = lambda b, j: b * nt + j
    pstate, sstate = (ns, HG_HEADS, HG_D, HG_D), (nb, HG_HEADS, HG_D, HG_D)
    in_specs = [pl.BlockSpec((tl, PROJ_W), lambda b, j, s=s: ((b * ns + s) * nt + j, 0))
                for s in range(ns)]
    in_specs += [pl.BlockSpec((srows, PROJ_W), lambda b, j: (srow0 + step(b, j), 0)),
                 pl.BlockSpec((None,) + sstate, lambda b, j: (l, step(b, j), 0, 0, 0)),
                 pl.BlockSpec((None,) + sstate, lambda b, j: (l, step(b, j), 0, 0, 0)),
                 pl.BlockSpec((None, CONV_W - 1, SUBLANES, CONV_DIM),
                              lambda b, j: (l, 0, step(b, j) // (SUBLANES // nb), 0))]
    in_specs += [pl.BlockSpec(prm.shape, lambda b, j: (0, 0))]
    in_specs += [pl.BlockSpec((l,) + sstate, lambda b, j: (0, step(b, j), 0, 0, 0))] * (2 if l else 0)
    out_specs = [pl.BlockSpec((ns, tl, D_MODEL), lambda b, j: (b, j, 0)),
                 pl.BlockSpec(pstate, lambda b, j: (b, 0, 0, 0)),
                 pl.BlockSpec(pstate, lambda b, j: (b, 0, 0, 0)),
                 pl.BlockSpec((ns, CONV_W - 1, CONV_DIM), lambda b, j: (b, 0, 0)),
                 pl.BlockSpec((srows, D_MODEL), lambda b, j: (step(b, j), 0)),
                 pl.BlockSpec((l + 1,) + sstate, lambda b, j: (0, step(b, j), 0, 0, 0)),
                 pl.BlockSpec((l + 1,) + sstate, lambda b, j: (0, step(b, j), 0, 0, 0)),
                 pl.BlockSpec((CONV_W - 1, SUBLANES, CONV_DIM),
                              lambda b, j: (0, step(b, j) // (SUBLANES // nb), 0))]
    stacked = jax.ShapeDtypeStruct((l + 1, DEC_BATCH) + sstate[1:], F32)
    out_shape = [jax.ShapeDtypeStruct((BATCH, SEQ, D_MODEL), BF16),
                 jax.ShapeDtypeStruct((BATCH,) + pstate[1:], F32),
                 jax.ShapeDtypeStruct((BATCH,) + pstate[1:], F32),
                 jax.ShapeDtypeStruct((BATCH, CONV_W - 1, CONV_DIM), F32),
                 jax.ShapeDtypeStruct((T_SAMPLE, D_MODEL), BF16), stacked, stacked,
                 jax.ShapeDtypeStruct((CONV_W - 1, DEC_BATCH, CONV_DIM), F32)]
    outs = pl.pallas_call(
        functools.partial(_scan_body, l),
        grid=(BATCH // ns, nt),
        in_specs=in_specs, out_specs=out_specs, out_shape=out_shape,
        scratch_shapes=[pltpu.VMEM(pstate, F32), pltpu.VMEM(pstate, F32),
                        pltpu.VMEM((ns, tl + 8, CONV_DIM), F32),
                        pltpu.VMEM((5, HG_CHUNK, HG_W), F32),
                        pltpu.VMEM((16 * nb, CONV_DIM), F32), pltpu.VMEM((5, srows, HG_W), F32)],
        compiler_params=pltpu.CompilerParams(dimension_semantics=("arbitrary", "arbitrary"),
                                             vmem_limit_bytes=VMEM_LIMIT),
        name="scan",
    )(*([proj] * (ns + 1)), s_hg, s_ssm_t, s_conv_t, prm, *(prev if l else ()))
    return (outs[0].reshape(T_PROMPT, D_MODEL),) + tuple(outs[1:])


def _scan_sample_tile(n_prev, exact, seq0, p_ref, shg_ref, sssm_ref, sconv_ref, prev_refs,
                      gate_refs, hgn_ref, cw_ref, cb_ref, dtb_ref, an_ref, dsk_ref, ssn_ref,
                      mix_ref, ohg_ref, ossm_ref, oconv_ref, xpad_ref, ex_ref):
    if n_prev:
        ohg_ref[0:n_prev] = prev_refs[0][...]
        ossm_ref[0:n_prev] = prev_refs[1][...]
    nb, sl_len = SAMPLE_SEQS, DEC_SEQ
    rows = nb * sl_len
    causal, ref, same = _seq_masks(rows, sl_len)
    rowseq = _iota2((rows, LANES), 0) // sl_len

    q, logf, kin = _hgrn_gates(p_ref[:, C_Q:C_Q + HG_W], p_ref[:, C_F:C_F + HG_W],
                               *(r[...] for r in gate_refs))
    v = p_ref[:, C_I:C_I + HG_W]
    g, gmid, glast = _hgrn_decays(logf, causal, ref, same)
    if exact:
        o_intra_all = _hgrn_exact_intra(ex_ref, q, kin, g, v, sl_len)
    else:
        qt = q * jnp.exp(g - gmid)
        kt = kin * jnp.exp(gmid - g)
    qh = q * jnp.exp(g)
    kd = kin * jnp.exp(glast - g)
    ds_t = jnp.exp(glast).T
    kd_t = kd.T
    for h in range(HG_HEADS):
        sl = slice(h * HG_D, (h + 1) * HG_D)
        if exact:
            o_intra = o_intra_all[:, sl]
        else:
            sc = jnp.where(causal, _dot_nt(qt[:, sl], kt[:, sl]), 0.0)
            o_intra = _dot(sc, v[:, sl])
        o_inter = []
        for b in range(nb):
            s_old = shg_ref[b, h]
            o_inter.append(_dot(qh[b * sl_len:(b + 1) * sl_len, sl], s_old))
            vb = jnp.where(rowseq == b, v[:, sl], 0.0)
            dcol = _lane_bcast(ds_t[sl, :], b * sl_len, HG_D)
            ohg_ref[n_prev, b, h] = dcol * s_old + _dot(kd_t[sl, :], vb)
        o = o_intra + jnp.concatenate(o_inter, axis=0)
        og = p_ref[:, C_OG + h * HG_D:C_OG + (h + 1) * HG_D]
        mix_ref[:, sl] = _hgrn_head_out(o, hgn_ref[:, sl], og).astype(BF16)

    convs = []
    for b in range(nb):
        base = 16 * b
        for r in range(CONV_W - 1):
            xpad_ref[base + 5 + r:base + 6 + r, :] = sconv_ref[r, pl.ds(seq0 + b, 1), :]
        xpad_ref[base + 8:base + 16, :] = p_ref[b * sl_len:(b + 1) * sl_len, C_XBC:C_XBC + CONV_DIM]
        cv = cb_ref[...] + cw_ref[0:1, :] * xpad_ref[base + 5:base + 13, :]
        for tap in range(1, CONV_W):
            cv = cv + cw_ref[tap:tap + 1, :] * xpad_ref[base + 5 + tap:base + 13 + tap, :]
        convs.append(cv)
        for r in range(CONV_W - 1):
            oconv_ref[r, pl.ds(seq0 + b, 1), :] = xpad_ref[base + 13 + r:base + 14 + r, :]
    conv = jnp.concatenate(convs, axis=0)

    xs, bm, cm, dtf, a = _ssd_inputs(conv, p_ref[:, C_DT:C_DT + LANES], dtb_ref[...], an_ref[...])
    causf = causal.astype(F32)
    acum = _mask_dot(causf, a)
    a_t = a.T
    acum_t = _mask_dot_nt(a_t, causf)
    atot_t = _mask_dot_nt(a_t, same.astype(F32))
    wall_t = jnp.exp(atot_t - acum_t)
    eatot_t = jnp.exp(atot_t)
    eall = jnp.exp(acum)
    lo = _iota2((rows, LANES), 1) < SSD_P
    top = _iota2((LANES, rows), 0) < SSD_P
    top_sq = _iota2((LANES, SSD_N), 0) < SSD_P
    colseq = _iota2((LANES, rows), 1) // sl_len
    ys = []
    for pair in range(SSD_PAIRS):
        grp = pair // 2
        cg = cm[:, grp * SSD_N:(grp + 1) * SSD_N]
        bg = bm[:, grp * SSD_N:(grp + 1) * SSD_N]
        cb = _dot_nt(cg, bg)
        psl = slice(pair * LANES, (pair + 1) * LANES)
        r0, r1 = 2 * pair, 2 * pair + 1
        xp = xs[:, psl] * jnp.where(lo, _lane_bcast(dtf, r0, LANES), _lane_bcast(dtf, r1, LANES))
        xw_t = xp.T * jnp.where(top, wall_t[r0:r0 + 1, :], wall_t[r1:r1 + 1, :])
        intra = []
        for r in (r0, r1):
            col = _lane_bcast(acum, r, rows)
            row = jnp.broadcast_to(acum_t[r:r + 1, :], (rows, rows))
            dec = jnp.exp(jnp.where(causal, col - row, -jnp.inf))
            intra.append(_dot(cb * dec, xp))
        y_intra = jnp.where(lo, intra[0], intra[1])
        y_inter = []
        for b in range(nb):
            tr = slice(b * sl_len, (b + 1) * sl_len)
            h_old = sssm_ref[b, pair]
            y_inter.append(_dot_nt(cg[tr, :], h_old))
            upd = _dot(jnp.where(colseq == b, xw_t, 0.0), bg)
            c0 = b * sl_len
            ea = jnp.where(top_sq, eatot_t[r0:r0 + 1, c0:c0 + 1], eatot_t[r1:r1 + 1, c0:c0 + 1])
            ossm_ref[n_prev, b, pair] = ea * h_old + upd
        e_pair = jnp.where(lo, _lane_bcast(eall, r0, LANES), _lane_bcast(eall, r1, LANES))
        ys.append(y_intra + e_pair * jnp.concatenate(y_inter, axis=0))
    y = jnp.concatenate(ys, axis=1)
    y = _ssd_finish(y, xs, p_ref[:, C_Z:C_Z + SSD_W], dsk_ref[...], ssn_ref[...])
    mix_ref[:, HG_W:] = y.astype(BF16)


def _out_ffn_body(n_h, *refs):
    h_refs, mix_refs = refs[:n_h], refs[n_h:n_h + 2]
    wo_ref, g_ref, wg_ref, wu_ref, wd_ref, o_ref, hn_ref = refs[n_h + 2:]
    i = pl.program_id(0)
    f = pl.program_id(1)

    @pl.when(f == 0)
    def _():
        h1 = _rows(i, h_refs) + _dot(_rows(i, mix_refs), wo_ref[...])
        o_ref[...] = h1
        hn_ref[...] = _rms(h1, g_ref[...]).astype(BF16)

    o_ref[...] += _swiglu_part(hn_ref[...], wg_ref[...], wu_ref[...], wd_ref[...])


def _out_ffn_call(h, mixed, wo_bf, g, wi_bf, wd_bf):
    tm, tf = 1024, D_FF // 2
    nf = D_FF // tf
    return pl.pallas_call(
        functools.partial(_out_ffn_body, len(h)),
        grid=(T_ALL // tm, nf),
        in_specs=_row_specs(h, tm) + _row_specs(mixed, tm) + [
                  pl.BlockSpec((D_MODEL, D_MODEL), lambda i, f: (0, 0)),
                  pl.BlockSpec((1, D_MODEL), lambda i, f: (0, 0)),
                  pl.BlockSpec((D_MODEL, tf), lambda i, f: (0, f)),
                  pl.BlockSpec((D_MODEL, tf), lambda i, f: (0, nf + f)),
                  pl.BlockSpec((tf, D_MODEL), lambda i, f: (f, 0))],
        out_specs=pl.BlockSpec((tm, D_MODEL), lambda i, f: (i, 0)),
        out_shape=jax.ShapeDtypeStruct((T_ALL, D_MODEL), F32),
        scratch_shapes=[pltpu.VMEM((tm, D_MODEL), BF16)],
        compiler_params=pltpu.CompilerParams(dimension_semantics=("arbitrary", "arbitrary"),
                                             vmem_limit_bytes=VMEM_LIMIT),
        name="out_ffn",
    )(*h, *mixed, wo_bf, g, wi_bf, wi_bf, wd_bf)


def _out_router_body(h_ref, mixp_ref, mixs_ref, wo_ref, g_ref, wr_ref,
                     h1_ref, hn_ref, route_ref, seg_ref):
    h1 = h_ref[...] + _dot(_rows(pl.program_id(0), (mixp_ref, mixs_ref)), wo_ref[...])
    h1_ref[...] = h1
    hn = _rms(h1, g_ref[...])
    hn_ref[...] = hn.astype(BF16)
    logits = _dot_f32x3(hn, wr_ref[...])
    lane = _iota2(logits.shape, 1)
    lg = jnp.where(lane < N_EXPERTS, logits, -jnp.inf)
    m1 = jnp.max(lg, axis=1, keepdims=True)
    i1 = jnp.min(jnp.where(lg == m1, lane, LANES), axis=1, keepdims=True)
    lg2 = jnp.where(lane == i1, -jnp.inf, lg)
    m2 = jnp.max(lg2, axis=1, keepdims=True)
    i2 = jnp.min(jnp.where(lg2 == m2, lane, LANES), axis=1, keepdims=True)
    e2 = jnp.exp(m2 - m1)
    g1 = 1.0 / (1.0 + e2)
    g2 = e2 / (1.0 + e2)

    tm = logits.shape[0]
    chosen = jnp.where((lane == i1) | (lane == i2), 1.0, 0.0)
    t_row, t_col = _iota2((tm, tm), 0), _iota2((tm, tm), 1)
    earlier = ((t_col < t_row) & (t_col // MOE_TOKENS == t_row // MOE_TOKENS)).astype(F32)
    rank = _dot(earlier, chosen)
    below = (_iota2((LANES, LANES), 0) < _iota2((LANES, LANES), 1)).astype(F32)
    pos = []
    for k in range(tm // MOE_TOKENS):
        rows = slice(k * MOE_TOKENS, (k + 1) * MOE_TOKENS)
        cnt = jnp.sum(chosen[rows], axis=0, keepdims=True)
        seg = jnp.floor((cnt + (SUBLANES - 1)) * (1.0 / SUBLANES)) * SUBLANES
        seg8 = jnp.broadcast_to(seg, (SUBLANES, LANES))
        pos.append(_dot(seg8, below)[0:1, :] + rank[rows])
        seg_ref[k] = seg8.astype(jnp.int32)
    pos = jnp.concatenate(pos, axis=0)
    p1 = jnp.sum(jnp.where(lane == i1, pos, 0.0), axis=1, keepdims=True)
    p2 = jnp.sum(jnp.where(lane == i2, pos, 0.0), axis=1, keepdims=True)
    route_ref[...] = jnp.where(lane == 0, p1, jnp.where(lane == 1, p2, jnp.where(
        lane == 2, g1, jnp.where(lane == 3, g2, 0.0))))


def _out_router_call(h, mixed, wo_bf, g, wr_pad):
    tm = ROUTER_TILES * MOE_TOKENS
    return pl.pallas_call(
        _out_router_body,
        grid=(T_ALL // tm,),
        in_specs=[pl.BlockSpec((tm, D_MODEL), lambda i: (i, 0))] + _row_specs(mixed, tm) + [
                  pl.BlockSpec((D_MODEL, D_MODEL), lambda i: (0, 0)),
                  pl.BlockSpec((1, D_MODEL), lambda i: (0, 0)),
                  pl.BlockSpec((D_MODEL, LANES), lambda i: (0, 0))],
        out_specs=[pl.BlockSpec((tm, D_MODEL), lambda i: (i, 0)),
                   pl.BlockSpec((tm, D_MODEL), lambda i: (i, 0)),
                   pl.BlockSpec((tm, LANES), lambda i: (i, 0)),
                   pl.BlockSpec((ROUTER_TILES, SUBLANES, LANES), lambda i: (i, 0, 0))],
        out_shape=[jax.ShapeDtypeStruct((T_ALL, D_MODEL), F32),
                   jax.ShapeDtypeStruct((T_ALL, D_MODEL), BF16),
                   jax.ShapeDtypeStruct((T_ALL, LANES), F32),
                   jax.ShapeDtypeStruct((MOE_TILES, SUBLANES, LANES), jnp.int32)],
        compiler_params=pltpu.CompilerParams(dimension_semantics=("arbitrary",),
                                             vmem_limit_bytes=VMEM_LIMIT),
        name="out_router",
    )(h, *mixed, wo_bf, g, wr_pad)


def _row_tile_copy(tile_ref, hbm_ref, tile_row, hbm_row, sem, to_hbm, rows=SUBLANES):
    t = tile_ref.at[pl.ds(pl.multiple_of(tile_row, SUBLANES), rows), :]
    g = hbm_ref.at[pl.ds(pl.multiple_of(hbm_row, SUBLANES), rows), :]
    return pltpu.make_async_copy(t, g, sem) if to_hbm else pltpu.make_async_copy(g, t, sem)


def _seg_plan(n):
    counts = [lax.shift_right_logical(n, 3)]
    for shift in (2, 1, 0):
        counts.append(lax.shift_right_logical(n, shift) & 1)
    return counts


def _seg_copies(i, n_ref, loc_ref, start_ref, tile_ref, hbm_ref, sem, to_hbm):
    for e in range(N_EXPERTS):
        k = i * N_EXPERTS + e
        lo = loc_ref[k]
        st = start_ref[k]
        counts = _seg_plan(n_ref[k])
        big = SEG_COPY_ROWS[0]

        def body(c, carry, lo=lo, st=st):
            _row_tile_copy(tile_ref, hbm_ref, lo + c * big, st + c * big, sem, to_hbm, big).start()
            return carry

        lax.fori_loop(0, counts[0], body, 0)
        done = counts[0] * big
        for rows, cnt in zip(SEG_COPY_ROWS[1:], counts[1:]):
            @pl.when(cnt == 1)
            def _(rows=rows, done=done, lo=lo, st=st):
                _row_tile_copy(tile_ref, hbm_ref, lo + done, st + done, sem, to_hbm, rows).start()

            done = done + cnt * rows


def _seg_wait(i, n_ref, tile_ref, hbm_ref, sem, to_hbm):
    totals = None
    for e in range(N_EXPERTS):
        counts = _seg_plan(n_ref[i * N_EXPERTS + e])
        totals = counts if totals is None else [a + b for a, b in zip(totals, counts)]
    for rows, total in zip(SEG_COPY_ROWS, totals):
        def body(c, carry, rows=rows):
            _row_tile_copy(tile_ref, hbm_ref, 0, 0, sem, to_hbm, rows).wait()
            return carry

        lax.fori_loop(0, total, body, 0)


def _dispatch_body(n_ref, loc_ref, start_ref, zn_ref, zstart_ref, zb_ref, zbn_ref,
                   hn_ref, route_ref, xs_ref, stage_ref, zero_ref, sem):
    i = pl.program_id(0)
    last = pl.num_programs(0) - 1
    slot = i % 2

    def blk_copy(b):
        return pltpu.make_async_copy(
            zero_ref, xs_ref.at[pl.ds(pl.multiple_of(b * GMM_SUB, GMM_SUB), GMM_SUB), :], sem.at[3])

    @pl.when(i == 0)
    def _():
        zero_ref[...] = jnp.zeros_like(zero_ref)

        def bbody(b, carry):
            @pl.when(zb_ref[b] == 1)
            def _():
                blk_copy(b).start()
            return carry

        lax.fori_loop(0, GMM_TILES * GMM_ROWS // GMM_SUB, bbody, 0)
        for e in range(N_EXPERTS):
            st = zstart_ref[e]

            def body(c, carry, st=st):
                _row_tile_copy(zero_ref, xs_ref, 0, st + c * SUBLANES, sem.at[2], True).start()
                return carry

            lax.fori_loop(0, zn_ref[e], body, 0)

    @pl.when(i >= 2)
    def _():
        _seg_wait(i - 2, n_ref, stage_ref.at[slot], xs_ref, sem.at[slot], True)

    rt = route_ref[...].T
    s = _iota2((MOE_SLOTS, MOE_TOKENS), 0).astype(F32)
    perm = jnp.where((s == rt[0:1, :]) | (s == rt[1:2, :]), 1.0, 0.0)
    stage_ref[slot] = _dot(perm, hn_ref[...])
    _seg_copies(i, n_ref, loc_ref, start_ref, stage_ref.at[slot], xs_ref, sem.at[slot], True)

    @pl.when(i == last)
    def _():
        @pl.when(last >= 1)
        def _():
            _seg_wait(i - 1, n_ref, stage_ref.at[1 - slot], xs_ref, sem.at[1 - slot], True)

        _seg_wait(i, n_ref, stage_ref.at[slot], xs_ref, sem.at[slot], True)
        nz = zn_ref[0]
        for e in range(1, N_EXPERTS):
            nz = nz + zn_ref[e]

        def zbody(c, carry):
            _row_tile_copy(zero_ref, xs_ref, 0, 0, sem.at[2], True).wait()
            return carry

        lax.fori_loop(0, nz, zbody, 0)

        def bwait(c, carry):
            blk_copy(0).wait()
            return carry

        lax.fori_loop(0, zbn_ref[0], bwait, 0)


def _dispatch_call(sched, hn_bf, route):
    grid_spec = pltpu.PrefetchScalarGridSpec(
        num_scalar_prefetch=7,
        grid=(MOE_TILES,),
        in_specs=[pl.BlockSpec((MOE_TOKENS, D_MODEL), lambda i, *_: (i, 0)),
                  pl.BlockSpec((MOE_TOKENS, LANES), lambda i, *_: (i, 0))],
        out_specs=pl.BlockSpec(memory_space=pl.ANY),
        scratch_shapes=[pltpu.VMEM((2, MOE_SLOTS, D_MODEL), F32),
                        pltpu.VMEM((GMM_SUB, D_MODEL), F32),
                        pltpu.SemaphoreType.DMA((4,))])
    return pl.pallas_call(
        _dispatch_body,
        grid_spec=grid_spec,
        out_shape=jax.ShapeDtypeStruct((GMM_TILES * GMM_ROWS, D_MODEL), F32),
        compiler_params=pltpu.CompilerParams(dimension_semantics=("arbitrary",),
                                             vmem_limit_bytes=VMEM_LIMIT),
        name="moe_dispatch",
    )(sched["nch"], sched["loc"], sched["start"], sched["zn"], sched["zstart"], sched["zb"],
      sched["zbn"], hn_bf, route)


def _swiglu_part(x, wg, wu, wd):
    gate = jnp.dot(x, wg, preferred_element_type=F32)
    up = jnp.dot(x, wu, preferred_element_type=F32)
    return jnp.dot((_silu(gate) * up).astype(BF16), wd, preferred_element_type=F32)


def _experts_body(te_ref, tv_ref, nu_ref, x_ref, wg_ref, wu_ref, wd_ref, o_ref,
                  wgb_ref, wub_ref, wdb_ref, xb_ref):
    del te_ref, nu_ref
    i = pl.program_id(0)
    f = pl.program_id(1)
    nv = tv_ref[i]

    @pl.when(f == 0)
    def _():
        o_ref[...] = jnp.zeros_like(o_ref)
        xb_ref[...] = x_ref[...].astype(BF16)

    @pl.when(nv == GMM_ROWS)
    def _():
        o_ref[...] += _swiglu_part(xb_ref[...], wg_ref[0].astype(BF16),
                                   wu_ref[0].astype(BF16), wd_ref[0].astype(BF16))

    @pl.when((nv > 0) & (nv < GMM_ROWS))
    def _():
        wgb_ref[...] = wg_ref[0].astype(BF16)
        wub_ref[...] = wu_ref[0].astype(BF16)
        wdb_ref[...] = wd_ref[0].astype(BF16)
        for sub in range(GMM_ROWS // GMM_SUB):
            rows = slice(sub * GMM_SUB, (sub + 1) * GMM_SUB)

            @pl.when(sub * GMM_SUB < nv)
            def _(rows=rows):
                o_ref[rows, :] += _swiglu_part(xb_ref[rows, :], wgb_ref[...], wub_ref[...],
                                               wdb_ref[...])


def _experts_call(sched, x_sorted, w_in_e, w_out_e):
    tf = 512
    nf = D_FF_EXPERT // tf

    def used(i, nu):
        return jnp.maximum(jnp.minimum(i, nu[0] - 1), 0)

    def fidx(i, f, nu):
        last = nu[0] - 1
        return jnp.where(i <= last, jnp.where(i % 2 == 0, f, nf - 1 - f),
                         jnp.where(last % 2 == 0, nf - 1, 0))

    grid_spec = pltpu.PrefetchScalarGridSpec(
        num_scalar_prefetch=3,
        grid=(GMM_TILES, nf),
        in_specs=[pl.BlockSpec((GMM_ROWS, D_MODEL), lambda i, f, te, tv, nu: (used(i, nu), 0)),
                  pl.BlockSpec((1, D_MODEL, tf), lambda i, f, te, tv, nu: (te[i], 0, fidx(i, f, nu))),
                  pl.BlockSpec((1, D_MODEL, tf),
                               lambda i, f, te, tv, nu: (te[i], 0, nf + fidx(i, f, nu))),
                  pl.BlockSpec((1, tf, D_MODEL), lambda i, f, te, tv, nu: (te[i], fidx(i, f, nu), 0))],
        out_specs=pl.BlockSpec((GMM_ROWS, D_MODEL), lambda i, f, te, tv, nu: (i, 0)),
        scratch_shapes=[pltpu.VMEM((D_MODEL, tf), BF16), pltpu.VMEM((D_MODEL, tf), BF16),
                        pltpu.VMEM((tf, D_MODEL), BF16), pltpu.VMEM((GMM_ROWS, D_MODEL), BF16)])
    return pl.pallas_call(
        _experts_body,
        grid_spec=grid_spec,
        out_shape=jax.ShapeDtypeStruct((GMM_TILES * GMM_ROWS, D_MODEL), F32),
        compiler_params=pltpu.CompilerParams(dimension_semantics=("arbitrary", "arbitrary"),
                                             vmem_limit_bytes=VMEM_LIMIT),
        name="moe_experts",
    )(sched["te"], sched["tv"], sched["nu"], x_sorted, w_in_e, w_in_e, w_out_e)


def _combine_body(n_ref, loc_ref, start_ref, h1_ref, route_ref, gf_ref, ys_ref, op_ref, os_ref,
                  buf_ref, sem):
    i = pl.program_id(0)
    last = pl.num_programs(0) - 1
    slot = i % 2

    @pl.when(i == 0)
    def _():
        buf_ref[...] = jnp.zeros_like(buf_ref)
        _seg_copies(0, n_ref, loc_ref, start_ref, buf_ref.at[0], ys_ref, sem.at[0], False)

    @pl.when(i < last)
    def _():
        _seg_copies(i + 1, n_ref, loc_ref, start_ref, buf_ref.at[1 - slot], ys_ref,
                    sem.at[1 - slot], False)

    _seg_wait(i, n_ref, buf_ref.at[slot], ys_ref, sem.at[slot], False)

    route = route_ref[...]
    s = _iota2((MOE_TOKENS, MOE_SLOTS), 1).astype(F32)
    rows = buf_ref[slot]
    y1 = _dot(jnp.where(s == route[:, 0:1], 1.0, 0.0), rows)
    y2 = _dot(jnp.where(s == route[:, 1:2], 1.0, 0.0), rows)
    y = _rms(h1_ref[...] + route[:, 2:3] * y1 + route[:, 3:4] * y2, gf_ref[...])

    @pl.when(i < T_PROMPT // MOE_TOKENS)
    def _():
        op_ref[...] = y

    @pl.when(i >= T_PROMPT // MOE_TOKENS)
    def _():
        os_ref[...] = y


def _combine_call(sched, h1, route, gfin, y_sorted):
    npt = T_PROMPT // MOE_TOKENS
    grid_spec = pltpu.PrefetchScalarGridSpec(
        num_scalar_prefetch=3,
        grid=(MOE_TILES,),
        in_specs=[pl.BlockSpec((MOE_TOKENS, D_MODEL), lambda i, *_: (i, 0)),
                  pl.BlockSpec((MOE_TOKENS, LANES), lambda i, *_: (i, 0)),
                  pl.BlockSpec((1, D_MODEL), lambda i, *_: (0, 0)),
                  pl.BlockSpec(memory_space=pl.ANY)],
        out_specs=[pl.BlockSpec((MOE_TOKENS, D_MODEL), lambda i, *_: (jnp.minimum(i, npt - 1), 0)),
                   pl.BlockSpec((MOE_TOKENS, D_MODEL), lambda i, *_: (jnp.maximum(i - npt, 0), 0))],
        scratch_shapes=[pltpu.VMEM((2, MOE_SLOTS, D_MODEL), F32),
                        pltpu.SemaphoreType.DMA((2,))])
    return pl.pallas_call(
        _combine_body,
        grid_spec=grid_spec,
        out_shape=[jax.ShapeDtypeStruct((T_PROMPT, D_MODEL), F32),
                   jax.ShapeDtypeStruct((T_SAMPLE, D_MODEL), F32)],
        compiler_params=pltpu.CompilerParams(dimension_semantics=("arbitrary",),
                                             vmem_limit_bytes=VMEM_LIMIT),
        name="moe_combine",
    )(sched["nch"], sched["loc"], sched["start"], h1, route, gfin, y_sorted)


def _moe_schedule(seg):
    tot = jnp.sum(seg, axis=0)
    region = (tot + GMM_ROWS - 1) // GMM_ROWS * GMM_ROWS
    base = jnp.cumsum(region) - region
    start = base[None, :] + jnp.cumsum(seg, axis=0) - seg
    loc = jnp.cumsum(seg, axis=1) - seg
    ntile = region // GMM_ROWS
    cum = jnp.cumsum(ntile)
    nu = cum[-1]
    i = jnp.arange(GMM_TILES, dtype=jnp.int32)[:, None]
    first = (cum - ntile)[None, :]
    mine = (i >= first) & (i < cum[None, :])
    experts = jnp.arange(N_EXPERTS, dtype=jnp.int32)
    te = jnp.sum(jnp.where(mine, experts[None, :], 0), axis=1)
    tv = jnp.sum(jnp.where(mine, jnp.clip(tot[None, :] - (i - first) * GMM_ROWS, 0, GMM_ROWS), 0),
                 axis=1)
    te = jnp.where(i[:, 0] < nu, te, jnp.max(jnp.where(ntile > 0, experts, 0)))
    bstart = jnp.arange(GMM_TILES * GMM_ROWS // GMM_SUB, dtype=jnp.int32)[:, None] * GMM_SUB
    used_end = (base + (tot + GMM_SUB - 1) // GMM_SUB * GMM_SUB)[None, :]
    zb = jnp.any((bstart >= used_end) & (bstart < (base + region)[None, :]), axis=1)
    zb = (zb | (bstart[:, 0] >= jnp.sum(region))).astype(jnp.int32)
    return dict(nch=(seg // SUBLANES).reshape(-1), loc=loc.reshape(-1), start=start.reshape(-1),
                zn=((-tot) % GMM_SUB) // SUBLANES, zstart=base + tot,
                zb=zb, zbn=jnp.sum(zb).reshape(1),
                te=te, tv=tv.astype(jnp.int32), nu=nu.reshape(1).astype(jnp.int32))


def _row(x, width=None):
    x = x.astype(F32).reshape(1, -1)
    if width is not None and x.shape[1] < width:
        x = jnp.pad(x, ((0, 0), (0, width - x.shape[1])))
    return x


def _layer_params(l, lb_p, conv_w, conv_b, a_log, dt_bias, d_skip, hg_norm, ssd_norm):
    lb = jnp.sum(lb_p[1:l + 1], axis=0)
    rows = {P_LA: jnp.log(lb), P_LC: jnp.log1p(-lb), P_OMLB: 1.0 - lb, P_HGN: hg_norm[l],
            P_CB: conv_b[l], P_DTB: dt_bias[l], P_AN: -jnp.exp(a_log[l].astype(F32)),
            P_DSK: jnp.repeat(d_skip[l].astype(F32), SSD_P), P_SSN: ssd_norm[l]}
    rows.update({P_CW + tap: conv_w[l, tap] for tap in range(CONV_W)})
    zero = jnp.zeros((0,), F32)
    return jnp.stack([jnp.pad(rows.get(r, zero).astype(F32), (0, CONV_DIM - rows.get(r, zero).shape[0]))
                      for r in range(P_ROWS)])


def kernel(x_prompt, x_sample, state_hgrn, state_ssm, state_conv, norm_mix, w_in, conv_w, conv_b,
           a_log, dt_bias, d_skip, lb_param, hg_norm, ssd_norm, w_out, norm_ffn, w_ffn_in,
           w_ffn_out, w_router, w_exp_in, w_exp_out, norm_final):
    h = (x_prompt.reshape(T_PROMPT, D_MODEL), x_sample.reshape(T_SAMPLE, D_MODEL))
    lb_p = jax.nn.softmax(lb_param.astype(F32), axis=0)
    packed = (SSD_PAIRS, LANES, SSD_N)
    s_ssm_t = jnp.swapaxes(state_ssm, 3, 4).reshape((DEPTH, DEC_BATCH) + packed)
    s_conv_t = jnp.swapaxes(state_conv, 1, 2)
    outs = {k: [] for k in ("hg_p", "ssm_p", "conv_p", "conv_s")}
    sample_states = None
    for l in range(DEPTH):
        prm = _layer_params(l, lb_p, conv_w, conv_b, a_log, dt_bias, d_skip, hg_norm, ssd_norm)
        proj = _proj_call(l, h, _row(norm_mix[l]), w_in)
        mixed_p, hg_p, ssm_p, conv_p, mixed_s, hg_s, ssm_s, conv_s = _scan_call(
            l, proj, state_hgrn, s_ssm_t, s_conv_t, prm, sample_states)
        sample_states = (hg_s, ssm_s)
        mixed = (mixed_p, mixed_s)
        for k, val in zip(outs, (hg_p, ssm_p, conv_p, conv_s)):
            outs[k].append(val)
        wo_bf = w_out[l].astype(BF16)
        if l % 2 == 0:
            h = (_out_ffn_call(h, mixed, wo_bf, _row(norm_ffn[l]), w_ffn_in[l // 2].astype(BF16),
                               w_ffn_out[l // 2].astype(BF16)),)
        else:
            wr_pad = jnp.pad(w_router[l // 2].astype(F32), ((0, 0), (0, LANES - N_EXPERTS)))
            h1, hn_bf, route, seg = _out_router_call(h[0], mixed, wo_bf, _row(norm_ffn[l]), wr_pad)
            sched = _moe_schedule(seg[:, 0, :N_EXPERTS])
            x_sorted = _dispatch_call(sched, hn_bf, route)
            y_sorted = _experts_call(sched, x_sorted, w_exp_in[l // 2], w_exp_out[l // 2])
            h = _combine_call(sched, h1, route, _row(norm_final), y_sorted)
    y_prompt = h[0].reshape(BATCH, SEQ, D_MODEL)
    y_sample = h[1].reshape(DEC_BATCH, DEC_SEQ, D_MODEL)
    def unpack_ssm(s_t, batch):
        return jnp.swapaxes(s_t.reshape(DEPTH, batch, SSD_HEADS, SSD_P, SSD_N), 3, 4)

    return (y_prompt, y_sample, jnp.stack(outs["hg_p"]), unpack_ssm(jnp.stack(outs["ssm_p"]), BATCH),
            jnp.stack(outs["conv_p"]), sample_states[0], unpack_ssm(sample_states[1], DEC_BATCH),
            jnp.swapaxes(jnp.stack(outs["conv_s"]), 1, 2))
```

```python
import functools

import jax
import jax.numpy as jnp
from jax import lax
from jax.experimental import pallas as pl
from jax.experimental.pallas import tpu as pltpu

F32 = jnp.float32
BF16 = jnp.bfloat16

D_MODEL = 1024
BATCH = 8
SEQ = 2048
DEPTH = 2
DEC_BATCH = 128
DEC_SEQ = 8
HG_HEADS = 4
HG_D = 128
HG_W = HG_HEADS * HG_D
SSD_HEADS = 8
SSD_P = 64
SSD_N = 128
SSD_W = SSD_HEADS * SSD_P
SSD_PAIRS = SSD_HEADS // 2
CONV_W = 4
CONV_DIM = 1024
D_FF = 2816
N_EXPERTS = 8
D_FF_EXPERT = 3584
EPS = 1e-6

LANES = 128
C_Q, C_F, C_I, C_OG, C_Z, C_XBC, C_DT = 0, 512, 1024, 1536, 2048, 2560, 3584
PROJ_W = C_DT + LANES
HG_CHUNK = 64
SSD_CHUNK = 128
SAMPLE_SEQS = 4
PROMPT_SEQS = 4
HG_SAFE_RANGE = 80.0
LOG2 = 0.6931471805599453
P_LA, P_LC, P_OMLB, P_HGN, P_CW, P_CB, P_DTB, P_AN, P_DSK, P_SSN, P_ROWS = 0, 1, 2, 3, 4, 8, 9, 10, 11, 12, 16

T_PROMPT = BATCH * SEQ
T_SAMPLE = DEC_BATCH * DEC_SEQ
T_ALL = T_PROMPT + T_SAMPLE

SUBLANES = 8
TOP_K = 2
MOE_TOKENS = 256
MOE_TILES = T_ALL // MOE_TOKENS
ROUTER_TILES = 1
MOE_SLOTS = TOP_K * MOE_TOKENS + N_EXPERTS * SUBLANES
SEG_COPY_ROWS = (64, 32, 16, 8)
GMM_ROWS = 1024
GMM_SUB = 256
GMM_TILES = (TOP_K * T_ALL + MOE_TILES * N_EXPERTS * (SUBLANES - 1)
             + N_EXPERTS * (GMM_ROWS - 1)) // GMM_ROWS + 1

VMEM_LIMIT = 60000 * 1024


def _sigmoid(x):
    return 0.5 * jnp.tanh(0.5 * x) + 0.5


def _silu(x):
    return x * _sigmoid(x)


def _softplus(x):
    return jnp.maximum(x, 0.0) + jnp.log(1.0 + jnp.exp(-jnp.abs(x)))


def _rms(x, g):
    return x * lax.rsqrt(jnp.mean(x * x, axis=-1, keepdims=True) + EPS) * g


def _dot(a, b):
    return jnp.dot(a.astype(BF16), b.astype(BF16), preferred_element_type=F32)


def _dot_nt(a, b):
    return lax.dot_general(a.astype(BF16), b.astype(BF16), (((1,), (1,)), ((), ())),
                           preferred_element_type=F32)


def _split3(x):
    x1 = x.astype(BF16)
    r1 = x - x1.astype(F32)
    x2 = r1.astype(BF16)
    x3 = (r1 - x2.astype(F32)).astype(BF16)
    return x1, x2, x3


def _mask_dot(m, x):
    mb = m.astype(BF16)
    x1, x2, x3 = _split3(x)
    return (jnp.dot(mb, x1, preferred_element_type=F32) + jnp.dot(mb, x2, preferred_element_type=F32)
            + jnp.dot(mb, x3, preferred_element_type=F32))


def _mask_dot_nt(x, m):
    mb = m.astype(BF16)
    dn = (((1,), (1,)), ((), ()))
    x1, x2, x3 = _split3(x)
    return (lax.dot_general(x1, mb, dn, preferred_element_type=F32)
            + lax.dot_general(x2, mb, dn, preferred_element_type=F32)
            + lax.dot_general(x3, mb, dn, preferred_element_type=F32))


def _dot_f32x3(a, b):
    a1, a2, _ = _split3(a)
    b1, b2, _ = _split3(b)
    return (jnp.dot(a1, b1, preferred_element_type=F32) + jnp.dot(a1, b2, preferred_element_type=F32)
            + jnp.dot(a2, b1, preferred_element_type=F32))


def _iota2(shape, dim):
    return lax.broadcasted_iota(jnp.int32, shape, dim)


def _seq_masks(rows, seq_len):
    t = _iota2((rows, rows), 0)
    s = _iota2((rows, rows), 1)
    same = (t // seq_len) == (s // seq_len)
    causal = same & (s <= t)
    ref = same & ((s % seq_len) < seq_len // 2)
    return causal, ref, same


def _row_specs(arrs, tm):
    if len(arrs) == 1:
        return [pl.BlockSpec((tm, arrs[0].shape[1]), lambda i, *_: (i, 0))]
    npt = T_PROMPT // tm
    return [pl.BlockSpec((tm, arrs[0].shape[1]), lambda i, *_: (jnp.minimum(i, npt - 1), 0)),
            pl.BlockSpec((tm, arrs[1].shape[1]), lambda i, *_: (jnp.maximum(i - npt, 0), 0),
                         pipeline_mode=pl.Buffered(1))]


def _rows(i, refs):
    if len(refs) == 1:
        return refs[0][...]
    return jnp.where(i < T_PROMPT // refs[0].shape[0], refs[0][...], refs[1][...])


def _proj_body(n_h, *refs):
    h_refs, (g_ref, w_ref, o_ref, wb_ref) = refs[:n_h], refs[n_h:]
    i = pl.program_id(0)

    @pl.when(i == 0)
    def _():
        for r0 in range(0, C_DT, HG_W):
            wb_ref[r0:r0 + HG_W, :] = w_ref[r0:r0 + HG_W, :].astype(BF16)
        tail = w_ref[C_DT:, :]
        pad = jnp.zeros((LANES - tail.shape[0], D_MODEL), F32)
        wb_ref[C_DT:, :] = jnp.concatenate([tail, pad], axis=0).astype(BF16)

    hn = _rms(_rows(i, h_refs), g_ref[...])
    o_ref[...] = _dot_nt(hn, wb_ref[...])


def _proj_call(l, h, g, w_in):
    tm = 512
    w_t = jnp.swapaxes(w_in, 1, 2)
    return pl.pallas_call(
        functools.partial(_proj_body, len(h)),
        grid=(T_ALL // tm,),
        in_specs=_row_specs(h, tm) + [pl.BlockSpec((1, D_MODEL), lambda i: (0, 0)),
                                      pl.BlockSpec((None,) + w_t.shape[1:], lambda i: (l, 0, 0),
                                                   pipeline_mode=pl.Buffered(1))],
        out_specs=pl.BlockSpec((tm, PROJ_W), lambda i: (i, 0)),
        out_shape=jax.ShapeDtypeStruct((T_ALL, PROJ_W), F32),
        scratch_shapes=[pltpu.VMEM((PROJ_W, D_MODEL), BF16)],
        compiler_params=pltpu.CompilerParams(dimension_semantics=("arbitrary",),
                                             vmem_limit_bytes=VMEM_LIMIT),
        name="in_proj",
    )(*h, g, w_t)


def _hgrn_gates(p_q, p_f, la, lc, omlb):
    q = _silu(p_q)
    e = jnp.exp(-jnp.abs(p_f))
    b = lc + jnp.minimum(p_f, 0.0) - jnp.log(1.0 + e)
    logf = jnp.maximum(la, b) + jnp.log(1.0 + jnp.exp(-jnp.abs(la - b)))
    r = 1.0 / (1.0 + e)
    kin = omlb * jnp.where(p_f >= 0.0, e * r, r)
    return q, logf, kin


def _hgrn_decays(logf, causal, ref, same):
    rows = logf.shape[0]
    if ref is None:
        g = _mask_dot(causal, logf)
        return g, g[rows // 2 - 1:rows // 2, :], g[rows - 1:rows, :]
    m = jnp.concatenate([causal.astype(F32), ref.astype(F32), same.astype(F32)], axis=0)
    g3 = _mask_dot(m, logf)
    return g3[:rows], g3[rows:2 * rows], g3[2 * rows:]


def _hgrn_exact_intra(ex_ref, q, kin, g, v, seq_len):
    rows = q.shape[0]
    ex_ref[0] = g
    ex_ref[1] = q
    ex_ref[2] = kin
    ex_ref[3] = v
    s_idx = _iota2((rows, HG_W), 0)

    def body(t, carry):
        gt = ex_ref[0, pl.ds(t, 1), :]
        qt = ex_ref[1, pl.ds(t, 1), :]
        live = (s_idx <= t) & (s_idx // seq_len == t // seq_len)
        w = jnp.where(live, qt * ex_ref[2] * jnp.exp(jnp.minimum(gt - ex_ref[0], 0.0)), 0.0)
        outs = []
        for h in range(HG_HEADS):
            sl = slice(h * HG_D, (h + 1) * HG_D)
            score = jnp.sum(w[:, sl], axis=1, keepdims=True)
            outs.append(jnp.sum(score * ex_ref[3, :, sl], axis=0, keepdims=True))
        ex_ref[4, pl.ds(t, 1), :] = jnp.concatenate(outs, axis=1)
        return carry

    lax.fori_loop(0, rows, body, 0)
    return ex_ref[4]


def _hgrn_head_out(o, hgn, og):
    return _rms(o, hgn) * _silu(og)


def _ssd_inputs(conv, p_dt, dtb, a_neg):
    conv = _silu(conv)
    xs = conv[:, :SSD_W]
    bm = conv[:, SSD_W:SSD_W + 2 * SSD_N]
    cm = conv[:, SSD_W + 2 * SSD_N:]
    dtf = _softplus(p_dt + dtb)
    a = dtf * a_neg
    return xs, bm, cm, dtf, a


def _lane_bcast(x, lane, width):
    return jnp.broadcast_to(x[:, lane:lane + 1], (x.shape[0], width))


def _ssd_finish(y, xs, z, dsk, ssn):
    y = (y + dsk * xs) * _silu(z)
    half = SSD_W // 2
    return jnp.concatenate([_rms(y[:, :half], ssn[:, :half]), _rms(y[:, half:], ssn[:, half:])],
                           axis=1)


def _scan_body(n_prev, *refs):
    ns = PROMPT_SEQS
    p_refs, refs = refs[:ns], refs[ns:]
    (ps_ref, shg_ref, sssm_ref, sconv_ref), refs = refs[:4], refs[4:]
    prm_ref, refs = refs[0], refs[1:]
    row = lambda r, width: prm_ref.at[r:r + 1, 0:width]
    la_ref, lc_ref, omlb_ref = row(P_LA, HG_W), row(P_LC, HG_W), row(P_OMLB, HG_W)
    params = (row(P_HGN, HG_W), prm_ref.at[P_CW:P_CW + CONV_W, :], row(P_CB, CONV_DIM),
              row(P_DTB, LANES), row(P_AN, LANES), row(P_DSK, SSD_W), row(P_SSN, SSD_W))
    prev_refs, refs = refs[:2 * bool(n_prev)], refs[2 * bool(n_prev):]
    (mix_ref, ohg_ref, ossm_ref, oconv_ref, mixs_ref, ohgs_ref, ossms_ref, oconvs_ref,
     st_ref, hp_ref, xpad_ref, ex_ref, xpads_ref, exs_ref) = refs
    j = pl.program_id(1)
    tl = SSD_CHUNK

    @pl.when(j == 0)
    def _():
        st_ref[...] = jnp.zeros_like(st_ref)
        hp_ref[...] = jnp.zeros_like(hp_ref)
        xpad_ref[:, 0:8, :] = jnp.zeros((ns, 8, CONV_DIM), F32)

    def decay_bound(z):
        low = jnp.maximum(la_ref[...], lc_ref[...] + jnp.minimum(z, 0.0) - LOG2)
        return jnp.sum(low, axis=0, keepdims=True)

    bounds = [decay_bound(p_refs[s][r0:r0 + HG_CHUNK // 2, C_F:C_F + HG_W])
              for s in range(ns) for r0 in range(0, tl, HG_CHUNK // 2)]
    bounds += [decay_bound(ps_ref[b * DEC_SEQ:(b + 1) * DEC_SEQ, C_F:C_F + HG_W])
               for b in range(SAMPLE_SEQS)]
    safe = jnp.min(functools.reduce(jnp.minimum, bounds)) >= -HG_SAFE_RANGE
    gate_refs = (la_ref, lc_ref, omlb_ref)

    for exact in (False, True):
        @pl.when(safe != exact)
        def _(exact=exact):
            for s in range(ns):
                _scan_prompt_tile(exact, ex_ref, p_refs[s], gate_refs, *params, mix_ref.at[s],
                                  st_ref.at[s], hp_ref.at[s], xpad_ref.at[s])
            seq0 = (pl.program_id(0) * pl.num_programs(1) + j) % (SUBLANES // SAMPLE_SEQS) * SAMPLE_SEQS
            _scan_sample_tile(n_prev, exact, seq0, ps_ref, shg_ref, sssm_ref, sconv_ref, prev_refs,
                              gate_refs, *params, mixs_ref, ohgs_ref, ossms_ref, oconvs_ref,
                              xpads_ref, exs_ref)

    @pl.when(j == pl.num_programs(1) - 1)
    def _():
        for s in range(ns):
            oconv_ref[s] = xpad_ref[s, 5:8, :]
            for h in range(HG_HEADS):
                ohg_ref[s, h] = st_ref[s, h].T
        ossm_ref[...] = hp_ref[...]


def _scan_prompt_tile(exact, ex_ref, p_ref, gate_refs, hgn_ref, cw_ref, cb_ref, dtb_ref, an_ref,
                      dsk_ref, ssn_ref, mix_ref, st_ref, hp_ref, xpad_ref):
    tl = SSD_CHUNK
    causal, _, _ = _seq_masks(HG_CHUNK, HG_CHUNK)
    for c in range(tl // HG_CHUNK):
        r0 = c * HG_CHUNK
        rows = slice(r0, r0 + HG_CHUNK)
        q, logf, kin = _hgrn_gates(p_ref[rows, C_Q:C_Q + HG_W], p_ref[rows, C_F:C_F + HG_W],
                                   *(r[...] for r in gate_refs))
        v = p_ref[rows, C_I:C_I + HG_W]
        g, gmid, glast = _hgrn_decays(logf, causal, None, None)
        if exact:
            qh = q * jnp.exp(g)
            kd = kin * jnp.exp(glast - g)
            o_intra = _hgrn_exact_intra(ex_ref, q, kin, g, v, HG_CHUNK)
        else:
            qt = q * jnp.exp(g - gmid)
            kt = kin * jnp.exp(gmid - g)
            qh = qt * jnp.exp(gmid)
            kd = kt * jnp.exp(glast - gmid)
        ds = jnp.exp(glast[0:1, :])
        for h in range(HG_HEADS):
            sl = slice(h * HG_D, (h + 1) * HG_D)
            st = st_ref[h]
            if exact:
                o = o_intra[:, sl] + _dot_nt(qh[:, sl], st)
            else:
                sc = jnp.where(causal, _dot_nt(qt[:, sl], kt[:, sl]), 0.0)
                o = _dot(sc, v[:, sl]) + _dot_nt(qh[:, sl], st)
            st_ref[h] = st * ds[:, sl] + _dot(v[:, sl].T, kd[:, sl])
            og = p_ref[rows, C_OG + h * HG_D:C_OG + (h + 1) * HG_D]
            mix_ref[rows, sl] = _hgrn_head_out(o, hgn_ref[:, sl], og).astype(BF16)

    xpad_ref[8:8 + tl, :] = p_ref[:, C_XBC:C_XBC + CONV_DIM]
    conv = cb_ref[...] + cw_ref[0:1, :] * xpad_ref[5:5 + tl, :]
    for tap in range(1, CONV_W):
        conv = conv + cw_ref[tap:tap + 1, :] * xpad_ref[5 + tap:5 + tap + tl, :]
    xpad_ref[0:8, :] = xpad_ref[tl:tl + 8, :]

    xs, bm, cm, dtf, a = _ssd_inputs(conv, p_ref[:, C_DT:C_DT + LANES], dtb_ref[...], an_ref[...])
    tri, _, _ = _seq_masks(tl, tl)
    trif = tri.astype(F32)
    acum = _mask_dot(trif, a)
    acum_t = _mask_dot_nt(a.T, trif)
    lo = _iota2((tl, LANES), 1) < SSD_P
    top = _iota2((LANES, tl), 0) < SSD_P
    ys = []
    for pair in range(SSD_PAIRS):
        grp = pair // 2
        cg = cm[:, grp * SSD_N:(grp + 1) * SSD_N]
        bg = bm[:, grp * SSD_N:(grp + 1) * SSD_N]
        cb = _dot_nt(cg, bg)
        psl = slice(pair * LANES, (pair + 1) * LANES)
        r0, r1 = 2 * pair, 2 * pair + 1
        xp = xs[:, psl] * jnp.where(lo, _lane_bcast(dtf, r0, LANES), _lane_bcast(dtf, r1, LANES))
        intra, einter, wrow, ea = [], [], [], []
        for r in (r0, r1):
            col = _lane_bcast(acum, r, tl)
            row = jnp.broadcast_to(acum_t[r:r + 1, :], (tl, tl))
            dec = jnp.exp(jnp.where(tri, col - row, -jnp.inf))
            intra.append(_dot(cb * dec, xp))
            einter.append(jnp.exp(col))
            alast = acum_t[r:r + 1, tl - 1:tl]
            wrow.append(jnp.exp(alast - acum_t[r:r + 1, :]))
            ea.append(jnp.exp(alast))
        hp = hp_ref[pair]
        y = jnp.where(lo, intra[0], intra[1]) + jnp.where(lo, einter[0], einter[1]) * _dot_nt(cg, hp)
        xw_t = xp.T * jnp.where(top, wrow[0], wrow[1])
        hp_ref[pair] = jnp.where(top, ea[0], ea[1]) * hp + _dot(xw_t, bg)
        ys.append(y)
    y = jnp.concatenate(ys, axis=1)
    y = _ssd_finish(y, xs, p_ref[:, C_Z:C_Z + SSD_W], dsk_ref[...], ssn_ref[...])
    mix_ref[:, HG_W:] = y.astype(BF16)


def _scan_call(l, proj, s_hg, s_ssm_t, s_conv_t, prm, prev):
    tl = SSD_CHUNK
    nt = SEQ // tl
    ns, nb = PROMPT_SEQS, SAMPLE_SEQS
    assert BATCH // ns * nt * nb == DEC_BATCH
    srows = nb * DEC_SEQ
    srow0 = T_PROMPT // srows
    step = lambda b, j: b * nt + j
    pstate, sstate = (ns, HG_HEADS, HG_D, HG_D), (nb, HG_HEADS, HG_D, HG_D)
    in_specs = [pl.BlockSpec((tl, PROJ_W), lambda b, j, s=s: ((b * ns + s) * nt + j, 0))
                for s in range(ns)]
    in_specs += [pl.BlockSpec((srows, PROJ_W), lambda b, j: (srow0 + step(b, j), 0)),
                 pl.BlockSpec((None,) + sstate, lambda b, j: (l, step(b, j), 0, 0, 0)),
                 pl.BlockSpec((None,) + sstate, lambda b, j: (l, step(b, j), 0, 0, 0)),
                 pl.BlockSpec((None, CONV_W - 1, SUBLANES, CONV_DIM),
                              lambda b, j: (l, 0, step(b, j) // (SUBLANES // nb), 0))]
    in_specs += [pl.BlockSpec(prm.shape, lambda b, j: (0, 0))]
    in_specs += [pl.BlockSpec((l,) + sstate, lambda b, j: (0, step(b, j), 0, 0, 0))] * (2 if l else 0)
    out_specs = [pl.BlockSpec((ns, tl, D_MODEL), lambda b, j: (b, j, 0)),
                 pl.BlockSpec(pstate, lambda b, j: (b, 0, 0, 0)),
                 pl.BlockSpec(pstate, lambda b, j: (b, 0, 0, 0)),
                 pl.BlockSpec((ns, CONV_W - 1, CONV_DIM), lambda b, j: (b, 0, 0)),
                 pl.BlockSpec((srows, D_MODEL), lambda b, j: (step(b, j), 0)),
                 pl.BlockSpec((l + 1,) + sstate, lambda b, j: (0, step(b, j), 0, 0, 0)),
                 pl.BlockSpec((l + 1,) + sstate, lambda b, j: (0, step(b, j), 0, 0, 0)),
                 pl.BlockSpec((CONV_W - 1, SUBLANES, CONV_DIM),
                              lambda b, j: (0, step(b, j) // (SUBLANES // nb), 0))]
    stacked = jax.ShapeDtypeStruct((l + 1, DEC_BATCH) + sstate[1:], F32)
    out_shape = [jax.ShapeDtypeStruct((BATCH, SEQ, D_MODEL), BF16),
                 jax.ShapeDtypeStruct((BATCH,) + pstate[1:], F32),
                 jax.ShapeDtypeStruct((BATCH,) + pstate[1:], F32),
                 jax.ShapeDtypeStruct((BATCH, CONV_W - 1, CONV_DIM), F32),
                 jax.ShapeDtypeStruct((T_SAMPLE, D_MODEL), BF16), stacked, stacked,
                 jax.ShapeDtypeStruct((CONV_W - 1, DEC_BATCH, CONV_DIM), F32)]
    outs = pl.pallas_call(
        functools.partial(_scan_body, l),
        grid=(BATCH // ns, nt),
        in_specs=in_specs, out_specs=out_specs, out_shape=out_shape,
        scratch_shapes=[pltpu.VMEM(pstate, F32), pltpu.VMEM(pstate, F32),
                        pltpu.VMEM((ns, tl + 8, CONV_DIM), F32),
                        pltpu.VMEM((5, HG_CHUNK, HG_W), F32),
                        pltpu.VMEM((16 * nb, CONV_DIM), F32), pltpu.VMEM((5, srows, HG_W), F32)],
        compiler_params=pltpu.CompilerParams(dimension_semantics=("arbitrary", "arbitrary"),
                                             vmem_limit_bytes=VMEM_LIMIT),
        name="scan",
    )(*([proj] * (ns + 1)), s_hg, s_ssm_t, s_conv_t, prm, *(prev if l else ()))
    return (outs[0].reshape(T_PROMPT, D_MODEL),) + tuple(outs[1:])


def _scan_sample_tile(n_prev, exact, seq0, p_ref, shg_ref, sssm_ref, sconv_ref, prev_refs,
                      gate_refs, hgn_ref, cw_ref, cb_ref, dtb_ref, an_ref, dsk_ref, ssn_ref,
                      mix_ref, ohg_ref, ossm_ref, oconv_ref, xpad_ref, ex_ref):
    if n_prev:
        ohg_ref[0:n_prev] = prev_refs[0][...]
        ossm_ref[0:n_prev] = prev_refs[1][...]
    nb, sl_len = SAMPLE_SEQS, DEC_SEQ
    rows = nb * sl_len
    causal, ref, same = _seq_masks(rows, sl_len)
    rowseq = _iota2((rows, LANES), 0) // sl_len

    q, logf, kin = _hgrn_gates(p_ref[:, C_Q:C_Q + HG_W], p_ref[:, C_F:C_F + HG_W],
                               *(r[...] for r in gate_refs))
    v = p_ref[:, C_I:C_I + HG_W]
    g, gmid, glast = _hgrn_decays(logf, causal, ref, same)
    if exact:
        o_intra_all = _hgrn_exact_intra(ex_ref, q, kin, g, v, sl_len)
    else:
        qt = q * jnp.exp(g - gmid)
        kt = kin * jnp.exp(gmid - g)
    qh = q * jnp.exp(g)
    kd = kin * jnp.exp(glast - g)
    ds_t = jnp.exp(glast).T
    kd_t = kd.T
    for h in range(HG_HEADS):
        sl = slice(h * HG_D, (h + 1) * HG_D)
        if exact:
            o_intra = o_intra_all[:, sl]
        else:
            sc = jnp.where(causal, _dot_nt(qt[:, sl], kt[:, sl]), 0.0)
            o_intra = _dot(sc, v[:, sl])
        o_inter = []
        for b in range(nb):
            s_old = shg_ref[b, h]
            o_inter.append(_dot(qh[b * sl_len:(b + 1) * sl_len, sl], s_old))
            vb = jnp.where(rowseq == b, v[:, sl], 0.0)
            dcol = _lane_bcast(ds_t[sl, :], b * sl_len, HG_D)
            ohg_ref[n_prev, b, h] = dcol * s_old + _dot(kd_t[sl, :], vb)
        o = o_intra + jnp.concatenate(o_inter, axis=0)
        og = p_ref[:, C_OG + h * HG_D:C_OG + (h + 1) * HG_D]
        mix_ref[:, sl] = _hgrn_head_out(o, hgn_ref[:, sl], og).astype(BF16)

    convs = []
    for b in range(nb):
        base = 16 * b
        for r in range(CONV_W - 1):
            xpad_ref[base + 5 + r:base + 6 + r, :] = sconv_ref[r, pl.ds(seq0 + b, 1), :]
        xpad_ref[base + 8:base + 16, :] = p_ref[b * sl_len:(b + 1) * sl_len, C_XBC:C_XBC + CONV_DIM]
        cv = cb_ref[...] + cw_ref[0:1, :] * xpad_ref[base + 5:base + 13, :]
        for tap in range(1, CONV_W):
            cv = cv + cw_ref[tap:tap + 1, :] * xpad_ref[base + 5 + tap:base + 13 + tap, :]
        convs.append(cv)
        for r in range(CONV_W - 1):
            oconv_ref[r, pl.ds(seq0 + b, 1), :] = xpad_ref[base + 13 + r:base + 14 + r, :]
    conv = jnp.concatenate(convs, axis=0)

    xs, bm, cm, dtf, a = _ssd_inputs(conv, p_ref[:, C_DT:C_DT + LANES], dtb_ref[...], an_ref[...])
    causf = causal.astype(F32)
    acum = _mask_dot(causf, a)
    a_t = a.T
    acum_t = _mask_dot_nt(a_t, causf)
    atot_t = _mask_dot_nt(a_t, same.astype(F32))
    wall_t = jnp.exp(atot_t - acum_t)
    eatot_t = jnp.exp(atot_t)
    eall = jnp.exp(acum)
    lo = _iota2((rows, LANES), 1) < SSD_P
    top = _iota2((LANES, rows), 0) < SSD_P
    top_sq = _iota2((LANES, SSD_N), 0) < SSD_P
    colseq = _iota2((LANES, rows), 1) // sl_len
    ys = []
    for pair in range(SSD_PAIRS):
        grp = pair // 2
        cg = cm[:, grp * SSD_N:(grp + 1) * SSD_N]
        bg = bm[:, grp * SSD_N:(grp + 1) * SSD_N]
        cb = _dot_nt(cg, bg)
        psl = slice(pair * LANES, (pair + 1) * LANES)
        r0, r1 = 2 * pair, 2 * pair + 1
        xp = xs[:, psl] * jnp.where(lo, _lane_bcast(dtf, r0, LANES), _lane_bcast(dtf, r1, LANES))
        xw_t = xp.T * jnp.where(top, wall_t[r0:r0 + 1, :], wall_t[r1:r1 + 1, :])
        intra = []
        for r in (r0, r1):
            col = _lane_bcast(acum, r, rows)
            row = jnp.broadcast_to(acum_t[r:r + 1, :], (rows, rows))
            dec = jnp.exp(jnp.where(causal, col - row, -jnp.inf))
            intra.append(_dot(cb * dec, xp))
        y_intra = jnp.where(lo, intra[0], intra[1])
        y_inter = []
        for b in range(nb):
            tr = slice(b * sl_len, (b + 1) * sl_len)
            h_old = sssm_ref[b, pair]
            y_inter.append(_dot_nt(cg[tr, :], h_old))
            upd = _dot(jnp.where(colseq == b, xw_t, 0.0), bg)
            c0 = b * sl_len
            ea = jnp.where(top_sq, eatot_t[r0:r0 + 1, c0:c0 + 1], eatot_t[r1:r1 + 1, c0:c0 + 1])
            ossm_ref[n_prev, b, pair] = ea * h_old + upd
        e_pair = jnp.where(lo, _lane_bcast(eall, r0, LANES), _lane_bcast(eall, r1, LANES))
        ys.append(y_intra + e_pair * jnp.concatenate(y_inter, axis=0))
    y = jnp.concatenate(ys, axis=1)
    y = _ssd_finish(y, xs, p_ref[:, C_Z:C_Z + SSD_W], dsk_ref[...], ssn_ref[...])
    mix_ref[:, HG_W:] = y.astype(BF16)


def _out_ffn_body(n_h, *refs):
    h_refs, mix_refs = refs[:n_h], refs[n_h:n_h + 2]
    wo_ref, g_ref, wg_ref, wu_ref, wd_ref, o_ref, hn_ref = refs[n_h + 2:]
    i = pl.program_id(0)
    f = pl.program_id(1)

    @pl.when(f == 0)
    def _():
        h1 = _rows(i, h_refs) + _dot(_rows(i, mix_refs), wo_ref[...])
        o_ref[...] = h1
        hn_ref[...] = _rms(h1, g_ref[...]).astype(BF16)

    o_ref[...] += _swiglu_part(hn_ref[...], wg_ref[...], wu_ref[...], wd_ref[...])


def _out_ffn_call(h, mixed, wo_bf, g, wi_bf, wd_bf):
    tm, tf = 1024, D_FF // 2
    nf = D_FF // tf
    return pl.pallas_call(
        functools.partial(_out_ffn_body, len(h)),
        grid=(T_ALL // tm, nf),
        in_specs=_row_specs(h, tm) + _row_specs(mixed, tm) + [
                  pl.BlockSpec((D_MODEL, D_MODEL), lambda i, f: (0, 0)),
                  pl.BlockSpec((1, D_MODEL), lambda i, f: (0, 0)),
                  pl.BlockSpec((D_MODEL, tf), lambda i, f: (0, f)),
                  pl.BlockSpec((D_MODEL, tf), lambda i, f: (0, nf + f)),
                  pl.BlockSpec((tf, D_MODEL), lambda i, f: (f, 0))],
        out_specs=pl.BlockSpec((tm, D_MODEL), lambda i, f: (i, 0)),
        out_shape=jax.ShapeDtypeStruct((T_ALL, D_MODEL), F32),
        scratch_shapes=[pltpu.VMEM((tm, D_MODEL), BF16)],
        compiler_params=pltpu.CompilerParams(dimension_semantics=("arbitrary", "arbitrary"),
                                             vmem_limit_bytes=VMEM_LIMIT),
        name="out_ffn",
    )(*h, *mixed, wo_bf, g, wi_bf, wi_bf, wd_bf)


def _out_router_body(h_ref, mixp_ref, mixs_ref, wo_ref, g_ref, wr_ref,
                     h1_ref, hn_ref, route_ref, seg_ref):
    h1 = h_ref[...] + _dot(_rows(pl.program_id(0), (mixp_ref, mixs_ref)), wo_ref[...])
    h1_ref[...] = h1
    hn = _rms(h1, g_ref[...])
    hn_ref[...] = hn.astype(BF16)
    logits = _dot_f32x3(hn, wr_ref[...])
    lane = _iota2(logits.shape, 1)
    lg = jnp.where(lane < N_EXPERTS, logits, -jnp.inf)
    m1 = jnp.max(lg, axis=1, keepdims=True)
    i1 = jnp.min(jnp.where(lg == m1, lane, LANES), axis=1, keepdims=True)
    lg2 = jnp.where(lane == i1, -jnp.inf, lg)
    m2 = jnp.max(lg2, axis=1, keepdims=True)
    i2 = jnp.min(jnp.where(lg2 == m2, lane, LANES), axis=1, keepdims=True)
    e2 = jnp.exp(m2 - m1)
    g1 = 1.0 / (1.0 + e2)
    g2 = e2 / (1.0 + e2)

    tm = logits.shape[0]
    chosen = jnp.where((lane == i1) | (lane == i2), 1.0, 0.0)
    t_row, t_col = _iota2((tm, tm), 0), _iota2((tm, tm), 1)
    earlier = ((t_col < t_row) & (t_col // MOE_TOKENS == t_row // MOE_TOKENS)).astype(F32)
    rank = _dot(earlier, chosen)
    below = (_iota2((LANES, LANES), 0) < _iota2((LANES, LANES), 1)).astype(F32)
    pos = []
    for k in range(tm // MOE_TOKENS):
        rows = slice(k * MOE_TOKENS, (k + 1) * MOE_TOKENS)
        cnt = jnp.sum(chosen[rows], axis=0, keepdims=True)
        seg = jnp.floor((cnt + (SUBLANES - 1)) * (1.0 / SUBLANES)) * SUBLANES
        seg8 = jnp.broadcast_to(seg, (SUBLANES, LANES))
        pos.append(_dot(seg8, below)[0:1, :] + rank[rows])
        seg_ref[k] = seg8.astype(jnp.int32)
    pos = jnp.concatenate(pos, axis=0)
    p1 = jnp.sum(jnp.where(lane == i1, pos, 0.0), axis=1, keepdims=True)
    p2 = jnp.sum(jnp.where(lane == i2, pos, 0.0), axis=1, keepdims=True)
    route_ref[...] = jnp.where(lane == 0, p1, jnp.where(lane == 1, p2, jnp.where(
        lane == 2, g1, jnp.where(lane == 3, g2, 0.0))))


def _out_router_call(h, mixed, wo_bf, g, wr_pad):
    tm = ROUTER_TILES * MOE_TOKENS
    return pl.pallas_call(
        _out_router_body,
        grid=(T_ALL // tm,),
        in_specs=[pl.BlockSpec((tm, D_MODEL), lambda i: (i, 0))] + _row_specs(mixed, tm) + [
                  pl.BlockSpec((D_MODEL, D_MODEL), lambda i: (0, 0)),
                  pl.BlockSpec((1, D_MODEL), lambda i: (0, 0)),
                  pl.BlockSpec((D_MODEL, LANES), lambda i: (0, 0))],
        out_specs=[pl.BlockSpec((tm, D_MODEL), lambda i: (i, 0)),
                   pl.BlockSpec((tm, D_MODEL), lambda i: (i, 0)),
                   pl.BlockSpec((tm, LANES), lambda i: (i, 0)),
                   pl.BlockSpec((ROUTER_TILES, SUBLANES, LANES), lambda i: (i, 0, 0))],
        out_shape=[jax.ShapeDtypeStruct((T_ALL, D_MODEL), F32),
                   jax.ShapeDtypeStruct((T_ALL, D_MODEL), BF16),
                   jax.ShapeDtypeStruct((T_ALL, LANES), F32),
                   jax.ShapeDtypeStruct((MOE_TILES, SUBLANES, LANES), jnp.int32)],
        compiler_params=pltpu.CompilerParams(dimension_semantics=("arbitrary",),
                                             vmem_limit_bytes=VMEM_LIMIT),
        name="out_router",
    )(h, *mixed, wo_bf, g, wr_pad)


def _row_tile_copy(tile_ref, hbm_ref, tile_row, hbm_row, sem, to_hbm, rows=SUBLANES):
    t = tile_ref.at[pl.ds(pl.multiple_of(tile_row, SUBLANES), rows), :]
    g = hbm_ref.at[pl.ds(pl.multiple_of(hbm_row, SUBLANES), rows), :]
    return pltpu.make_async_copy(t, g, sem) if to_hbm else pltpu.make_async_copy(g, t, sem)


def _seg_plan(n):
    counts = [lax.shift_right_logical(n, 3)]
    for shift in (2, 1, 0):
        counts.append(lax.shift_right_logical(n, shift) & 1)
    return counts


def _seg_copies(i, n_ref, loc_ref, start_ref, tile_ref, hbm_ref, sem, to_hbm):
    for e in range(N_EXPERTS):
        k = i * N_EXPERTS + e
        lo = loc_ref[k]
        st = start_ref[k]
        counts = _seg_plan(n_ref[k])
        big = SEG_COPY_ROWS[0]
        prio = e % 2

        def body(c, carry, lo=lo, st=st, prio=prio):
            _row_tile_copy(tile_ref, hbm_ref, lo + c * big, st + c * big, sem, to_hbm,
                           big).start(priority=prio)
            return carry

        lax.fori_loop(0, counts[0], body, 0)
        done = counts[0] * big
        for rows, cnt in zip(SEG_COPY_ROWS[1:], counts[1:]):
            @pl.when(cnt == 1)
            def _(rows=rows, done=done, lo=lo, st=st, prio=prio):
                _row_tile_copy(tile_ref, hbm_ref, lo + done, st + done, sem, to_hbm,
                               rows).start(priority=prio)

            done = done + cnt * rows


def _seg_wait(i, n_ref, tile_ref, hbm_ref, sem, to_hbm):
    totals = None
    for e in range(N_EXPERTS):
        counts = _seg_plan(n_ref[i * N_EXPERTS + e])
        totals = counts if totals is None else [a + b for a, b in zip(totals, counts)]
    for rows, total in zip(SEG_COPY_ROWS, totals):
        def body(c, carry, rows=rows):
            _row_tile_copy(tile_ref, hbm_ref, 0, 0, sem, to_hbm, rows).wait()
            return carry

        lax.fori_loop(0, total, body, 0)


def _dispatch_body(n_ref, loc_ref, start_ref, zn_ref, zstart_ref, zb_ref, zbn_ref,
                   hn_ref, route_ref, xs_ref, stage_ref, zero_ref, sem):
    i = pl.program_id(0)
    last = pl.num_programs(0) - 1
    slot = i % 2

    def blk_copy(b):
        return pltpu.make_async_copy(
            zero_ref, xs_ref.at[pl.ds(pl.multiple_of(b * GMM_SUB, GMM_SUB), GMM_SUB), :], sem.at[3])

    @pl.when(i == 0)
    def _():
        zero_ref[...] = jnp.zeros_like(zero_ref)

        def bbody(b, carry):
            @pl.when(zb_ref[b] == 1)
            def _():
                blk_copy(b).start()
            return carry

        lax.fori_loop(0, GMM_TILES * GMM_ROWS // GMM_SUB, bbody, 0)
        for e in range(N_EXPERTS):
            st = zstart_ref[e]

            def body(c, carry, st=st):
                _row_tile_copy(zero_ref, xs_ref, 0, st + c * SUBLANES, sem.at[2], True).start()
                return carry

            lax.fori_loop(0, zn_ref[e], body, 0)

    @pl.when(i >= 2)
    def _():
        _seg_wait(i - 2, n_ref, stage_ref.at[slot], xs_ref, sem.at[slot], True)

    rt = route_ref[...].T
    s = _iota2((MOE_SLOTS, MOE_TOKENS), 0).astype(F32)
    perm = jnp.where((s == rt[0:1, :]) | (s == rt[1:2, :]), 1.0, 0.0)
    stage_ref[slot] = _dot(perm, hn_ref[...])
    _seg_copies(i, n_ref, loc_ref, start_ref, stage_ref.at[slot], xs_ref, sem.at[slot], True)

    @pl.when(i == last)
    def _():
        @pl.when(last >= 1)
        def _():
            _seg_wait(i - 1, n_ref, stage_ref.at[1 - slot], xs_ref, sem.at[1 - slot], True)

        _seg_wait(i, n_ref, stage_ref.at[slot], xs_ref, sem.at[slot], True)
        nz = zn_ref[0]
        for e in range(1, N_EXPERTS):
            nz = nz + zn_ref[e]

        def zbody(c, carry):
            _row_tile_copy(zero_ref, xs_ref, 0, 0, sem.at[2], True).wait()
            return carry

        lax.fori_loop(0, nz, zbody, 0)

        def bwait(c, carry):
            blk_copy(0).wait()
            return carry

        lax.fori_loop(0, zbn_ref[0], bwait, 0)


def _dispatch_call(sched, hn_bf, route):
    grid_spec = pltpu.PrefetchScalarGridSpec(
        num_scalar_prefetch=7,
        grid=(MOE_TILES,),
        in_specs=[pl.BlockSpec((MOE_TOKENS, D_MODEL), lambda i, *_: (i, 0)),
                  pl.BlockSpec((MOE_TOKENS, LANES), lambda i, *_: (i, 0))],
        out_specs=pl.BlockSpec(memory_space=pl.ANY),
        scratch_shapes=[pltpu.VMEM((2, MOE_SLOTS, D_MODEL), F32),
                        pltpu.VMEM((GMM_SUB, D_MODEL), F32),
                        pltpu.SemaphoreType.DMA((4,))])
    return pl.pallas_call(
        _dispatch_body,
        grid_spec=grid_spec,
        out_shape=jax.ShapeDtypeStruct((GMM_TILES * GMM_ROWS, D_MODEL), F32),
        compiler_params=pltpu.CompilerParams(dimension_semantics=("arbitrary",),
                                             vmem_limit_bytes=VMEM_LIMIT),
        name="moe_dispatch",
    )(sched["nch"], sched["loc"], sched["start"], sched["zn"], sched["zstart"], sched["zb"],
      sched["zbn"], hn_bf, route)


def _swiglu_part(x, wg, wu, wd):
    gate = jnp.dot(x, wg, preferred_element_type=F32)
    up = jnp.dot(x, wu, preferred_element_type=F32)
    return jnp.dot((_silu(gate) * up).astype(BF16), wd, preferred_element_type=F32)


def _experts_body(te_ref, tv_ref, nu_ref, x_ref, wg_ref, wu_ref, wd_ref, o_ref,
                  wgb_ref, wub_ref, wdb_ref):
    del te_ref, nu_ref
    i = pl.program_id(0)
    f = pl.program_id(1)
    nv = tv_ref[i]

    @pl.when(f == 0)
    def _():
        o_ref[...] = jnp.zeros_like(o_ref)

    @pl.when(nv == GMM_ROWS)
    def _():
        o_ref[...] += _swiglu_part(x_ref[...].astype(BF16), wg_ref[0].astype(BF16),
                                   wu_ref[0].astype(BF16), wd_ref[0].astype(BF16))

    @pl.when((nv > 0) & (nv < GMM_ROWS))
    def _():
        wgb_ref[...] = wg_ref[0].astype(BF16)
        wub_ref[...] = wu_ref[0].astype(BF16)
        wdb_ref[...] = wd_ref[0].astype(BF16)
        for sub in range(GMM_ROWS // GMM_SUB):
            rows = slice(sub * GMM_SUB, (sub + 1) * GMM_SUB)

            @pl.when(sub * GMM_SUB < nv)
            def _(rows=rows):
                o_ref[rows, :] += _swiglu_part(x_ref[rows, :].astype(BF16), wgb_ref[...],
                                               wub_ref[...], wdb_ref[...])


def _experts_call(sched, x_sorted, w_in_e, w_out_e):
    tf = 512
    nf = D_FF_EXPERT // tf

    def used(i, nu):
        return jnp.maximum(jnp.minimum(i, nu[0] - 1), 0)

    def fidx(i, f, nu):
        last = nu[0] - 1
        return jnp.where(i <= last, jnp.where(i % 2 == 0, f, nf - 1 - f),
                         jnp.where(last % 2 == 0, nf - 1, 0))

    grid_spec = pltpu.PrefetchScalarGridSpec(
        num_scalar_prefetch=3,
        grid=(GMM_TILES, nf),
        in_specs=[pl.BlockSpec((GMM_ROWS, D_MODEL), lambda i, f, te, tv, nu: (used(i, nu), 0)),
                  pl.BlockSpec((1, D_MODEL, tf), lambda i, f, te, tv, nu: (te[i], 0, fidx(i, f, nu))),
                  pl.BlockSpec((1, D_MODEL, tf),
                               lambda i, f, te, tv, nu: (te[i], 0, nf + fidx(i, f, nu))),
                  pl.BlockSpec((1, tf, D_MODEL), lambda i, f, te, tv, nu: (te[i], fidx(i, f, nu), 0))],
        out_specs=pl.BlockSpec((GMM_ROWS, D_MODEL), lambda i, f, te, tv, nu: (i, 0)),
        scratch_shapes=[pltpu.VMEM((D_MODEL, tf), BF16), pltpu.VMEM((D_MODEL, tf), BF16),
                        pltpu.VMEM((tf, D_MODEL), BF16)])
    return pl.pallas_call(
        _experts_body,
        grid_spec=grid_spec,
        out_shape=jax.ShapeDtypeStruct((GMM_TILES * GMM_ROWS, D_MODEL), F32),
        compiler_params=pltpu.CompilerParams(dimension_semantics=("arbitrary", "arbitrary"),
                                             vmem_limit_bytes=VMEM_LIMIT),
        name="moe_experts",
    )(sched["te"], sched["tv"], sched["nu"], x_sorted, w_in_e, w_in_e, w_out_e)


def _combine_body(n_ref, loc_ref, start_ref, h1_ref, route_ref, gf_ref, ys_ref, op_ref, os_ref,
                  buf_ref, sem):
    i = pl.program_id(0)
    last = pl.num_programs(0) - 1
    slot = i % 2

    @pl.when(i == 0)
    def _():
        buf_ref[...] = jnp.zeros_like(buf_ref)
        _seg_copies(0, n_ref, loc_ref, start_ref, buf_ref.at[0], ys_ref, sem.at[0], False)

    @pl.when(i < last)
    def _():
        _seg_copies(i + 1, n_ref, loc_ref, start_ref, buf_ref.at[1 - slot], ys_ref,
                    sem.at[1 - slot], False)

    _seg_wait(i, n_ref, buf_ref.at[slot], ys_ref, sem.at[slot], False)

    route = route_ref[...]
    s = _iota2((MOE_TOKENS, MOE_SLOTS), 1).astype(F32)
    rows = buf_ref[slot]
    y1 = _dot(jnp.where(s == route[:, 0:1], 1.0, 0.0), rows)
    y2 = _dot(jnp.where(s == route[:, 1:2], 1.0, 0.0), rows)
    y = _rms(h1_ref[...] + route[:, 2:3] * y1 + route[:, 3:4] * y2, gf_ref[...])

    @pl.when(i < T_PROMPT // MOE_TOKENS)
    def _():
        op_ref[...] = y

    @pl.when(i >= T_PROMPT // MOE_TOKENS)
    def _():
        os_ref[...] = y


def _combine_call(sched, h1, route, gfin, y_sorted):
    npt = T_PROMPT // MOE_TOKENS
    grid_spec = pltpu.PrefetchScalarGridSpec(
        num_scalar_prefetch=3,
        grid=(MOE_TILES,),
        in_specs=[pl.BlockSpec((MOE_TOKENS, D_MODEL), lambda i, *_: (i, 0)),
                  pl.BlockSpec((MOE_TOKENS, LANES), lambda i, *_: (i, 0)),
                  pl.BlockSpec((1, D_MODEL), lambda i, *_: (0, 0)),
                  pl.BlockSpec(memory_space=pl.ANY)],
        out_specs=[pl.BlockSpec((MOE_TOKENS, D_MODEL), lambda i, *_: (jnp.minimum(i, npt - 1), 0)),
                   pl.BlockSpec((MOE_TOKENS, D_MODEL), lambda i, *_: (jnp.maximum(i - npt, 0), 0))],
        scratch_shapes=[pltpu.VMEM((2, MOE_SLOTS, D_MODEL), F32),
                        pltpu.SemaphoreType.DMA((2,))])
    return pl.pallas_call(
        _combine_body,
        grid_spec=grid_spec,
        out_shape=[jax.ShapeDtypeStruct((T_PROMPT, D_MODEL), F32),
                   jax.ShapeDtypeStruct((T_SAMPLE, D_MODEL), F32)],
        compiler_params=pltpu.CompilerParams(dimension_semantics=("arbitrary",),
                                             vmem_limit_bytes=VMEM_LIMIT),
        name="moe_combine",
    )(sched["nch"], sched["loc"], sched["start"], h1, route, gfin, y_sorted)


def _moe_schedule(seg):
    tot = jnp.sum(seg, axis=0)
    region = (tot + GMM_ROWS - 1) // GMM_ROWS * GMM_ROWS
    base = jnp.cumsum(region) - region
    start = base[None, :] + jnp.cumsum(seg, axis=0) - seg
    loc = jnp.cumsum(seg, axis=1) - seg
    ntile = region // GMM_ROWS
    cum = jnp.cumsum(ntile)
    nu = cum[-1]
    i = jnp.arange(GMM_TILES, dtype=jnp.int32)[:, None]
    first = (cum - ntile)[None, :]
    mine = (i >= first) & (i < cum[None, :])
    experts = jnp.arange(N_EXPERTS, dtype=jnp.int32)
    te = jnp.sum(jnp.where(mine, experts[None, :], 0), axis=1)
    tv = jnp.sum(jnp.where(mine, jnp.clip(tot[None, :] - (i - first) * GMM_ROWS, 0, GMM_ROWS), 0),
                 axis=1)
    te = jnp.where(i[:, 0] < nu, te, jnp.max(jnp.where(ntile > 0, experts, 0)))
    bstart = jnp.arange(GMM_TILES * GMM_ROWS // GMM_SUB, dtype=jnp.int32)[:, None] * GMM_SUB
    used_end = (base + (tot + GMM_SUB - 1) // GMM_SUB * GMM_SUB)[None, :]
    zb = jnp.any((bstart >= used_end) & (bstart < (base + region)[None, :]), axis=1)
    zb = (zb | (bstart[:, 0] >= jnp.sum(region))).astype(jnp.int32)
    return dict(nch=(seg // SUBLANES).reshape(-1), loc=loc.reshape(-1), start=start.reshape(-1),
                zn=((-tot) % GMM_SUB) // SUBLANES, zstart=base + tot,
                zb=zb, zbn=jnp.sum(zb).reshape(1),
                te=te, tv=tv.astype(jnp.int32), nu=nu.reshape(1).astype(jnp.int32))


def _row(x, width=None):
    x = x.astype(F32).reshape(1, -1)
    if width is not None and x.shape[1] < width:
        x = jnp.pad(x, ((0, 0), (0, width - x.shape[1])))
    return x


def _layer_params(l, lb_p, conv_w, conv_b, a_log, dt_bias, d_skip, hg_norm, ssd_norm):
    lb = jnp.sum(lb_p[1:l + 1], axis=0)
    rows = {P_LA: jnp.log(lb), P_LC: jnp.log1p(-lb), P_OMLB: 1.0 - lb, P_HGN: hg_norm[l],
            P_CB: conv_b[l], P_DTB: dt_bias[l], P_AN: -jnp.exp(a_log[l].astype(F32)),
            P_DSK: jnp.repeat(d_skip[l].astype(F32), SSD_P), P_SSN: ssd_norm[l]}
    rows.update({P_CW + tap: conv_w[l, tap] for tap in range(CONV_W)})
    pieces = []
    for r in range(P_ROWS):
        used = rows[r].shape[0] if r in rows else 0
        if used:
            pieces.append(rows[r].astype(F32))
        if used < CONV_DIM:
            pieces.append(jnp.zeros((CONV_DIM - used,), F32))
    return jnp.concatenate(pieces).reshape(P_ROWS, CONV_DIM)


def kernel(x_prompt, x_sample, state_hgrn, state_ssm, state_conv, norm_mix, w_in, conv_w, conv_b,
           a_log, dt_bias, d_skip, lb_param, hg_norm, ssd_norm, w_out, norm_ffn, w_ffn_in,
           w_ffn_out, w_router, w_exp_in, w_exp_out, norm_final):
    h = (x_prompt.reshape(T_PROMPT, D_MODEL), x_sample.reshape(T_SAMPLE, D_MODEL))
    lb_p = jax.nn.softmax(lb_param.astype(F32), axis=0)
    packed = (SSD_PAIRS, LANES, SSD_N)
    s_ssm_t = jnp.swapaxes(state_ssm, 3, 4).reshape((DEPTH, DEC_BATCH) + packed)
    s_conv_t = jnp.swapaxes(state_conv, 1, 2)
    outs = {k: [] for k in ("hg_p", "ssm_p", "conv_p", "conv_s")}
    sample_states = None
    for l in range(DEPTH):
        prm = _layer_params(l, lb_p, conv_w, conv_b, a_log, dt_bias, d_skip, hg_norm, ssd_norm)
        proj = _proj_call(l, h, _row(norm_mix[l]), w_in)
        mixed_p, hg_p, ssm_p, conv_p, mixed_s, hg_s, ssm_s, conv_s = _scan_call(
            l, proj, state_hgrn, s_ssm_t, s_conv_t, prm, sample_states)
        sample_states = (hg_s, ssm_s)
        mixed = (mixed_p, mixed_s)
        for k, val in zip(outs, (hg_p, ssm_p, conv_p, conv_s)):
            outs[k].append(val)
        wo_bf = w_out[l].astype(BF16)
        if l % 2 == 0:
            h = (_out_ffn_call(h, mixed, wo_bf, _row(norm_ffn[l]), w_ffn_in[l // 2].astype(BF16),
                               w_ffn_out[l // 2].astype(BF16)),)
        else:
            wr_pad = jnp.pad(w_router[l // 2].astype(F32), ((0, 0), (0, LANES - N_EXPERTS)))
            h1, hn_bf, route, seg = _out_router_call(h[0], mixed, wo_bf, _row(norm_ffn[l]), wr_pad)
            sched = _moe_schedule(seg[:, 0, :N_EXPERTS])
            x_sorted = _dispatch_call(sched, hn_bf, route)
            y_sorted = _experts_call(sched, x_sorted, w_exp_in[l // 2], w_exp_out[l // 2])
            h = _combine_call(sched, h1, route, _row(norm_final), y_sorted)
    y_prompt = h[0].reshape(BATCH, SEQ, D_MODEL)
    y_sample = h[1].reshape(DEC_BATCH, DEC_SEQ, D_MODEL)
    def unpack_ssm(s_t, batch):
        return jnp.swapaxes(s_t.reshape(DEPTH, batch, SSD_HEADS, SSD_P, SSD_N), 3, 4)

    return (y_prompt, y_sample, jnp.stack(outs["hg_p"]), unpack_ssm(jnp.stack(outs["ssm_p"]), BATCH),
            jnp.stack(outs["conv_p"]), sample_states[0], unpack_ssm(sample_states[1], DEC_BATCH),
            jnp.swapaxes(jnp.stack(outs["conv_s"]), 1, 2))
```
